```python
import jax, jax.numpy as jnp
from jax import lax
import numpy as np

D_MODEL = 1024
BATCH = 16
SEQ = 2048
DEPTH = 1

RWKV_HEAD_DIM = 64
RWKV_WIDTH = D_MODEL // 2
RWKV_HEADS = RWKV_WIDTH // RWKV_HEAD_DIM
W_LORA = 64
A_LORA = 64
G_LORA = 128
RWKV_GN_EPS = RWKV_HEAD_DIM * 1e-5
N_DIRECTIONS = 2
ATT_HEAD_DIM = 64
ATT_WIDTH = D_MODEL // 2
ATT_HEADS = ATT_WIDTH // ATT_HEAD_DIM
ATT_KV_HEADS = 2
ATT_GROUP = ATT_HEADS // ATT_KV_HEADS
KV_WIDTH = ATT_KV_HEADS * ATT_HEAD_DIM
WINDOW = 128
BLOCK = 128
ROPE_THETA = 500000.0
ROT_DIM = ATT_HEAD_DIM // 4
N_BRANCHES = 2
N_EXPERTS = 16
EXPERT_FF = D_MODEL
CAPACITY_FACTOR = 2
NORM_EPS = 1e-6

RWKV_COLS = 3 * RWKV_WIDTH + N_DIRECTIONS * W_LORA + N_DIRECTIONS * A_LORA + G_LORA
IN_COLS = RWKV_COLS + ATT_WIDTH + 2 * KV_WIDTH + N_BRANCHES * D_MODEL

kernel_name = 'hybrid_rwkv7_swa_ecmoe_block'


def rms_norm(x, g):
    xf = x.astype(jnp.float32)
    y = xf * lax.rsqrt(jnp.mean(xf * xf, axis=-1, keepdims=True) + NORM_EPS)
    return (y * g.astype(jnp.float32)).astype(x.dtype)


def centred_token_shift(z, mu_prev, mu_next):
    z_prev = jnp.pad(z[:, :-1], ((0, 0), (1, 0), (0, 0)))
    z_next = jnp.pad(z[:, 1:], ((0, 0), (0, 1), (0, 0)))
    return z + mu_prev * (z_prev - z) + mu_next * (z_next - z)


def rwkv7_scan(r, w, k, v, a, b, reverse):
    B, T, H, N = r.shape

    def step(S, inp):
        r_t, w_t, k_t, v_t, a_t, b_t = inp
        sa = jnp.einsum('bhij,bhj->bhi', S, a_t)
        S = (S * w_t[:, :, None, :] + sa[..., None] * b_t[:, :, None, :]
             + v_t[..., None] * k_t[:, :, None, :])
        y = jnp.einsum('bhij,bhj->bhi', S, r_t)
        return S, y

    xs = tuple(jnp.moveaxis(t, 1, 0) for t in (r, w, k, v, a, b))
    S0 = jnp.zeros((B, H, N, N), jnp.float32)
    _, y = lax.scan(step, S0, xs, reverse=reverse)
    return jnp.moveaxis(y, 0, 1)


def rwkv7_branch(z, w0, w2, a0, a2, g2, k_k, k_a, r_k, gn_w, gn_b):
    B, T, _ = z.shape
    H, N, RW = RWKV_HEADS, RWKV_HEAD_DIM, RWKV_WIDTH
    f32 = jnp.float32
    splits = [RW, 2 * RW, 3 * RW, 3 * RW + N_DIRECTIONS * W_LORA,
              3 * RW + N_DIRECTIONS * (W_LORA + A_LORA)]
    r, k, v, wd, ad, gd = jnp.split(z.astype(f32), splits, axis=-1)
    wd = wd.reshape(B, T, N_DIRECTIONS, W_LORA)
    ad = ad.reshape(B, T, N_DIRECTIONS, A_LORA)
    w = -jax.nn.softplus(-(w0.astype(f32) + jnp.einsum('btdl,dlc->btdc', jnp.tanh(wd), w2.astype(f32)))) - 0.5
    decay = jnp.exp(-jnp.exp(w))
    a = jax.nn.sigmoid(a0.astype(f32) + jnp.einsum('btdl,dlc->btdc', ad, a2.astype(f32)))
    g = jax.nn.sigmoid(gd) @ g2.astype(f32)

    def heads(t):
        return t.reshape(B, T, H, N)

    kk = heads(k * k_k.astype(f32))
    kk = kk / jnp.maximum(jnp.sqrt(jnp.sum(kk * kk, axis=-1, keepdims=True)), 1e-12)
    ys = []
    ks = []
    for d, rev in ((0, False), (1, True)):
        a_d = a[:, :, d]
        k_d = k * (1.0 + (a_d - 1.0) * k_a.astype(f32))
        ys.append(rwkv7_scan(heads(r), heads(decay[:, :, d]), heads(k_d), heads(v),
                             -kk, kk * heads(a_d), rev))
        ks.append(k_d)
    y = ys[0] + ys[1]
    mu = jnp.mean(y, axis=-1, keepdims=True)
    var = jnp.mean(jnp.square(y - mu), axis=-1, keepdims=True)
    y = ((y - mu) * lax.rsqrt(var + RWKV_GN_EPS)).reshape(B, T, RW) * gn_w.astype(f32) + gn_b.astype(f32)
    bonus = jnp.sum(heads(r) * heads(ks[0] + ks[1]) * r_k.astype(f32), axis=-1, keepdims=True) * heads(v)
    y = (y + bonus.reshape(B, T, RW)) * g
    return y.astype(z.dtype)


def partial_rope(x, cos, sin):
    half = ROT_DIM // 2
    x1 = x[..., :half]
    x2 = x[..., half:ROT_DIM]
    return jnp.concatenate([x1 * cos - x2 * sin, x2 * cos + x1 * sin, x[..., ROT_DIM:]], axis=-1)


def window_attention(q, k, v, sink):
    B, T, _, hd = q.shape
    nb = T // BLOCK
    qb = q.reshape(B, nb, BLOCK, ATT_KV_HEADS, ATT_GROUP, hd)

    def band(t):
        tp = jnp.pad(t, ((0, 0), (BLOCK, BLOCK), (0, 0), (0, 0))).reshape(B, nb + 2, BLOCK, ATT_KV_HEADS, hd)
        return jnp.concatenate([tp[:, :-2], tp[:, 1:-1], tp[:, 2:]], axis=2)

    kb = band(k)
    vb = band(v)
    s = jnp.einsum('bnqkgd,bnskd->bnkgqs', qb, kb).astype(jnp.float32) * (hd ** -0.5)
    blk = jnp.arange(nb)[:, None, None] * BLOCK
    qpos = blk + jnp.arange(BLOCK)[None, :, None]
    kpos = blk - BLOCK + jnp.arange(3 * BLOCK)[None, None, :]
    valid = (jnp.abs(kpos - qpos) <= WINDOW) & (kpos >= 0) & (kpos < T)
    s = jnp.where(valid[None, :, None, None], s, -jnp.inf)
    sk = sink.astype(jnp.float32).reshape(1, 1, ATT_KV_HEADS, ATT_GROUP, 1, 1)
    m = jnp.maximum(jnp.max(s, axis=-1, keepdims=True), sk)
    p = jnp.exp(s - m)
    denom = jnp.sum(p, axis=-1, keepdims=True) + jnp.exp(sk - m)
    o = jnp.einsum('bnkgqs,bnskd->bnqkgd', (p / denom).astype(v.dtype), vb)
    return o.reshape(B, T, ATT_HEADS * hd)


def expert_choice_moe(u, w_router, w_gate, w_up, w_down):
    B, T, D = u.shape
    cap = CAPACITY_FACTOR * T // N_EXPERTS
    aff = jax.nn.softmax(jnp.einsum('btd,de->bte', u, w_router).astype(jnp.float32), axis=-1)
    top_aff, top_idx = lax.top_k(jnp.swapaxes(aff, 1, 2), cap)
    bidx = jnp.arange(B)[:, None, None]
    xe = u[bidx, top_idx]
    h = jax.nn.silu(jnp.einsum('becd,edf->becf', xe, w_gate)) * jnp.einsum('becd,edf->becf', xe, w_up)
    ye = jnp.einsum('becf,efd->becd', h, w_down) * top_aff[..., None].astype(u.dtype)
    return jnp.zeros_like(u).at[bidx, top_idx].add(ye)


def setup_inputs(seed: int = 0) -> dict:
    key = jax.random.key(seed)
    ks = jax.random.split(key, 32)
    f32 = jnp.float32
    L, D, RW, E, F = DEPTH, D_MODEL, RWKV_WIDTH, N_EXPERTS, EXPERT_FF

    def nrm(k, shape, scale):
        return jax.random.normal(k, shape, f32) * scale

    return {
        'x': nrm(ks[0], (BATCH, SEQ, D), 1.0),
        'c': nrm(ks[1], (BATCH, D), 1.0),
        'positions': jnp.broadcast_to(jnp.arange(SEQ, dtype=jnp.int32)[None, :], (BATCH, SEQ)),
        'w_ada': nrm(ks[2], (L, D, 6 * D), 0.5 * D ** -0.5),
        'b_ada': nrm(ks[3], (L, 6 * D), 0.02),
        'norm1_g': 1.0 + nrm(ks[4], (L, D), 0.05),
        'w_in': nrm(ks[5], (L, D, IN_COLS), D ** -0.5),
        'mu_prev': jax.random.uniform(ks[6], (L, RWKV_COLS), f32, 0.0, 0.5),
        'mu_next': jax.random.uniform(ks[7], (L, RWKV_COLS), f32, 0.0, 0.5),
        'rwkv_w0': -2.0 + nrm(ks[8], (L, N_DIRECTIONS, RW), 1.0),
        'rwkv_w2': nrm(ks[9], (L, N_DIRECTIONS, W_LORA, RW), 0.5 * W_LORA ** -0.5),
        'rwkv_a0': nrm(ks[10], (L, N_DIRECTIONS, RW), 0.1),
        'rwkv_a2': nrm(ks[11], (L, N_DIRECTIONS, A_LORA, RW), 0.5 * A_LORA ** -0.5),
        'rwkv_g2': nrm(ks[12], (L, G_LORA, RW), G_LORA ** -0.5),
        'rwkv_k_k': 0.85 + nrm(ks[13], (L, RW), 0.05),
        'rwkv_k_a': 1.0 + nrm(ks[14], (L, RW), 0.05),
        'rwkv_r_k': nrm(ks[15], (L, RWKV_HEADS, RWKV_HEAD_DIM), 0.1),
        'rwkv_gn_w': 1.0 + nrm(ks[16], (L, RW), 0.05),
        'rwkv_gn_b': nrm(ks[17], (L, RW), 0.02),
        'q_norm_g': 1.0 + nrm(ks[18], (L, ATT_HEAD_DIM), 0.05),
        'k_norm_g': 1.0 + nrm(ks[19], (L, ATT_HEAD_DIM), 0.05),
        'attn_sink': nrm(ks[20], (L, ATT_HEADS), 1.0),
        'p_rwkv': nrm(ks[21], (L, RW, D), RW ** -0.5),
        'p_attn': nrm(ks[22], (L, ATT_WIDTH, D), ATT_WIDTH ** -0.5),
        'w_out': nrm(ks[23], (L, D, D), D ** -0.5),
        'norm2_g': 1.0 + nrm(ks[24], (L, D), 0.05),
        'w_router': nrm(ks[25], (L, D, E), D ** -0.5),
        'w_gate': nrm(ks[26], (L, E, D, F), D ** -0.5),
        'w_up': nrm(ks[27], (L, E, D, F), D ** -0.5),
        'w_down': nrm(ks[28], (L, E, F, D), F ** -0.5),
    }


def reference(x, c, positions, w_ada, b_ada, norm1_g, w_in, mu_prev, mu_next,
              rwkv_w0, rwkv_w2, rwkv_a0, rwkv_a2, rwkv_g2, rwkv_k_k, rwkv_k_a, rwkv_r_k,
              rwkv_gn_w, rwkv_gn_b, q_norm_g, k_norm_g, attn_sink, p_rwkv, p_attn, w_out,
              norm2_g, w_router, w_gate, w_up, w_down):
    B, T, D = x.shape
    hd = ATT_HEAD_DIM
    inv_freq = ROPE_THETA ** (-jnp.arange(0, ROT_DIM, 2, dtype=jnp.float32) / ROT_DIM)
    ang = positions.astype(jnp.float32)[..., None] * inv_freq
    cos = jnp.cos(ang)[:, :, None, :].astype(x.dtype)
    sin = jnp.sin(ang)[:, :, None, :].astype(x.dtype)
    col_splits = [RWKV_COLS, RWKV_COLS + ATT_WIDTH, RWKV_COLS + ATT_WIDTH + KV_WIDTH,
                  RWKV_COLS + ATT_WIDTH + 2 * KV_WIDTH]
    c_act = jax.nn.silu(c)

    for l in range(DEPTH):
        mod = jnp.einsum('bd,de->be', c_act, w_ada[l]) + b_ada[l]
        sh1, sc1, gt1, sh2, sc2, gt2 = jnp.split(mod[:, None, :], 6, axis=-1)

        u = rms_norm(x, norm1_g[l]) * (1.0 + sc1) + sh1
        z = u @ w_in[l]
        z_rwkv, q, k, v, gate_logits = jnp.split(z, col_splits, axis=-1)

        z_rwkv = centred_token_shift(z_rwkv, mu_prev[l], mu_next[l])
        y_a = rwkv7_branch(z_rwkv, rwkv_w0[l], rwkv_w2[l], rwkv_a0[l], rwkv_a2[l], rwkv_g2[l],
                           rwkv_k_k[l], rwkv_k_a[l], rwkv_r_k[l], rwkv_gn_w[l], rwkv_gn_b[l]) @ p_rwkv[l]

        q = partial_rope(rms_norm(q.reshape(B, T, ATT_HEADS, hd), q_norm_g[l]), cos, sin)
        k = partial_rope(rms_norm(k.reshape(B, T, ATT_KV_HEADS, hd), k_norm_g[l]), cos, sin)
        v = v.reshape(B, T, ATT_KV_HEADS, hd)
        y_b = window_attention(q, k, v, attn_sink[l]) @ p_attn[l]

        g_a, g_b = jnp.split(jax.nn.sigmoid(gate_logits), N_BRANCHES, axis=-1)
        x = x + gt1 * ((g_a * y_a + g_b * y_b) @ w_out[l])

        u2 = rms_norm(x, norm2_g[l]) * (1.0 + sc2) + sh2
        x = x + gt2 * expert_choice_moe(u2, w_router[l], w_gate[l], w_up[l], w_down[l])
    return x
```

```python
import functools
import math

import jax
import jax.numpy as jnp
from jax import lax
from jax.experimental import pallas as pl
from jax.experimental.pallas import tpu as pltpu

F32 = jnp.float32
BF16 = jnp.bfloat16

D_MODEL = 1024
RW = 512
HEADS = 8
HD = 64
LORA_W = 64
LORA_A = 64
LORA_G = 128
GN_EPS = HD * 1e-5
KV_HEADS = 2
GROUP = HEADS // KV_HEADS
KV_W = KV_HEADS * HD
BLOCK = 128
ROPE_THETA = 500000.0
ROT = HD // 4
N_EXPERTS = 16
CAP_FACTOR = 2
NORM_EPS = 1e-6
RWKV_COLS = 3 * RW + 2 * LORA_W + 2 * LORA_A + LORA_G
Q0 = RWKV_COLS
K0 = Q0 + RW
V0 = K0 + KV_W
G0 = V0 + KV_W
IN_COLS = G0 + 2 * D_MODEL
CHUNK = 64
SUB = 16
NEG = -1e30
VMEM_LIMIT = 56 * 1024 * 1024


def _dot(a, b):
    return jnp.dot(a, b, preferred_element_type=F32)


def _dot_nt(a, b):
    return lax.dot_general(a, b, (((1,), (1,)), ((), ())), preferred_element_type=F32)


def _dot_tn(a, b):
    return lax.dot_general(a, b, (((0,), (0,)), ((), ())), preferred_element_type=F32)


def _segsum(x, ones):
    xh = x.astype(BF16)
    xl = (x - xh.astype(F32)).astype(BF16)
    return _dot(xh, ones) + _dot(xl, ones)


def _params(sem):
    return pltpu.CompilerParams(dimension_semantics=sem, vmem_limit_bytes=VMEM_LIMIT)


def _mod_kernel(c_ref, w_ref, b_ref, o_ref):
    c = c_ref[...]
    ca = c * jax.nn.sigmoid(c)
    o_ref[...] = jnp.dot(ca, w_ref[...], preferred_element_type=F32,
                         precision=lax.Precision.HIGHEST) + b_ref[...]


def _mod_call(c, w_ada, b_ada):
    B, D = c.shape
    n = w_ada.shape[1] // D
    return pl.pallas_call(
        _mod_kernel,
        grid=(n,),
        in_specs=[pl.BlockSpec((B, D), lambda j: (0, 0)),
                  pl.BlockSpec((D, D), lambda j: (0, j)),
                  pl.BlockSpec((1, D), lambda j: (0, j))],
        out_specs=pl.BlockSpec((B, D), lambda j: (0, j)),
        out_shape=jax.ShapeDtypeStruct((B, n * D), F32),
        compiler_params=_params(("arbitrary",)),
        name="mod",
    )(c, w_ada, b_ada.reshape(1, -1))


def _rope(xn, cos, sin, width):
    lane = lax.broadcasted_iota(jnp.int32, xn.shape, 1) & (HD - 1)
    rot = jnp.where(lane < ROT // 2, pltpu.roll(xn, width - ROT // 2, 1), pltpu.roll(xn, ROT // 2, 1))
    return xn * cos + rot * sin


def _inproj_kernel(x_ref, mod_ref, g_ref, w_ref, cos_ref, sin_ref, qg_ref, kg_ref, ones_ref,
                   zr_ref, q_ref, k_ref, v_ref, gl_ref):
    x = x_ref[0]
    ms = jnp.mean(x * x, axis=-1, keepdims=True)
    y = x * lax.rsqrt(ms + NORM_EPS) * g_ref[...]
    u = y * (1.0 + mod_ref[0, 1:2, :]) + mod_ref[0, 0:1, :]
    ub = u.astype(BF16)
    zr_ref[0] = _dot(ub, w_ref[:, 0:RWKV_COLS])
    q = _dot(ub, w_ref[:, Q0:K0])
    k = _dot(ub, w_ref[:, K0:V0])
    v_ref[0] = _dot(ub, w_ref[:, V0:G0]).astype(BF16)
    gl_ref[0] = _dot(ub, w_ref[:, G0:IN_COLS]).astype(BF16)
    ones = ones_ref[...]
    cos = cos_ref[0]
    sin = sin_ref[0]
    qn = q * lax.rsqrt(_segsum(q * q, ones) * (1.0 / HD) + NORM_EPS) * qg_ref[...]
    cos4 = jnp.concatenate([cos] * (RW // KV_W), axis=-1)
    sin4 = jnp.concatenate([sin] * (RW // KV_W), axis=-1)
    q_ref[0] = _rope(qn, cos4, sin4, RW).astype(BF16)
    kn = k * lax.rsqrt(_segsum(k * k, ones[:KV_W, :KV_W]) * (1.0 / HD) + NORM_EPS) * kg_ref[...]
    k_ref[0] = _rope(kn, cos, sin, KV_W).astype(BF16)


def _inproj_call(x, mod3, norm1_g, w_in_b, cos_t, sin_t, qg, kg, ones, tm):
    B, T, D = x.shape
    row = lambda b, i: (b, i, 0)
    const2 = lambda b, i: (0, 0)
    return pl.pallas_call(
        _inproj_kernel,
        grid=(B, T // tm),
        in_specs=[pl.BlockSpec((1, tm, D), row),
                  pl.BlockSpec((1, 6, D), lambda b, i: (b, 0, 0)),
                  pl.BlockSpec((1, D), const2),
                  pl.BlockSpec((D, IN_COLS), const2),
                  pl.BlockSpec((1, tm, KV_W), row),
                  pl.BlockSpec((1, tm, KV_W), row),
                  pl.BlockSpec((1, RW), const2),
                  pl.BlockSpec((1, KV_W), const2),
                  pl.BlockSpec((RW, RW), const2)],
        out_specs=[pl.BlockSpec((1, tm, RWKV_COLS), row),
                   pl.BlockSpec((1, tm, RW), row),
                   pl.BlockSpec((1, tm, KV_W), row),
                   pl.BlockSpec((1, tm, KV_W), row),
                   pl.BlockSpec((1, tm, 2 * D), row)],
        out_shape=[jax.ShapeDtypeStruct((B, T, RWKV_COLS), F32),
                   jax.ShapeDtypeStruct((B, T, RW), BF16),
                   jax.ShapeDtypeStruct((B, T, KV_W), BF16),
                   jax.ShapeDtypeStruct((B, T, KV_W), BF16),
                   jax.ShapeDtypeStruct((B, T, 2 * D), BF16)],
        compiler_params=_params(("parallel", "parallel")),
        name="inproj",
    )(x, mod3, norm1_g, w_in_b, cos_t, sin_t, qg, kg, ones)


def _prep_kernel(z_ref, hp_ref, hn_ref, mup_ref, mun_ref, w0_ref, w2_ref, a0_ref, a2_ref, g2_ref,
                 kk_ref, ka_ref, rk_ref, ones_ref,
                 r_o, v_o, kkn_o, g_o, bonus_o, lw0_o, lw1_o, k0_o, k1_o, b0_o, b1_o):
    i = pl.program_id(1)
    last = pl.num_programs(1) - 1
    z = z_ref[0]
    tm = z.shape[0]
    row = lax.broadcasted_iota(jnp.int32, z.shape, 0)
    prev_row = jnp.where(i == 0, 0.0, hp_ref[0, 7:8, :])
    next_row = jnp.where(i == last, 0.0, hn_ref[0, 0:1, :])
    zp = jnp.where(row == 0, prev_row, pltpu.roll(z, 1, 0))
    zn = jnp.where(row == tm - 1, next_row, pltpu.roll(z, tm - 1, 0))
    zs = z + mup_ref[...] * (zp - z) + mun_ref[...] * (zn - z)
    r = zs[:, 0:RW]
    k = zs[:, RW:2 * RW]
    v = zs[:, 2 * RW:3 * RW]
    c0 = 3 * RW
    wd = zs[:, c0:c0 + 2 * LORA_W]
    ad = zs[:, c0 + 2 * LORA_W:c0 + 2 * LORA_W + 2 * LORA_A]
    gd = zs[:, c0 + 2 * LORA_W + 2 * LORA_A:RWKV_COLS]
    wl = _dot(jnp.tanh(wd).astype(BF16), w2_ref[...]) + w0_ref[...]
    lw = -math.exp(-0.5) * jax.nn.sigmoid(wl)
    al = jax.nn.sigmoid(_dot(ad.astype(BF16), a2_ref[...]) + a0_ref[...])
    g = _dot(jax.nn.sigmoid(gd).astype(BF16), g2_ref[...])
    ones = ones_ref[...]
    kk = k * kk_ref[...]
    kkn = kk / jnp.maximum(jnp.sqrt(_segsum(kk * kk, ones)), 1e-12)
    ka = ka_ref[...]
    a_0 = al[:, 0:RW]
    a_1 = al[:, RW:2 * RW]
    k_0 = k * (1.0 + (a_0 - 1.0) * ka)
    k_1 = k * (1.0 + (a_1 - 1.0) * ka)
    bonus = _segsum(r * (k_0 + k_1) * rk_ref[...], ones) * v
    r_o[0] = r
    v_o[0] = v
    kkn_o[0] = kkn
    g_o[0] = g
    bonus_o[0] = bonus
    lw0_o[0] = lw[:, 0:RW]
    lw1_o[0] = lw[:, RW:2 * RW]
    k0_o[0] = k_0
    k1_o[0] = k_1
    b0_o[0] = kkn * a_0
    b1_o[0] = kkn * a_1


def _prep_call(zr, mup, mun, w0c, w2c, a0c, a2c, g2b, k_k, k_a, r_k, ones, tm):
    B, T, _ = zr.shape
    nh = tm // 8
    nt8 = T // 8
    row = lambda b, i: (b, i, 0)
    const2 = lambda b, i: (0, 0)
    out = pl.BlockSpec((1, tm, RW), row)
    return pl.pallas_call(
        _prep_kernel,
        grid=(B, T // tm),
        in_specs=[pl.BlockSpec((1, tm, RWKV_COLS), row),
                  pl.BlockSpec((1, 8, RWKV_COLS), lambda b, i: (b, jnp.maximum(i * nh - 1, 0), 0)),
                  pl.BlockSpec((1, 8, RWKV_COLS), lambda b, i: (b, jnp.minimum((i + 1) * nh, nt8 - 1), 0)),
                  pl.BlockSpec((1, RWKV_COLS), const2),
                  pl.BlockSpec((1, RWKV_COLS), const2),
                  pl.BlockSpec((1, 2 * RW), const2),
                  pl.BlockSpec((2 * LORA_W, 2 * RW), const2),
                  pl.BlockSpec((1, 2 * RW), const2),
                  pl.BlockSpec((2 * LORA_A, 2 * RW), const2),
                  pl.BlockSpec((LORA_G, RW), const2),
                  pl.BlockSpec((1, RW), const2),
                  pl.BlockSpec((1, RW), const2),
                  pl.BlockSpec((1, RW), const2),
                  pl.BlockSpec((RW, RW), const2)],
        out_specs=[out] * 11,
        out_shape=[jax.ShapeDtypeStruct((B, T, RW), F32)] * 11,
        compiler_params=_params(("parallel", "parallel")),
        name="prep",
    )(zr, zr, zr, mup, mun, w0c, w2c, a0c, a2c, g2b, k_k, k_a, r_k, ones)


def _chunk_solve(lab, rhs, blk):
    mm = lambda a, b: _dot(a.astype(BF16), b.astype(BF16))
    c = lab.shape[0]
    ld = jnp.where(blk, lab, 0.0)
    lo = lab - ld
    a2 = mm(ld, ld)
    a4 = mm(a2, a2)
    a8 = mm(a4, a4)
    x = jnp.concatenate([lo, rhs], axis=-1)
    x = x + mm(a8, x)
    x = x + mm(a4, x)
    x = x + mm(a2, x)
    x = x + mm(ld, x)
    n = x[:, :c]
    z = x[:, c:]
    z = z + mm(mm(n, n), z)
    z = z + mm(n, z)
    return z


def _chunk_dir(rev, r, v, kk, lw, kd, bd, st_ref, d):
    c = r.shape[0]
    ti = lax.broadcasted_iota(jnp.int32, (c, c), 0)
    si = lax.broadcasted_iota(jnp.int32, (c, c), 1)
    if rev:
        strict = si > ti
        incl = si >= ti
    else:
        strict = si < ti
        incl = si <= ti
    blk = (ti >> 4) == (si >> 4)
    mi = incl.astype(BF16)
    l1 = lw.astype(BF16)
    r1 = lw - l1.astype(F32)
    l2 = r1.astype(BF16)
    l3 = (r1 - l2.astype(F32)).astype(BF16)
    cum = _dot(mi, l1) + _dot(mi, l2) + _dot(mi, l3)
    wt = jnp.exp(cum)
    winv = jnp.exp(-cum)
    ah = -(kk * jnp.exp(cum - lw))
    bt = bd * winv
    kt = kd * winv
    rt = r * wt
    wc = wt[0:1, :] if rev else wt[c - 1:c, :]
    bh = bt * wc
    kh = kt * wc
    ys = []
    for h in range(HEADS):
        sl = slice(h * HD, (h + 1) * HD)
        x1 = jnp.concatenate([ah[:, sl], rt[:, sl]], axis=0).astype(BF16)
        lb = _dot_nt(x1, bt[:, sl].astype(BF16))
        lk = _dot_nt(x1, kt[:, sl].astype(BF16))
        s = st_ref[d, h]
        a_s = _dot_nt(x1, s.astype(BF16))
        lab = jnp.where(strict, lb[:c], 0.0)
        mrb = jnp.where(incl, lb[c:], 0.0)
        lak = jnp.where(strict, lk[:c], 0.0)
        mrk = jnp.where(incl, lk[c:], 0.0)
        vh = v[:, sl].astype(BF16)
        lkv = _dot(jnp.concatenate([lak, mrk], axis=0).astype(BF16), vh)
        u = _chunk_solve(lab, a_s[:c] + lkv[:c], blk)
        ub = u.astype(BF16)
        ys.append(a_s[c:] + lkv[c:] + _dot(mrb.astype(BF16), ub))
        st_ref[d, h] = (s * wc[:, sl] + _dot_tn(ub, bh[:, sl].astype(BF16))
                        + _dot_tn(vh, kh[:, sl].astype(BF16)))
    return jnp.concatenate(ys, axis=-1)


def _scan_kernel(rf, vf, kkf, lwf, kf, bf, rb, vb, kkb, lwb, kb, bb, yf_ref, yb_ref, st_ref):
    @pl.when(pl.program_id(1) == 0)
    def _():
        st_ref[...] = jnp.zeros_like(st_ref)

    yf_ref[0] = _chunk_dir(False, rf[0], vf[0], kkf[0], lwf[0], kf[0], bf[0], st_ref, 0)
    yb_ref[0] = _chunk_dir(True, rb[0], vb[0], kkb[0], lwb[0], kb[0], bb[0], st_ref, 1)


def _scan_call(r, v, kkn, lw0, lw1, k0, k1, b0, b1):
    B, T, _ = r.shape
    nc = T // CHUNK
    fwd = pl.BlockSpec((1, CHUNK, RW), lambda b, c: (b, c, 0))
    bwd = pl.BlockSpec((1, CHUNK, RW), lambda b, c: (b, nc - 1 - c, 0))
    return pl.pallas_call(
        _scan_kernel,
        grid=(B, nc),
        in_specs=[fwd] * 6 + [bwd] * 6,
        out_specs=[fwd, bwd],
        out_shape=[jax.ShapeDtypeStruct((B, T, RW), F32)] * 2,
        scratch_shapes=[pltpu.VMEM((2, HEADS, HD, HD), F32)],
        compiler_params=_params(("parallel", "arbitrary")),
        name="scan",
    )(r, v, kkn, lw0, k0, b0, r, v, kkn, lw1, k1, b1)


def _attn_kernel(q_ref, kp_ref, kc_ref, kn_ref, vp_ref, vc_ref, vn_ref, sink_ref, o_ref):
    n = pl.program_id(1)
    nb = pl.num_programs(1)
    q = q_ref[0]
    kband = jnp.concatenate([kp_ref[0], kc_ref[0], kn_ref[0]], axis=0)
    vband = jnp.concatenate([vp_ref[0], vc_ref[0], vn_ref[0]], axis=0)
    rows = GROUP * BLOCK
    ri = lax.broadcasted_iota(jnp.int32, (rows, 3 * BLOCK), 0)
    ci = lax.broadcasted_iota(jnp.int32, (rows, 3 * BLOCK), 1)
    qi = ri & (BLOCK - 1)
    valid = (ci >= qi) & (ci <= qi + 2 * BLOCK)
    valid = valid & ((n > 0) | (ci >= BLOCK)) & ((n < nb - 1) | (ci < 2 * BLOCK))
    rcol = lax.broadcasted_iota(jnp.int32, (rows, 1), 0) >> 7
    outs = [None] * HEADS
    for g in range(KV_HEADS):
        kg = kband[:, g * HD:(g + 1) * HD]
        vg = vband[:, g * HD:(g + 1) * HD]
        qs = jnp.concatenate([q[:, (g * GROUP + i) * HD:(g * GROUP + i + 1) * HD] for i in range(GROUP)],
                             axis=0)
        s = _dot_nt(qs, kg) * (HD ** -0.5)
        s = jnp.where(valid, s, NEG)
        sk = jnp.zeros((rows, 1), F32)
        for i in range(GROUP):
            sk = jnp.where(rcol == i, sink_ref[0:1, g * GROUP + i:g * GROUP + i + 1], sk)
        m = jnp.maximum(jnp.max(s, axis=-1, keepdims=True), sk)
        p = jnp.exp(s - m)
        denom = jnp.sum(p, axis=-1, keepdims=True) + jnp.exp(sk - m)
        o = _dot((p / denom).astype(BF16), vg)
        for i in range(GROUP):
            outs[g * GROUP + i] = o[i * BLOCK:(i + 1) * BLOCK]
    o_ref[0] = jnp.concatenate(outs, axis=-1).astype(BF16)


def _attn_call(q, k, v, sink):
    B, T, _ = q.shape
    nb = T // BLOCK
    prev = lambda b, n: (b, jnp.maximum(n - 1, 0), 0)
    cur = lambda b, n: (b, n, 0)
    nxt = lambda b, n: (b, jnp.minimum(n + 1, nb - 1), 0)
    kv = lambda im: pl.BlockSpec((1, BLOCK, KV_W), im)
    return pl.pallas_call(
        _attn_kernel,
        grid=(B, nb),
        in_specs=[pl.BlockSpec((1, BLOCK, RW), cur), kv(prev), kv(cur), kv(nxt), kv(prev), kv(cur), kv(nxt),
                  pl.BlockSpec((1, HEADS), lambda b, n: (0, 0))],
        out_specs=pl.BlockSpec((1, BLOCK, RW), cur),
        out_shape=jax.ShapeDtypeStruct((B, T, RW), BF16),
        compiler_params=_params(("parallel", "parallel")),
        name="attn",
    )(q, k, k, k, v, v, v, sink)


def _merge_kernel(x_ref, mod_ref, yf_ref, yb_ref, bonus_ref, g_ref, yatt_ref, gl_ref,
                  gnw_ref, gnb_ref, ones_ref, pr_ref, pa_ref, wo_ref, n2g_ref, wr_ref,
                  x1_ref, u2_ref, aff_ref):
    ones = ones_ref[...]
    y = yf_ref[0] + yb_ref[0]
    mu = _segsum(y, ones) * (1.0 / HD)
    yc = y - mu
    var = _segsum(yc * yc, ones) * (1.0 / HD)
    yn = yc * lax.rsqrt(var + GN_EPS) * gnw_ref[...] + gnb_ref[...]
    ya = ((yn + bonus_ref[0]) * g_ref[0]).astype(BF16)
    pa = _dot(ya, pr_ref[...])
    pb = _dot(yatt_ref[0], pa_ref[...])
    gates = jax.nn.sigmoid(gl_ref[0].astype(F32))
    m = gates[:, :D_MODEL] * pa + gates[:, D_MODEL:] * pb
    x1 = x_ref[0] + mod_ref[0, 2:3, :] * _dot(m.astype(BF16), wo_ref[...])
    x1_ref[0] = x1
    ms = jnp.mean(x1 * x1, axis=-1, keepdims=True)
    u2 = x1 * lax.rsqrt(ms + NORM_EPS) * n2g_ref[...] * (1.0 + mod_ref[0, 4:5, :]) + mod_ref[0, 3:4, :]
    u2_ref[0] = u2.astype(BF16)
    logits = lax.dot_general(wr_ref[...], u2, (((1,), (1,)), ((), ())), preferred_element_type=F32,
                             precision=lax.Precision.HIGHEST)
    e = jnp.exp(logits - jnp.max(logits, axis=0, keepdims=True))
    aff_ref[0] = e / jnp.sum(e, axis=0, keepdims=True)


def _merge_call(x, mod3, yf, yb, bonus, g, yatt, gl, gnw, gnb, ones, p_r, p_a, w_o, n2g, w_rt, tm):
    B, T, D = x.shape
    row = lambda b, i: (b, i, 0)
    const2 = lambda b, i: (0, 0)
    rw = pl.BlockSpec((1, tm, RW), row)
    return pl.pallas_call(
        _merge_kernel,
        grid=(B, T // tm),
        in_specs=[pl.BlockSpec((1, tm, D), row),
                  pl.BlockSpec((1, 6, D), lambda b, i: (b, 0, 0)),
                  rw, rw, rw, rw, rw,
                  pl.BlockSpec((1, tm, 2 * D), row),
                  pl.BlockSpec((1, RW), const2),
                  pl.BlockSpec((1, RW), const2),
                  pl.BlockSpec((RW, RW), const2),
                  pl.BlockSpec((RW, D), const2),
                  pl.BlockSpec((RW, D), const2),
                  pl.BlockSpec((D, D), const2),
                  pl.BlockSpec((1, D), const2),
                  pl.BlockSpec((N_EXPERTS, D), const2)],
        out_specs=[pl.BlockSpec((1, tm, D), row),
                   pl.BlockSpec((1, tm, D), row),
                   pl.BlockSpec((1, N_EXPERTS, tm), lambda b, i: (b, 0, i))],
        out_shape=[jax.ShapeDtypeStruct((B, T, D), F32),
                   jax.ShapeDtypeStruct((B, T, D), BF16),
                   jax.ShapeDtypeStruct((B, N_EXPERTS, T), F32)],
        compiler_params=_params(("parallel", "parallel")),
        name="merge",
    )(x, mod3, yf, yb, bonus, g, yatt, gl, gnw, gnb, ones, p_r, p_a, w_o, n2g, w_rt)


def _excl_prefix(mask, tri):
    e, t = mask.shape
    mb = mask.astype(BF16)
    carry = jnp.zeros((e, 1), F32)
    outs = []
    for j in range(t // 128):
        tile = mb[:, j * 128:(j + 1) * 128]
        outs.append(_dot(tile, tri) + carry)
        carry = carry + jnp.sum(tile.astype(F32), axis=1, keepdims=True)
    return jnp.concatenate(outs, axis=-1)


def _route_kernel(aff_ref, tri_ref, slot_ref, *, cap):
    a = aff_ref[0]
    bits = pltpu.bitcast(a, jnp.int32)
    e = a.shape[0]
    capf = jnp.float32(cap)

    def count_ge(th):
        return jnp.sum((bits >= th).astype(F32), axis=1, keepdims=True)

    def body(_, lohi):
        lo, hi = lohi
        mid = lo + ((hi - lo) >> 1)
        ge = count_ge(mid) >= capf
        return jnp.where(ge, mid, lo), jnp.where(ge, hi, mid)

    lo0 = jnp.zeros((e, 1), jnp.int32)
    hi0 = jnp.full((e, 1), 0x7F800000, jnp.int32)
    thr, _ = lax.fori_loop(0, 31, body, (lo0, hi0))
    gt = bits > thr
    eq = bits == thr
    need = capf - jnp.sum(gt.astype(F32), axis=1, keepdims=True)
    tri = tri_ref[...]
    sel = gt | (eq & (_excl_prefix(eq, tri) < need))
    pos = _excl_prefix(sel, tri)
    slot_ref[0] = jnp.where(sel, pos.astype(jnp.int32), -1)


def _route_call(aff, tri, cap):
    B, E, T = aff.shape
    return pl.pallas_call(
        functools.partial(_route_kernel, cap=cap),
        grid=(B,),
        in_specs=[pl.BlockSpec((1, E, T), lambda b: (b, 0, 0)),
                  pl.BlockSpec((128, 128), lambda b: (0, 0))],
        out_specs=pl.BlockSpec((1, E, T), lambda b: (b, 0, 0)),
        out_shape=jax.ShapeDtypeStruct((B, E, T), jnp.int32),
        compiler_params=_params(("parallel",)),
        name="route",
    )(aff, tri)


def _moe_kernel(u2_ref, slot_ref, aff_ref, wg_ref, wu_ref, wd_ref, o_ref, acc_ref, *, cap, tt):
    e = pl.program_id(1)

    @pl.when(e == 0)
    def _():
        acc_ref[...] = jnp.zeros_like(acc_ref)

    t = u2_ref.shape[1]
    srow = slot_ref[0, pl.ds(e, 1), :]
    arow = aff_ref[0, pl.ds(e, 1), :]
    hit = lax.broadcasted_iota(jnp.int32, (cap, t), 0) == srow
    xe = _dot(hit.astype(BF16), u2_ref[0]).astype(BF16)
    hg = _dot(xe, wg_ref[0])
    hu = _dot(xe, wu_ref[0])
    h = (hg * jax.nn.sigmoid(hg) * hu).astype(BF16)
    ye = _dot(h, wd_ref[0]).astype(BF16)
    wsc = jnp.where(hit, arow, 0.0).astype(BF16)
    for j in range(t // tt):
        acc_ref[j * tt:(j + 1) * tt, :] += _dot_tn(wsc[:, j * tt:(j + 1) * tt], ye)

    @pl.when(e == pl.num_programs(1) - 1)
    def _():
        o_ref[0] = acc_ref[...].astype(BF16)


def _moe_call(u2, slot, aff, wg, wu, wd, cap):
    B, T, D = u2.shape
    E = wg.shape[0]
    F = wg.shape[2]
    tt = min(T, 512)
    return pl.pallas_call(
        functools.partial(_moe_kernel, cap=cap, tt=tt),
        grid=(B, E),
        in_specs=[pl.BlockSpec((1, T, D), lambda b, e: (b, 0, 0)),
                  pl.BlockSpec((1, E, T), lambda b, e: (b, 0, 0)),
                  pl.BlockSpec((1, E, T), lambda b, e: (b, 0, 0)),
                  pl.BlockSpec((1, D, F), lambda b, e: (e, 0, 0)),
                  pl.BlockSpec((1, D, F), lambda b, e: (e, 0, 0)),
                  pl.BlockSpec((1, F, D), lambda b, e: (e, 0, 0))],
        out_specs=pl.BlockSpec((1, T, D), lambda b, e: (b, 0, 0)),
        out_shape=jax.ShapeDtypeStruct((B, T, D), BF16),
        scratch_shapes=[pltpu.VMEM((T, D), F32)],
        compiler_params=_params(("parallel", "arbitrary")),
        name="moe",
    )(u2, slot, aff, wg, wu, wd)


def _final_kernel(x1_ref, mod_ref, moe_ref, o_ref):
    o_ref[0] = x1_ref[0] + mod_ref[0, 5:6, :] * moe_ref[0].astype(F32)


def _final_call(x1, mod3, moe, tm):
    B, T, D = x1.shape
    row = lambda b, i: (b, i, 0)
    return pl.pallas_call(
        _final_kernel,
        grid=(B, T // tm),
        in_specs=[pl.BlockSpec((1, tm, D), row),
                  pl.BlockSpec((1, 6, D), lambda b, i: (b, 0, 0)),
                  pl.BlockSpec((1, tm, D), row)],
        out_specs=pl.BlockSpec((1, tm, D), row),
        out_shape=jax.ShapeDtypeStruct((B, T, D), F32),
        compiler_params=_params(("parallel", "parallel")),
        name="final",
    )(x1, mod3, moe)


def _blockdiag2(w):
    z = jnp.zeros_like(w[0])
    return jnp.concatenate([jnp.concatenate([w[0], z], axis=1), jnp.concatenate([z, w[1]], axis=1)], axis=0)


def _layer(x, mod3, cos_t, sin_t, ones, tri, norm1_g, w_in, mu_prev, mu_next, rwkv_w0, rwkv_w2, rwkv_a0,
           rwkv_a2, rwkv_g2, rwkv_k_k, rwkv_k_a, rwkv_r_k, rwkv_gn_w, rwkv_gn_b, q_norm_g, k_norm_g,
           attn_sink, p_rwkv, p_attn, w_out, norm2_g, w_router, w_gate, w_up, w_down):
    B, T, D = x.shape
    tm = min(T, 256)
    cap = CAP_FACTOR * T // N_EXPERTS
    row = lambda a: a.reshape(1, -1)
    zr, q, k, v, gl = _inproj_call(x, mod3, row(norm1_g), w_in.astype(BF16), cos_t, sin_t,
                                   row(jnp.tile(q_norm_g, HEADS)), row(jnp.tile(k_norm_g, KV_HEADS)),
                                   ones, tm)
    r, vv, kkn, g, bonus, lw0, lw1, k0, k1, b0, b1 = _prep_call(
        zr, row(mu_prev), row(mu_next), row(rwkv_w0), _blockdiag2(rwkv_w2).astype(BF16),
        row(rwkv_a0), _blockdiag2(rwkv_a2).astype(BF16), rwkv_g2.astype(BF16),
        row(rwkv_k_k), row(rwkv_k_a), row(rwkv_r_k), ones, tm)
    yf, yb = _scan_call(r, vv, kkn, lw0, lw1, k0, k1, b0, b1)
    yatt = _attn_call(q, k, v, row(attn_sink))
    x1, u2, aff = _merge_call(x, mod3, yf, yb, bonus, g, yatt, gl, row(rwkv_gn_w), row(rwkv_gn_b), ones,
                              p_rwkv.astype(BF16), p_attn.astype(BF16), w_out.astype(BF16),
                              row(norm2_g), w_router.T, tm)
    slot = _route_call(aff, tri, cap)
    moe = _moe_call(u2, slot, aff, w_gate.astype(BF16), w_up.astype(BF16), w_down.astype(BF16), cap)
    return _final_call(x1, mod3, moe, tm)


def kernel(x, c, positions, w_ada, b_ada, norm1_g, w_in, mu_prev, mu_next, rwkv_w0, rwkv_w2, rwkv_a0, rwkv_a2, rwkv_g2, rwkv_k_k, rwkv_k_a, rwkv_r_k, rwkv_gn_w, rwkv_gn_b, q_norm_g, k_norm_g, attn_sink, p_rwkv, p_attn, w_out, norm2_g, w_router, w_gate, w_up, w_down):
    B, T, D = x.shape
    depth = w_ada.shape[0]
    half = ROT // 2
    inv_freq = ROPE_THETA ** (-jnp.arange(0, ROT, 2, dtype=F32) / ROT)
    ang = positions.astype(F32)[..., None] * inv_freq
    cos8, sin8 = jnp.cos(ang), jnp.sin(ang)
    pad1 = jnp.ones((B, T, HD - ROT), F32)
    pad0 = jnp.zeros((B, T, HD - ROT), F32)
    cos_t = jnp.tile(jnp.concatenate([cos8, cos8, pad1], axis=-1), (1, 1, KV_HEADS))
    sin_t = jnp.tile(jnp.concatenate([-sin8, sin8, pad0], axis=-1), (1, 1, KV_HEADS))
    seg = jnp.arange(RW) // HD
    ones = (seg[:, None] == seg[None, :]).astype(BF16)
    idx = jnp.arange(128)
    tri = (idx[:, None] < idx[None, :]).astype(BF16)
    for l in range(depth):
        mod3 = _mod_call(c, w_ada[l], b_ada[l]).reshape(B, 6, D)
        x = _layer(x, mod3, cos_t, sin_t, ones, tri, norm1_g[l], w_in[l], mu_prev[l], mu_next[l],
                   rwkv_w0[l], rwkv_w2[l], rwkv_a0[l], rwkv_a2[l], rwkv_g2[l], rwkv_k_k[l], rwkv_k_a[l],
                   rwkv_r_k[l], rwkv_gn_w[l], rwkv_gn_b[l], q_norm_g[l], k_norm_g[l], attn_sink[l],
                   p_rwkv[l], p_attn[l], w_out[l], norm2_g[l], w_router[l], w_gate[l], w_up[l], w_down[l])
    return x
```

```python
import functools
import math

import jax
import jax.numpy as jnp
from jax import lax
from jax.experimental import pallas as pl
from jax.experimental.pallas import tpu as pltpu

F32 = jnp.float32
BF16 = jnp.bfloat16

D_MODEL = 1024
RW = 512
HEADS = 8
HD = 64
LORA_W = 64
LORA_A = 64
LORA_G = 128
GN_EPS = HD * 1e-5
KV_HEADS = 2
GROUP = HEADS // KV_HEADS
KV_W = KV_HEADS * HD
BLOCK = 128
ROPE_THETA = 500000.0
ROT = HD // 4
N_EXPERTS = 16
CAP_FACTOR = 2
NORM_EPS = 1e-6
RWKV_COLS = 3 * RW + 2 * LORA_W + 2 * LORA_A + LORA_G
Q0 = RWKV_COLS
K0 = Q0 + RW
V0 = K0 + KV_W
G0 = V0 + KV_W
IN_COLS = G0 + 2 * D_MODEL
CHUNK = 64
SUB = 16
SCAN_CHUNKS = 2
BISECT_ITERS = 160
NEG = -1e30
VMEM_LIMIT = 56 * 1024 * 1024


def _dot(a, b):
    return jnp.dot(a, b, preferred_element_type=F32)


def _dot_nt(a, b):
    return lax.dot_general(a, b, (((1,), (1,)), ((), ())), preferred_element_type=F32)


def _dot_tn(a, b):
    return lax.dot_general(a, b, (((0,), (0,)), ((), ())), preferred_element_type=F32)


def _segsum(x, ones):
    xh = x.astype(BF16)
    xl = (x - xh.astype(F32)).astype(BF16)
    return _dot(xh, ones) + _dot(xl, ones)


def _params(sem):
    return pltpu.CompilerParams(dimension_semantics=sem, vmem_limit_bytes=VMEM_LIMIT)


def _mod_kernel(c_ref, w_ref, b_ref, o_ref):
    c = c_ref[...]
    ca = c * jax.nn.sigmoid(c)
    o_ref[...] = jnp.dot(ca, w_ref[...], preferred_element_type=F32,
                         precision=lax.Precision.HIGHEST) + b_ref[...]


def _mod_call(c, w_ada, b_ada):
    B, D = c.shape
    n = w_ada.shape[1] // D
    return pl.pallas_call(
        _mod_kernel,
        grid=(n,),
        in_specs=[pl.BlockSpec((B, D), lambda j: (0, 0)),
                  pl.BlockSpec((D, D), lambda j: (0, j)),
                  pl.BlockSpec((1, D), lambda j: (0, j))],
        out_specs=pl.BlockSpec((B, D), lambda j: (0, j)),
        out_shape=jax.ShapeDtypeStruct((B, n * D), F32),
        compiler_params=_params(("arbitrary",)),
        name="mod",
    )(c, w_ada, b_ada.reshape(1, -1))


def _rope(xn, cos, sin, width):
    lane = lax.broadcasted_iota(jnp.int32, xn.shape, 1) & (HD - 1)
    rot = jnp.where(lane < ROT // 2, pltpu.roll(xn, width - ROT // 2, 1), pltpu.roll(xn, ROT // 2, 1))
    return xn * cos + rot * sin


def _inproj_kernel(x_ref, mod_ref, g_ref, w_ref, cos_ref, sin_ref, qg_ref, kg_ref, ones_ref,
                   zr_ref, q_ref, k_ref, v_ref, gl_ref):
    x = x_ref[0]
    ms = jnp.mean(x * x, axis=-1, keepdims=True)
    y = x * lax.rsqrt(ms + NORM_EPS) * g_ref[...]
    u = y * (1.0 + mod_ref[0, 1:2, :]) + mod_ref[0, 0:1, :]
    ub = u.astype(BF16)
    zr_ref[0] = _dot(ub, w_ref[:, 0:RWKV_COLS])
    q = _dot(ub, w_ref[:, Q0:K0])
    k = _dot(ub, w_ref[:, K0:V0])
    v_ref[0] = _dot(ub, w_ref[:, V0:G0]).astype(BF16)
    gl_ref[0] = _dot(ub, w_ref[:, G0:IN_COLS]).astype(BF16)
    ones = ones_ref[...]
    cos = cos_ref[0]
    sin = sin_ref[0]
    qn = q * lax.rsqrt(_segsum(q * q, ones) * (1.0 / HD) + NORM_EPS) * qg_ref[...]
    cos4 = jnp.concatenate([cos] * (RW // KV_W), axis=-1)
    sin4 = jnp.concatenate([sin] * (RW // KV_W), axis=-1)
    q_ref[0] = _rope(qn, cos4, sin4, RW).astype(BF16)
    kn = k * lax.rsqrt(_segsum(k * k, ones[:KV_W, :KV_W]) * (1.0 / HD) + NORM_EPS) * kg_ref[...]
    k_ref[0] = _rope(kn, cos, sin, KV_W).astype(BF16)


def _inproj_call(x, mod3, norm1_g, w_in_b, cos_t, sin_t, qg, kg, ones, tm):
    B, T, D = x.shape
    row = lambda b, i: (b, i, 0)
    const2 = lambda b, i: (0, 0)
    return pl.pallas_call(
        _inproj_kernel,
        grid=(B, T // tm),
        in_specs=[pl.BlockSpec((1, tm, D), row),
                  pl.BlockSpec((1, 6, D), lambda b, i: (b, 0, 0)),
                  pl.BlockSpec((1, D), const2),
                  pl.BlockSpec((D, IN_COLS), const2),
                  pl.BlockSpec((1, tm, KV_W), row),
                  pl.BlockSpec((1, tm, KV_W), row),
                  pl.BlockSpec((1, RW), const2),
                  pl.BlockSpec((1, KV_W), const2),
                  pl.BlockSpec((RW, RW), const2)],
        out_specs=[pl.BlockSpec((1, tm, RWKV_COLS), row),
                   pl.BlockSpec((1, tm, RW), row),
                   pl.BlockSpec((1, tm, KV_W), row),
                   pl.BlockSpec((1, tm, KV_W), row),
                   pl.BlockSpec((1, tm, 2 * D), row)],
        out_shape=[jax.ShapeDtypeStruct((B, T, RWKV_COLS), F32),
                   jax.ShapeDtypeStruct((B, T, RW), BF16),
                   jax.ShapeDtypeStruct((B, T, KV_W), BF16),
                   jax.ShapeDtypeStruct((B, T, KV_W), BF16),
                   jax.ShapeDtypeStruct((B, T, 2 * D), BF16)],
        compiler_params=_params(("parallel", "parallel")),
        name="inproj",
    )(x, mod3, norm1_g, w_in_b, cos_t, sin_t, qg, kg, ones)


def _prep_kernel(z_ref, hp_ref, hn_ref, mup_ref, mun_ref, w0_ref, w2_ref, a0_ref, a2_ref, g2_ref,
                 kk_ref, ka_ref, rk_ref, ones_ref,
                 r_o, v_o, kkn_o, g_o, bonus_o, lw0_o, lw1_o, k0_o, k1_o, b0_o, b1_o):
    i = pl.program_id(1)
    last = pl.num_programs(1) - 1
    z = z_ref[0]
    tm = z.shape[0]
    row = lax.broadcasted_iota(jnp.int32, z.shape, 0)
    prev_row = jnp.where(i == 0, 0.0, hp_ref[0, 7:8, :])
    next_row = jnp.where(i == last, 0.0, hn_ref[0, 0:1, :])
    zp = jnp.where(row == 0, prev_row, pltpu.roll(z, 1, 0))
    zn = jnp.where(row == tm - 1, next_row, pltpu.roll(z, tm - 1, 0))
    zs = z + mup_ref[...] * (zp - z) + mun_ref[...] * (zn - z)
    r = zs[:, 0:RW]
    k = zs[:, RW:2 * RW]
    v = zs[:, 2 * RW:3 * RW]
    c0 = 3 * RW
    wd = zs[:, c0:c0 + 2 * LORA_W]
    ad = zs[:, c0 + 2 * LORA_W:c0 + 2 * LORA_W + 2 * LORA_A]
    gd = zs[:, c0 + 2 * LORA_W + 2 * LORA_A:RWKV_COLS]
    wl = _dot(jnp.tanh(wd).astype(BF16), w2_ref[...]) + w0_ref[...]
    lw = -math.exp(-0.5) * jax.nn.sigmoid(wl)
    al = jax.nn.sigmoid(_dot(ad.astype(BF16), a2_ref[...]) + a0_ref[...])
    g = _dot(jax.nn.sigmoid(gd).astype(BF16), g2_ref[...])
    ones = ones_ref[...]
    kk = k * kk_ref[...]
    kkn = kk / jnp.maximum(jnp.sqrt(_segsum(kk * kk, ones)), 1e-12)
    ka = ka_ref[...]
    a_0 = al[:, 0:RW]
    a_1 = al[:, RW:2 * RW]
    k_0 = k * (1.0 + (a_0 - 1.0) * ka)
    k_1 = k * (1.0 + (a_1 - 1.0) * ka)
    bonus = _segsum(r * (k_0 + k_1) * rk_ref[...], ones) * v
    r_o[0] = r
    v_o[0] = v
    kkn_o[0] = kkn
    g_o[0] = g
    bonus_o[0] = bonus
    lw0_o[0] = lw[:, 0:RW]
    lw1_o[0] = lw[:, RW:2 * RW]
    k0_o[0] = k_0
    k1_o[0] = k_1
    b0_o[0] = kkn * a_0
    b1_o[0] = kkn * a_1


def _prep_call(zr, mup, mun, w0c, w2c, a0c, a2c, g2b, k_k, k_a, r_k, ones, tm):
    B, T, _ = zr.shape
    nh = tm // 8
    nt8 = T // 8
    row = lambda b, i: (b, i, 0)
    const2 = lambda b, i: (0, 0)
    out = pl.BlockSpec((1, tm, RW), row)
    return pl.pallas_call(
        _prep_kernel,
        grid=(B, T // tm),
        in_specs=[pl.BlockSpec((1, tm, RWKV_COLS), row),
                  pl.BlockSpec((1, 8, RWKV_COLS), lambda b, i: (b, jnp.maximum(i * nh - 1, 0), 0)),
                  pl.BlockSpec((1, 8, RWKV_COLS), lambda b, i: (b, jnp.minimum((i + 1) * nh, nt8 - 1), 0)),
                  pl.BlockSpec((1, RWKV_COLS), const2),
                  pl.BlockSpec((1, RWKV_COLS), const2),
                  pl.BlockSpec((1, 2 * RW), const2),
                  pl.BlockSpec((2 * LORA_W, 2 * RW), const2),
                  pl.BlockSpec((1, 2 * RW), const2),
                  pl.BlockSpec((2 * LORA_A, 2 * RW), const2),
                  pl.BlockSpec((LORA_G, RW), const2),
                  pl.BlockSpec((1, RW), const2),
                  pl.BlockSpec((1, RW), const2),
                  pl.BlockSpec((1, RW), const2),
                  pl.BlockSpec((RW, RW), const2)],
        out_specs=[out] * 11,
        out_shape=[jax.ShapeDtypeStruct((B, T, RW), F32)] * 11,
        compiler_params=_params(("parallel", "parallel")),
        name="prep",
    )(zr, zr, zr, mup, mun, w0c, w2c, a0c, a2c, g2b, k_k, k_a, r_k, ones)


def _mm(a, b):
    return _dot(a.astype(BF16), b.astype(BF16))


def _dir_operands(rev, nch, r, v, kk, lw, kd, bd):
    n = r.shape[0]
    ti = lax.broadcasted_iota(jnp.int32, (n, n), 0)
    si = lax.broadcasted_iota(jnp.int32, (n, n), 1)
    same = (ti // CHUNK) == (si // CHUNK)
    incl = ((si >= ti) if rev else (si <= ti)) & same
    mi = incl.astype(BF16)
    l1 = lw.astype(BF16)
    r1 = lw - l1.astype(F32)
    l2 = r1.astype(BF16)
    l3 = (r1 - l2.astype(F32)).astype(BF16)
    cum = _dot(mi, l1) + _dot(mi, l2) + _dot(mi, l3)
    wt = jnp.exp(cum)
    winv = jnp.exp(-cum)
    ops = dict(ah=-(kk * jnp.exp(cum - lw)), rt=r * wt, bt=bd * winv, kt=kd * winv, v=v)
    wcs, bhs, khs = [], [], []
    for j in range(nch):
        last = j * CHUNK if rev else (j + 1) * CHUNK - 1
        wc = wt[last:last + 1, :]
        rows = slice(j * CHUNK, (j + 1) * CHUNK)
        wcs.append(wc)
        bhs.append(ops["bt"][rows] * wc)
        khs.append(ops["kt"][rows] * wc)
    return ops, wcs, bhs, khs


def _scan_kernel(rf, vf, kkf, lwf, kf, bf, rb, vb, kkb, lwb, kb, bb, yf_ref, yb_ref, st_ref, *, nch):
    @pl.when(pl.program_id(1) == 0)
    def _():
        st_ref[...] = jnp.zeros_like(st_ref)

    c = CHUNK
    ti = lax.broadcasted_iota(jnp.int32, (c, c), 0)
    si = lax.broadcasted_iota(jnp.int32, (c, c), 1)
    blk = (ti // SUB) == (si // SUB)
    eye = ti == si
    masks = {False: (si < ti, si <= ti), True: (si > ti, si >= ti)}

    dirs = [(False, rf, vf, kkf, lwf, kf, bf), (True, rb, vb, kkb, lwb, kb, bb)]
    units = []
    for d, (rev, r_, v_, kk_, lw_, k_, b_) in enumerate(dirs):
        ops, wcs, bhs, khs = _dir_operands(rev, nch, r_[0], v_[0], kk_[0], lw_[0], k_[0], b_[0])
        for j in range(nch):
            rows = slice(j * c, (j + 1) * c)
            for h in range(HEADS):
                sl = slice(h * HD, (h + 1) * HD)
                u = dict(d=d, j=j, h=h, rev=rev, wc=wcs[j][:, sl],
                         bh=bhs[j][:, sl].astype(BF16), kh=khs[j][:, sl].astype(BF16))
                for name in ("ah", "rt", "bt", "kt", "v"):
                    u[name] = ops[name][rows, sl]
                units.append(u)

    for u in units:
        x1 = jnp.concatenate([u["ah"], u["rt"]], axis=0).astype(BF16)
        u["vb"] = u["v"].astype(BF16)
        u["lb"] = _dot_nt(x1, u["bt"].astype(BF16))
        u["lk"] = _dot_nt(x1, u["kt"].astype(BF16))
    for u in units:
        strict, incl = masks[u["rev"]]
        lab = jnp.where(strict, u["lb"][:c], 0.0)
        u["mrb"] = jnp.where(incl, u["lb"][c:], 0.0).astype(BF16)
        u["ld"] = jnp.where(blk, lab, 0.0)
        u["lo"] = lab - u["ld"]
        u["lm"] = jnp.concatenate([jnp.where(strict, u["lk"][:c], 0.0),
                                   jnp.where(incl, u["lk"][c:], 0.0)], axis=0).astype(BF16)
    for u in units:
        u["a2"] = _mm(u["ld"], u["ld"])
        u["lkv"] = _dot(u["lm"], u["vb"])
    for u in units:
        u["a4"] = _mm(u["a2"], u["a2"])
        u["t"] = jnp.where(eye, 1.0, 0.0) + u["ld"] + u["a2"] + _mm(u["ld"], u["a2"])
    for u in units:
        u["a8"] = _mm(u["a4"], u["a4"])
    for u in units:
        u["t2"] = jnp.where(eye, 1.0, 0.0) + u["a4"] + u["a8"] + _mm(u["a4"], u["a8"])
    for u in units:
        u["td"] = _mm(u["t"], u["t2"])
    for u in units:
        x = _mm(u["td"], jnp.concatenate([u["lo"], u["ah"], u["lkv"][:c]], axis=-1))
        u["n"] = x[:, :c]
        u["z"] = x[:, c:]
    for u in units:
        u["n2"] = _mm(u["n"], u["n"])
    for u in units:
        u["z"] = u["z"] + _mm(u["n2"], u["z"])
    for u in units:
        u["z"] = u["z"] + _mm(u["n"], u["z"])
    for u in units:
        zb = u["z"].astype(BF16)
        y10 = jnp.concatenate([u["rt"], u["lkv"][c:]], axis=-1) + _dot(u["mrb"], zb)
        gh = _dot_tn(u["bh"], zb)
        g = gh[:, :c] + jnp.where(eye, u["wc"], 0.0)
        hh = gh[:, c:] + _dot_tn(u["kh"], u["vb"])
        u["gy"] = jnp.concatenate([g, y10[:, :c]], axis=0).astype(BF16)
        u["h0"] = hh
        u["y0"] = y10[:, c:]

    ys = {}
    for d in range(2):
        order = range(nch - 1, -1, -1) if d == 1 else range(nch)
        zs = [st_ref[d, h] for h in range(HEADS)]
        for j in order:
            for h in range(HEADS):
                u = units[(d * nch + j) * HEADS + h]
                m = _dot(u["gy"], zs[h].astype(BF16))
                zs[h] = m[:c] + u["h0"]
                ys[(d, j, h)] = m[c:] + u["y0"]
        for h in range(HEADS):
            st_ref[d, h] = zs[h]
    for d, y_ref in enumerate((yf_ref, yb_ref)):
        y_ref[0] = jnp.concatenate(
            [jnp.concatenate([ys[(d, j, h)] for h in range(HEADS)], axis=-1) for j in range(nch)], axis=0)


def _scan_call(r, v, kkn, lw0, lw1, k0, k1, b0, b1, nch):
    B, T, _ = r.shape
    rows = nch * CHUNK
    ns = T // rows
    fwd = pl.BlockSpec((1, rows, RW), lambda b, c: (b, c, 0))
    bwd = pl.BlockSpec((1, rows, RW), lambda b, c: (b, ns - 1 - c, 0))
    return pl.pallas_call(
        functools.partial(_scan_kernel, nch=nch),
        grid=(B, ns),
        in_specs=[fwd] * 6 + [bwd] * 6,
        out_specs=[fwd, bwd],
        out_shape=[jax.ShapeDtypeStruct((B, T, RW), F32)] * 2,
        scratch_shapes=[pltpu.VMEM((2, HEADS, HD, HD), F32)],
        compiler_params=_params(("parallel", "arbitrary")),
        name="scan",
    )(r, v, kkn, lw0, k0, b0, r, v, kkn, lw1, k1, b1)


def _attn_kernel(q_ref, kp_ref, kc_ref, kn_ref, vp_ref, vc_ref, vn_ref, sink_ref, o_ref):
    n = pl.program_id(1)
    nb = pl.num_programs(1)
    q = q_ref[0]
    kband = jnp.concatenate([kp_ref[0], kc_ref[0], kn_ref[0]], axis=0)
    vband = jnp.concatenate([vp_ref[0], vc_ref[0], vn_ref[0]], axis=0)
    rows = GROUP * BLOCK
    ri = lax.broadcasted_iota(jnp.int32, (rows, 3 * BLOCK), 0)
    ci = lax.broadcasted_iota(jnp.int32, (rows, 3 * BLOCK), 1)
    qi = ri & (BLOCK - 1)
    valid = (ci >= qi) & (ci <= qi + 2 * BLOCK)
    valid = valid & ((n > 0) | (ci >= BLOCK)) & ((n < nb - 1) | (ci < 2 * BLOCK))
    rcol = lax.broadcasted_iota(jnp.int32, (rows, 1), 0) // BLOCK
    outs = [None] * HEADS
    for g in range(KV_HEADS):
        kg = kband[:, g * HD:(g + 1) * HD]
        vg = vband[:, g * HD:(g + 1) * HD]
        qs = jnp.concatenate([q[:, (g * GROUP + i) * HD:(g * GROUP + i + 1) * HD] for i in range(GROUP)],
                             axis=0)
        s = _dot_nt(qs, kg) * (HD ** -0.5)
        s = jnp.where(valid, s, NEG)
        sk = jnp.zeros((rows, 1), F32)
        for i in range(GROUP):
            sk = jnp.where(rcol == i, sink_ref[0:1, g * GROUP + i:g * GROUP + i + 1], sk)
        m = jnp.maximum(jnp.max(s, axis=-1, keepdims=True), sk)
        p = jnp.exp(s - m)
        denom = jnp.sum(p, axis=-1, keepdims=True) + jnp.exp(sk - m)
        o = _dot((p / denom).astype(BF16), vg)
        for i in range(GROUP):
            outs[g * GROUP + i] = o[i * BLOCK:(i + 1) * BLOCK]
    o_ref[0] = jnp.concatenate(outs, axis=-1).astype(BF16)


def _attn_call(q, k, v, sink):
    B, T, _ = q.shape
    nb = T // BLOCK
    prev = lambda b, n: (b, jnp.maximum(n - 1, 0), 0)
    cur = lambda b, n: (b, n, 0)
    nxt = lambda b, n: (b, jnp.minimum(n + 1, nb - 1), 0)
    kv = lambda im: pl.BlockSpec((1, BLOCK, KV_W), im)
    return pl.pallas_call(
        _attn_kernel,
        grid=(B, nb),
        in_specs=[pl.BlockSpec((1, BLOCK, RW), cur), kv(prev), kv(cur), kv(nxt), kv(prev), kv(cur), kv(nxt),
                  pl.BlockSpec((1, HEADS), lambda b, n: (0, 0))],
        out_specs=pl.BlockSpec((1, BLOCK, RW), cur),
        out_shape=jax.ShapeDtypeStruct((B, T, RW), BF16),
        compiler_params=_params(("parallel", "parallel")),
        name="attn",
    )(q, k, k, k, v, v, v, sink)


def _merge_kernel(x_ref, mod_ref, yf_ref, yb_ref, bonus_ref, g_ref, yatt_ref, gl_ref,
                  gnw_ref, gnb_ref, ones_ref, pr_ref, pa_ref, wo_ref, n2g_ref, wr_ref,
                  x1_ref, u2_ref, aff_ref):
    ones = ones_ref[...]
    y = yf_ref[0] + yb_ref[0]
    mu = _segsum(y, ones) * (1.0 / HD)
    yc = y - mu
    var = _segsum(yc * yc, ones) * (1.0 / HD)
    yn = yc * lax.rsqrt(var + GN_EPS) * gnw_ref[...] + gnb_ref[...]
    ya = ((yn + bonus_ref[0]) * g_ref[0]).astype(BF16)
    pa = _dot(ya, pr_ref[...])
    pb = _dot(yatt_ref[0], pa_ref[...])
    gates = jax.nn.sigmoid(gl_ref[0].astype(F32))
    m = gates[:, :D_MODEL] * pa + gates[:, D_MODEL:] * pb
    x1 = x_ref[0] + mod_ref[0, 2:3, :] * _dot(m.astype(BF16), wo_ref[...])
    x1_ref[0] = x1
    ms = jnp.mean(x1 * x1, axis=-1, keepdims=True)
    u2 = x1 * lax.rsqrt(ms + NORM_EPS) * n2g_ref[...] * (1.0 + mod_ref[0, 4:5, :]) + mod_ref[0, 3:4, :]
    u2_ref[0] = u2.astype(BF16)
    logits = lax.dot_general(wr_ref[...], u2, (((1,), (1,)), ((), ())), preferred_element_type=F32,
                             precision=lax.Precision.HIGHEST)
    e = jnp.exp(logits - jnp.max(logits, axis=0, keepdims=True))
    aff_ref[0] = e / jnp.sum(e, axis=0, keepdims=True)


def _merge_call(x, mod3, yf, yb, bonus, g, yatt, gl, gnw, gnb, ones, p_r, p_a, w_o, n2g, w_rt, tm):
    B, T, D = x.shape
    row = lambda b, i: (b, i, 0)
    const2 = lambda b, i: (0, 0)
    rw = pl.BlockSpec((1, tm, RW), row)
    return pl.pallas_call(
        _merge_kernel,
        grid=(B, T // tm),
        in_specs=[pl.BlockSpec((1, tm, D), row),
                  pl.BlockSpec((1, 6, D), lambda b, i: (b, 0, 0)),
                  rw, rw, rw, rw, rw,
                  pl.BlockSpec((1, tm, 2 * D), row),
                  pl.BlockSpec((1, RW), const2),
                  pl.BlockSpec((1, RW), const2),
                  pl.BlockSpec((RW, RW), const2),
                  pl.BlockSpec((RW, D), const2),
                  pl.BlockSpec((RW, D), const2),
                  pl.BlockSpec((D, D), const2),
                  pl.BlockSpec((1, D), const2),
                  pl.BlockSpec((N_EXPERTS, D), const2)],
        out_specs=[pl.BlockSpec((1, tm, D), row),
                   pl.BlockSpec((1, tm, D), row),
                   pl.BlockSpec((1, N_EXPERTS, tm), lambda b, i: (b, 0, i))],
        out_shape=[jax.ShapeDtypeStruct((B, T, D), F32),
                   jax.ShapeDtypeStruct((B, T, D), BF16),
                   jax.ShapeDtypeStruct((B, N_EXPERTS, T), F32)],
        compiler_params=_params(("parallel", "parallel")),
        name="merge",
    )(x, mod3, yf, yb, bonus, g, yatt, gl, gnw, gnb, ones, p_r, p_a, w_o, n2g, w_rt)


def _excl_prefix(mask, tri):
    e, t = mask.shape
    mb = mask.astype(BF16)
    carry = jnp.zeros((e, 1), F32)
    outs = []
    for j in range(t // 128):
        tile = mb[:, j * 128:(j + 1) * 128]
        outs.append(_dot(tile, tri) + carry)
        carry = carry + jnp.sum(tile.astype(F32), axis=1, keepdims=True)
    return jnp.concatenate(outs, axis=-1)


def _route_kernel(aff_ref, tri_ref, slot_ref, *, cap):
    a = aff_ref[0]
    e = a.shape[0]
    capf = jnp.float32(cap)

    def body(_, lohi):
        lo, hi = lohi
        mid = 0.5 * (lo + hi)
        ge = jnp.sum((a >= mid).astype(F32), axis=1, keepdims=True) >= capf
        return jnp.where(ge, mid, lo), jnp.where(ge, hi, mid)

    lo0 = jnp.zeros((e, 1), F32)
    hi0 = jnp.full((e, 1), 2.0, F32)
    lo, hi = lax.fori_loop(0, BISECT_ITERS, body, (lo0, hi0))
    gt = a >= hi
    eq = (a >= lo) & (a < hi)
    need = capf - jnp.sum(gt.astype(F32), axis=1, keepdims=True)
    tri = tri_ref[...]
    sel = gt | (eq & (_excl_prefix(eq, tri) < need))
    pos = _excl_prefix(sel, tri)
    slot_ref[0] = jnp.where(sel, pos.astype(jnp.int32), -1)


def _route_call(aff, tri, cap):
    B, E, T = aff.shape
    return pl.pallas_call(
        functools.partial(_route_kernel, cap=cap),
        grid=(B,),
        in_specs=[pl.BlockSpec((1, E, T), lambda b: (b, 0, 0)),
                  pl.BlockSpec((128, 128), lambda b: (0, 0))],
        out_specs=pl.BlockSpec((1, E, T), lambda b: (b, 0, 0)),
        out_shape=jax.ShapeDtypeStruct((B, E, T), jnp.int32),
        compiler_params=_params(("parallel",)),
        name="route",
    )(aff, tri)


def _moe_kernel(u2_ref, slot_ref, aff_ref, wg_ref, wu_ref, wd_ref, o_ref, acc_ref, *, cap, tt):
    e = pl.program_id(1)

    @pl.when(e == 0)
    def _():
        acc_ref[...] = jnp.zeros_like(acc_ref)

    t = u2_ref.shape[1]
    srow = slot_ref[0, pl.ds(e, 1), :]
    arow = aff_ref[0, pl.ds(e, 1), :]
    hit = lax.broadcasted_iota(jnp.int32, (cap, t), 0) == srow
    xe = _dot(hit.astype(BF16), u2_ref[0]).astype(BF16)
    hg = _dot(xe, wg_ref[0])
    hu = _dot(xe, wu_ref[0])
    h = (hg * jax.nn.sigmoid(hg) * hu).astype(BF16)
    ye = _dot(h, wd_ref[0]).astype(BF16)
    wsc = jnp.where(hit, arow, 0.0).astype(BF16)
    for j in range(t // tt):
        acc_ref[j * tt:(j + 1) * tt, :] += _dot_tn(wsc[:, j * tt:(j + 1) * tt], ye)

    @pl.when(e == pl.num_programs(1) - 1)
    def _():
        o_ref[0] = acc_ref[...].astype(BF16)


def _moe_call(u2, slot, aff, wg, wu, wd, cap):
    B, T, D = u2.shape
    E = wg.shape[0]
    F = wg.shape[2]
    tt = min(T, 512)
    return pl.pallas_call(
        functools.partial(_moe_kernel, cap=cap, tt=tt),
        grid=(B, E),
        in_specs=[pl.BlockSpec((1, T, D), lambda b, e: (b, 0, 0)),
                  pl.BlockSpec((1, E, T), lambda b, e: (b, 0, 0)),
                  pl.BlockSpec((1, E, T), lambda b, e: (b, 0, 0)),
                  pl.BlockSpec((1, D, F), lambda b, e: (e, 0, 0)),
                  pl.BlockSpec((1, D, F), lambda b, e: (e, 0, 0)),
                  pl.BlockSpec((1, F, D), lambda b, e: (e, 0, 0))],
        out_specs=pl.BlockSpec((1, T, D), lambda b, e: (b, 0, 0)),
        out_shape=jax.ShapeDtypeStruct((B, T, D), BF16),
        scratch_shapes=[pltpu.VMEM((T, D), F32)],
        compiler_params=_params(("parallel", "arbitrary")),
        name="moe",
    )(u2, slot, aff, wg, wu, wd)


def _final_kernel(x1_ref, mod_ref, moe_ref, o_ref):
    o_ref[0] = x1_ref[0] + mod_ref[0, 5:6, :] * moe_ref[0].astype(F32)


def _final_call(x1, mod3, moe, tm):
    B, T, D = x1.shape
    row = lambda b, i: (b, i, 0)
    return pl.pallas_call(
        _final_kernel,
        grid=(B, T // tm),
        in_specs=[pl.BlockSpec((1, tm, D), row),
                  pl.BlockSpec((1, 6, D), lambda b, i: (b, 0, 0)),
                  pl.BlockSpec((1, tm, D), row)],
        out_specs=pl.BlockSpec((1, tm, D), row),
        out_shape=jax.ShapeDtypeStruct((B, T, D), F32),
        compiler_params=_params(("parallel", "parallel")),
        name="final",
    )(x1, mod3, moe)


def _blockdiag2(w):
    z = jnp.zeros_like(w[0])
    return jnp.concatenate([jnp.concatenate([w[0], z], axis=1), jnp.concatenate([z, w[1]], axis=1)], axis=0)


def _layer(x, mod3, cos_t, sin_t, ones, tri, norm1_g, w_in, mu_prev, mu_next, rwkv_w0, rwkv_w2, rwkv_a0,
           rwkv_a2, rwkv_g2, rwkv_k_k, rwkv_k_a, rwkv_r_k, rwkv_gn_w, rwkv_gn_b, q_norm_g, k_norm_g,
           attn_sink, p_rwkv, p_attn, w_out, norm2_g, w_router, w_gate, w_up, w_down):
    B, T, D = x.shape
    tm = min(T, 256)
    cap = CAP_FACTOR * T // N_EXPERTS
    row = lambda a: a.reshape(1, -1)
    zr, q, k, v, gl = _inproj_call(x, mod3, row(norm1_g), w_in.astype(BF16), cos_t, sin_t,
                                   row(jnp.tile(q_norm_g, HEADS)), row(jnp.tile(k_norm_g, KV_HEADS)),
                                   ones, tm)
    r, vv, kkn, g, bonus, lw0, lw1, k0, k1, b0, b1 = _prep_call(
        zr, row(mu_prev), row(mu_next), row(rwkv_w0), _blockdiag2(rwkv_w2).astype(BF16),
        row(rwkv_a0), _blockdiag2(rwkv_a2).astype(BF16), rwkv_g2.astype(BF16),
        row(rwkv_k_k), row(rwkv_k_a), row(rwkv_r_k), ones, tm)
    yf, yb = _scan_call(r, vv, kkn, lw0, lw1, k0, k1, b0, b1, SCAN_CHUNKS)
    yatt = _attn_call(q, k, v, row(attn_sink))
    x1, u2, aff = _merge_call(x, mod3, yf, yb, bonus, g, yatt, gl, row(rwkv_gn_w), row(rwkv_gn_b), ones,
                              p_rwkv.astype(BF16), p_attn.astype(BF16), w_out.astype(BF16),
                              row(norm2_g), w_router.T, tm)
    slot = _route_call(aff, tri, cap)
    moe = _moe_call(u2, slot, aff, w_gate.astype(BF16), w_up.astype(BF16), w_down.astype(BF16), cap)
    return _final_call(x1, mod3, moe, tm)


def kernel(x, c, positions, w_ada, b_ada, norm1_g, w_in, mu_prev, mu_next, rwkv_w0, rwkv_w2, rwkv_a0, rwkv_a2, rwkv_g2, rwkv_k_k, rwkv_k_a, rwkv_r_k, rwkv_gn_w, rwkv_gn_b, q_norm_g, k_norm_g, attn_sink, p_rwkv, p_attn, w_out, norm2_g, w_router, w_gate, w_up, w_down):
    B, T, D = x.shape
    depth = w_ada.shape[0]
    half = ROT // 2
    inv_freq = ROPE_THETA ** (-jnp.arange(0, ROT, 2, dtype=F32) / ROT)
    ang = positions.astype(F32)[..., None] * inv_freq
    cos8, sin8 = jnp.cos(ang), jnp.sin(ang)
    pad1 = jnp.ones((B, T, HD - ROT), F32)
    pad0 = jnp.zeros((B, T, HD - ROT), F32)
    cos_t = jnp.tile(jnp.concatenate([cos8, cos8, pad1], axis=-1), (1, 1, KV_HEADS))
    sin_t = jnp.tile(jnp.concatenate([-sin8, sin8, pad0], axis=-1), (1, 1, KV_HEADS))
    seg = jnp.arange(RW) // HD
    ones = (seg[:, None] == seg[None, :]).astype(BF16)
    idx = jnp.arange(128)
    tri = (idx[:, None] < idx[None, :]).astype(BF16)
    for l in range(depth):
        mod3 = _mod_call(c, w_ada[l], b_ada[l]).reshape(B, 6, D)
        x = _layer(x, mod3, cos_t, sin_t, ones, tri, norm1_g[l], w_in[l], mu_prev[l], mu_next[l],
                   rwkv_w0[l], rwkv_w2[l], rwkv_a0[l], rwkv_a2[l], rwkv_g2[l], rwkv_k_k[l], rwkv_k_a[l],
                   rwkv_r_k[l], rwkv_gn_w[l], rwkv_gn_b[l], q_norm_g[l], k_norm_g[l], attn_sink[l],
                   p_rwkv[l], p_attn[l], w_out[l], norm2_g[l], w_router[l], w_gate[l], w_up[l], w_down[l])
    return x
```

```python
import functools
import math

import jax
import jax.numpy as jnp
from jax import lax
from jax.experimental import pallas as pl
from jax.experimental.pallas import tpu as pltpu

F32 = jnp.float32
BF16 = jnp.bfloat16

D_MODEL = 1024
RW = 512
HEADS = 8
HD = 64
LORA_W = 64
LORA_A = 64
LORA_G = 128
GN_EPS = HD * 1e-5
KV_HEADS = 2
GROUP = HEADS // KV_HEADS
KV_W = KV_HEADS * HD
BLOCK = 128
ROPE_THETA = 500000.0
ROT = HD // 4
N_EXPERTS = 16
CAP_FACTOR = 2
NORM_EPS = 1e-6
RWKV_COLS = 3 * RW + 2 * LORA_W + 2 * LORA_A + LORA_G
Q0 = RWKV_COLS
K0 = Q0 + RW
V0 = K0 + KV_W
G0 = V0 + KV_W
IN_COLS = G0 + 2 * D_MODEL
CHUNK = 64
SUB = 16
SCAN_CHUNKS = 2
ATT_QB = 2
BISECT_ITERS = 160
NEG = -1e30
VMEM_LIMIT = 56 * 1024 * 1024


def _dot(a, b):
    return jnp.dot(a, b, preferred_element_type=F32)


def _dot_nt(a, b):
    return lax.dot_general(a, b, (((1,), (1,)), ((), ())), preferred_element_type=F32)


def _dot_tn(a, b):
    return lax.dot_general(a, b, (((0,), (0,)), ((), ())), preferred_element_type=F32)


def _segsum(x, ones):
    xh = x.astype(BF16)
    xl = (x - xh.astype(F32)).astype(BF16)
    return _dot(xh, ones) + _dot(xl, ones)


def _params(sem):
    return pltpu.CompilerParams(dimension_semantics=sem, vmem_limit_bytes=VMEM_LIMIT)


def _mod_kernel(c_ref, w_ref, b_ref, o_ref):
    c = c_ref[...]
    ca = c * jax.nn.sigmoid(c)
    o_ref[...] = jnp.dot(ca, w_ref[...], preferred_element_type=F32,
                         precision=lax.Precision.HIGHEST) + b_ref[...]


def _mod_call(c, w_ada, b_ada):
    B, D = c.shape
    n = w_ada.shape[1] // D
    return pl.pallas_call(
        _mod_kernel,
        grid=(n,),
        in_specs=[pl.BlockSpec((B, D), lambda j: (0, 0)),
                  pl.BlockSpec((D, D), lambda j: (0, j)),
                  pl.BlockSpec((1, D), lambda j: (0, j))],
        out_specs=pl.BlockSpec((B, D), lambda j: (0, j)),
        out_shape=jax.ShapeDtypeStruct((B, n * D), F32),
        compiler_params=_params(("arbitrary",)),
        name="mod",
    )(c, w_ada, b_ada.reshape(1, -1))


def _rope(xn, cos, sin, width):
    lane = lax.broadcasted_iota(jnp.int32, xn.shape, 1) & (HD - 1)
    rot = jnp.where(lane < ROT // 2, pltpu.roll(xn, width - ROT // 2, 1), pltpu.roll(xn, ROT // 2, 1))
    return xn * cos + rot * sin


def _inproj_kernel(x_ref, mod_ref, g_ref, w_ref, wq_ref, cos_ref, sin_ref, cosq_ref, sinq_ref,
                   qg_ref, kg_ref, ones_ref, zr_ref, qt_ref, k_ref, v_ref, gl_ref):
    x = x_ref[0]
    tm = x.shape[0]
    ms = jnp.mean(x * x, axis=-1, keepdims=True)
    y = x * lax.rsqrt(ms + NORM_EPS) * g_ref[...]
    u = y * (1.0 + mod_ref[0, 1:2, :]) + mod_ref[0, 0:1, :]
    ub = u.astype(BF16)
    c_k, c_v, c_g = RWKV_COLS, RWKV_COLS + KV_W, RWKV_COLS + 2 * KV_W
    zr_ref[0] = _dot(ub, w_ref[:, 0:c_k])
    k = _dot(ub, w_ref[:, c_k:c_v])
    v_ref[0] = _dot(ub, w_ref[:, c_v:c_g]).astype(BF16)
    gl_ref[0] = _dot(ub, w_ref[:, c_g:]).astype(BF16)
    kn = k * lax.rsqrt(_segsum(k * k, ones_ref[...]) * (1.0 / HD) + NORM_EPS) * kg_ref[...]
    k_ref[0] = _rope(kn, cos_ref[0], sin_ref[0], KV_W).astype(BF16)
    qt = _dot_nt(wq_ref[...], ub)
    qg = jnp.concatenate([qg_ref[...]] * (tm // 128), axis=1)
    cq = cosq_ref[0]
    sq = sinq_ref[0]
    half = ROT // 2
    pieces = []
    for h in range(HEADS):
        xh = qt[h * HD:(h + 1) * HD]
        xn = (xh * lax.rsqrt(jnp.mean(xh * xh, axis=0, keepdims=True) + NORM_EPS)
              * qg[h * HD:(h + 1) * HD] * (HD ** -0.5))
        x1, x2 = xn[0:half], xn[half:ROT]
        pieces += [x1 * cq - x2 * sq, x2 * cq + x1 * sq, xn[ROT:]]
    qt_ref[0] = jnp.concatenate(pieces, axis=0).astype(BF16)


def _inproj_call(x, mod3, norm1_g, w_rest, wq_t, cos_t, sin_t, cos_q, sin_q, qg, kg, ones, tm):
    B, T, D = x.shape
    row = lambda b, i: (b, i, 0)
    col = lambda b, i: (b, 0, i)
    const2 = lambda b, i: (0, 0)
    half = ROT // 2
    return pl.pallas_call(
        _inproj_kernel,
        grid=(B, T // tm),
        in_specs=[pl.BlockSpec((1, tm, D), row),
                  pl.BlockSpec((1, 6, D), lambda b, i: (b, 0, 0)),
                  pl.BlockSpec((1, D), const2),
                  pl.BlockSpec(w_rest.shape, const2),
                  pl.BlockSpec((RW, D), const2),
                  pl.BlockSpec((1, tm, KV_W), row),
                  pl.BlockSpec((1, tm, KV_W), row),
                  pl.BlockSpec((1, half, tm), col),
                  pl.BlockSpec((1, half, tm), col),
                  pl.BlockSpec((RW, 128), const2),
                  pl.BlockSpec((1, KV_W), const2),
                  pl.BlockSpec((KV_W, KV_W), const2)],
        out_specs=[pl.BlockSpec((1, tm, RWKV_COLS), row),
                   pl.BlockSpec((1, RW, tm), col),
                   pl.BlockSpec((1, tm, KV_W), row),
                   pl.BlockSpec((1, tm, KV_W), row),
                   pl.BlockSpec((1, tm, 2 * D), row)],
        out_shape=[jax.ShapeDtypeStruct((B, T, RWKV_COLS), F32),
                   jax.ShapeDtypeStruct((B, RW, T), BF16),
                   jax.ShapeDtypeStruct((B, T, KV_W), BF16),
                   jax.ShapeDtypeStruct((B, T, KV_W), BF16),
                   jax.ShapeDtypeStruct((B, T, 2 * D), BF16)],
        compiler_params=_params(("parallel", "parallel")),
        name="inproj",
    )(x, mod3, norm1_g, w_rest, wq_t, cos_t, sin_t, cos_q, sin_q, qg, kg, ones)


def _prep_kernel(z_ref, hp_ref, hn_ref, mup_ref, mun_ref, w0_ref, w2_ref, a0_ref, a2_ref, g2_ref,
                 kk_ref, ka_ref, rk_ref, ones_ref,
                 r_o, v_o, kkn_o, g_o, bonus_o, lw0_o, lw1_o, k0_o, k1_o, b0_o, b1_o):
    i = pl.program_id(1)
    last = pl.num_programs(1) - 1
    z = z_ref[0]
    tm = z.shape[0]
    row = lax.broadcasted_iota(jnp.int32, z.shape, 0)
    prev_row = jnp.where(i == 0, 0.0, hp_ref[0, 7:8, :])
    next_row = jnp.where(i == last, 0.0, hn_ref[0, 0:1, :])
    zp = jnp.where(row == 0, prev_row, pltpu.roll(z, 1, 0))
    zn = jnp.where(row == tm - 1, next_row, pltpu.roll(z, tm - 1, 0))
    zs = z + mup_ref[...] * (zp - z) + mun_ref[...] * (zn - z)
    r = zs[:, 0:RW]
    k = zs[:, RW:2 * RW]
    v = zs[:, 2 * RW:3 * RW]
    c0 = 3 * RW
    wd = zs[:, c0:c0 + 2 * LORA_W]
    ad = zs[:, c0 + 2 * LORA_W:c0 + 2 * LORA_W + 2 * LORA_A]
    gd = zs[:, c0 + 2 * LORA_W + 2 * LORA_A:RWKV_COLS]
    wl = _dot(jnp.tanh(wd).astype(BF16), w2_ref[...]) + w0_ref[...]
    lw = -math.exp(-0.5) * jax.nn.sigmoid(wl)
    al = jax.nn.sigmoid(_dot(ad.astype(BF16), a2_ref[...]) + a0_ref[...])
    g = _dot(jax.nn.sigmoid(gd).astype(BF16), g2_ref[...])
    ones = ones_ref[...]
    kk = k * kk_ref[...]
    kkn = kk / jnp.maximum(jnp.sqrt(_segsum(kk * kk, ones)), 1e-12)
    ka = ka_ref[...]
    a_0 = al[:, 0:RW]
    a_1 = al[:, RW:2 * RW]
    k_0 = k * (1.0 + (a_0 - 1.0) * ka)
    k_1 = k * (1.0 + (a_1 - 1.0) * ka)
    bonus = _segsum(r * (k_0 + k_1) * rk_ref[...], ones) * v
    r_o[0] = r.astype(BF16)
    v_o[0] = v.astype(BF16)
    kkn_o[0] = kkn.astype(BF16)
    g_o[0] = g.astype(BF16)
    bonus_o[0] = bonus.astype(BF16)
    lw0_o[0] = lw[:, 0:RW]
    lw1_o[0] = lw[:, RW:2 * RW]
    k0_o[0] = k_0.astype(BF16)
    k1_o[0] = k_1.astype(BF16)
    b0_o[0] = (kkn * a_0).astype(BF16)
    b1_o[0] = (kkn * a_1).astype(BF16)


def _prep_call(zr, mup, mun, w0c, w2c, a0c, a2c, g2b, k_k, k_a, r_k, ones, tm):
    B, T, _ = zr.shape
    nh = tm // 8
    nt8 = T // 8
    row = lambda b, i: (b, i, 0)
    const2 = lambda b, i: (0, 0)
    out = pl.BlockSpec((1, tm, RW), row)
    return pl.pallas_call(
        _prep_kernel,
        grid=(B, T // tm),
        in_specs=[pl.BlockSpec((1, tm, RWKV_COLS), row),
                  pl.BlockSpec((1, 8, RWKV_COLS), lambda b, i: (b, jnp.maximum(i * nh - 1, 0), 0)),
                  pl.BlockSpec((1, 8, RWKV_COLS), lambda b, i: (b, jnp.minimum((i + 1) * nh, nt8 - 1), 0)),
                  pl.BlockSpec((1, RWKV_COLS), const2),
                  pl.BlockSpec((1, RWKV_COLS), const2),
                  pl.BlockSpec((1, 2 * RW), const2),
                  pl.BlockSpec((2 * LORA_W, 2 * RW), const2),
                  pl.BlockSpec((1, 2 * RW), const2),
                  pl.BlockSpec((2 * LORA_A, 2 * RW), const2),
                  pl.BlockSpec((LORA_G, RW), const2),
                  pl.BlockSpec((1, RW), const2),
                  pl.BlockSpec((1, RW), const2),
                  pl.BlockSpec((1, RW), const2),
                  pl.BlockSpec((RW, RW), const2)],
        out_specs=[out] * 11,
        out_shape=[jax.ShapeDtypeStruct((B, T, RW), F32 if i in (5, 6) else BF16) for i in range(11)],
        compiler_params=_params(("parallel", "parallel")),
        name="prep",
    )(zr, zr, zr, mup, mun, w0c, w2c, a0c, a2c, g2b, k_k, k_a, r_k, ones)


def _mm(a, b):
    return _dot(a.astype(BF16), b.astype(BF16))


def _dir_operands(rev, nch, r, v, kk, lw, kd, bd):
    n = r.shape[0]
    ti = lax.broadcasted_iota(jnp.int32, (n, n), 0)
    si = lax.broadcasted_iota(jnp.int32, (n, n), 1)
    same = (ti // CHUNK) == (si // CHUNK)
    incl = ((si >= ti) if rev else (si <= ti)) & same
    mi = incl.astype(BF16)
    l1 = lw.astype(BF16)
    r1 = lw - l1.astype(F32)
    l2 = r1.astype(BF16)
    l3 = (r1 - l2.astype(F32)).astype(BF16)
    cum = _dot(mi, l1) + _dot(mi, l2) + _dot(mi, l3)
    wt = jnp.exp(cum)
    winv = jnp.exp(-cum)
    ops = dict(ah=-(kk * jnp.exp(cum - lw)), rt=r * wt, bt=bd * winv, kt=kd * winv, v=v)
    wcs, bhs, khs = [], [], []
    for j in range(nch):
        last = j * CHUNK if rev else (j + 1) * CHUNK - 1
        wc = wt[last:last + 1, :]
        rows = slice(j * CHUNK, (j + 1) * CHUNK)
        wcs.append(wc)
        bhs.append(ops["bt"][rows] * wc)
        khs.append(ops["kt"][rows] * wc)
    return ops, wcs, bhs, khs


def _scan_kernel(rf, vf, kkf, lwf, kf, bf, rb, vb, kkb, lwb, kb, bb, yf_ref, yb_ref, st_ref, *, nch):
    @pl.when(pl.program_id(1) == 0)
    def _():
        st_ref[...] = jnp.zeros_like(st_ref)

    c = CHUNK
    ti = lax.broadcasted_iota(jnp.int32, (c, c), 0)
    si = lax.broadcasted_iota(jnp.int32, (c, c), 1)
    blk = (ti // SUB) == (si // SUB)
    eye = ti == si
    masks = {False: (si < ti, si <= ti), True: (si > ti, si >= ti)}

    dirs = [(False, rf, vf, kkf, lwf, kf, bf), (True, rb, vb, kkb, lwb, kb, bb)]
    units = []
    for d, (rev, r_, v_, kk_, lw_, k_, b_) in enumerate(dirs):
        f32 = lambda ref: ref[0].astype(F32)
        ops, wcs, bhs, khs = _dir_operands(rev, nch, f32(r_), f32(v_), f32(kk_), lw_[0], f32(k_), f32(b_))
        for j in range(nch):
            rows = slice(j * c, (j + 1) * c)
            for h in range(HEADS):
                sl = slice(h * HD, (h + 1) * HD)
                u = dict(d=d, j=j, h=h, rev=rev, wc=wcs[j][:, sl],
                         bh=bhs[j][:, sl].astype(BF16), kh=khs[j][:, sl].astype(BF16))
                for name in ("ah", "rt", "bt", "kt", "v"):
                    u[name] = ops[name][rows, sl]
                units.append(u)

    for u in units:
        x1 = jnp.concatenate([u["ah"], u["rt"]], axis=0).astype(BF16)
        u["vb"] = u["v"].astype(BF16)
        u["lb"] = _dot_nt(x1, u["bt"].astype(BF16))
        u["lk"] = _dot_nt(x1, u["kt"].astype(BF16))
    for u in units:
        strict, incl = masks[u["rev"]]
        lab = jnp.where(strict, u["lb"][:c], 0.0)
        u["mrb"] = jnp.where(incl, u["lb"][c:], 0.0).astype(BF16)
        u["ld"] = jnp.where(blk, lab, 0.0)
        u["lo"] = lab - u["ld"]
        u["lm"] = jnp.concatenate([jnp.where(strict, u["lk"][:c], 0.0),
                                   jnp.where(incl, u["lk"][c:], 0.0)], axis=0).astype(BF16)
    for u in units:
        u["a2"] = _mm(u["ld"], u["ld"])
        u["lkv"] = _dot(u["lm"], u["vb"])
    for u in units:
        u["a4"] = _mm(u["a2"], u["a2"])
        u["t"] = jnp.where(eye, 1.0, 0.0) + u["ld"] + u["a2"] + _mm(u["ld"], u["a2"])
    for u in units:
        u["a8"] = _mm(u["a4"], u["a4"])
    for u in units:
        u["t2"] = jnp.where(eye, 1.0, 0.0) + u["a4"] + u["a8"] + _mm(u["a4"], u["a8"])
    for u in units:
        u["td"] = _mm(u["t"], u["t2"])
    for u in units:
        x = _mm(u["td"], jnp.concatenate([u["lo"], u["ah"], u["lkv"][:c]], axis=-1))
        u["n"] = x[:, :c]
        u["z"] = x[:, c:]
    for u in units:
        u["n2"] = _mm(u["n"], u["n"])
    for u in units:
        u["z"] = u["z"] + _mm(u["n2"], u["z"])
    for u in units:
        u["z"] = u["z"] + _mm(u["n"], u["z"])
    for u in units:
        zb = u["z"].astype(BF16)
        y10 = jnp.concatenate([u["rt"], u["lkv"][c:]], axis=-1) + _dot(u["mrb"], zb)
        gh = _dot_tn(u["bh"], zb)
        g = gh[:, :c] + jnp.where(eye, u["wc"], 0.0)
        hh = gh[:, c:] + _dot_tn(u["kh"], u["vb"])
        u["gy"] = jnp.concatenate([g, y10[:, :c]], axis=0).astype(BF16)
        u["h0"] = hh
        u["y0"] = y10[:, c:]

    ys = {}
    for d in range(2):
        order = range(nch - 1, -1, -1) if d == 1 else range(nch)
        zs = [st_ref[d, h] for h in range(HEADS)]
        for j in order:
            for h in range(HEADS):
                u = units[(d * nch + j) * HEADS + h]
                m = _dot(u["gy"], zs[h].astype(BF16))
                zs[h] = m[:c] + u["h0"]
                ys[(d, j, h)] = m[c:] + u["y0"]
        for h in range(HEADS):
            st_ref[d, h] = zs[h]
    for d, y_ref in enumerate((yf_ref, yb_ref)):
        y_ref[0] = jnp.concatenate(
            [jnp.concatenate([ys[(d, j, h)] for h in range(HEADS)], axis=-1) for j in range(nch)], axis=0)


def _scan_call(r, v, kkn, lw0, lw1, k0, k1, b0, b1, nch):
    B, T, _ = r.shape
    rows = nch * CHUNK
    ns = T // rows
    fwd = pl.BlockSpec((1, rows, RW), lambda b, c: (b, c, 0))
    bwd = pl.BlockSpec((1, rows, RW), lambda b, c: (b, ns - 1 - c, 0))
    return pl.pallas_call(
        functools.partial(_scan_kernel, nch=nch),
        grid=(B, ns),
        in_specs=[fwd] * 6 + [bwd] * 6,
        out_specs=[fwd, bwd],
        out_shape=[jax.ShapeDtypeStruct((B, T, RW), F32)] * 2,
        scratch_shapes=[pltpu.VMEM((2, HEADS, HD, HD), F32)],
        compiler_params=_params(("parallel", "arbitrary")),
        name="scan",
    )(r, v, kkn, lw0, k0, b0, r, v, kkn, lw1, k1, b1)


def _attn_kernel(sink_ref, qt_ref, k_ref, v_ref, o_ref, *, qb):
    n = pl.program_id(1)
    t = k_ref.shape[1]
    band = 3 * BLOCK
    cols = GROUP * BLOCK
    ki = lax.broadcasted_iota(jnp.int32, (band, BLOCK), 0)
    qi = lax.broadcasted_iota(jnp.int32, (band, BLOCK), 1)
    lane_head = lax.broadcasted_iota(jnp.int32, (1, cols), 1) // BLOCK
    zeros = jnp.zeros((HD, cols), BF16)
    units = []
    for j in range(qb):
        blk = n * qb + j
        start = pl.multiple_of(jnp.clip((blk - 1) * BLOCK, 0, t - band), BLOCK)
        bias = jnp.where(jnp.abs(start + ki - (blk * BLOCK + qi)) <= BLOCK, 0.0, NEG)
        bias = jnp.concatenate([bias] * GROUP, axis=1)
        kb = k_ref[0, pl.ds(start, band), :]
        vb = v_ref[0, pl.ds(start, band), :]
        for g in range(KV_HEADS):
            qg = jnp.concatenate([qt_ref[0, (g * GROUP + i) * HD:(g * GROUP + i + 1) * HD,
                                         j * BLOCK:(j + 1) * BLOCK] for i in range(GROUP)], axis=1)
            rhs = jnp.concatenate([qg if gg == g else zeros for gg in range(KV_HEADS)], axis=0)
            sk = jnp.zeros((1, cols), F32)
            for i in range(GROUP):
                sk = jnp.where(lane_head == i, sink_ref[g * GROUP + i], sk)
            units.append(dict(j=j, g=g, kb=kb, vb=vb, rhs=rhs, sk=sk, bias=bias))
    for u in units:
        u["s"] = _dot(u["kb"], u["rhs"]) + u["bias"]
    for u in units:
        u["m"] = jnp.maximum(jnp.max(u["s"], axis=0, keepdims=True), u["sk"])
    for u in units:
        p = jnp.exp(u["s"] - u["m"])
        u["r"] = 1.0 / (jnp.sum(p, axis=0, keepdims=True) + jnp.exp(u["sk"] - u["m"]))
        u["p"] = p.astype(BF16)
    for u in units:
        g = u["g"]
        u["o"] = _dot_tn(u["vb"], u["p"])[g * HD:(g + 1) * HD] * u["r"]
    for u in units:
        j, g = u["j"], u["g"]
        for i in range(GROUP):
            o_ref[0, (g * GROUP + i) * HD:(g * GROUP + i + 1) * HD, j * BLOCK:(j + 1) * BLOCK] = (
                u["o"][:, i * BLOCK:(i + 1) * BLOCK].astype(BF16))


def _attn_call(qt, k, v, sink, qb):
    B, _, T = qt.shape
    assert T >= 3 * BLOCK and T % (qb * BLOCK) == 0
    qspec = pl.BlockSpec((1, RW, qb * BLOCK), lambda b, n: (b, 0, n))
    kvspec = pl.BlockSpec((1, T, KV_W), lambda b, n: (b, 0, 0))
    return pl.pallas_call(
        functools.partial(_attn_kernel, qb=qb),
        grid=(B, T // (qb * BLOCK)),
        in_specs=[pl.BlockSpec(memory_space=pltpu.SMEM), qspec, kvspec, kvspec],
        out_specs=qspec,
        out_shape=jax.ShapeDtypeStruct((B, RW, T), BF16),
        compiler_params=_params(("parallel", "parallel")),
        name="attn",
    )(sink, qt, k, v)


def _merge_kernel(x_ref, mod_ref, yf_ref, yb_ref, bonus_ref, g_ref, yatt_ref, gl_ref,
                  gnw_ref, gnb_ref, ones_ref, pr_ref, pa_ref, wo_ref, n2g_ref, wr_ref,
                  x1_ref, u2_ref, aff_ref):
    ones = ones_ref[...]
    y = yf_ref[0] + yb_ref[0]
    mu = _segsum(y, ones) * (1.0 / HD)
    yc = y - mu
    var = _segsum(yc * yc, ones) * (1.0 / HD)
    yn = yc * lax.rsqrt(var + GN_EPS) * gnw_ref[...] + gnb_ref[...]
    ya = ((yn + bonus_ref[0].astype(F32)) * g_ref[0].astype(F32)).astype(BF16)
    pa = _dot(ya, pr_ref[...])
    pb = _dot_tn(yatt_ref[0], pa_ref[...])
    gates = jax.nn.sigmoid(gl_ref[0].astype(F32))
    m = gates[:, :D_MODEL] * pa + gates[:, D_MODEL:] * pb
    x1 = x_ref[0] + mod_ref[0, 2:3, :] * _dot(m.astype(BF16), wo_ref[...])
    x1_ref[0] = x1
    ms = jnp.mean(x1 * x1, axis=-1, keepdims=True)
    u2 = x1 * lax.rsqrt(ms + NORM_EPS) * n2g_ref[...] * (1.0 + mod_ref[0, 4:5, :]) + mod_ref[0, 3:4, :]
    u2_ref[0] = u2.astype(BF16)
    logits = lax.dot_general(wr_ref[...], u2, (((1,), (1,)), ((), ())), preferred_element_type=F32,
                             precision=lax.Precision.HIGHEST)
    e = jnp.exp(logits - jnp.max(logits, axis=0, keepdims=True))
    aff_ref[0] = e / jnp.sum(e, axis=0, keepdims=True)


def _merge_call(x, mod3, yf, yb, bonus, g, yatt, gl, gnw, gnb, ones, p_r, p_a, w_o, n2g, w_rt, tm):
    B, T, D = x.shape
    row = lambda b, i: (b, i, 0)
    const2 = lambda b, i: (0, 0)
    rw = pl.BlockSpec((1, tm, RW), row)
    return pl.pallas_call(
        _merge_kernel,
        grid=(B, T // tm),
        in_specs=[pl.BlockSpec((1, tm, D), row),
                  pl.BlockSpec((1, 6, D), lambda b, i: (b, 0, 0)),
                  rw, rw, rw, rw,
                  pl.BlockSpec((1, RW, tm), lambda b, i: (b, 0, i)),
                  pl.BlockSpec((1, tm, 2 * D), row),
                  pl.BlockSpec((1, RW), const2),
                  pl.BlockSpec((1, RW), const2),
                  pl.BlockSpec((RW, RW), const2),
                  pl.BlockSpec((RW, D), const2),
                  pl.BlockSpec((RW, D), const2),
                  pl.BlockSpec((D, D), const2),
                  pl.BlockSpec((1, D), const2),
                  pl.BlockSpec((N_EXPERTS, D), const2)],
        out_specs=[pl.BlockSpec((1, tm, D), row),
                   pl.BlockSpec((1, tm, D), row),
                   pl.BlockSpec((1, N_EXPERTS, tm), lambda b, i: (b, 0, i))],
        out_shape=[jax.ShapeDtypeStruct((B, T, D), F32),
                   jax.ShapeDtypeStruct((B, T, D), BF16),
                   jax.ShapeDtypeStruct((B, N_EXPERTS, T), F32)],
        compiler_params=_params(("parallel", "parallel")),
        name="merge",
    )(x, mod3, yf, yb, bonus, g, yatt, gl, gnw, gnb, ones, p_r, p_a, w_o, n2g, w_rt)


def _excl_prefix(mask, tri):
    e, t = mask.shape
    mb = mask.astype(BF16)
    carry = jnp.zeros((e, 1), F32)
    outs = []
    for j in range(t // 128):
        tile = mb[:, j * 128:(j + 1) * 128]
        outs.append(_dot(tile, tri) + carry)
        carry = carry + jnp.sum(tile.astype(F32), axis=1, keepdims=True)
    return jnp.concatenate(outs, axis=-1)


def _route_kernel(aff_ref, tri_ref, slot_ref, *, cap):
    nb, ne, t = aff_ref.shape
    a = aff_ref[...].reshape(nb * ne, t)
    e = a.shape[0]
    capf = jnp.float32(cap)

    def cond(s):
        it, lo, hi = s
        mid = 0.5 * (lo + hi)
        still_open = jnp.max(jnp.where((mid > lo) & (mid < hi), 1.0, 0.0))
        return (it < BISECT_ITERS) & (still_open > 0.0)

    def body(s):
        it, lo, hi = s
        mid = 0.5 * (lo + hi)
        ge = jnp.sum((a >= mid).astype(F32), axis=1, keepdims=True) >= capf
        return it + 1, jnp.where(ge, mid, lo), jnp.where(ge, hi, mid)

    lo0 = jnp.zeros((e, 1), F32)
    hi0 = jnp.full((e, 1), 2.0, F32)
    _, lo, hi = lax.while_loop(cond, body, (jnp.int32(0), lo0, hi0))
    gt = a >= hi
    eq = (a >= lo) & (a < hi)
    need = capf - jnp.sum(gt.astype(F32), axis=1, keepdims=True)
    tri = tri_ref[...]
    sel = gt | (eq & (_excl_prefix(eq, tri) < need))
    pos = _excl_prefix(sel, tri)
    slot_ref[...] = jnp.where(sel, pos.astype(jnp.int32), -1).reshape(nb, ne, t)


def _route_call(aff, tri, cap):
    B, E, T = aff.shape
    return pl.pallas_call(
        functools.partial(_route_kernel, cap=cap),
        grid=(1,),
        in_specs=[pl.BlockSpec((B, E, T), lambda i: (0, 0, 0)),
                  pl.BlockSpec((128, 128), lambda i: (0, 0))],
        out_specs=pl.BlockSpec((B, E, T), lambda i: (0, 0, 0)),
        out_shape=jax.ShapeDtypeStruct((B, E, T), jnp.int32),
        compiler_params=_params(("arbitrary",)),
        name="route",
    )(aff, tri)


def _moe_kernel(u2_ref, slot_ref, aff_ref, wg_ref, wu_ref, wd_ref, o_ref, acc_ref, *, cap, tt):
    e = pl.program_id(1)

    @pl.when(e == 0)
    def _():
        acc_ref[...] = jnp.zeros_like(acc_ref)

    t = u2_ref.shape[1]
    srow = slot_ref[0, pl.ds(e, 1), :]
    arow = aff_ref[0, pl.ds(e, 1), :]
    hit = lax.broadcasted_iota(jnp.int32, (cap, t), 0) == srow
    xe = _dot(hit.astype(BF16), u2_ref[0]).astype(BF16)
    hg = _dot(xe, wg_ref[0])
    hu = _dot(xe, wu_ref[0])
    h = (hg * jax.nn.sigmoid(hg) * hu).astype(BF16)
    ye = _dot(h, wd_ref[0]).astype(BF16)
    wsc = jnp.where(hit, arow, 0.0).astype(BF16)
    for j in range(t // tt):
        acc_ref[j * tt:(j + 1) * tt, :] += _dot_tn(wsc[:, j * tt:(j + 1) * tt], ye)

    @pl.when(e == pl.num_programs(1) - 1)
    def _():
        o_ref[0] = acc_ref[...].astype(BF16)


def _moe_call(u2, slot, aff, wg, wu, wd, cap):
    B, T, D = u2.shape
    E = wg.shape[0]
    F = wg.shape[2]
    tt = min(T, 512)
    return pl.pallas_call(
        functools.partial(_moe_kernel, cap=cap, tt=tt),
        grid=(B, E),
        in_specs=[pl.BlockSpec((1, T, D), lambda b, e: (b, 0, 0)),
                  pl.BlockSpec((1, E, T), lambda b, e: (b, 0, 0)),
                  pl.BlockSpec((1, E, T), lambda b, e: (b, 0, 0)),
                  pl.BlockSpec((1, D, F), lambda b, e: (e, 0, 0)),
                  pl.BlockSpec((1, D, F), lambda b, e: (e, 0, 0)),
                  pl.BlockSpec((1, F, D), lambda b, e: (e, 0, 0))],
        out_specs=pl.BlockSpec((1, T, D), lambda b, e: (b, 0, 0)),
        out_shape=jax.ShapeDtypeStruct((B, T, D), BF16),
        scratch_shapes=[pltpu.VMEM((T, D), F32)],
        compiler_params=_params(("parallel", "arbitrary")),
        name="moe",
    )(u2, slot, aff, wg, wu, wd)


def _final_kernel(x1_ref, mod_ref, moe_ref, o_ref):
    o_ref[0] = x1_ref[0] + mod_ref[0, 5:6, :] * moe_ref[0].astype(F32)


def _final_call(x1, mod3, moe, tm):
    B, T, D = x1.shape
    row = lambda b, i: (b, i, 0)
    return pl.pallas_call(
        _final_kernel,
        grid=(B, T // tm),
        in_specs=[pl.BlockSpec((1, tm, D), row),
                  pl.BlockSpec((1, 6, D), lambda b, i: (b, 0, 0)),
                  pl.BlockSpec((1, tm, D), row)],
        out_specs=pl.BlockSpec((1, tm, D), row),
        out_shape=jax.ShapeDtypeStruct((B, T, D), F32),
        compiler_params=_params(("parallel", "parallel")),
        name="final",
    )(x1, mod3, moe)


def _blockdiag2(w):
    z = jnp.zeros_like(w[0])
    return jnp.concatenate([jnp.concatenate([w[0], z], axis=1), jnp.concatenate([z, w[1]], axis=1)], axis=0)


def _layer(x, mod3, cos_t, sin_t, cos_q, sin_q, ones, tri, norm1_g, w_in, mu_prev, mu_next, rwkv_w0, rwkv_w2, rwkv_a0,
           rwkv_a2, rwkv_g2, rwkv_k_k, rwkv_k_a, rwkv_r_k, rwkv_gn_w, rwkv_gn_b, q_norm_g, k_norm_g,
           attn_sink, p_rwkv, p_attn, w_out, norm2_g, w_router, w_gate, w_up, w_down):
    B, T, D = x.shape
    tm = min(T, 256)
    cap = CAP_FACTOR * T // N_EXPERTS
    row = lambda a: a.reshape(1, -1)
    w_b = w_in.astype(BF16)
    w_rest = jnp.concatenate([w_b[:, :Q0], w_b[:, K0:V0], w_b[:, V0:G0], w_b[:, G0:]], axis=1)
    qg = jnp.broadcast_to(jnp.tile(q_norm_g, HEADS)[:, None], (RW, 128))
    zr, qt, k, v, gl = _inproj_call(x, mod3, row(norm1_g), w_rest, w_b[:, Q0:K0].T, cos_t, sin_t, cos_q, sin_q,
                                    qg, row(jnp.tile(k_norm_g, KV_HEADS)), ones[:KV_W, :KV_W], tm)
    r, vv, kkn, g, bonus, lw0, lw1, k0, k1, b0, b1 = _prep_call(
        zr, row(mu_prev), row(mu_next), row(rwkv_w0), _blockdiag2(rwkv_w2).astype(BF16),
        row(rwkv_a0), _blockdiag2(rwkv_a2).astype(BF16), rwkv_g2.astype(BF16),
        row(rwkv_k_k), row(rwkv_k_a), row(rwkv_r_k), ones, tm)
    yf, yb = _scan_call(r, vv, kkn, lw0, lw1, k0, k1, b0, b1, SCAN_CHUNKS)
    yatt = _attn_call(qt, k, v, attn_sink, ATT_QB)
    x1, u2, aff = _merge_call(x, mod3, yf, yb, bonus, g, yatt, gl, row(rwkv_gn_w), row(rwkv_gn_b), ones,
                              p_rwkv.astype(BF16), p_attn.astype(BF16), w_out.astype(BF16),
                              row(norm2_g), w_router.T, tm)
    slot = _route_call(aff, tri, cap)
    moe = _moe_call(u2, slot, aff, w_gate.astype(BF16), w_up.astype(BF16), w_down.astype(BF16), cap)
    return _final_call(x1, mod3, moe, tm)


def kernel(x, c, positions, w_ada, b_ada, norm1_g, w_in, mu_prev, mu_next, rwkv_w0, rwkv_w2, rwkv_a0, rwkv_a2, rwkv_g2, rwkv_k_k, rwkv_k_a, rwkv_r_k, rwkv_gn_w, rwkv_gn_b, q_norm_g, k_norm_g, attn_sink, p_rwkv, p_attn, w_out, norm2_g, w_router, w_gate, w_up, w_down):
    B, T, D = x.shape
    depth = w_ada.shape[0]
    half = ROT // 2
    inv_freq = ROPE_THETA ** (-jnp.arange(0, ROT, 2, dtype=F32) / ROT)
    ang = positions.astype(F32)[..., None] * inv_freq
    cos8, sin8 = jnp.cos(ang), jnp.sin(ang)
    pad1 = jnp.ones((B, T, HD - ROT), F32)
    pad0 = jnp.zeros((B, T, HD - ROT), F32)
    cos_t = jnp.tile(jnp.concatenate([cos8, cos8, pad1], axis=-1), (1, 1, KV_HEADS))
    sin_t = jnp.tile(jnp.concatenate([-sin8, sin8, pad0], axis=-1), (1, 1, KV_HEADS))
    seg = jnp.arange(RW) // HD
    ones = (seg[:, None] == seg[None, :]).astype(BF16)
    idx = jnp.arange(128)
    tri = (idx[:, None] < idx[None, :]).astype(BF16)
    cos_q, sin_q = jnp.swapaxes(cos8, 1, 2), jnp.swapaxes(sin8, 1, 2)
    for l in range(depth):
        mod3 = _mod_call(c, w_ada[l], b_ada[l]).reshape(B, 6, D)
        x = _layer(x, mod3, cos_t, sin_t, cos_q, sin_q, ones, tri, norm1_g[l], w_in[l], mu_prev[l], mu_next[l],
                   rwkv_w0[l], rwkv_w2[l], rwkv_a0[l], rwkv_a2[l], rwkv_g2[l], rwkv_k_k[l], rwkv_k_a[l],
                   rwkv_r_k[l], rwkv_gn_w[l], rwkv_gn_b[l], q_norm_g[l], k_norm_g[l], attn_sink[l],
                   p_rwkv[l], p_attn[l], w_out[l], norm2_g[l], w_router[l], w_gate[l], w_up[l], w_down[l])
    return x
```

```python
import functools
import math

import jax
import jax.numpy as jnp
from jax import lax
from jax.experimental import pallas as pl
from jax.experimental.pallas import tpu as pltpu

F32 = jnp.float32
BF16 = jnp.bfloat16

D_MODEL = 1024
RW = 512
HEADS = 8
HD = 64
LORA_W = 64
LORA_A = 64
LORA_G = 128
GN_EPS = HD * 1e-5
KV_HEADS = 2
GROUP = HEADS // KV_HEADS
KV_W = KV_HEADS * HD
BLOCK = 128
ROPE_THETA = 500000.0
ROT = HD // 4
N_EXPERTS = 16
CAP_FACTOR = 2
NORM_EPS = 1e-6
RWKV_COLS = 3 * RW + 2 * LORA_W + 2 * LORA_A + LORA_G
Q0 = RWKV_COLS
K0 = Q0 + RW
V0 = K0 + KV_W
G0 = V0 + KV_W
IN_COLS = G0 + 2 * D_MODEL
CHUNK = 64
SUB = 16
SCAN_CHUNKS = 4
ATT_QB = 2
TM_MERGE = 512
BISECT_ITERS = 160
NEG = -1e30
VMEM_LIMIT = 56 * 1024 * 1024


def _dot(a, b):
    return jnp.dot(a, b, preferred_element_type=F32)


def _dot_nt(a, b):
    return lax.dot_general(a, b, (((1,), (1,)), ((), ())), preferred_element_type=F32)


def _dot_tn(a, b):
    return lax.dot_general(a, b, (((0,), (0,)), ((), ())), preferred_element_type=F32)


def _segsum(x, ones):
    xh = x.astype(BF16)
    xl = (x - xh.astype(F32)).astype(BF16)
    return _dot(xh, ones) + _dot(xl, ones)


def _params(sem):
    return pltpu.CompilerParams(dimension_semantics=sem, vmem_limit_bytes=VMEM_LIMIT)


def _mod_kernel(c_ref, w_ref, b_ref, o_ref):
    c = c_ref[...]
    ca = c * jax.nn.sigmoid(c)
    o_ref[...] = jnp.dot(ca, w_ref[...], preferred_element_type=F32,
                         precision=lax.Precision.HIGHEST) + b_ref[...]


def _mod_call(c, w_ada, b_ada):
    B, D = c.shape
    n = w_ada.shape[1] // D
    return pl.pallas_call(
        _mod_kernel,
        grid=(n,),
        in_specs=[pl.BlockSpec((B, D), lambda j: (0, 0)),
                  pl.BlockSpec((D, D), lambda j: (0, j)),
                  pl.BlockSpec((1, D), lambda j: (0, j))],
        out_specs=pl.BlockSpec((B, D), lambda j: (0, j)),
        out_shape=jax.ShapeDtypeStruct((B, n * D), F32),
        compiler_params=_params(("arbitrary",)),
        name="mod",
    )(c, w_ada, b_ada.reshape(1, -1))


def _rope(xn, cos, sin, width):
    lane = lax.broadcasted_iota(jnp.int32, xn.shape, 1) & (HD - 1)
    rot = jnp.where(lane < ROT // 2, pltpu.roll(xn, width - ROT // 2, 1), pltpu.roll(xn, ROT // 2, 1))
    return xn * cos + rot * sin


def _inproj_kernel(x_ref, mod_ref, g_ref, w_ref, wq_ref, cos_ref, sin_ref, cosq_ref, sinq_ref,
                   qg_ref, kg_ref, ones_ref, zr_ref, qt_ref, k_ref, v_ref, gl_ref):
    x = x_ref[0]
    tm = x.shape[0]
    ms = jnp.mean(x * x, axis=-1, keepdims=True)
    y = x * lax.rsqrt(ms + NORM_EPS) * g_ref[...]
    u = y * (1.0 + mod_ref[0, 1:2, :]) + mod_ref[0, 0:1, :]
    ub = u.astype(BF16)
    c_k, c_v, c_g = RWKV_COLS, RWKV_COLS + KV_W, RWKV_COLS + 2 * KV_W
    zr_ref[0] = _dot(ub, w_ref[:, 0:c_k])
    k = _dot(ub, w_ref[:, c_k:c_v])
    v_ref[0] = _dot(ub, w_ref[:, c_v:c_g]).astype(BF16)
    gl_ref[0] = _dot(ub, w_ref[:, c_g:]).astype(BF16)
    kn = k * lax.rsqrt(_segsum(k * k, ones_ref[...]) * (1.0 / HD) + NORM_EPS) * kg_ref[...]
    k_ref[0] = _rope(kn, cos_ref[0], sin_ref[0], KV_W).astype(BF16)
    qt = _dot_nt(wq_ref[...], ub)
    qg = jnp.concatenate([qg_ref[...]] * (tm // 128), axis=1)
    cq = cosq_ref[0]
    sq = sinq_ref[0]
    half = ROT // 2
    pieces = []
    for h in range(HEADS):
        xh = qt[h * HD:(h + 1) * HD]
        xn = (xh * lax.rsqrt(jnp.mean(xh * xh, axis=0, keepdims=True) + NORM_EPS)
              * qg[h * HD:(h + 1) * HD] * (HD ** -0.5))
        x1, x2 = xn[0:half], xn[half:ROT]
        pieces += [x1 * cq - x2 * sq, x2 * cq + x1 * sq, xn[ROT:]]
    qt_ref[0] = jnp.concatenate(pieces, axis=0).astype(BF16)


def _inproj_call(x, mod3, norm1_g, w_rest, wq_t, cos_t, sin_t, cos_q, sin_q, qg, kg, ones, tm):
    B, T, D = x.shape
    row = lambda b, i: (b, i, 0)
    col = lambda b, i: (b, 0, i)
    const2 = lambda b, i: (0, 0)
    half = ROT // 2
    return pl.pallas_call(
        _inproj_kernel,
        grid=(B, T // tm),
        in_specs=[pl.BlockSpec((1, tm, D), row),
                  pl.BlockSpec((1, 6, D), lambda b, i: (b, 0, 0)),
                  pl.BlockSpec((1, D), const2),
                  pl.BlockSpec(w_rest.shape, const2),
                  pl.BlockSpec((RW, D), const2),
                  pl.BlockSpec((1, tm, KV_W), row),
                  pl.BlockSpec((1, tm, KV_W), row),
                  pl.BlockSpec((1, half, tm), col),
                  pl.BlockSpec((1, half, tm), col),
                  pl.BlockSpec((RW, 128), const2),
                  pl.BlockSpec((1, KV_W), const2),
                  pl.BlockSpec((KV_W, KV_W), const2)],
        out_specs=[pl.BlockSpec((1, tm, RWKV_COLS), row),
                   pl.BlockSpec((1, RW, tm), col),
                   pl.BlockSpec((1, tm, KV_W), row),
                   pl.BlockSpec((1, tm, KV_W), row),
                   pl.BlockSpec((1, tm, 2 * D), row)],
        out_shape=[jax.ShapeDtypeStruct((B, T, RWKV_COLS), F32),
                   jax.ShapeDtypeStruct((B, RW, T), BF16),
                   jax.ShapeDtypeStruct((B, T, KV_W), BF16),
                   jax.ShapeDtypeStruct((B, T, KV_W), BF16),
                   jax.ShapeDtypeStruct((B, T, 2 * D), BF16)],
        compiler_params=_params(("parallel", "parallel")),
        name="inproj",
    )(x, mod3, norm1_g, w_rest, wq_t, cos_t, sin_t, cos_q, sin_q, qg, kg, ones)


def _prep_kernel(z_ref, hp_ref, hn_ref, mup_ref, mun_ref, w0_ref, w2_ref, a0_ref, a2_ref, g2_ref,
                 kk_ref, ka_ref, rk_ref, ones_ref,
                 r_o, v_o, kkn_o, g_o, bonus_o, lw0_o, lw1_o, k0_o, k1_o, b0_o, b1_o):
    i = pl.program_id(1)
    last = pl.num_programs(1) - 1
    z = z_ref[0]
    tm = z.shape[0]
    row = lax.broadcasted_iota(jnp.int32, z.shape, 0)
    prev_row = jnp.where(i == 0, 0.0, hp_ref[0, 7:8, :])
    next_row = jnp.where(i == last, 0.0, hn_ref[0, 0:1, :])
    zp = jnp.where(row == 0, prev_row, pltpu.roll(z, 1, 0))
    zn = jnp.where(row == tm - 1, next_row, pltpu.roll(z, tm - 1, 0))
    zs = z + mup_ref[...] * (zp - z) + mun_ref[...] * (zn - z)
    r = zs[:, 0:RW]
    k = zs[:, RW:2 * RW]
    v = zs[:, 2 * RW:3 * RW]
    c0 = 3 * RW
    wd = zs[:, c0:c0 + 2 * LORA_W]
    ad = zs[:, c0 + 2 * LORA_W:c0 + 2 * LORA_W + 2 * LORA_A]
    gd = zs[:, c0 + 2 * LORA_W + 2 * LORA_A:RWKV_COLS]
    wl = _dot(jnp.tanh(wd).astype(BF16), w2_ref[...]) + w0_ref[...]
    lw = -math.exp(-0.5) * jax.nn.sigmoid(wl)
    al = jax.nn.sigmoid(_dot(ad.astype(BF16), a2_ref[...]) + a0_ref[...])
    g = _dot(jax.nn.sigmoid(gd).astype(BF16), g2_ref[...])
    ones = ones_ref[...]
    kk = k * kk_ref[...]
    kkn = kk / jnp.maximum(jnp.sqrt(_segsum(kk * kk, ones)), 1e-12)
    ka = ka_ref[...]
    a_0 = al[:, 0:RW]
    a_1 = al[:, RW:2 * RW]
    k_0 = k * (1.0 + (a_0 - 1.0) * ka)
    k_1 = k * (1.0 + (a_1 - 1.0) * ka)
    bonus = _segsum(r * (k_0 + k_1) * rk_ref[...], ones) * v
    r_o[0] = r.astype(BF16)
    v_o[0] = v.astype(BF16)
    kkn_o[0] = kkn.astype(BF16)
    g_o[0] = g.astype(BF16)
    bonus_o[0] = bonus.astype(BF16)
    lw0_o[0] = lw[:, 0:RW]
    lw1_o[0] = lw[:, RW:2 * RW]
    k0_o[0] = k_0.astype(BF16)
    k1_o[0] = k_1.astype(BF16)
    b0_o[0] = (kkn * a_0).astype(BF16)
    b1_o[0] = (kkn * a_1).astype(BF16)


def _prep_call(zr, mup, mun, w0c, w2c, a0c, a2c, g2b, k_k, k_a, r_k, ones, tm):
    B, T, _ = zr.shape
    nh = tm // 8
    nt8 = T // 8
    row = lambda b, i: (b, i, 0)
    const2 = lambda b, i: (0, 0)
    out = pl.BlockSpec((1, tm, RW), row)
    return pl.pallas_call(
        _prep_kernel,
        grid=(B, T // tm),
        in_specs=[pl.BlockSpec((1, tm, RWKV_COLS), row),
                  pl.BlockSpec((1, 8, RWKV_COLS), lambda b, i: (b, jnp.maximum(i * nh - 1, 0), 0)),
                  pl.BlockSpec((1, 8, RWKV_COLS), lambda b, i: (b, jnp.minimum((i + 1) * nh, nt8 - 1), 0)),
                  pl.BlockSpec((1, RWKV_COLS), const2),
                  pl.BlockSpec((1, RWKV_COLS), const2),
                  pl.BlockSpec((1, 2 * RW), const2),
                  pl.BlockSpec((2 * LORA_W, 2 * RW), const2),
                  pl.BlockSpec((1, 2 * RW), const2),
                  pl.BlockSpec((2 * LORA_A, 2 * RW), const2),
                  pl.BlockSpec((LORA_G, RW), const2),
                  pl.BlockSpec((1, RW), const2),
                  pl.BlockSpec((1, RW), const2),
                  pl.BlockSpec((1, RW), const2),
                  pl.BlockSpec((RW, RW), const2)],
        out_specs=[out] * 11,
        out_shape=[jax.ShapeDtypeStruct((B, T, RW), F32 if i in (5, 6) else BF16) for i in range(11)],
        compiler_params=_params(("parallel", "parallel")),
        name="prep",
    )(zr, zr, zr, mup, mun, w0c, w2c, a0c, a2c, g2b, k_k, k_a, r_k, ones)


def _dir_operands(rev, nch, r, v, kk, lw, kd, bd):
    n = r.shape[0]
    ti = lax.broadcasted_iota(jnp.int32, (n, n), 0)
    si = lax.broadcasted_iota(jnp.int32, (n, n), 1)
    same = (ti // CHUNK) == (si // CHUNK)
    incl = ((si >= ti) if rev else (si <= ti)) & same
    mi = incl.astype(BF16)
    l1 = lw.astype(BF16)
    r1 = lw - l1.astype(F32)
    l2 = r1.astype(BF16)
    l3 = (r1 - l2.astype(F32)).astype(BF16)
    cum = _dot(mi, l1) + _dot(mi, l2) + _dot(mi, l3)
    wt = jnp.exp(cum)
    winv = jnp.exp(-cum)
    ops = dict(ah=-(kk * jnp.exp(cum - lw)), rt=r * wt, bt=bd * winv, kt=kd * winv, v=v)
    wcs, bhs, khs = [], [], []
    for j in range(nch):
        last = j * CHUNK if rev else (j + 1) * CHUNK - 1
        wc = wt[last:last + 1, :]
        rows = slice(j * CHUNK, (j + 1) * CHUNK)
        wcs.append(wc)
        bhs.append(ops["bt"][rows] * wc)
        khs.append(ops["kt"][rows] * wc)
    return ops, wcs, bhs, khs


def _scan_kernel(rf, vf, kkf, lwf, kf, bf, rb, vb, kkb, lwb, kb, bb, yf_ref, yb_ref, st_ref, *, nch):
    @pl.when(pl.program_id(1) == 0)
    def _():
        st_ref[...] = jnp.zeros_like(st_ref)

    c = CHUNK
    pw = 2 * HD
    npair = HEADS // 2
    ti = lax.broadcasted_iota(jnp.int32, (c, pw), 0)
    li = lax.broadcasted_iota(jnp.int32, (c, pw), 1)
    si = li & (HD - 1)
    head0 = li < HD
    blk = (ti // SUB) == (si // SUB)
    eye = ti == si
    two = lambda m: jnp.concatenate([m, m], axis=1)
    masks = {False: (two(si < ti), two(si <= ti)), True: (two(si > ti), two(si >= ti))}

    def bd(y):
        yb = y.astype(BF16)
        zero = jnp.zeros_like(yb)
        return jnp.concatenate([jnp.where(head0, yb, zero), jnp.where(head0, zero, yb)], axis=0)

    def bdcat(*ys):
        return jnp.concatenate([bd(y) for y in ys], axis=1)

    def diag_blocks(full):
        return jnp.where(head0, full[:c], full[c:])

    dirs = [(False, rf, vf, kkf, lwf, kf, bf), (True, rb, vb, kkb, lwb, kb, bb)]
    units = []
    for d, (rev, r_, v_, kk_, lw_, k_, b_) in enumerate(dirs):
        f32 = lambda ref: ref[0].astype(F32)
        ops, wcs, bhs, khs = _dir_operands(rev, nch, f32(r_), f32(v_), f32(kk_), lw_[0], f32(k_), f32(b_))
        for j in range(nch):
            rows = slice(j * c, (j + 1) * c)
            for p in range(npair):
                sl = slice(p * pw, (p + 1) * pw)
                u = dict(d=d, j=j, p=p, rev=rev, wc=wcs[j][:, sl],
                         bkh=jnp.concatenate([bhs[j][:, sl], khs[j][:, sl]], axis=0).astype(BF16))
                for name in ("ah", "rt", "bt", "kt", "v"):
                    u[name] = ops[name][rows, sl]
                units.append(u)

    for u in units:
        x1 = jnp.concatenate([u["ah"], u["rt"]], axis=0).astype(BF16)
        u["lbk"] = _dot_nt(x1, jnp.concatenate([bd(u["bt"]), bd(u["kt"])], axis=0))
    for u in units:
        strict, incl = masks[u["rev"]]
        top = jnp.where(strict, u["lbk"][:c], 0.0)
        bot = jnp.where(incl, u["lbk"][c:], 0.0)
        lab = top[:, :pw]
        u["mrb"] = bot[:, :pw].astype(BF16)
        u["ld"] = jnp.where(blk, lab, 0.0)
        u["lo"] = lab - u["ld"]
        u["lm"] = jnp.concatenate([top[:, pw:], bot[:, pw:]], axis=0).astype(BF16)
    for u in units:
        u["a2"] = _dot(u["ld"].astype(BF16), bd(u["ld"]))
        u["lkv"] = _dot(u["lm"], bd(u["v"]))
    for u in units:
        s = _dot(jnp.concatenate([u["a2"], u["ld"]], axis=0).astype(BF16), bd(u["a2"]))
        u["a4"] = s[:c]
        u["t"] = jnp.where(eye, 1.0, 0.0) + u["ld"] + u["a2"] + s[c:]
    for u in units:
        s = _dot(jnp.concatenate([u["a4"], u["t"]], axis=0).astype(BF16), bd(u["a4"]))
        u["a8"] = s[:c]
        u["t"] = u["t"] + s[c:]
    for u in units:
        u["td"] = u["t"] + _dot(u["t"].astype(BF16), bd(u["a8"]))
    for u in units:
        x = _dot(u["td"].astype(BF16), bdcat(u["lo"], u["ah"], u["lkv"][:c]))
        u["n"] = x[:, :pw]
        u["z"] = x[:, pw:]
    for u in units:
        s = _dot(u["n"].astype(BF16), bdcat(u["n"], u["z"][:, :pw], u["z"][:, pw:]))
        u["n2"] = s[:, :pw]
        u["z"] = u["z"] + s[:, pw:]
    for u in units:
        u["z"] = u["z"] + _dot(u["n2"].astype(BF16), bdcat(u["z"][:, :pw], u["z"][:, pw:]))
    for u in units:
        z = u["z"]
        y10 = (jnp.concatenate([u["rt"], u["lkv"][c:]], axis=1)
               + _dot(u["mrb"], bdcat(z[:, :pw], z[:, pw:])))
        vpad = jnp.concatenate([jnp.zeros_like(u["v"]), u["v"]], axis=1)
        full = _dot_tn(u["bkh"], jnp.concatenate([z, vpad], axis=0).astype(BF16))
        g = diag_blocks(full[:, :pw]) + jnp.where(eye, u["wc"], 0.0)
        u["gy"] = jnp.concatenate([g, y10[:, :pw]], axis=0).astype(BF16)
        u["h0"] = diag_blocks(full[:, pw:])
        u["y0"] = y10[:, pw:]

    ys = {}
    for d in range(2):
        order = range(nch - 1, -1, -1) if d == 1 else range(nch)
        zs = [st_ref[d, p] for p in range(npair)]
        for j in order:
            for p in range(npair):
                u = units[(d * nch + j) * npair + p]
                m = _dot(u["gy"], bd(zs[p]))
                zs[p] = m[:c] + u["h0"]
                ys[(d, j, p)] = m[c:] + u["y0"]
        for p in range(npair):
            st_ref[d, p] = zs[p]
    for d, y_ref in enumerate((yf_ref, yb_ref)):
        y_ref[0] = jnp.concatenate(
            [jnp.concatenate([ys[(d, j, p)] for p in range(npair)], axis=-1) for j in range(nch)], axis=0)


def _scan_call(r, v, kkn, lw0, lw1, k0, k1, b0, b1, nch):
    B, T, _ = r.shape
    rows = nch * CHUNK
    ns = T // rows
    fwd = pl.BlockSpec((1, rows, RW), lambda b, c: (b, c, 0))
    bwd = pl.BlockSpec((1, rows, RW), lambda b, c: (b, ns - 1 - c, 0))
    return pl.pallas_call(
        functools.partial(_scan_kernel, nch=nch),
        grid=(B, ns),
        in_specs=[fwd] * 6 + [bwd] * 6,
        out_specs=[fwd, bwd],
        out_shape=[jax.ShapeDtypeStruct((B, T, RW), F32)] * 2,
        scratch_shapes=[pltpu.VMEM((2, HEADS // 2, HD, 2 * HD), F32)],
        compiler_params=_params(("parallel", "arbitrary")),
        name="scan",
    )(r, v, kkn, lw0, k0, b0, r, v, kkn, lw1, k1, b1)


def _attn_kernel(sink_ref, qt_ref, k_ref, v_ref, o_ref, *, qb):
    n = pl.program_id(1)
    t = k_ref.shape[1]
    band = 3 * BLOCK
    cols = GROUP * BLOCK
    ki = lax.broadcasted_iota(jnp.int32, (band, BLOCK), 0)
    qi = lax.broadcasted_iota(jnp.int32, (band, BLOCK), 1)
    lane_head = lax.broadcasted_iota(jnp.int32, (1, cols), 1) // BLOCK
    zeros = jnp.zeros((HD, cols), BF16)
    units = []
    for j in range(qb):
        blk = n * qb + j
        start = pl.multiple_of(jnp.clip((blk - 1) * BLOCK, 0, t - band), BLOCK)
        bias = jnp.where(jnp.abs(start + ki - (blk * BLOCK + qi)) <= BLOCK, 0.0, NEG)
        bias = jnp.concatenate([bias] * GROUP, axis=1)
        kb = k_ref[0, pl.ds(start, band), :]
        vb = v_ref[0, pl.ds(start, band), :]
        for g in range(KV_HEADS):
            qg = jnp.concatenate([qt_ref[0, (g * GROUP + i) * HD:(g * GROUP + i + 1) * HD,
                                         j * BLOCK:(j + 1) * BLOCK] for i in range(GROUP)], axis=1)
            rhs = jnp.concatenate([qg if gg == g else zeros for gg in range(KV_HEADS)], axis=0)
            sk = jnp.zeros((1, cols), F32)
            for i in range(GROUP):
                sk = jnp.where(lane_head == i, sink_ref[g * GROUP + i], sk)
            units.append(dict(j=j, g=g, kb=kb, vb=vb, rhs=rhs, sk=sk, bias=bias))
    for u in units:
        u["s"] = _dot(u["kb"], u["rhs"]) + u["bias"]
    for u in units:
        u["m"] = jnp.maximum(jnp.max(u["s"], axis=0, keepdims=True), u["sk"])
    for u in units:
        p = jnp.exp(u["s"] - u["m"])
        u["r"] = 1.0 / (jnp.sum(p, axis=0, keepdims=True) + jnp.exp(u["sk"] - u["m"]))
        u["p"] = p.astype(BF16)
    for u in units:
        g = u["g"]
        u["o"] = _dot_tn(u["vb"], u["p"])[g * HD:(g + 1) * HD] * u["r"]
    for u in units:
        j, g = u["j"], u["g"]
        for i in range(GROUP):
            o_ref[0, (g * GROUP + i) * HD:(g * GROUP + i + 1) * HD, j * BLOCK:(j + 1) * BLOCK] = (
                u["o"][:, i * BLOCK:(i + 1) * BLOCK].astype(BF16))


def _attn_call(qt, k, v, sink, qb):
    B, _, T = qt.shape
    assert T >= 3 * BLOCK and T % (qb * BLOCK) == 0
    qspec = pl.BlockSpec((1, RW, qb * BLOCK), lambda b, n: (b, 0, n))
    kvspec = pl.BlockSpec((1, T, KV_W), lambda b, n: (b, 0, 0))
    return pl.pallas_call(
        functools.partial(_attn_kernel, qb=qb),
        grid=(B, T // (qb * BLOCK)),
        in_specs=[pl.BlockSpec(memory_space=pltpu.SMEM), qspec, kvspec, kvspec],
        out_specs=qspec,
        out_shape=jax.ShapeDtypeStruct((B, RW, T), BF16),
        compiler_params=_params(("parallel", "parallel")),
        name="attn",
    )(sink, qt, k, v)


def _merge_kernel(x_ref, mod_ref, yf_ref, yb_ref, bonus_ref, g_ref, yatt_ref, gl_ref,
                  gnw_ref, gnb_ref, ones_ref, pr_ref, pa_ref, wo_ref, n2g_ref, wr_ref,
                  x1_ref, u2_ref, aff_ref):
    ones = ones_ref[...]
    y = yf_ref[0] + yb_ref[0]
    mu = _segsum(y, ones) * (1.0 / HD)
    yc = y - mu
    var = _segsum(yc * yc, ones) * (1.0 / HD)
    yn = yc * lax.rsqrt(var + GN_EPS) * gnw_ref[...] + gnb_ref[...]
    ya = ((yn + bonus_ref[0].astype(F32)) * g_ref[0].astype(F32)).astype(BF16)
    pa = _dot(ya, pr_ref[...])
    pb = _dot_tn(yatt_ref[0], pa_ref[...])
    gates = jax.nn.sigmoid(gl_ref[0].astype(F32))
    m = gates[:, :D_MODEL] * pa + gates[:, D_MODEL:] * pb
    x1 = x_ref[0] + mod_ref[0, 2:3, :] * _dot(m.astype(BF16), wo_ref[...])
    x1_ref[0] = x1
    ms = jnp.mean(x1 * x1, axis=-1, keepdims=True)
    u2 = x1 * lax.rsqrt(ms + NORM_EPS) * n2g_ref[...] * (1.0 + mod_ref[0, 4:5, :]) + mod_ref[0, 3:4, :]
    u2h = u2.astype(BF16)
    u2_ref[0] = u2h
    u2l = (u2 - u2h.astype(F32)).astype(BF16)
    wr = wr_ref[...]
    wrh = wr.astype(BF16)
    wrl = (wr - wrh.astype(F32)).astype(BF16)
    t1 = _dot_nt(jnp.concatenate([wrh, wrl], axis=0), u2h)
    logits = t1[:N_EXPERTS] + t1[N_EXPERTS:] + _dot_nt(wrh, u2l)
    e = jnp.exp(logits - jnp.max(logits, axis=0, keepdims=True))
    aff_ref[0] = e / jnp.sum(e, axis=0, keepdims=True)


def _merge_call(x, mod3, yf, yb, bonus, g, yatt, gl, gnw, gnb, ones, p_r, p_a, w_o, n2g, w_rt, tm):
    B, T, D = x.shape
    row = lambda b, i: (b, i, 0)
    const2 = lambda b, i: (0, 0)
    rw = pl.BlockSpec((1, tm, RW), row)
    return pl.pallas_call(
        _merge_kernel,
        grid=(B, T // tm),
        in_specs=[pl.BlockSpec((1, tm, D), row),
                  pl.BlockSpec((1, 6, D), lambda b, i: (b, 0, 0)),
                  rw, rw, rw, rw,
                  pl.BlockSpec((1, RW, tm), lambda b, i: (b, 0, i)),
                  pl.BlockSpec((1, tm, 2 * D), row),
                  pl.BlockSpec((1, RW), const2),
                  pl.BlockSpec((1, RW), const2),
                  pl.BlockSpec((RW, RW), const2),
                  pl.BlockSpec((RW, D), const2),
                  pl.BlockSpec((RW, D), const2),
                  pl.BlockSpec((D, D), const2),
                  pl.BlockSpec((1, D), const2),
                  pl.BlockSpec((N_EXPERTS, D), const2)],
        out_specs=[pl.BlockSpec((1, tm, D), row),
                   pl.BlockSpec((1, tm, D), row),
                   pl.BlockSpec((1, N_EXPERTS, tm), lambda b, i: (b, 0, i))],
        out_shape=[jax.ShapeDtypeStruct((B, T, D), F32),
                   jax.ShapeDtypeStruct((B, T, D), BF16),
                   jax.ShapeDtypeStruct((B, N_EXPERTS, T), F32)],
        compiler_params=_params(("parallel", "parallel")),
        name="merge",
    )(x, mod3, yf, yb, bonus, g, yatt, gl, gnw, gnb, ones, p_r, p_a, w_o, n2g, w_rt)


def _excl_prefix(mask, tri):
    e, t = mask.shape
    mb = mask.astype(BF16)
    carry = jnp.zeros((e, 1), F32)
    outs = []
    for j in range(t // 128):
        tile = mb[:, j * 128:(j + 1) * 128]
        outs.append(_dot(tile, tri) + carry)
        carry = carry + jnp.sum(tile.astype(F32), axis=1, keepdims=True)
    return jnp.concatenate(outs, axis=-1)


def _route_kernel(aff_ref, tri_ref, slot_ref, *, cap):
    nb, ne, t = aff_ref.shape
    a = aff_ref[...].reshape(nb * ne, t)
    e = a.shape[0]
    capf = jnp.float32(cap)

    def cond(s):
        it, lo, hi = s
        mid = 0.5 * (lo + hi)
        still_open = jnp.max(jnp.where((mid > lo) & (mid < hi), 1.0, 0.0))
        return (it < BISECT_ITERS) & (still_open > 0.0)

    def body(s):
        it, lo, hi = s
        mid = 0.5 * (lo + hi)
        ge = jnp.sum((a >= mid).astype(F32), axis=1, keepdims=True) >= capf
        return it + 1, jnp.where(ge, mid, lo), jnp.where(ge, hi, mid)

    lo0 = jnp.zeros((e, 1), F32)
    hi0 = jnp.full((e, 1), 2.0, F32)
    _, lo, hi = lax.while_loop(cond, body, (jnp.int32(0), lo0, hi0))
    gt = a >= hi
    eq = (a >= lo) & (a < hi)
    need = capf - jnp.sum(gt.astype(F32), axis=1, keepdims=True)
    tri = tri_ref[...]
    sel = gt | (eq & (_excl_prefix(eq, tri) < need))
    pos = _excl_prefix(sel, tri)
    slot_ref[...] = jnp.where(sel, pos.astype(jnp.int32), -1).reshape(nb, ne, t)


def _route_call(aff, tri, cap):
    B, E, T = aff.shape
    return pl.pallas_call(
        functools.partial(_route_kernel, cap=cap),
        grid=(1,),
        in_specs=[pl.BlockSpec((B, E, T), lambda i: (0, 0, 0)),
                  pl.BlockSpec((128, 128), lambda i: (0, 0))],
        out_specs=pl.BlockSpec((B, E, T), lambda i: (0, 0, 0)),
        out_shape=jax.ShapeDtypeStruct((B, E, T), jnp.int32),
        compiler_params=_params(("arbitrary",)),
        name="route",
    )(aff, tri)


def _moe_kernel(u2_ref, slot_ref, aff_ref, wg_ref, wu_ref, wd_ref, o_ref, acc_ref, *, cap, tt):
    e = pl.program_id(1)

    @pl.when(e == 0)
    def _():
        acc_ref[...] = jnp.zeros_like(acc_ref)

    t = u2_ref.shape[1]
    srow = slot_ref[0, pl.ds(e, 1), :]
    arow = aff_ref[0, pl.ds(e, 1), :]
    hit = lax.broadcasted_iota(jnp.int32, (cap, t), 0) == srow
    xe = _dot(hit.astype(BF16), u2_ref[0]).astype(BF16)
    hg = _dot(xe, wg_ref[0])
    hu = _dot(xe, wu_ref[0])
    h = (hg * jax.nn.sigmoid(hg) * hu).astype(BF16)
    ye = _dot(h, wd_ref[0]).astype(BF16)
    wsc = jnp.where(hit, arow, 0.0).astype(BF16)
    for j in range(t // tt):
        acc_ref[j * tt:(j + 1) * tt, :] += _dot_tn(wsc[:, j * tt:(j + 1) * tt], ye)

    @pl.when(e == pl.num_programs(1) - 1)
    def _():
        o_ref[0] = acc_ref[...].astype(BF16)


def _moe_call(u2, slot, aff, wg, wu, wd, cap):
    B, T, D = u2.shape
    E = wg.shape[0]
    F = wg.shape[2]
    tt = min(T, 512)
    return pl.pallas_call(
        functools.partial(_moe_kernel, cap=cap, tt=tt),
        grid=(B, E),
        in_specs=[pl.BlockSpec((1, T, D), lambda b, e: (b, 0, 0)),
                  pl.BlockSpec((1, E, T), lambda b, e: (b, 0, 0)),
                  pl.BlockSpec((1, E, T), lambda b, e: (b, 0, 0)),
                  pl.BlockSpec((1, D, F), lambda b, e: (e, 0, 0)),
                  pl.BlockSpec((1, D, F), lambda b, e: (e, 0, 0)),
                  pl.BlockSpec((1, F, D), lambda b, e: (e, 0, 0))],
        out_specs=pl.BlockSpec((1, T, D), lambda b, e: (b, 0, 0)),
        out_shape=jax.ShapeDtypeStruct((B, T, D), BF16),
        scratch_shapes=[pltpu.VMEM((T, D), F32)],
        compiler_params=_params(("parallel", "arbitrary")),
        name="moe",
    )(u2, slot, aff, wg, wu, wd)


def _final_kernel(x1_ref, mod_ref, moe_ref, o_ref):
    o_ref[0] = x1_ref[0] + mod_ref[0, 5:6, :] * moe_ref[0].astype(F32)


def _final_call(x1, mod3, moe, tm):
    B, T, D = x1.shape
    row = lambda b, i: (b, i, 0)
    return pl.pallas_call(
        _final_kernel,
        grid=(B, T // tm),
        in_specs=[pl.BlockSpec((1, tm, D), row),
                  pl.BlockSpec((1, 6, D), lambda b, i: (b, 0, 0)),
                  pl.BlockSpec((1, tm, D), row)],
        out_specs=pl.BlockSpec((1, tm, D), row),
        out_shape=jax.ShapeDtypeStruct((B, T, D), F32),
        compiler_params=_params(("parallel", "parallel")),
        name="final",
    )(x1, mod3, moe)


def _blockdiag2(w):
    z = jnp.zeros_like(w[0])
    return jnp.concatenate([jnp.concatenate([w[0], z], axis=1), jnp.concatenate([z, w[1]], axis=1)], axis=0)


def _layer(x, mod3, cos_t, sin_t, cos_q, sin_q, ones, tri, norm1_g, w_in, mu_prev, mu_next, rwkv_w0, rwkv_w2, rwkv_a0,
           rwkv_a2, rwkv_g2, rwkv_k_k, rwkv_k_a, rwkv_r_k, rwkv_gn_w, rwkv_gn_b, q_norm_g, k_norm_g,
           attn_sink, p_rwkv, p_attn, w_out, norm2_g, w_router, w_gate, w_up, w_down):
    B, T, D = x.shape
    tm = min(T, 256)
    cap = CAP_FACTOR * T // N_EXPERTS
    row = lambda a: a.reshape(1, -1)
    w_b = w_in.astype(BF16)
    w_rest = jnp.concatenate([w_b[:, :Q0], w_b[:, K0:V0], w_b[:, V0:G0], w_b[:, G0:]], axis=1)
    qg = jnp.broadcast_to(jnp.tile(q_norm_g, HEADS)[:, None], (RW, 128))
    zr, qt, k, v, gl = _inproj_call(x, mod3, row(norm1_g), w_rest, w_b[:, Q0:K0].T, cos_t, sin_t, cos_q, sin_q,
                                    qg, row(jnp.tile(k_norm_g, KV_HEADS)), ones[:KV_W, :KV_W], tm)
    r, vv, kkn, g, bonus, lw0, lw1, k0, k1, b0, b1 = _prep_call(
        zr, row(mu_prev), row(mu_next), row(rwkv_w0), _blockdiag2(rwkv_w2).astype(BF16),
        row(rwkv_a0), _blockdiag2(rwkv_a2).astype(BF16), rwkv_g2.astype(BF16),
        row(rwkv_k_k), row(rwkv_k_a), row(rwkv_r_k), ones, tm)
    yf, yb = _scan_call(r, vv, kkn, lw0, lw1, k0, k1, b0, b1, SCAN_CHUNKS)
    yatt = _attn_call(qt, k, v, attn_sink, ATT_QB)
    x1, u2, aff = _merge_call(x, mod3, yf, yb, bonus, g, yatt, gl, row(rwkv_gn_w), row(rwkv_gn_b), ones,
                              p_rwkv.astype(BF16), p_attn.astype(BF16), w_out.astype(BF16),
                              row(norm2_g), w_router.T, TM_MERGE if T % TM_MERGE == 0 else tm)
    slot = _route_call(aff, tri, cap)
    moe = _moe_call(u2, slot, aff, w_gate.astype(BF16), w_up.astype(BF16), w_down.astype(BF16), cap)
    return _final_call(x1, mod3, moe, tm)


def kernel(x, c, positions, w_ada, b_ada, norm1_g, w_in, mu_prev, mu_next, rwkv_w0, rwkv_w2, rwkv_a0, rwkv_a2, rwkv_g2, rwkv_k_k, rwkv_k_a, rwkv_r_k, rwkv_gn_w, rwkv_gn_b, q_norm_g, k_norm_g, attn_sink, p_rwkv, p_attn, w_out, norm2_g, w_router, w_gate, w_up, w_down):
    B, T, D = x.shape
    depth = w_ada.shape[0]
    half = ROT // 2
    inv_freq = ROPE_THETA ** (-jnp.arange(0, ROT, 2, dtype=F32) / ROT)
    ang = positions.astype(F32)[..., None] * inv_freq
    cos8, sin8 = jnp.cos(ang), jnp.sin(ang)
    pad1 = jnp.ones((B, T, HD - ROT), F32)
    pad0 = jnp.zeros((B, T, HD - ROT), F32)
    cos_t = jnp.tile(jnp.concatenate([cos8, cos8, pad1], axis=-1), (1, 1, KV_HEADS))
    sin_t = jnp.tile(jnp.concatenate([-sin8, sin8, pad0], axis=-1), (1, 1, KV_HEADS))
    seg = jnp.arange(RW) // HD
    ones = (seg[:, None] == seg[None, :]).astype(BF16)
    idx = jnp.arange(128)
    tri = (idx[:, None] < idx[None, :]).astype(BF16)
    cos_q, sin_q = jnp.swapaxes(cos8, 1, 2), jnp.swapaxes(sin8, 1, 2)
    for l in range(depth):
        mod3 = _mod_call(c, w_ada[l], b_ada[l]).reshape(B, 6, D)
        x = _layer(x, mod3, cos_t, sin_t, cos_q, sin_q, ones, tri, norm1_g[l], w_in[l], mu_prev[l], mu_next[l],
                   rwkv_w0[l], rwkv_w2[l], rwkv_a0[l], rwkv_a2[l], rwkv_g2[l], rwkv_k_k[l], rwkv_k_a[l],
                   rwkv_r_k[l], rwkv_gn_w[l], rwkv_gn_b[l], q_norm_g[l], k_norm_g[l], attn_sink[l],
                   p_rwkv[l], p_attn[l], w_out[l], norm2_g[l], w_router[l], w_gate[l], w_up[l], w_down[l])
    return x
```

```python
import functools
import math

import jax
import jax.numpy as jnp
from jax import lax
from jax.experimental import pallas as pl
from jax.experimental.pallas import tpu as pltpu

F32 = jnp.float32
BF16 = jnp.bfloat16

D_MODEL = 1024
RW = 512
HEADS = 8
HD = 64
LORA_W = 64
LORA_A = 64
LORA_G = 128
GN_EPS = HD * 1e-5
KV_HEADS = 2
GROUP = HEADS // KV_HEADS
KV_W = KV_HEADS * HD
BLOCK = 128
ROPE_THETA = 500000.0
ROT = HD // 4
N_EXPERTS = 16
CAP_FACTOR = 2
NORM_EPS = 1e-6
RWKV_COLS = 3 * RW + 2 * LORA_W + 2 * LORA_A + LORA_G
Q0 = RWKV_COLS
K0 = Q0 + RW
V0 = K0 + KV_W
G0 = V0 + KV_W
IN_COLS = G0 + 2 * D_MODEL
CHUNK = 64
SUB = 16
SCAN_CHUNKS = 4
ATT_QB = 4
TM_MERGE = 512
BISECT_ITERS = 160
NEG = -1e30
LOG2E = math.log2(math.e)
QK_SCALE = HD ** -0.5 * LOG2E
VMEM_LIMIT = 56 * 1024 * 1024


def _dot(a, b):
    return jnp.dot(a, b, preferred_element_type=F32)


def _dot_nt(a, b):
    return lax.dot_general(a, b, (((1,), (1,)), ((), ())), preferred_element_type=F32)


def _dot_tn(a, b):
    return lax.dot_general(a, b, (((0,), (0,)), ((), ())), preferred_element_type=F32)


def _segsum(x, ones):
    xh = x.astype(BF16)
    xl = (x - xh.astype(F32)).astype(BF16)
    return _dot(xh, ones) + _dot(xl, ones)


def _params(sem):
    return pltpu.CompilerParams(dimension_semantics=sem, vmem_limit_bytes=VMEM_LIMIT)


def _mod_kernel(c_ref, w_ref, b_ref, o_ref):
    c = c_ref[...]
    ca = c * jax.nn.sigmoid(c)
    o_ref[...] = jnp.dot(ca, w_ref[...], preferred_element_type=F32,
                         precision=lax.Precision.HIGHEST) + b_ref[...]


def _mod_call(c, w_ada, b_ada):
    B, D = c.shape
    n = w_ada.shape[1] // D
    return pl.pallas_call(
        _mod_kernel,
        grid=(n,),
        in_specs=[pl.BlockSpec((B, D), lambda j: (0, 0)),
                  pl.BlockSpec((D, D), lambda j: (0, j)),
                  pl.BlockSpec((1, D), lambda j: (0, j))],
        out_specs=pl.BlockSpec((B, D), lambda j: (0, j)),
        out_shape=jax.ShapeDtypeStruct((B, n * D), F32),
        compiler_params=_params(("arbitrary",)),
        name="mod",
    )(c, w_ada, b_ada.reshape(1, -1))


def _rope(xn, cos, sin, width):
    lane = lax.broadcasted_iota(jnp.int32, xn.shape, 1) & (HD - 1)
    rot = jnp.where(lane < ROT // 2, pltpu.roll(xn, width - ROT // 2, 1), pltpu.roll(xn, ROT // 2, 1))
    return xn * cos + rot * sin


HALO = 8


def _inproj_kernel(x_ref, xp_ref, xn_ref, mod_ref, g_ref, w_ref, wq_ref, cos_ref, sin_ref, cosq_ref, sinq_ref,
                   qg_ref, kg_ref, ones8_ref,
                   mup_ref, mun_ref, w0_ref, w2_ref, a0_ref, a2_ref, g2_ref, kk_ref, ka_ref, rk_ref,
                   qt_ref, k_ref, v_ref, gl_ref,
                   r_o, v_o, kkn_o, g_o, bonus_o, lw0_o, lw1_o, k0_o, k1_o, b0_o, b1_o):
    i = pl.program_id(1)
    last = pl.num_programs(1) - 1
    tm = x_ref.shape[1]
    xx = jnp.concatenate([xp_ref[0], x_ref[0], xn_ref[0]], axis=0)
    ms = jnp.mean(xx * xx, axis=-1, keepdims=True)
    y = xx * lax.rsqrt(ms + NORM_EPS) * g_ref[...]
    u_all = y * (1.0 + mod_ref[0, 1:2, :]) + mod_ref[0, 0:1, :]
    ub = u_all[HALO:HALO + tm].astype(BF16)
    c_k, c_v, c_g = RWKV_COLS, RWKV_COLS + KV_W, RWKV_COLS + 2 * KV_W
    z_all = _dot(u_all.astype(BF16), w_ref[:, 0:c_k])
    k = _dot(ub, w_ref[:, c_k:c_v])
    v_ref[0] = _dot(ub, w_ref[:, c_v:c_g]).astype(BF16)
    gl_ref[0] = _dot(ub, w_ref[:, c_g:]).astype(BF16)
    ones8 = ones8_ref[...]
    kn = k * lax.rsqrt(_segsum(k * k, ones8[:KV_W, :KV_W]) * (1.0 / HD) + NORM_EPS) * kg_ref[...]
    k_ref[0] = _rope(kn, cos_ref[0], sin_ref[0], KV_W).astype(BF16)

    z = z_all[HALO:HALO + tm]
    row = lax.broadcasted_iota(jnp.int32, z.shape, 0)
    zp = jnp.where((row == 0) & (i == 0), 0.0, z_all[HALO - 1:HALO - 1 + tm])
    zn = jnp.where((row == tm - 1) & (i == last), 0.0, z_all[HALO + 1:HALO + 1 + tm])
    zs = z + mup_ref[...] * (zp - z) + mun_ref[...] * (zn - z)
    r = zs[:, 0:RW]
    kr = zs[:, RW:2 * RW]
    vr = zs[:, 2 * RW:3 * RW]
    c0 = 3 * RW
    wd = zs[:, c0:c0 + 2 * LORA_W]
    ad = zs[:, c0 + 2 * LORA_W:c0 + 2 * LORA_W + 2 * LORA_A]
    gd = zs[:, c0 + 2 * LORA_W + 2 * LORA_A:RWKV_COLS]
    wl = _dot(jnp.tanh(wd).astype(BF16), w2_ref[...]) + w0_ref[...]
    lw = -math.exp(-0.5) * jax.nn.sigmoid(wl)
    al = jax.nn.sigmoid(_dot(ad.astype(BF16), a2_ref[...]) + a0_ref[...])
    g = _dot(jax.nn.sigmoid(gd).astype(BF16), g2_ref[...])
    kk = kr * kk_ref[...]
    kkn = kk / jnp.maximum(jnp.sqrt(_segsum(kk * kk, ones8)), 1e-12)
    ka = ka_ref[...]
    a_0 = al[:, 0:RW]
    a_1 = al[:, RW:2 * RW]
    k_0 = kr * (1.0 + (a_0 - 1.0) * ka)
    k_1 = kr * (1.0 + (a_1 - 1.0) * ka)
    bonus = _segsum(r * (k_0 + k_1) * rk_ref[...], ones8) * vr
    r_o[0] = r.astype(BF16)
    v_o[0] = vr.astype(BF16)
    kkn_o[0] = kkn.astype(BF16)
    g_o[0] = g.astype(BF16)
    bonus_o[0] = bonus.astype(BF16)
    lw0_o[0] = lw[:, 0:RW]
    lw1_o[0] = lw[:, RW:2 * RW]
    k0_o[0] = k_0.astype(BF16)
    k1_o[0] = k_1.astype(BF16)
    b0_o[0] = (kkn * a_0).astype(BF16)
    b1_o[0] = (kkn * a_1).astype(BF16)

    qt = _dot_nt(wq_ref[...], ub)
    qg = jnp.concatenate([qg_ref[...]] * (tm // 128), axis=1)
    cq = cosq_ref[0]
    sq = sinq_ref[0]
    half = ROT // 2
    pieces = []
    for h in range(HEADS):
        xh = qt[h * HD:(h + 1) * HD]
        xn = (xh * lax.rsqrt(jnp.mean(xh * xh, axis=0, keepdims=True) + NORM_EPS)
              * qg[h * HD:(h + 1) * HD] * QK_SCALE)
        x1, x2 = xn[0:half], xn[half:ROT]
        pieces += [x1 * cq - x2 * sq, x2 * cq + x1 * sq, xn[ROT:]]
    qt_ref[0] = jnp.concatenate(pieces, axis=0).astype(BF16)


def _inproj_call(x, mod3, norm1_g, w_rest, wq_t, cos_t, sin_t, cos_q, sin_q, qg, kg, ones,
                 mup, mun, w0c, w2c, a0c, a2c, g2b, k_k, k_a, r_k, tm):
    B, T, D = x.shape
    nh = tm // HALO
    nth = T // HALO
    row = lambda b, i: (b, i, 0)
    col = lambda b, i: (b, 0, i)
    const2 = lambda b, i: (0, 0)
    half = ROT // 2
    rw = pl.BlockSpec((1, tm, RW), row)
    vec = lambda n: pl.BlockSpec((1, n), const2)
    return pl.pallas_call(
        _inproj_kernel,
        grid=(B, T // tm),
        in_specs=[pl.BlockSpec((1, tm, D), row),
                  pl.BlockSpec((1, HALO, D), lambda b, i: (b, jnp.maximum(i * nh - 1, 0), 0)),
                  pl.BlockSpec((1, HALO, D), lambda b, i: (b, jnp.minimum((i + 1) * nh, nth - 1), 0)),
                  pl.BlockSpec((1, 6, D), lambda b, i: (b, 0, 0)),
                  vec(D),
                  pl.BlockSpec(w_rest.shape, const2),
                  pl.BlockSpec((RW, D), const2),
                  pl.BlockSpec((1, tm, KV_W), row),
                  pl.BlockSpec((1, tm, KV_W), row),
                  pl.BlockSpec((1, half, tm), col),
                  pl.BlockSpec((1, half, tm), col),
                  pl.BlockSpec((RW, 128), const2),
                  vec(KV_W),
                  pl.BlockSpec((RW, RW), const2),
                  vec(RWKV_COLS), vec(RWKV_COLS),
                  vec(2 * RW), pl.BlockSpec((2 * LORA_W, 2 * RW), const2),
                  vec(2 * RW), pl.BlockSpec((2 * LORA_A, 2 * RW), const2),
                  pl.BlockSpec((LORA_G, RW), const2),
                  vec(RW), vec(RW), vec(RW)],
        out_specs=[pl.BlockSpec((1, RW, tm), col),
                   pl.BlockSpec((1, tm, KV_W), row),
                   pl.BlockSpec((1, tm, KV_W), row),
                   pl.BlockSpec((1, tm, 2 * D), row)] + [rw] * 11,
        out_shape=[jax.ShapeDtypeStruct((B, RW, T), BF16),
                   jax.ShapeDtypeStruct((B, T, KV_W), BF16),
                   jax.ShapeDtypeStruct((B, T, KV_W), BF16),
                   jax.ShapeDtypeStruct((B, T, 2 * D), BF16)]
        + [jax.ShapeDtypeStruct((B, T, RW), F32 if i in (5, 6) else BF16) for i in range(11)],
        compiler_params=_params(("parallel", "parallel")),
        name="inproj",
    )(x, x, x, mod3, norm1_g, w_rest, wq_t, cos_t, sin_t, cos_q, sin_q, qg, kg, ones,
      mup, mun, w0c, w2c, a0c, a2c, g2b, k_k, k_a, r_k)


def _dir_operands(rev, nch, r, v, kk, lw, kd, bd):
    n = r.shape[0]
    ti = lax.broadcasted_iota(jnp.int32, (n, n), 0)
    si = lax.broadcasted_iota(jnp.int32, (n, n), 1)
    same = (ti // CHUNK) == (si // CHUNK)
    incl = ((si >= ti) if rev else (si <= ti)) & same
    mi = incl.astype(BF16)
    l1 = lw.astype(BF16)
    r1 = lw - l1.astype(F32)
    l2 = r1.astype(BF16)
    l3 = (r1 - l2.astype(F32)).astype(BF16)
    cum = _dot(mi, l1) + _dot(mi, l2) + _dot(mi, l3)
    wt = jnp.exp(cum)
    winv = jnp.exp(-cum)
    ops = dict(ah=-(kk * jnp.exp(cum - lw)), rt=r * wt, bt=bd * winv, kt=kd * winv, v=v)
    wcs, bhs, khs = [], [], []
    for j in range(nch):
        last = j * CHUNK if rev else (j + 1) * CHUNK - 1
        wc = wt[last:last + 1, :]
        rows = slice(j * CHUNK, (j + 1) * CHUNK)
        wcs.append(wc)
        bhs.append(ops["bt"][rows] * wc)
        khs.append(ops["kt"][rows] * wc)
    return ops, wcs, bhs, khs


def _scan_kernel(rf, vf, kkf, lwf, kf, bf, rb, vb, kkb, lwb, kb, bb, yf_ref, yb_ref, st_ref, *, nch):
    @pl.when(pl.program_id(1) == 0)
    def _():
        st_ref[...] = jnp.zeros_like(st_ref)

    c = CHUNK
    pw = 2 * HD
    npair = HEADS // 2
    ti = lax.broadcasted_iota(jnp.int32, (c, pw), 0)
    li = lax.broadcasted_iota(jnp.int32, (c, pw), 1)
    si = li & (HD - 1)
    head0 = li < HD
    blk = (ti // SUB) == (si // SUB)
    eye = ti == si
    two = lambda m: jnp.concatenate([m, m], axis=1)
    masks = {False: (two(si < ti), two(si <= ti)), True: (two(si > ti), two(si >= ti))}

    def bd(y):
        yb = y.astype(BF16)
        zero = jnp.zeros_like(yb)
        return jnp.concatenate([jnp.where(head0, yb, zero), jnp.where(head0, zero, yb)], axis=0)

    def bdcat(*ys):
        return jnp.concatenate([bd(y) for y in ys], axis=1)

    def diag_blocks(full):
        return jnp.where(head0, full[:c], full[c:])

    dirs = [(False, rf, vf, kkf, lwf, kf, bf), (True, rb, vb, kkb, lwb, kb, bb)]
    units = []
    for d, (rev, r_, v_, kk_, lw_, k_, b_) in enumerate(dirs):
        f32 = lambda ref: ref[0].astype(F32)
        ops, wcs, bhs, khs = _dir_operands(rev, nch, f32(r_), f32(v_), f32(kk_), lw_[0], f32(k_), f32(b_))
        for j in range(nch):
            rows = slice(j * c, (j + 1) * c)
            for p in range(npair):
                sl = slice(p * pw, (p + 1) * pw)
                u = dict(d=d, j=j, p=p, rev=rev, wc=wcs[j][:, sl],
                         bkh=jnp.concatenate([bhs[j][:, sl], khs[j][:, sl]], axis=0).astype(BF16))
                for name in ("ah", "rt", "bt", "kt", "v"):
                    u[name] = ops[name][rows, sl]
                units.append(u)

    for u in units:
        x1 = jnp.concatenate([u["ah"], u["rt"]], axis=0).astype(BF16)
        u["lbk"] = _dot_nt(x1, jnp.concatenate([bd(u["bt"]), bd(u["kt"])], axis=0))
    for u in units:
        strict, incl = masks[u["rev"]]
        top = jnp.where(strict, u["lbk"][:c], 0.0)
        bot = jnp.where(incl, u["lbk"][c:], 0.0)
        lab = top[:, :pw]
        u["mrb"] = bot[:, :pw].astype(BF16)
        u["ld"] = jnp.where(blk, lab, 0.0)
        u["lo"] = lab - u["ld"]
        u["lm"] = jnp.concatenate([top[:, pw:], bot[:, pw:]], axis=0).astype(BF16)
    for u in units:
        u["a2"] = _dot(u["ld"].astype(BF16), bd(u["ld"]))
        u["lkv"] = _dot(u["lm"], bd(u["v"]))
    for u in units:
        s = _dot(jnp.concatenate([u["a2"], u["ld"]], axis=0).astype(BF16), bd(u["a2"]))
        u["a4"] = s[:c]
        u["t"] = jnp.where(eye, 1.0, 0.0) + u["ld"] + u["a2"] + s[c:]
    for u in units:
        s = _dot(jnp.concatenate([u["a4"], u["t"]], axis=0).astype(BF16), bd(u["a4"]))
        u["a8"] = s[:c]
        u["t"] = u["t"] + s[c:]
    for u in units:
        u["td"] = u["t"] + _dot(u["t"].astype(BF16), bd(u["a8"]))
    for u in units:
        x = _dot(u["td"].astype(BF16), bdcat(u["lo"], u["ah"], u["lkv"][:c]))
        u["n"] = x[:, :pw]
        u["z"] = x[:, pw:]
    for u in units:
        s = _dot(u["n"].astype(BF16), bdcat(u["n"], u["z"][:, :pw], u["z"][:, pw:]))
        u["n2"] = s[:, :pw]
        u["z"] = u["z"] + s[:, pw:]
    for u in units:
        u["z"] = u["z"] + _dot(u["n2"].astype(BF16), bdcat(u["z"][:, :pw], u["z"][:, pw:]))
    for u in units:
        z = u["z"]
        y10 = (jnp.concatenate([u["rt"], u["lkv"][c:]], axis=1)
               + _dot(u["mrb"], bdcat(z[:, :pw], z[:, pw:])))
        vpad = jnp.concatenate([jnp.zeros_like(u["v"]), u["v"]], axis=1)
        full = _dot_tn(u["bkh"], jnp.concatenate([z, vpad], axis=0).astype(BF16))
        g = diag_blocks(full[:, :pw]) + jnp.where(eye, u["wc"], 0.0)
        u["gy"] = jnp.concatenate([g, y10[:, :pw]], axis=0).astype(BF16)
        u["h0"] = diag_blocks(full[:, pw:])
        u["y0"] = y10[:, pw:]

    ys = {}
    for d in range(2):
        order = range(nch - 1, -1, -1) if d == 1 else range(nch)
        zs = [st_ref[d, p] for p in range(npair)]
        for j in order:
            for p in range(npair):
                u = units[(d * nch + j) * npair + p]
                m = _dot(u["gy"], bd(zs[p]))
                zs[p] = m[:c] + u["h0"]
                ys[(d, j, p)] = m[c:] + u["y0"]
        for p in range(npair):
            st_ref[d, p] = zs[p]
    for d, y_ref in enumerate((yf_ref, yb_ref)):
        y_ref[0] = jnp.concatenate(
            [jnp.concatenate([ys[(d, j, p)] for p in range(npair)], axis=-1) for j in range(nch)], axis=0)


def _scan_call(r, v, kkn, lw0, lw1, k0, k1, b0, b1, nch):
    B, T, _ = r.shape
    rows = nch * CHUNK
    ns = T // rows
    fwd = pl.BlockSpec((1, rows, RW), lambda b, c: (b, c, 0))
    bwd = pl.BlockSpec((1, rows, RW), lambda b, c: (b, ns - 1 - c, 0))
    return pl.pallas_call(
        functools.partial(_scan_kernel, nch=nch),
        grid=(B, ns),
        in_specs=[fwd] * 6 + [bwd] * 6,
        out_specs=[fwd, bwd],
        out_shape=[jax.ShapeDtypeStruct((B, T, RW), F32)] * 2,
        scratch_shapes=[pltpu.VMEM((2, HEADS // 2, HD, 2 * HD), F32)],
        compiler_params=_params(("parallel", "arbitrary")),
        name="scan",
    )(r, v, kkn, lw0, k0, b0, r, v, kkn, lw1, k1, b1)


def _attn_kernel(sink_ref, qt_ref, k_ref, v_ref, o_ref, *, qb):
    n = pl.program_id(1)
    t = k_ref.shape[1]
    band = 3 * BLOCK
    cols = GROUP * BLOCK
    ki = lax.broadcasted_iota(jnp.int32, (band, BLOCK), 0)
    qi = lax.broadcasted_iota(jnp.int32, (band, BLOCK), 1)
    lane_head = lax.broadcasted_iota(jnp.int32, (1, cols), 1) // BLOCK
    zeros = jnp.zeros((HD, cols), BF16)
    units = []
    for j in range(qb):
        blk = n * qb + j
        start = pl.multiple_of(jnp.clip((blk - 1) * BLOCK, 0, t - band), BLOCK)
        bias = jnp.where(jnp.abs(start + ki - (blk * BLOCK + qi)) <= BLOCK, 0.0, NEG)
        bias = jnp.concatenate([bias] * GROUP, axis=1)
        kb = k_ref[0, pl.ds(start, band), :]
        vb = v_ref[0, pl.ds(start, band), :]
        for g in range(KV_HEADS):
            qg = jnp.concatenate([qt_ref[0, (g * GROUP + i) * HD:(g * GROUP + i + 1) * HD,
                                         j * BLOCK:(j + 1) * BLOCK] for i in range(GROUP)], axis=1)
            rhs = jnp.concatenate([qg if gg == g else zeros for gg in range(KV_HEADS)], axis=0)
            sk = jnp.zeros((1, cols), F32)
            for i in range(GROUP):
                sk = jnp.where(lane_head == i, sink_ref[g * GROUP + i] * LOG2E, sk)
            units.append(dict(j=j, g=g, kb=kb, vb=vb, rhs=rhs, sk=sk, bias=bias))
    for u in units:
        u["s"] = _dot(u["kb"], u["rhs"]) + u["bias"]
    for u in units:
        u["m"] = jnp.maximum(jnp.max(u["s"], axis=0, keepdims=True), u["sk"])
    for u in units:
        p = jnp.exp2(u["s"] - u["m"])
        u["r"] = 1.0 / (jnp.sum(p, axis=0, keepdims=True) + jnp.exp2(u["sk"] - u["m"]))
        u["p"] = p.astype(BF16)
    for u in units:
        g = u["g"]
        u["o"] = _dot_tn(u["vb"], u["p"])[g * HD:(g + 1) * HD] * u["r"]
    for u in units:
        j, g = u["j"], u["g"]
        for i in range(GROUP):
            o_ref[0, (g * GROUP + i) * HD:(g * GROUP + i + 1) * HD, j * BLOCK:(j + 1) * BLOCK] = (
                u["o"][:, i * BLOCK:(i + 1) * BLOCK].astype(BF16))


def _attn_call(qt, k, v, sink, qb):
    B, _, T = qt.shape
    assert T >= 3 * BLOCK and T % (qb * BLOCK) == 0
    qspec = pl.BlockSpec((1, RW, qb * BLOCK), lambda b, n: (b, 0, n))
    kvspec = pl.BlockSpec((1, T, KV_W), lambda b, n: (b, 0, 0))
    return pl.pallas_call(
        functools.partial(_attn_kernel, qb=qb),
        grid=(B, T // (qb * BLOCK)),
        in_specs=[pl.BlockSpec(memory_space=pltpu.SMEM), qspec, kvspec, kvspec],
        out_specs=qspec,
        out_shape=jax.ShapeDtypeStruct((B, RW, T), BF16),
        compiler_params=_params(("parallel", "parallel")),
        name="attn",
    )(sink, qt, k, v)


def _merge_kernel(x_ref, mod_ref, yf_ref, yb_ref, bonus_ref, g_ref, yatt_ref, gl_ref,
                  gnw_ref, gnb_ref, ones_ref, pr_ref, pa_ref, wo_ref, n2g_ref, wr_ref,
                  x1_ref, u2_ref, aff_ref):
    ones = ones_ref[...]
    y = yf_ref[0] + yb_ref[0]
    mu = _segsum(y, ones) * (1.0 / HD)
    yc = y - mu
    var = _segsum(yc * yc, ones) * (1.0 / HD)
    yn = yc * lax.rsqrt(var + GN_EPS) * gnw_ref[...] + gnb_ref[...]
    ya = ((yn + bonus_ref[0].astype(F32)) * g_ref[0].astype(F32)).astype(BF16)
    pa = _dot(ya, pr_ref[...])
    pb = _dot_tn(yatt_ref[0], pa_ref[...])
    gates = jax.nn.sigmoid(gl_ref[0].astype(F32))
    m = gates[:, :D_MODEL] * pa + gates[:, D_MODEL:] * pb
    x1 = x_ref[0] + mod_ref[0, 2:3, :] * _dot(m.astype(BF16), wo_ref[...])
    x1_ref[0] = x1
    ms = jnp.mean(x1 * x1, axis=-1, keepdims=True)
    u2 = x1 * lax.rsqrt(ms + NORM_EPS) * n2g_ref[...] * (1.0 + mod_ref[0, 4:5, :]) + mod_ref[0, 3:4, :]
    u2h = u2.astype(BF16)
    u2_ref[0] = u2h
    u2l = (u2 - u2h.astype(F32)).astype(BF16)
    wr = wr_ref[...]
    wrh = wr.astype(BF16)
    wrl = (wr - wrh.astype(F32)).astype(BF16)
    t1 = _dot_nt(jnp.concatenate([wrh, wrl], axis=0), u2h)
    logits = t1[:N_EXPERTS] + t1[N_EXPERTS:] + _dot_nt(wrh, u2l)
    e = jnp.exp(logits - jnp.max(logits, axis=0, keepdims=True))
    aff_ref[0] = e / jnp.sum(e, axis=0, keepdims=True)


def _merge_call(x, mod3, yf, yb, bonus, g, yatt, gl, gnw, gnb, ones, p_r, p_a, w_o, n2g, w_rt, tm):
    B, T, D = x.shape
    row = lambda b, i: (b, i, 0)
    const2 = lambda b, i: (0, 0)
    rw = pl.BlockSpec((1, tm, RW), row)
    return pl.pallas_call(
        _merge_kernel,
        grid=(B, T // tm),
        in_specs=[pl.BlockSpec((1, tm, D), row),
                  pl.BlockSpec((1, 6, D), lambda b, i: (b, 0, 0)),
                  rw, rw, rw, rw,
                  pl.BlockSpec((1, RW, tm), lambda b, i: (b, 0, i)),
                  pl.BlockSpec((1, tm, 2 * D), row),
                  pl.BlockSpec((1, RW), const2),
                  pl.BlockSpec((1, RW), const2),
                  pl.BlockSpec((RW, RW), const2),
                  pl.BlockSpec((RW, D), const2),
                  pl.BlockSpec((RW, D), const2),
                  pl.BlockSpec((D, D), const2),
                  pl.BlockSpec((1, D), const2),
                  pl.BlockSpec((N_EXPERTS, D), const2)],
        out_specs=[pl.BlockSpec((1, tm, D), row),
                   pl.BlockSpec((1, tm, D), row),
                   pl.BlockSpec((1, N_EXPERTS, tm), lambda b, i: (b, 0, i))],
        out_shape=[jax.ShapeDtypeStruct((B, T, D), F32),
                   jax.ShapeDtypeStruct((B, T, D), BF16),
                   jax.ShapeDtypeStruct((B, N_EXPERTS, T), F32)],
        compiler_params=_params(("parallel", "parallel")),
        name="merge",
    )(x, mod3, yf, yb, bonus, g, yatt, gl, gnw, gnb, ones, p_r, p_a, w_o, n2g, w_rt)


def _excl_prefix(mask, tri):
    e, t = mask.shape
    mb = mask.astype(BF16)
    carry = jnp.zeros((e, 1), F32)
    outs = []
    for j in range(t // 128):
        tile = mb[:, j * 128:(j + 1) * 128]
        outs.append(_dot(tile, tri) + carry)
        carry = carry + jnp.sum(tile.astype(F32), axis=1, keepdims=True)
    return jnp.concatenate(outs, axis=-1)


def _route_kernel(aff_ref, tri_ref, slot_ref, *, cap):
    nb, ne, t = aff_ref.shape
    a = aff_ref[...].reshape(nb * ne, t)
    e = a.shape[0]
    capf = jnp.float32(cap)

    def cond(s):
        it, lo, hi = s
        mid = 0.5 * (lo + hi)
        still_open = jnp.max(jnp.where((mid > lo) & (mid < hi), 1.0, 0.0))
        return (it < BISECT_ITERS) & (still_open > 0.0)

    def body(s):
        it, lo, hi = s
        mid = 0.5 * (lo + hi)
        ge = jnp.sum((a >= mid).astype(F32), axis=1, keepdims=True) >= capf
        return it + 1, jnp.where(ge, mid, lo), jnp.where(ge, hi, mid)

    lo0 = jnp.zeros((e, 1), F32)
    hi0 = jnp.full((e, 1), 2.0, F32)
    _, lo, hi = lax.while_loop(cond, body, (jnp.int32(0), lo0, hi0))
    gt = a >= hi
    eq = (a >= lo) & (a < hi)
    need = capf - jnp.sum(gt.astype(F32), axis=1, keepdims=True)
    tri = tri_ref[...]
    sel = gt | (eq & (_excl_prefix(eq, tri) < need))
    pos = _excl_prefix(sel, tri)
    slot_ref[...] = jnp.where(sel, pos.astype(jnp.int32), -1).reshape(nb, ne, t)


def _route_call(aff, tri, cap):
    B, E, T = aff.shape
    return pl.pallas_call(
        functools.partial(_route_kernel, cap=cap),
        grid=(1,),
        in_specs=[pl.BlockSpec((B, E, T), lambda i: (0, 0, 0)),
                  pl.BlockSpec((128, 128), lambda i: (0, 0))],
        out_specs=pl.BlockSpec((B, E, T), lambda i: (0, 0, 0)),
        out_shape=jax.ShapeDtypeStruct((B, E, T), jnp.int32),
        compiler_params=_params(("arbitrary",)),
        name="route",
    )(aff, tri)


def _moe_kernel(u2_ref, slot_ref, aff_ref, wg_ref, wu_ref, wd_ref, o_ref, acc_ref, *, cap, tt):
    e = pl.program_id(1)

    @pl.when(e == 0)
    def _():
        acc_ref[...] = jnp.zeros_like(acc_ref)

    t = u2_ref.shape[1]
    srow = slot_ref[0, pl.ds(e, 1), :]
    arow = aff_ref[0, pl.ds(e, 1), :]
    hit = lax.broadcasted_iota(jnp.int32, (cap, t), 0) == srow
    xe = _dot(hit.astype(BF16), u2_ref[0]).astype(BF16)
    hg = _dot(xe, wg_ref[0])
    hu = _dot(xe, wu_ref[0])
    h = (hg * jax.nn.sigmoid(hg) * hu).astype(BF16)
    ye = _dot(h, wd_ref[0]).astype(BF16)
    wsc = jnp.where(hit, arow, 0.0).astype(BF16)
    for j in range(t // tt):
        acc_ref[j * tt:(j + 1) * tt, :] += _dot_tn(wsc[:, j * tt:(j + 1) * tt], ye)

    @pl.when(e == pl.num_programs(1) - 1)
    def _():
        o_ref[0] = acc_ref[...].astype(BF16)


def _moe_call(u2, slot, aff, wg, wu, wd, cap):
    B, T, D = u2.shape
    E = wg.shape[0]
    F = wg.shape[2]
    tt = min(T, 512)
    return pl.pallas_call(
        functools.partial(_moe_kernel, cap=cap, tt=tt),
        grid=(B, E),
        in_specs=[pl.BlockSpec((1, T, D), lambda b, e: (b, 0, 0)),
                  pl.BlockSpec((1, E, T), lambda b, e: (b, 0, 0)),
                  pl.BlockSpec((1, E, T), lambda b, e: (b, 0, 0)),
                  pl.BlockSpec((1, D, F), lambda b, e: (e, 0, 0)),
                  pl.BlockSpec((1, D, F), lambda b, e: (e, 0, 0)),
                  pl.BlockSpec((1, F, D), lambda b, e: (e, 0, 0))],
        out_specs=pl.BlockSpec((1, T, D), lambda b, e: (b, 0, 0)),
        out_shape=jax.ShapeDtypeStruct((B, T, D), BF16),
        scratch_shapes=[pltpu.VMEM((T, D), F32)],
        compiler_params=_params(("parallel", "arbitrary")),
        name="moe",
    )(u2, slot, aff, wg, wu, wd)


def _final_kernel(x1_ref, mod_ref, moe_ref, o_ref):
    o_ref[0] = x1_ref[0] + mod_ref[0, 5:6, :] * moe_ref[0].astype(F32)


def _final_call(x1, mod3, moe, tm):
    B, T, D = x1.shape
    row = lambda b, i: (b, i, 0)
    return pl.pallas_call(
        _final_kernel,
        grid=(B, T // tm),
        in_specs=[pl.BlockSpec((1, tm, D), row),
                  pl.BlockSpec((1, 6, D), lambda b, i: (b, 0, 0)),
                  pl.BlockSpec((1, tm, D), row)],
        out_specs=pl.BlockSpec((1, tm, D), row),
        out_shape=jax.ShapeDtypeStruct((B, T, D), F32),
        compiler_params=_params(("parallel", "parallel")),
        name="final",
    )(x1, mod3, moe)


def _blockdiag2(w):
    z = jnp.zeros_like(w[0])
    return jnp.concatenate([jnp.concatenate([w[0], z], axis=1), jnp.concatenate([z, w[1]], axis=1)], axis=0)


def _layer(x, mod3, cos_t, sin_t, cos_q, sin_q, ones, tri, norm1_g, w_in, mu_prev, mu_next, rwkv_w0, rwkv_w2, rwkv_a0,
           rwkv_a2, rwkv_g2, rwkv_k_k, rwkv_k_a, rwkv_r_k, rwkv_gn_w, rwkv_gn_b, q_norm_g, k_norm_g,
           attn_sink, p_rwkv, p_attn, w_out, norm2_g, w_router, w_gate, w_up, w_down):
    B, T, D = x.shape
    tm = min(T, 256)
    cap = CAP_FACTOR * T // N_EXPERTS
    row = lambda a: a.reshape(1, -1)
    w_b = w_in.astype(BF16)
    w_rest = jnp.concatenate([w_b[:, :Q0], w_b[:, K0:V0], w_b[:, V0:G0], w_b[:, G0:]], axis=1)
    qg = jnp.broadcast_to(jnp.tile(q_norm_g, HEADS)[:, None], (RW, 128))
    qt, k, v, gl, r, vv, kkn, g, bonus, lw0, lw1, k0, k1, b0, b1 = _inproj_call(
        x, mod3, row(norm1_g), w_rest, w_b[:, Q0:K0].T, cos_t, sin_t, cos_q, sin_q,
        qg, row(jnp.tile(k_norm_g, KV_HEADS)), ones,
        row(mu_prev), row(mu_next), row(rwkv_w0), _blockdiag2(rwkv_w2).astype(BF16),
        row(rwkv_a0), _blockdiag2(rwkv_a2).astype(BF16), rwkv_g2.astype(BF16),
        row(rwkv_k_k), row(rwkv_k_a), row(rwkv_r_k), tm)
    yf, yb = _scan_call(r, vv, kkn, lw0, lw1, k0, k1, b0, b1, SCAN_CHUNKS)
    yatt = _attn_call(qt, k, v, attn_sink, ATT_QB)
    x1, u2, aff = _merge_call(x, mod3, yf, yb, bonus, g, yatt, gl, row(rwkv_gn_w), row(rwkv_gn_b), ones,
                              p_rwkv.astype(BF16), p_attn.astype(BF16), w_out.astype(BF16),
                              row(norm2_g), w_router.T, TM_MERGE if T % TM_MERGE == 0 else tm)
    slot = _route_call(aff, tri, cap)
    moe = _moe_call(u2, slot, aff, w_gate.astype(BF16), w_up.astype(BF16), w_down.astype(BF16), cap)
    return _final_call(x1, mod3, moe, tm)


def kernel(x, c, positions, w_ada, b_ada, norm1_g, w_in, mu_prev, mu_next, rwkv_w0, rwkv_w2, rwkv_a0, rwkv_a2, rwkv_g2, rwkv_k_k, rwkv_k_a, rwkv_r_k, rwkv_gn_w, rwkv_gn_b, q_norm_g, k_norm_g, attn_sink, p_rwkv, p_attn, w_out, norm2_g, w_router, w_gate, w_up, w_down):
    B, T, D = x.shape
    depth = w_ada.shape[0]
    half = ROT // 2
    inv_freq = ROPE_THETA ** (-jnp.arange(0, ROT, 2, dtype=F32) / ROT)
    ang = positions.astype(F32)[..., None] * inv_freq
    cos8, sin8 = jnp.cos(ang), jnp.sin(ang)
    pad1 = jnp.ones((B, T, HD - ROT), F32)
    pad0 = jnp.zeros((B, T, HD - ROT), F32)
    cos_t = jnp.tile(jnp.concatenate([cos8, cos8, pad1], axis=-1), (1, 1, KV_HEADS))
    sin_t = jnp.tile(jnp.concatenate([-sin8, sin8, pad0], axis=-1), (1, 1, KV_HEADS))
    seg = jnp.arange(RW) // HD
    ones = (seg[:, None] == seg[None, :]).astype(BF16)
    idx = jnp.arange(128)
    tri = (idx[:, None] < idx[None, :]).astype(BF16)
    cos_q, sin_q = jnp.swapaxes(cos8, 1, 2), jnp.swapaxes(sin8, 1, 2)
    for l in range(depth):
        mod3 = _mod_call(c, w_ada[l], b_ada[l]).reshape(B, 6, D)
        x = _layer(x, mod3, cos_t, sin_t, cos_q, sin_q, ones, tri, norm1_g[l], w_in[l], mu_prev[l], mu_next[l],
                   rwkv_w0[l], rwkv_w2[l], rwkv_a0[l], rwkv_a2[l], rwkv_g2[l], rwkv_k_k[l], rwkv_k_a[l],
                   rwkv_r_k[l], rwkv_gn_w[l], rwkv_gn_b[l], q_norm_g[l], k_norm_g[l], attn_sink[l],
                   p_rwkv[l], p_attn[l], w_out[l], norm2_g[l], w_router[l], w_gate[l], w_up[l], w_down[l])
    return x
```

```python
import functools
import math

import jax
import jax.numpy as jnp
from jax import lax
from jax.experimental import pallas as pl
from jax.experimental.pallas import tpu as pltpu

F32 = jnp.float32
BF16 = jnp.bfloat16

D_MODEL = 1024
RW = 512
HEADS = 8
HD = 64
LORA_W = 64
LORA_A = 64
LORA_G = 128
GN_EPS = HD * 1e-5
KV_HEADS = 2
GROUP = HEADS // KV_HEADS
KV_W = KV_HEADS * HD
BLOCK = 128
ROPE_THETA = 500000.0
ROT = HD // 4
N_EXPERTS = 16
CAP_FACTOR = 2
NORM_EPS = 1e-6
RWKV_COLS = 3 * RW + 2 * LORA_W + 2 * LORA_A + LORA_G
Q0 = RWKV_COLS
K0 = Q0 + RW
V0 = K0 + KV_W
G0 = V0 + KV_W
IN_COLS = G0 + 2 * D_MODEL
CHUNK = 64
SUB = 16
SCAN_CHUNKS = 4
ATT_QB = 4
TM_MERGE = 512
TOKEN_RADIX = 64
BISECT_ITERS = 160
NEG = -1e30
LOG2E = math.log2(math.e)
QK_SCALE = HD ** -0.5 * LOG2E
VMEM_LIMIT = 56 * 1024 * 1024
MOE_VMEM_LIMIT = 60 * 1024 * 1024


def _dot(a, b):
    return jnp.dot(a, b, preferred_element_type=F32)


def _dot_nt(a, b):
    return lax.dot_general(a, b, (((1,), (1,)), ((), ())), preferred_element_type=F32)


def _dot_tn(a, b):
    return lax.dot_general(a, b, (((0,), (0,)), ((), ())), preferred_element_type=F32)


def _segsum(x, ones):
    xh = x.astype(BF16)
    xl = (x - xh.astype(F32)).astype(BF16)
    return _dot(xh, ones) + _dot(xl, ones)


def _params(sem):
    return pltpu.CompilerParams(dimension_semantics=sem, vmem_limit_bytes=VMEM_LIMIT)


def _mod_kernel(c_ref, w_ref, b_ref, o_ref):
    c = c_ref[...]
    ca = c * jax.nn.sigmoid(c)
    o_ref[...] = jnp.dot(ca, w_ref[...], preferred_element_type=F32,
                         precision=lax.Precision.HIGHEST) + b_ref[...]


def _mod_call(c, w_ada, b_ada):
    B, D = c.shape
    n = w_ada.shape[1] // D
    return pl.pallas_call(
        _mod_kernel,
        grid=(n,),
        in_specs=[pl.BlockSpec((B, D), lambda j: (0, 0)),
                  pl.BlockSpec((D, D), lambda j: (0, j)),
                  pl.BlockSpec((1, D), lambda j: (0, j))],
        out_specs=pl.BlockSpec((B, D), lambda j: (0, j)),
        out_shape=jax.ShapeDtypeStruct((B, n * D), F32),
        compiler_params=_params(("arbitrary",)),
        name="mod",
    )(c, w_ada, b_ada.reshape(1, -1))


def _rope(xn, cos, sin, width):
    lane = lax.broadcasted_iota(jnp.int32, xn.shape, 1) & (HD - 1)
    rot = jnp.where(lane < ROT // 2, pltpu.roll(xn, width - ROT // 2, 1), pltpu.roll(xn, ROT // 2, 1))
    return xn * cos + rot * sin


HALO = 8


def _inproj_kernel(x_ref, xp_ref, xn_ref, mod_ref, g_ref, w_ref, wq_ref, cos_ref, sin_ref, cosq_ref, sinq_ref,
                   qg_ref, kg_ref, ones8_ref,
                   mup_ref, mun_ref, w0_ref, w2_ref, a0_ref, a2_ref, g2_ref, kk_ref, ka_ref, rk_ref,
                   qt_ref, k_ref, v_ref, gl_ref,
                   r_o, v_o, kkn_o, g_o, bonus_o, lw0_o, lw1_o, k0_o, k1_o, b0_o, b1_o):
    i = pl.program_id(1)
    last = pl.num_programs(1) - 1
    tm = x_ref.shape[1]
    xx = jnp.concatenate([xp_ref[0], x_ref[0], xn_ref[0]], axis=0)
    ms = jnp.mean(xx * xx, axis=-1, keepdims=True)
    y = xx * lax.rsqrt(ms + NORM_EPS) * g_ref[...]
    u_all = y * (1.0 + mod_ref[0, 1:2, :]) + mod_ref[0, 0:1, :]
    ub = u_all[HALO:HALO + tm].astype(BF16)
    c_k, c_v, c_g = RWKV_COLS, RWKV_COLS + KV_W, RWKV_COLS + 2 * KV_W
    z_all = _dot(u_all.astype(BF16), w_ref[:, 0:c_k])
    k = _dot(ub, w_ref[:, c_k:c_v])
    v_ref[0] = _dot(ub, w_ref[:, c_v:c_g]).astype(BF16)
    gl_ref[0] = _dot(ub, w_ref[:, c_g:]).astype(BF16)
    ones8 = ones8_ref[...]
    kn = k * lax.rsqrt(_segsum(k * k, ones8[:KV_W, :KV_W]) * (1.0 / HD) + NORM_EPS) * kg_ref[...]
    k_ref[0] = _rope(kn, cos_ref[0], sin_ref[0], KV_W).astype(BF16)

    z = z_all[HALO:HALO + tm]
    row = lax.broadcasted_iota(jnp.int32, z.shape, 0)
    zp = jnp.where((row == 0) & (i == 0), 0.0, z_all[HALO - 1:HALO - 1 + tm])
    zn = jnp.where((row == tm - 1) & (i == last), 0.0, z_all[HALO + 1:HALO + 1 + tm])
    zs = z + mup_ref[...] * (zp - z) + mun_ref[...] * (zn - z)
    r = zs[:, 0:RW]
    kr = zs[:, RW:2 * RW]
    vr = zs[:, 2 * RW:3 * RW]
    c0 = 3 * RW
    wd = zs[:, c0:c0 + 2 * LORA_W]
    ad = zs[:, c0 + 2 * LORA_W:c0 + 2 * LORA_W + 2 * LORA_A]
    gd = zs[:, c0 + 2 * LORA_W + 2 * LORA_A:RWKV_COLS]
    wl = _dot(jnp.tanh(wd).astype(BF16), w2_ref[...]) + w0_ref[...]
    lw = -math.exp(-0.5) * jax.nn.sigmoid(wl)
    al = jax.nn.sigmoid(_dot(ad.astype(BF16), a2_ref[...]) + a0_ref[...])
    g = _dot(jax.nn.sigmoid(gd).astype(BF16), g2_ref[...])
    kk = kr * kk_ref[...]
    kkn = kk / jnp.maximum(jnp.sqrt(_segsum(kk * kk, ones8)), 1e-12)
    ka = ka_ref[...]
    a_0 = al[:, 0:RW]
    a_1 = al[:, RW:2 * RW]
    k_0 = kr * (1.0 + (a_0 - 1.0) * ka)
    k_1 = kr * (1.0 + (a_1 - 1.0) * ka)
    bonus = _segsum(r * (k_0 + k_1) * rk_ref[...], ones8) * vr
    r_o[0] = r.astype(BF16)
    v_o[0] = vr.astype(BF16)
    kkn_o[0] = kkn.astype(BF16)
    g_o[0] = g.astype(BF16)
    bonus_o[0] = bonus.astype(BF16)
    lw0_o[0] = lw[:, 0:RW]
    lw1_o[0] = lw[:, RW:2 * RW]
    k0_o[0] = k_0.astype(BF16)
    k1_o[0] = k_1.astype(BF16)
    b0_o[0] = (kkn * a_0).astype(BF16)
    b1_o[0] = (kkn * a_1).astype(BF16)

    qt = _dot_nt(wq_ref[...], ub)
    qg = jnp.concatenate([qg_ref[...]] * (tm // 128), axis=1)
    cq = cosq_ref[0]
    sq = sinq_ref[0]
    half = ROT // 2
    pieces = []
    for h in range(HEADS):
        xh = qt[h * HD:(h + 1) * HD]
        xn = (xh * lax.rsqrt(jnp.mean(xh * xh, axis=0, keepdims=True) + NORM_EPS)
              * qg[h * HD:(h + 1) * HD] * QK_SCALE)
        x1, x2 = xn[0:half], xn[half:ROT]
        pieces += [x1 * cq - x2 * sq, x2 * cq + x1 * sq, xn[ROT:]]
    qt_ref[0] = jnp.concatenate(pieces, axis=0).astype(BF16)


def _inproj_call(x, mod3, norm1_g, w_rest, wq_t, cos_t, sin_t, cos_q, sin_q, qg, kg, ones,
                 mup, mun, w0c, w2c, a0c, a2c, g2b, k_k, k_a, r_k, tm):
    B, T, D = x.shape
    nh = tm // HALO
    nth = T // HALO
    row = lambda b, i: (b, i, 0)
    col = lambda b, i: (b, 0, i)
    const2 = lambda b, i: (0, 0)
    half = ROT // 2
    rw = pl.BlockSpec((1, tm, RW), row)
    vec = lambda n: pl.BlockSpec((1, n), const2)
    return pl.pallas_call(
        _inproj_kernel,
        grid=(B, T // tm),
        in_specs=[pl.BlockSpec((1, tm, D), row),
                  pl.BlockSpec((1, HALO, D), lambda b, i: (b, jnp.maximum(i * nh - 1, 0), 0)),
                  pl.BlockSpec((1, HALO, D), lambda b, i: (b, jnp.minimum((i + 1) * nh, nth - 1), 0)),
                  pl.BlockSpec((1, 6, D), lambda b, i: (b, 0, 0)),
                  vec(D),
                  pl.BlockSpec(w_rest.shape, const2),
                  pl.BlockSpec((RW, D), const2),
                  pl.BlockSpec((1, tm, KV_W), row),
                  pl.BlockSpec((1, tm, KV_W), row),
                  pl.BlockSpec((1, half, tm), col),
                  pl.BlockSpec((1, half, tm), col),
                  pl.BlockSpec((RW, 128), const2),
                  vec(KV_W),
                  pl.BlockSpec((RW, RW), const2),
                  vec(RWKV_COLS), vec(RWKV_COLS),
                  vec(2 * RW), pl.BlockSpec((2 * LORA_W, 2 * RW), const2),
                  vec(2 * RW), pl.BlockSpec((2 * LORA_A, 2 * RW), const2),
                  pl.BlockSpec((LORA_G, RW), const2),
                  vec(RW), vec(RW), vec(RW)],
        out_specs=[pl.BlockSpec((1, RW, tm), col),
                   pl.BlockSpec((1, tm, KV_W), row),
                   pl.BlockSpec((1, tm, KV_W), row),
                   pl.BlockSpec((1, tm, 2 * D), row)] + [rw] * 11,
        out_shape=[jax.ShapeDtypeStruct((B, RW, T), BF16),
                   jax.ShapeDtypeStruct((B, T, KV_W), BF16),
                   jax.ShapeDtypeStruct((B, T, KV_W), BF16),
                   jax.ShapeDtypeStruct((B, T, 2 * D), BF16)]
        + [jax.ShapeDtypeStruct((B, T, RW), F32 if i in (5, 6) else BF16) for i in range(11)],
        compiler_params=_params(("parallel", "parallel")),
        name="inproj",
    )(x, x, x, mod3, norm1_g, w_rest, wq_t, cos_t, sin_t, cos_q, sin_q, qg, kg, ones,
      mup, mun, w0c, w2c, a0c, a2c, g2b, k_k, k_a, r_k)


def _dir_operands(rev, nch, r, v, kk, lw, kd, bd):
    n = r.shape[0]
    ti = lax.broadcasted_iota(jnp.int32, (n, n), 0)
    si = lax.broadcasted_iota(jnp.int32, (n, n), 1)
    same = (ti // CHUNK) == (si // CHUNK)
    incl = ((si >= ti) if rev else (si <= ti)) & same
    mi = incl.astype(BF16)
    l1 = lw.astype(BF16)
    r1 = lw - l1.astype(F32)
    l2 = r1.astype(BF16)
    l3 = (r1 - l2.astype(F32)).astype(BF16)
    cum = _dot(mi, l1) + _dot(mi, l2) + _dot(mi, l3)
    wt = jnp.exp(cum)
    winv = jnp.exp(-cum)
    ops = dict(ah=-(kk * jnp.exp(cum - lw)), rt=r * wt, bt=bd * winv, kt=kd * winv, v=v)
    wcs, bhs, khs = [], [], []
    for j in range(nch):
        last = j * CHUNK if rev else (j + 1) * CHUNK - 1
        wc = wt[last:last + 1, :]
        rows = slice(j * CHUNK, (j + 1) * CHUNK)
        wcs.append(wc)
        bhs.append(ops["bt"][rows] * wc)
        khs.append(ops["kt"][rows] * wc)
    return ops, wcs, bhs, khs


def _scan_kernel(rf, vf, kkf, lwf, kf, bf, rb, vb, kkb, lwb, kb, bb, yf_ref, yb_ref, st_ref, *, nch):
    @pl.when(pl.program_id(1) == 0)
    def _():
        st_ref[...] = jnp.zeros_like(st_ref)

    c = CHUNK
    pw = 2 * HD
    npair = HEADS // 2
    ti = lax.broadcasted_iota(jnp.int32, (c, pw), 0)
    li = lax.broadcasted_iota(jnp.int32, (c, pw), 1)
    si = li & (HD - 1)
    head0 = li < HD
    blk = (ti // SUB) == (si // SUB)
    eye = ti == si
    two = lambda m: jnp.concatenate([m, m], axis=1)
    masks = {False: (two(si < ti), two(si <= ti)), True: (two(si > ti), two(si >= ti))}

    def bd(y):
        yb = y.astype(BF16)
        zero = jnp.zeros_like(yb)
        return jnp.concatenate([jnp.where(head0, yb, zero), jnp.where(head0, zero, yb)], axis=0)

    def bdcat(*ys):
        return jnp.concatenate([bd(y) for y in ys], axis=1)

    def diag_blocks(full):
        return jnp.where(head0, full[:c], full[c:])

    dirs = [(False, rf, vf, kkf, lwf, kf, bf), (True, rb, vb, kkb, lwb, kb, bb)]
    units = []
    for d, (rev, r_, v_, kk_, lw_, k_, b_) in enumerate(dirs):
        f32 = lambda ref: ref[0].astype(F32)
        ops, wcs, bhs, khs = _dir_operands(rev, nch, f32(r_), f32(v_), f32(kk_), lw_[0], f32(k_), f32(b_))
        for j in range(nch):
            rows = slice(j * c, (j + 1) * c)
            for p in range(npair):
                sl = slice(p * pw, (p + 1) * pw)
                u = dict(d=d, j=j, p=p, rev=rev, wc=wcs[j][:, sl],
                         bkh=jnp.concatenate([bhs[j][:, sl], khs[j][:, sl]], axis=0).astype(BF16))
                for name in ("ah", "rt", "bt", "kt", "v"):
                    u[name] = ops[name][rows, sl]
                units.append(u)

    for u in units:
        x1 = jnp.concatenate([u["ah"], u["rt"]], axis=0).astype(BF16)
        u["lbk"] = _dot_nt(x1, jnp.concatenate([bd(u["bt"]), bd(u["kt"])], axis=0))
    for u in units:
        strict, incl = masks[u["rev"]]
        top = jnp.where(strict, u["lbk"][:c], 0.0)
        bot = jnp.where(incl, u["lbk"][c:], 0.0)
        lab = top[:, :pw]
        u["mrb"] = bot[:, :pw].astype(BF16)
        u["ld"] = jnp.where(blk, lab, 0.0)
        u["lo"] = lab - u["ld"]
        u["lm"] = jnp.concatenate([top[:, pw:], bot[:, pw:]], axis=0).astype(BF16)
    for u in units:
        u["a2"] = _dot(u["ld"].astype(BF16), bd(u["ld"]))
        u["lkv"] = _dot(u["lm"], bd(u["v"]))
    for u in units:
        s = _dot(jnp.concatenate([u["a2"], u["ld"]], axis=0).astype(BF16), bd(u["a2"]))
        u["a4"] = s[:c]
        u["t"] = jnp.where(eye, 1.0, 0.0) + u["ld"] + u["a2"] + s[c:]
    for u in units:
        s = _dot(jnp.concatenate([u["a4"], u["t"]], axis=0).astype(BF16), bd(u["a4"]))
        u["a8"] = s[:c]
        u["t"] = u["t"] + s[c:]
    for u in units:
        u["td"] = u["t"] + _dot(u["t"].astype(BF16), bd(u["a8"]))
    for u in units:
        x = _dot(u["td"].astype(BF16), bdcat(u["lo"], u["ah"], u["lkv"][:c]))
        u["n"] = x[:, :pw]
        u["z"] = x[:, pw:]
    for u in units:
        s = _dot(u["n"].astype(BF16), bdcat(u["n"], u["z"][:, :pw], u["z"][:, pw:]))
        u["n2"] = s[:, :pw]
        u["z"] = u["z"] + s[:, pw:]
    for u in units:
        u["z"] = u["z"] + _dot(u["n2"].astype(BF16), bdcat(u["z"][:, :pw], u["z"][:, pw:]))
    for u in units:
        z = u["z"]
        y10 = (jnp.concatenate([u["rt"], u["lkv"][c:]], axis=1)
               + _dot(u["mrb"], bdcat(z[:, :pw], z[:, pw:])))
        vpad = jnp.concatenate([jnp.zeros_like(u["v"]), u["v"]], axis=1)
        full = _dot_tn(u["bkh"], jnp.concatenate([z, vpad], axis=0).astype(BF16))
        g = diag_blocks(full[:, :pw]) + jnp.where(eye, u["wc"], 0.0)
        u["gy"] = jnp.concatenate([g, y10[:, :pw]], axis=0).astype(BF16)
        u["h0"] = diag_blocks(full[:, pw:])
        u["y0"] = y10[:, pw:]

    ys = {}
    for d in range(2):
        order = range(nch - 1, -1, -1) if d == 1 else range(nch)
        zs = [st_ref[d, p] for p in range(npair)]
        for j in order:
            for p in range(npair):
                u = units[(d * nch + j) * npair + p]
                m = _dot(u["gy"], bd(zs[p]))
                zs[p] = m[:c] + u["h0"]
                ys[(d, j, p)] = m[c:] + u["y0"]
        for p in range(npair):
            st_ref[d, p] = zs[p]
    for d, y_ref in enumerate((yf_ref, yb_ref)):
        y_ref[0] = jnp.concatenate(
            [jnp.concatenate([ys[(d, j, p)] for p in range(npair)], axis=-1) for j in range(nch)], axis=0)


def _scan_call(r, v, kkn, lw0, lw1, k0, k1, b0, b1, nch):
    B, T, _ = r.shape
    rows = nch * CHUNK
    ns = T // rows
    fwd = pl.BlockSpec((1, rows, RW), lambda b, c: (b, c, 0))
    bwd = pl.BlockSpec((1, rows, RW), lambda b, c: (b, ns - 1 - c, 0))
    return pl.pallas_call(
        functools.partial(_scan_kernel, nch=nch),
        grid=(B, ns),
        in_specs=[fwd] * 6 + [bwd] * 6,
        out_specs=[fwd, bwd],
        out_shape=[jax.ShapeDtypeStruct((B, T, RW), F32)] * 2,
        scratch_shapes=[pltpu.VMEM((2, HEADS // 2, HD, 2 * HD), F32)],
        compiler_params=_params(("parallel", "arbitrary")),
        name="scan",
    )(r, v, kkn, lw0, k0, b0, r, v, kkn, lw1, k1, b1)


def _attn_kernel(sink_ref, qt_ref, k_ref, v_ref, o_ref, *, qb):
    n = pl.program_id(1)
    t = k_ref.shape[1]
    band = 3 * BLOCK
    cols = GROUP * BLOCK
    ki = lax.broadcasted_iota(jnp.int32, (band, BLOCK), 0)
    qi = lax.broadcasted_iota(jnp.int32, (band, BLOCK), 1)
    lane_head = lax.broadcasted_iota(jnp.int32, (1, cols), 1) // BLOCK
    zeros = jnp.zeros((HD, cols), BF16)
    units = []
    for j in range(qb):
        blk = n * qb + j
        start = pl.multiple_of(jnp.clip((blk - 1) * BLOCK, 0, t - band), BLOCK)
        bias = jnp.where(jnp.abs(start + ki - (blk * BLOCK + qi)) <= BLOCK, 0.0, NEG)
        bias = jnp.concatenate([bias] * GROUP, axis=1)
        kb = k_ref[0, pl.ds(start, band), :]
        vb = v_ref[0, pl.ds(start, band), :]
        for g in range(KV_HEADS):
            qg = jnp.concatenate([qt_ref[0, (g * GROUP + i) * HD:(g * GROUP + i + 1) * HD,
                                         j * BLOCK:(j + 1) * BLOCK] for i in range(GROUP)], axis=1)
            rhs = jnp.concatenate([qg if gg == g else zeros for gg in range(KV_HEADS)], axis=0)
            sk = jnp.zeros((1, cols), F32)
            for i in range(GROUP):
                sk = jnp.where(lane_head == i, sink_ref[g * GROUP + i] * LOG2E, sk)
            units.append(dict(j=j, g=g, kb=kb, vb=vb, rhs=rhs, sk=sk, bias=bias))
    for u in units:
        u["s"] = _dot(u["kb"], u["rhs"]) + u["bias"]
    for u in units:
        u["m"] = jnp.maximum(jnp.max(u["s"], axis=0, keepdims=True), u["sk"])
    for u in units:
        p = jnp.exp2(u["s"] - u["m"])
        u["r"] = 1.0 / (jnp.sum(p, axis=0, keepdims=True) + jnp.exp2(u["sk"] - u["m"]))
        u["p"] = p.astype(BF16)
    for u in units:
        g = u["g"]
        u["o"] = _dot_tn(u["vb"], u["p"])[g * HD:(g + 1) * HD] * u["r"]
    for u in units:
        j, g = u["j"], u["g"]
        for i in range(GROUP):
            o_ref[0, (g * GROUP + i) * HD:(g * GROUP + i + 1) * HD, j * BLOCK:(j + 1) * BLOCK] = (
                u["o"][:, i * BLOCK:(i + 1) * BLOCK].astype(BF16))


def _attn_call(qt, k, v, sink, qb):
    B, _, T = qt.shape
    assert T >= 3 * BLOCK and T % (qb * BLOCK) == 0
    qspec = pl.BlockSpec((1, RW, qb * BLOCK), lambda b, n: (b, 0, n))
    kvspec = pl.BlockSpec((1, T, KV_W), lambda b, n: (b, 0, 0))
    return pl.pallas_call(
        functools.partial(_attn_kernel, qb=qb),
        grid=(B, T // (qb * BLOCK)),
        in_specs=[pl.BlockSpec(memory_space=pltpu.SMEM), qspec, kvspec, kvspec],
        out_specs=qspec,
        out_shape=jax.ShapeDtypeStruct((B, RW, T), BF16),
        compiler_params=_params(("parallel", "parallel")),
        name="attn",
    )(sink, qt, k, v)


def _merge_kernel(x_ref, mod_ref, yf_ref, yb_ref, bonus_ref, g_ref, yatt_ref, gl_ref,
                  gnw_ref, gnb_ref, ones_ref, pr_ref, pa_ref, wo_ref, n2g_ref, wr_ref,
                  x1_ref, u2_ref, aff_ref):
    ones = ones_ref[...]
    y = yf_ref[0] + yb_ref[0]
    mu = _segsum(y, ones) * (1.0 / HD)
    yc = y - mu
    var = _segsum(yc * yc, ones) * (1.0 / HD)
    yn = yc * lax.rsqrt(var + GN_EPS) * gnw_ref[...] + gnb_ref[...]
    ya = ((yn + bonus_ref[0].astype(F32)) * g_ref[0].astype(F32)).astype(BF16)
    pa = _dot(ya, pr_ref[...])
    pb = _dot_tn(yatt_ref[0], pa_ref[...])
    gates = jax.nn.sigmoid(gl_ref[0].astype(F32))
    m = gates[:, :D_MODEL] * pa + gates[:, D_MODEL:] * pb
    x1 = x_ref[0] + mod_ref[0, 2:3, :] * _dot(m.astype(BF16), wo_ref[...])
    x1_ref[0] = x1
    ms = jnp.mean(x1 * x1, axis=-1, keepdims=True)
    u2 = x1 * lax.rsqrt(ms + NORM_EPS) * n2g_ref[...] * (1.0 + mod_ref[0, 4:5, :]) + mod_ref[0, 3:4, :]
    u2_ref[0] = u2
    u2h = u2.astype(BF16)
    u2l = (u2 - u2h.astype(F32)).astype(BF16)
    wr = wr_ref[...]
    wrh = wr.astype(BF16)
    wrl = (wr - wrh.astype(F32)).astype(BF16)
    t1 = _dot_nt(jnp.concatenate([wrh, wrl], axis=0), u2h)
    logits = t1[:N_EXPERTS] + t1[N_EXPERTS:] + _dot_nt(wrh, u2l)
    e = jnp.exp(logits - jnp.max(logits, axis=0, keepdims=True))
    aff_ref[0] = e / jnp.sum(e, axis=0, keepdims=True)


def _merge_call(x, mod3, yf, yb, bonus, g, yatt, gl, gnw, gnb, ones, p_r, p_a, w_o, n2g, w_rt, tm):
    B, T, D = x.shape
    row = lambda b, i: (b, i, 0)
    const2 = lambda b, i: (0, 0)
    rw = pl.BlockSpec((1, tm, RW), row)
    return pl.pallas_call(
        _merge_kernel,
        grid=(B, T // tm),
        in_specs=[pl.BlockSpec((1, tm, D), row),
                  pl.BlockSpec((1, 6, D), lambda b, i: (b, 0, 0)),
                  rw, rw, rw, rw,
                  pl.BlockSpec((1, RW, tm), lambda b, i: (b, 0, i)),
                  pl.BlockSpec((1, tm, 2 * D), row),
                  pl.BlockSpec((1, RW), const2),
                  pl.BlockSpec((1, RW), const2),
                  pl.BlockSpec((RW, RW), const2),
                  pl.BlockSpec((RW, D), const2),
                  pl.BlockSpec((RW, D), const2),
                  pl.BlockSpec((D, D), const2),
                  pl.BlockSpec((1, D), const2),
                  pl.BlockSpec((N_EXPERTS, D), const2)],
        out_specs=[pl.BlockSpec((1, tm, D), row),
                   pl.BlockSpec((1, tm, D), row),
                   pl.BlockSpec((1, N_EXPERTS, tm), lambda b, i: (b, 0, i))],
        out_shape=[jax.ShapeDtypeStruct((B, T, D), F32),
                   jax.ShapeDtypeStruct((B, T, D), F32),
                   jax.ShapeDtypeStruct((B, N_EXPERTS, T), F32)],
        compiler_params=_params(("parallel", "parallel")),
        name="merge",
    )(x, mod3, yf, yb, bonus, g, yatt, gl, gnw, gnb, ones, p_r, p_a, w_o, n2g, w_rt)


def _excl_prefix(mask, tri):
    e, t = mask.shape
    mb = mask.astype(BF16)
    carry = jnp.zeros((e, 1), F32)
    outs = []
    for j in range(t // 128):
        tile = mb[:, j * 128:(j + 1) * 128]
        outs.append(_dot(tile, tri) + carry)
        carry = carry + jnp.sum(tile.astype(F32), axis=1, keepdims=True)
    return jnp.concatenate(outs, axis=-1)


def _route_kernel(aff_ref, tri_ref, dig_ref, slot_ref, idx_ref, *, cap):
    nb, ne, t = aff_ref.shape
    a = aff_ref[...].reshape(nb * ne, t)
    e = a.shape[0]
    capf = jnp.float32(cap)

    def cond(s):
        it, lo, hi = s
        mid = 0.5 * (lo + hi)
        still_open = jnp.max(jnp.where((mid > lo) & (mid < hi), 1.0, 0.0))
        return (it < BISECT_ITERS) & (still_open > 0.0)

    def body(s):
        it, lo, hi = s
        mid = 0.5 * (lo + hi)
        ge = jnp.sum((a >= mid).astype(F32), axis=1, keepdims=True) >= capf
        return it + 1, jnp.where(ge, mid, lo), jnp.where(ge, hi, mid)

    lo0 = jnp.zeros((e, 1), F32)
    hi0 = jnp.full((e, 1), 2.0, F32)
    _, lo, hi = lax.while_loop(cond, body, (jnp.int32(0), lo0, hi0))
    gt = a >= hi
    eq = (a >= lo) & (a < hi)
    need = capf - jnp.sum(gt.astype(F32), axis=1, keepdims=True)
    tri = tri_ref[...]
    sel = gt | (eq & (_excl_prefix(eq, tri) < need))
    pos = _excl_prefix(sel, tri)
    slot_ref[...] = jnp.where(sel, pos.astype(jnp.int32), -1).reshape(nb, ne, t)

    digits = dig_ref[...]
    slots = lax.broadcasted_iota(jnp.int32, (cap, t), 0)

    def row_body(r, carry):
        hit = (slots == slot_ref[r // ne, pl.ds(r % ne, 1), :]).astype(BF16)
        d = _dot_nt(digits, hit)
        idx_ref[r] = (d[0:1] * TOKEN_RADIX + d[1:2]).astype(jnp.int32)
        return carry

    lax.fori_loop(0, nb * ne, row_body, 0)


def _route_call(aff, tri, digits, cap):
    B, E, T = aff.shape
    return pl.pallas_call(
        functools.partial(_route_kernel, cap=cap),
        grid=(1,),
        in_specs=[pl.BlockSpec((B, E, T), lambda i: (0, 0, 0)),
                  pl.BlockSpec((128, 128), lambda i: (0, 0)),
                  pl.BlockSpec((8, T), lambda i: (0, 0))],
        out_specs=[pl.BlockSpec((B, E, T), lambda i: (0, 0, 0)),
                   pl.BlockSpec((B * E, 1, cap), lambda i: (0, 0, 0))],
        out_shape=[jax.ShapeDtypeStruct((B, E, T), jnp.int32),
                   jax.ShapeDtypeStruct((B * E, 1, cap), jnp.int32)],
        compiler_params=_params(("arbitrary",)),
        name="route",
    )(aff, tri, digits)


def _moe_kernel(idx0_ref, idx1_ref, u2_ref, slot_ref, aff_ref, wg_ref, wu_ref, wd_ref, o_ref,
                acc_ref, xa_ref, xb_ref, *, cap, tt):
    e = pl.program_id(1)
    t = u2_ref.shape[1]

    def gather(idx_ref, dst_ref):
        for i in range(cap):
            dst_ref[pl.ds(i, 1), :] = u2_ref[0, pl.ds(idx_ref[0, 0, i], 1), :]

    def expert(x_ref):
        xe = x_ref[...].astype(BF16)
        hg = _dot(xe, wg_ref[0])
        hu = _dot(xe, wu_ref[0])
        h = (hg * jax.nn.sigmoid(hg) * hu).astype(BF16)
        ye = _dot(h, wd_ref[0]).astype(BF16)
        hit = lax.broadcasted_iota(jnp.int32, (cap, t), 0) == slot_ref[0, pl.ds(e, 1), :]
        wsc = jnp.where(hit, aff_ref[0, pl.ds(e, 1), :], 0.0).astype(BF16)
        for j in range(t // tt):
            acc_ref[j * tt:(j + 1) * tt, :] += _dot_tn(wsc[:, j * tt:(j + 1) * tt], ye)

    @pl.when(e == 0)
    def _():
        acc_ref[...] = jnp.zeros_like(acc_ref)
        gather(idx0_ref, xa_ref)

    @pl.when(e % 2 == 0)
    def _():
        gather(idx1_ref, xb_ref)
        expert(xa_ref)

    @pl.when(e % 2 == 1)
    def _():
        gather(idx1_ref, xa_ref)
        expert(xb_ref)

    @pl.when(e == pl.num_programs(1) - 1)
    def _():
        o_ref[0] = acc_ref[...].astype(BF16)


def _moe_call(u2, slot, aff, idx, wg, wu, wd, cap):
    B, T, D = u2.shape
    E = wg.shape[0]
    F = wg.shape[2]
    assert E % 2 == 0
    tt = min(T, 512)
    smem_row = lambda im: pl.BlockSpec((1, 1, cap), im, memory_space=pltpu.SMEM)
    return pl.pallas_call(
        functools.partial(_moe_kernel, cap=cap, tt=tt),
        grid=(B, E),
        in_specs=[smem_row(lambda b, e: (b * E + e, 0, 0)),
                  smem_row(lambda b, e: (b * E + jnp.minimum(e + 1, E - 1), 0, 0)),
                  pl.BlockSpec((1, T, D), lambda b, e: (b, 0, 0)),
                  pl.BlockSpec((1, E, T), lambda b, e: (b, 0, 0)),
                  pl.BlockSpec((1, E, T), lambda b, e: (b, 0, 0)),
                  pl.BlockSpec((1, D, F), lambda b, e: (e, 0, 0)),
                  pl.BlockSpec((1, D, F), lambda b, e: (e, 0, 0)),
                  pl.BlockSpec((1, F, D), lambda b, e: (e, 0, 0))],
        out_specs=pl.BlockSpec((1, T, D), lambda b, e: (b, 0, 0)),
        out_shape=jax.ShapeDtypeStruct((B, T, D), BF16),
        scratch_shapes=[pltpu.VMEM((T, D), F32), pltpu.VMEM((cap, D), F32), pltpu.VMEM((cap, D), F32)],
        compiler_params=pltpu.CompilerParams(dimension_semantics=("parallel", "arbitrary"),
                                             vmem_limit_bytes=MOE_VMEM_LIMIT),
        name="moe",
    )(idx, idx, u2, slot, aff, wg, wu, wd)


def _final_kernel(x1_ref, mod_ref, moe_ref, o_ref):
    o_ref[0] = x1_ref[0] + mod_ref[0, 5:6, :] * moe_ref[0].astype(F32)


def _final_call(x1, mod3, moe, tm):
    B, T, D = x1.shape
    row = lambda b, i: (b, i, 0)
    return pl.pallas_call(
        _final_kernel,
        grid=(B, T // tm),
        in_specs=[pl.BlockSpec((1, tm, D), row),
                  pl.BlockSpec((1, 6, D), lambda b, i: (b, 0, 0)),
                  pl.BlockSpec((1, tm, D), row)],
        out_specs=pl.BlockSpec((1, tm, D), row),
        out_shape=jax.ShapeDtypeStruct((B, T, D), F32),
        compiler_params=_params(("parallel", "parallel")),
        name="final",
    )(x1, mod3, moe)


def _blockdiag2(w):
    z = jnp.zeros_like(w[0])
    return jnp.concatenate([jnp.concatenate([w[0], z], axis=1), jnp.concatenate([z, w[1]], axis=1)], axis=0)


def _layer(x, mod3, cos_t, sin_t, cos_q, sin_q, ones, tri, norm1_g, w_in, mu_prev, mu_next, rwkv_w0, rwkv_w2, rwkv_a0,
           rwkv_a2, rwkv_g2, rwkv_k_k, rwkv_k_a, rwkv_r_k, rwkv_gn_w, rwkv_gn_b, q_norm_g, k_norm_g,
           attn_sink, p_rwkv, p_attn, w_out, norm2_g, w_router, w_gate, w_up, w_down):
    B, T, D = x.shape
    tm = min(T, 256)
    cap = CAP_FACTOR * T // N_EXPERTS
    row = lambda a: a.reshape(1, -1)
    w_b = w_in.astype(BF16)
    w_rest = jnp.concatenate([w_b[:, :Q0], w_b[:, K0:V0], w_b[:, V0:G0], w_b[:, G0:]], axis=1)
    qg = jnp.broadcast_to(jnp.tile(q_norm_g, HEADS)[:, None], (RW, 128))
    qt, k, v, gl, r, vv, kkn, g, bonus, lw0, lw1, k0, k1, b0, b1 = _inproj_call(
        x, mod3, row(norm1_g), w_rest, w_b[:, Q0:K0].T, cos_t, sin_t, cos_q, sin_q,
        qg, row(jnp.tile(k_norm_g, KV_HEADS)), ones,
        row(mu_prev), row(mu_next), row(rwkv_w0), _blockdiag2(rwkv_w2).astype(BF16),
        row(rwkv_a0), _blockdiag2(rwkv_a2).astype(BF16), rwkv_g2.astype(BF16),
        row(rwkv_k_k), row(rwkv_k_a), row(rwkv_r_k), tm)
    yf, yb = _scan_call(r, vv, kkn, lw0, lw1, k0, k1, b0, b1, SCAN_CHUNKS)
    yatt = _attn_call(qt, k, v, attn_sink, ATT_QB)
    x1, u2, aff = _merge_call(x, mod3, yf, yb, bonus, g, yatt, gl, row(rwkv_gn_w), row(rwkv_gn_b), ones,
                              p_rwkv.astype(BF16), p_attn.astype(BF16), w_out.astype(BF16),
                              row(norm2_g), w_router.T, TM_MERGE if T % TM_MERGE == 0 else tm)
    assert T <= 256 * TOKEN_RADIX
    tok = jnp.arange(T)
    digits = jnp.zeros((8, T), BF16).at[0].set((tok // TOKEN_RADIX).astype(BF16)).at[1].set(
        (tok % TOKEN_RADIX).astype(BF16))
    slot, idx = _route_call(aff, tri, digits, cap)
    moe = _moe_call(u2, slot, aff, idx, w_gate.astype(BF16), w_up.astype(BF16), w_down.astype(BF16), cap)
    return _final_call(x1, mod3, moe, tm)


def kernel(x, c, positions, w_ada, b_ada, norm1_g, w_in, mu_prev, mu_next, rwkv_w0, rwkv_w2, rwkv_a0, rwkv_a2, rwkv_g2, rwkv_k_k, rwkv_k_a, rwkv_r_k, rwkv_gn_w, rwkv_gn_b, q_norm_g, k_norm_g, attn_sink, p_rwkv, p_attn, w_out, norm2_g, w_router, w_gate, w_up, w_down):
    B, T, D = x.shape
    depth = w_ada.shape[0]
    half = ROT // 2
    inv_freq = ROPE_THETA ** (-jnp.arange(0, ROT, 2, dtype=F32) / ROT)
    ang = positions.astype(F32)[..., None] * inv_freq
    cos8, sin8 = jnp.cos(ang), jnp.sin(ang)
    pad1 = jnp.ones((B, T, HD - ROT), F32)
    pad0 = jnp.zeros((B, T, HD - ROT), F32)
    cos_t = jnp.tile(jnp.concatenate([cos8, cos8, pad1], axis=-1), (1, 1, KV_HEADS))
    sin_t = jnp.tile(jnp.concatenate([-sin8, sin8, pad0], axis=-1), (1, 1, KV_HEADS))
    seg = jnp.arange(RW) // HD
    ones = (seg[:, None] == seg[None, :]).astype(BF16)
    idx = jnp.arange(128)
    tri = (idx[:, None] < idx[None, :]).astype(BF16)
    cos_q, sin_q = jnp.swapaxes(cos8, 1, 2), jnp.swapaxes(sin8, 1, 2)
    for l in range(depth):
        mod3 = _mod_call(c, w_ada[l], b_ada[l]).reshape(B, 6, D)
        x = _layer(x, mod3, cos_t, sin_t, cos_q, sin_q, ones, tri, norm1_g[l], w_in[l], mu_prev[l], mu_next[l],
                   rwkv_w0[l], rwkv_w2[l], rwkv_a0[l], rwkv_a2[l], rwkv_g2[l], rwkv_k_k[l], rwkv_k_a[l],
                   rwkv_r_k[l], rwkv_gn_w[l], rwkv_gn_b[l], q_norm_g[l], k_norm_g[l], attn_sink[l],
                   p_rwkv[l], p_attn[l], w_out[l], norm2_g[l], w_router[l], w_gate[l], w_up[l], w_down[l])
    return x
```

```python
import functools
import math

import jax
import jax.numpy as jnp
from jax import lax
from jax.experimental import pallas as pl
from jax.experimental.pallas import tpu as pltpu

F32 = jnp.float32
BF16 = jnp.bfloat16

D_MODEL = 1024
RW = 512
HEADS = 8
HD = 64
LORA_W = 64
LORA_A = 64
LORA_G = 128
GN_EPS = HD * 1e-5
KV_HEADS = 2
GROUP = HEADS // KV_HEADS
KV_W = KV_HEADS * HD
BLOCK = 128
ROPE_THETA = 500000.0
ROT = HD // 4
N_EXPERTS = 16
CAP_FACTOR = 2
NORM_EPS = 1e-6
RWKV_COLS = 3 * RW + 2 * LORA_W + 2 * LORA_A + LORA_G
Q0 = RWKV_COLS
K0 = Q0 + RW
V0 = K0 + KV_W
G0 = V0 + KV_W
IN_COLS = G0 + 2 * D_MODEL
CHUNK = 64
SUB = 16
SCAN_CHUNKS = 4
ATT_QB = 4
TM_ROWS = 512
TOKEN_RADIX = 64
BISECT_ITERS = 160
NEG = -1e30
LOG2E = math.log2(math.e)
QK_SCALE = HD ** -0.5 * LOG2E
VMEM_LIMIT = 56 * 1024 * 1024
MOE_VMEM_LIMIT = 60 * 1024 * 1024


def _dot(a, b):
    return jnp.dot(a, b, preferred_element_type=F32)


def _dot_nt(a, b):
    return lax.dot_general(a, b, (((1,), (1,)), ((), ())), preferred_element_type=F32)


def _dot_tn(a, b):
    return lax.dot_general(a, b, (((0,), (0,)), ((), ())), preferred_element_type=F32)


def _segsum(x, ones):
    xh = x.astype(BF16)
    xl = (x - xh.astype(F32)).astype(BF16)
    return _dot(xh, ones) + _dot(xl, ones)


def _params(sem):
    return pltpu.CompilerParams(dimension_semantics=sem, vmem_limit_bytes=VMEM_LIMIT)


def _mod_kernel(c_ref, w_ref, b_ref, o_ref):
    c = c_ref[...]
    ca = c * jax.nn.sigmoid(c)
    o_ref[...] = jnp.dot(ca, w_ref[...], preferred_element_type=F32,
                         precision=lax.Precision.HIGHEST) + b_ref[...]


def _mod_call(c, w_ada, b_ada):
    B, D = c.shape
    n = w_ada.shape[1] // D
    return pl.pallas_call(
        _mod_kernel,
        grid=(n,),
        in_specs=[pl.BlockSpec((B, D), lambda j: (0, 0)),
                  pl.BlockSpec((D, D), lambda j: (0, j)),
                  pl.BlockSpec((1, D), lambda j: (0, j))],
        out_specs=pl.BlockSpec((B, D), lambda j: (0, j)),
        out_shape=jax.ShapeDtypeStruct((B, n * D), F32),
        compiler_params=_params(("arbitrary",)),
        name="mod",
    )(c, w_ada, b_ada.reshape(1, -1))


def _rope(xn, cos, sin, width):
    lane = lax.broadcasted_iota(jnp.int32, xn.shape, 1) & (HD - 1)
    rot = jnp.where(lane < ROT // 2, pltpu.roll(xn, width - ROT // 2, 1), pltpu.roll(xn, ROT // 2, 1))
    return xn * cos + rot * sin


HALO = 8


def _inproj_kernel(x_ref, xp_ref, xn_ref, mod_ref, g_ref, w_ref, wq_ref, cos_ref, sin_ref, cosq_ref, sinq_ref,
                   qg_ref, kg_ref, ones8_ref,
                   mup_ref, mun_ref, w0_ref, w2_ref, a0_ref, a2_ref, g2_ref, kk_ref, ka_ref, rk_ref,
                   qt_ref, k_ref, v_ref, gl_ref,
                   r_o, v_o, kkn_o, g_o, bonus_o, lw0_o, lw1_o, k0_o, k1_o, b0_o, b1_o):
    i = pl.program_id(1)
    last = pl.num_programs(1) - 1
    tm = x_ref.shape[1]
    xx = jnp.concatenate([xp_ref[0], x_ref[0], xn_ref[0]], axis=0)
    ms = jnp.mean(xx * xx, axis=-1, keepdims=True)
    y = xx * lax.rsqrt(ms + NORM_EPS) * g_ref[...]
    u_all = y * (1.0 + mod_ref[0, 1:2, :]) + mod_ref[0, 0:1, :]
    ub = u_all[HALO:HALO + tm].astype(BF16)
    c_k, c_v, c_g = RWKV_COLS, RWKV_COLS + KV_W, RWKV_COLS + 2 * KV_W
    z_all = _dot(u_all.astype(BF16), w_ref[:, 0:c_k])
    k = _dot(ub, w_ref[:, c_k:c_v])
    v_ref[0] = _dot(ub, w_ref[:, c_v:c_g]).astype(BF16)
    gl_ref[0] = _dot(ub, w_ref[:, c_g:]).astype(BF16)
    ones8 = ones8_ref[...]
    kn = k * lax.rsqrt(_segsum(k * k, ones8[:KV_W, :KV_W]) * (1.0 / HD) + NORM_EPS) * kg_ref[...]
    k_ref[0] = _rope(kn, cos_ref[0], sin_ref[0], KV_W).astype(BF16)

    z = z_all[HALO:HALO + tm]
    row = lax.broadcasted_iota(jnp.int32, z.shape, 0)
    zp = jnp.where((row == 0) & (i == 0), 0.0, z_all[HALO - 1:HALO - 1 + tm])
    zn = jnp.where((row == tm - 1) & (i == last), 0.0, z_all[HALO + 1:HALO + 1 + tm])
    zs = z + mup_ref[...] * (zp - z) + mun_ref[...] * (zn - z)
    r = zs[:, 0:RW]
    kr = zs[:, RW:2 * RW]
    vr = zs[:, 2 * RW:3 * RW]
    c0 = 3 * RW
    wd = zs[:, c0:c0 + 2 * LORA_W]
    ad = zs[:, c0 + 2 * LORA_W:c0 + 2 * LORA_W + 2 * LORA_A]
    gd = zs[:, c0 + 2 * LORA_W + 2 * LORA_A:RWKV_COLS]
    wl = _dot(jnp.tanh(wd).astype(BF16), w2_ref[...]) + w0_ref[...]
    lw = -math.exp(-0.5) * jax.nn.sigmoid(wl)
    al = jax.nn.sigmoid(_dot(ad.astype(BF16), a2_ref[...]) + a0_ref[...])
    g = _dot(jax.nn.sigmoid(gd).astype(BF16), g2_ref[...])
    kk = kr * kk_ref[...]
    kkn = kk / jnp.maximum(jnp.sqrt(_segsum(kk * kk, ones8)), 1e-12)
    ka = ka_ref[...]
    a_0 = al[:, 0:RW]
    a_1 = al[:, RW:2 * RW]
    k_0 = kr * (1.0 + (a_0 - 1.0) * ka)
    k_1 = kr * (1.0 + (a_1 - 1.0) * ka)
    bonus = _segsum(r * (k_0 + k_1) * rk_ref[...], ones8) * vr
    r_o[0] = r.astype(BF16)
    v_o[0] = vr.astype(BF16)
    kkn_o[0] = kkn.astype(BF16)
    g_o[0] = g.astype(BF16)
    bonus_o[0] = bonus.astype(BF16)
    lw0_o[0] = lw[:, 0:RW]
    lw1_o[0] = lw[:, RW:2 * RW]
    k0_o[0] = k_0.astype(BF16)
    k1_o[0] = k_1.astype(BF16)
    b0_o[0] = (kkn * a_0).astype(BF16)
    b1_o[0] = (kkn * a_1).astype(BF16)

    qt = _dot_nt(wq_ref[...], ub)
    qg = jnp.concatenate([qg_ref[...]] * (tm // 128), axis=1)
    cq = cosq_ref[0]
    sq = sinq_ref[0]
    half = ROT // 2
    pieces = []
    for h in range(HEADS):
        xh = qt[h * HD:(h + 1) * HD]
        xn = (xh * lax.rsqrt(jnp.mean(xh * xh, axis=0, keepdims=True) + NORM_EPS)
              * qg[h * HD:(h + 1) * HD] * QK_SCALE)
        x1, x2 = xn[0:half], xn[half:ROT]
        pieces += [x1 * cq - x2 * sq, x2 * cq + x1 * sq, xn[ROT:]]
    qt_ref[0] = jnp.concatenate(pieces, axis=0).astype(BF16)


def _inproj_call(x, mod3, norm1_g, w_rest, wq_t, cos_t, sin_t, cos_q, sin_q, qg, kg, ones,
                 mup, mun, w0c, w2c, a0c, a2c, g2b, k_k, k_a, r_k, tm):
    B, T, D = x.shape
    nh = tm // HALO
    nth = T // HALO
    row = lambda b, i: (b, i, 0)
    col = lambda b, i: (b, 0, i)
    const2 = lambda b, i: (0, 0)
    half = ROT // 2
    rw = pl.BlockSpec((1, tm, RW), row)
    vec = lambda n: pl.BlockSpec((1, n), const2)
    return pl.pallas_call(
        _inproj_kernel,
        grid=(B, T // tm),
        in_specs=[pl.BlockSpec((1, tm, D), row),
                  pl.BlockSpec((1, HALO, D), lambda b, i: (b, jnp.maximum(i * nh - 1, 0), 0)),
                  pl.BlockSpec((1, HALO, D), lambda b, i: (b, jnp.minimum((i + 1) * nh, nth - 1), 0)),
                  pl.BlockSpec((1, 6, D), lambda b, i: (b, 0, 0)),
                  vec(D),
                  pl.BlockSpec(w_rest.shape, const2),
                  pl.BlockSpec((RW, D), const2),
                  pl.BlockSpec((1, tm, KV_W), row),
                  pl.BlockSpec((1, tm, KV_W), row),
                  pl.BlockSpec((1, half, tm), col),
                  pl.BlockSpec((1, half, tm), col),
                  pl.BlockSpec((RW, 128), const2),
                  vec(KV_W),
                  pl.BlockSpec((RW, RW), const2),
                  vec(RWKV_COLS), vec(RWKV_COLS),
                  vec(2 * RW), pl.BlockSpec((2 * LORA_W, 2 * RW), const2),
                  vec(2 * RW), pl.BlockSpec((2 * LORA_A, 2 * RW), const2),
                  pl.BlockSpec((LORA_G, RW), const2),
                  vec(RW), vec(RW), vec(RW)],
        out_specs=[pl.BlockSpec((1, RW, tm), col),
                   pl.BlockSpec((1, tm, KV_W), row),
                   pl.BlockSpec((1, tm, KV_W), row),
                   pl.BlockSpec((1, tm, 2 * D), row)] + [rw] * 11,
        out_shape=[jax.ShapeDtypeStruct((B, RW, T), BF16),
                   jax.ShapeDtypeStruct((B, T, KV_W), BF16),
                   jax.ShapeDtypeStruct((B, T, KV_W), BF16),
                   jax.ShapeDtypeStruct((B, T, 2 * D), BF16)]
        + [jax.ShapeDtypeStruct((B, T, RW), F32 if i in (5, 6) else BF16) for i in range(11)],
        compiler_params=_params(("parallel", "parallel")),
        name="inproj",
    )(x, x, x, mod3, norm1_g, w_rest, wq_t, cos_t, sin_t, cos_q, sin_q, qg, kg, ones,
      mup, mun, w0c, w2c, a0c, a2c, g2b, k_k, k_a, r_k)


def _dir_operands(rev, nch, r, v, kk, lw, kd, bd):
    n = r.shape[0]
    ti = lax.broadcasted_iota(jnp.int32, (n, n), 0)
    si = lax.broadcasted_iota(jnp.int32, (n, n), 1)
    same = (ti // CHUNK) == (si // CHUNK)
    incl = ((si >= ti) if rev else (si <= ti)) & same
    mi = incl.astype(BF16)
    l1 = lw.astype(BF16)
    r1 = lw - l1.astype(F32)
    l2 = r1.astype(BF16)
    l3 = (r1 - l2.astype(F32)).astype(BF16)
    cum = _dot(mi, l1) + _dot(mi, l2) + _dot(mi, l3)
    wt = jnp.exp(cum)
    winv = jnp.exp(-cum)
    ops = dict(ah=-(kk * jnp.exp(cum - lw)), rt=r * wt, bt=bd * winv, kt=kd * winv, v=v)
    wcs, bhs, khs = [], [], []
    for j in range(nch):
        last = j * CHUNK if rev else (j + 1) * CHUNK - 1
        wc = wt[last:last + 1, :]
        rows = slice(j * CHUNK, (j + 1) * CHUNK)
        wcs.append(wc)
        bhs.append(ops["bt"][rows] * wc)
        khs.append(ops["kt"][rows] * wc)
    return ops, wcs, bhs, khs


def _scan_kernel(rf, vf, kkf, lwf, kf, bf, rb, vb, kkb, lwb, kb, bb, yf_ref, yb_ref, st_ref, *, nch):
    @pl.when(pl.program_id(1) == 0)
    def _():
        st_ref[...] = jnp.zeros_like(st_ref)

    c = CHUNK
    pw = 2 * HD
    npair = HEADS // 2
    ti = lax.broadcasted_iota(jnp.int32, (c, pw), 0)
    li = lax.broadcasted_iota(jnp.int32, (c, pw), 1)
    si = li & (HD - 1)
    head0 = li < HD
    blk = (ti // SUB) == (si // SUB)
    eye = ti == si
    two = lambda m: jnp.concatenate([m, m], axis=1)
    masks = {False: (two(si < ti), two(si <= ti)), True: (two(si > ti), two(si >= ti))}

    def bd(y):
        yb = y.astype(BF16)
        zero = jnp.zeros_like(yb)
        return jnp.concatenate([jnp.where(head0, yb, zero), jnp.where(head0, zero, yb)], axis=0)

    def bdcat(*ys):
        return jnp.concatenate([bd(y) for y in ys], axis=1)

    def diag_blocks(full):
        return jnp.where(head0, full[:c], full[c:])

    dirs = [(False, rf, vf, kkf, lwf, kf, bf), (True, rb, vb, kkb, lwb, kb, bb)]
    units = []
    for d, (rev, r_, v_, kk_, lw_, k_, b_) in enumerate(dirs):
        f32 = lambda ref: ref[0].astype(F32)
        ops, wcs, bhs, khs = _dir_operands(rev, nch, f32(r_), f32(v_), f32(kk_), lw_[0], f32(k_), f32(b_))
        for j in range(nch):
            rows = slice(j * c, (j + 1) * c)
            for p in range(npair):
                sl = slice(p * pw, (p + 1) * pw)
                u = dict(d=d, j=j, p=p, rev=rev, wc=wcs[j][:, sl],
                         bkh=jnp.concatenate([bhs[j][:, sl], khs[j][:, sl]], axis=0).astype(BF16))
                for name in ("ah", "rt", "bt", "kt", "v"):
                    u[name] = ops[name][rows, sl]
                units.append(u)

    for u in units:
        x1 = jnp.concatenate([u["ah"], u["rt"]], axis=0).astype(BF16)
        u["lbk"] = _dot_nt(x1, jnp.concatenate([bd(u["bt"]), bd(u["kt"])], axis=0))
    for u in units:
        strict, incl = masks[u["rev"]]
        top = jnp.where(strict, u["lbk"][:c], 0.0)
        bot = jnp.where(incl, u["lbk"][c:], 0.0)
        lab = top[:, :pw]
        u["mrb"] = bot[:, :pw].astype(BF16)
        u["ld"] = jnp.where(blk, lab, 0.0)
        u["lo"] = lab - u["ld"]
        u["lm"] = jnp.concatenate([top[:, pw:], bot[:, pw:]], axis=0).astype(BF16)
    for u in units:
        u["a2"] = _dot(u["ld"].astype(BF16), bd(u["ld"]))
        u["lkv"] = _dot(u["lm"], bd(u["v"]))
    for u in units:
        s = _dot(jnp.concatenate([u["a2"], u["ld"]], axis=0).astype(BF16), bd(u["a2"]))
        u["a4"] = s[:c]
        u["t"] = jnp.where(eye, 1.0, 0.0) + u["ld"] + u["a2"] + s[c:]
    for u in units:
        s = _dot(jnp.concatenate([u["a4"], u["t"]], axis=0).astype(BF16), bd(u["a4"]))
        u["a8"] = s[:c]
        u["t"] = u["t"] + s[c:]
    for u in units:
        u["td"] = u["t"] + _dot(u["t"].astype(BF16), bd(u["a8"]))
    for u in units:
        x = _dot(u["td"].astype(BF16), bdcat(u["lo"], u["ah"], u["lkv"][:c]))
        u["n"] = x[:, :pw]
        u["z"] = x[:, pw:]
    for u in units:
        s = _dot(u["n"].astype(BF16), bdcat(u["n"], u["z"][:, :pw], u["z"][:, pw:]))
        u["n2"] = s[:, :pw]
        u["z"] = u["z"] + s[:, pw:]
    for u in units:
        u["z"] = u["z"] + _dot(u["n2"].astype(BF16), bdcat(u["z"][:, :pw], u["z"][:, pw:]))
    for u in units:
        z = u["z"]
        y10 = (jnp.concatenate([u["rt"], u["lkv"][c:]], axis=1)
               + _dot(u["mrb"], bdcat(z[:, :pw], z[:, pw:])))
        vpad = jnp.concatenate([jnp.zeros_like(u["v"]), u["v"]], axis=1)
        full = _dot_tn(u["bkh"], jnp.concatenate([z, vpad], axis=0).astype(BF16))
        g = diag_blocks(full[:, :pw]) + jnp.where(eye, u["wc"], 0.0)
        u["gy"] = jnp.concatenate([g, y10[:, :pw]], axis=0).astype(BF16)
        u["h0"] = diag_blocks(full[:, pw:])
        u["y0"] = y10[:, pw:]

    ys = {}
    for d in range(2):
        order = range(nch - 1, -1, -1) if d == 1 else range(nch)
        zs = [st_ref[d, p] for p in range(npair)]
        for j in order:
            for p in range(npair):
                u = units[(d * nch + j) * npair + p]
                m = _dot(u["gy"], bd(zs[p]))
                zs[p] = m[:c] + u["h0"]
                ys[(d, j, p)] = m[c:] + u["y0"]
        for p in range(npair):
            st_ref[d, p] = zs[p]
    for d, y_ref in enumerate((yf_ref, yb_ref)):
        y_ref[0] = jnp.concatenate(
            [jnp.concatenate([ys[(d, j, p)] for p in range(npair)], axis=-1) for j in range(nch)], axis=0)


def _scan_call(r, v, kkn, lw0, lw1, k0, k1, b0, b1, nch):
    B, T, _ = r.shape
    rows = nch * CHUNK
    ns = T // rows
    fwd = pl.BlockSpec((1, rows, RW), lambda b, c: (b, c, 0))
    bwd = pl.BlockSpec((1, rows, RW), lambda b, c: (b, ns - 1 - c, 0))
    return pl.pallas_call(
        functools.partial(_scan_kernel, nch=nch),
        grid=(B, ns),
        in_specs=[fwd] * 6 + [bwd] * 6,
        out_specs=[fwd, bwd],
        out_shape=[jax.ShapeDtypeStruct((B, T, RW), F32)] * 2,
        scratch_shapes=[pltpu.VMEM((2, HEADS // 2, HD, 2 * HD), F32)],
        compiler_params=_params(("parallel", "arbitrary")),
        name="scan",
    )(r, v, kkn, lw0, k0, b0, r, v, kkn, lw1, k1, b1)


def _attn_kernel(sink_ref, qt_ref, k_ref, v_ref, o_ref, *, qb):
    n = pl.program_id(1)
    t = k_ref.shape[1]
    band = 3 * BLOCK
    cols = GROUP * BLOCK
    ki = lax.broadcasted_iota(jnp.int32, (band, BLOCK), 0)
    qi = lax.broadcasted_iota(jnp.int32, (band, BLOCK), 1)
    lane_head = lax.broadcasted_iota(jnp.int32, (1, cols), 1) // BLOCK
    zeros = jnp.zeros((HD, cols), BF16)
    units = []
    for j in range(qb):
        blk = n * qb + j
        start = pl.multiple_of(jnp.clip((blk - 1) * BLOCK, 0, t - band), BLOCK)
        bias = jnp.where(jnp.abs(start + ki - (blk * BLOCK + qi)) <= BLOCK, 0.0, NEG)
        bias = jnp.concatenate([bias] * GROUP, axis=1)
        kb = k_ref[0, pl.ds(start, band), :]
        vb = v_ref[0, pl.ds(start, band), :]
        for g in range(KV_HEADS):
            qg = jnp.concatenate([qt_ref[0, (g * GROUP + i) * HD:(g * GROUP + i + 1) * HD,
                                         j * BLOCK:(j + 1) * BLOCK] for i in range(GROUP)], axis=1)
            rhs = jnp.concatenate([qg if gg == g else zeros for gg in range(KV_HEADS)], axis=0)
            sk = jnp.zeros((1, cols), F32)
            for i in range(GROUP):
                sk = jnp.where(lane_head == i, sink_ref[g * GROUP + i] * LOG2E, sk)
            units.append(dict(j=j, g=g, kb=kb, vb=vb, rhs=rhs, sk=sk, bias=bias))
    for u in units:
        u["s"] = _dot(u["kb"], u["rhs"]) + u["bias"]
    for u in units:
        u["m"] = jnp.maximum(jnp.max(u["s"], axis=0, keepdims=True), u["sk"])
    for u in units:
        p = jnp.exp2(u["s"] - u["m"])
        u["r"] = 1.0 / (jnp.sum(p, axis=0, keepdims=True) + jnp.exp2(u["sk"] - u["m"]))
        u["p"] = p.astype(BF16)
    for u in units:
        g = u["g"]
        u["o"] = _dot_tn(u["vb"], u["p"])[g * HD:(g + 1) * HD] * u["r"]
    for u in units:
        j, g = u["j"], u["g"]
        for i in range(GROUP):
            o_ref[0, (g * GROUP + i) * HD:(g * GROUP + i + 1) * HD, j * BLOCK:(j + 1) * BLOCK] = (
                u["o"][:, i * BLOCK:(i + 1) * BLOCK].astype(BF16))


def _attn_call(qt, k, v, sink, qb):
    B, _, T = qt.shape
    assert T >= 3 * BLOCK and T % (qb * BLOCK) == 0
    qspec = pl.BlockSpec((1, RW, qb * BLOCK), lambda b, n: (b, 0, n))
    kvspec = pl.BlockSpec((1, T, KV_W), lambda b, n: (b, 0, 0))
    return pl.pallas_call(
        functools.partial(_attn_kernel, qb=qb),
        grid=(B, T // (qb * BLOCK)),
        in_specs=[pl.BlockSpec(memory_space=pltpu.SMEM), qspec, kvspec, kvspec],
        out_specs=qspec,
        out_shape=jax.ShapeDtypeStruct((B, RW, T), BF16),
        compiler_params=_params(("parallel", "parallel")),
        name="attn",
    )(sink, qt, k, v)


def _merge_kernel(x_ref, mod_ref, yf_ref, yb_ref, bonus_ref, g_ref, yatt_ref, gl_ref,
                  gnw_ref, gnb_ref, ones_ref, pr_ref, pa_ref, wo_ref, n2g_ref, wr_ref,
                  x1_ref, u2_ref, aff_ref):
    ones = ones_ref[...]
    y = yf_ref[0] + yb_ref[0]
    mu = _segsum(y, ones) * (1.0 / HD)
    yc = y - mu
    var = _segsum(yc * yc, ones) * (1.0 / HD)
    yn = yc * lax.rsqrt(var + GN_EPS) * gnw_ref[...] + gnb_ref[...]
    ya = ((yn + bonus_ref[0].astype(F32)) * g_ref[0].astype(F32)).astype(BF16)
    pa = _dot(ya, pr_ref[...])
    pb = _dot_tn(yatt_ref[0], pa_ref[...])
    gates = jax.nn.sigmoid(gl_ref[0].astype(F32))
    m = gates[:, :D_MODEL] * pa + gates[:, D_MODEL:] * pb
    x1 = x_ref[0] + mod_ref[0, 2:3, :] * _dot(m.astype(BF16), wo_ref[...])
    x1_ref[0] = x1
    ms = jnp.mean(x1 * x1, axis=-1, keepdims=True)
    u2 = x1 * lax.rsqrt(ms + NORM_EPS) * n2g_ref[...] * (1.0 + mod_ref[0, 4:5, :]) + mod_ref[0, 3:4, :]
    u2_ref[0] = u2
    u2h = u2.astype(BF16)
    u2l = (u2 - u2h.astype(F32)).astype(BF16)
    wr = wr_ref[...]
    wrh = wr.astype(BF16)
    wrl = (wr - wrh.astype(F32)).astype(BF16)
    t1 = _dot_nt(jnp.concatenate([wrh, wrl], axis=0), u2h)
    logits = t1[:N_EXPERTS] + t1[N_EXPERTS:] + _dot_nt(wrh, u2l)
    e = jnp.exp(logits - jnp.max(logits, axis=0, keepdims=True))
    aff_ref[0] = e / jnp.sum(e, axis=0, keepdims=True)


def _merge_call(x, mod3, yf, yb, bonus, g, yatt, gl, gnw, gnb, ones, p_r, p_a, w_o, n2g, w_rt, tm):
    B, T, D = x.shape
    row = lambda b, i: (b, i, 0)
    const2 = lambda b, i: (0, 0)
    rw = pl.BlockSpec((1, tm, RW), row)
    return pl.pallas_call(
        _merge_kernel,
        grid=(B, T // tm),
        in_specs=[pl.BlockSpec((1, tm, D), row),
                  pl.BlockSpec((1, 6, D), lambda b, i: (b, 0, 0)),
                  rw, rw, rw, rw,
                  pl.BlockSpec((1, RW, tm), lambda b, i: (b, 0, i)),
                  pl.BlockSpec((1, tm, 2 * D), row),
                  pl.BlockSpec((1, RW), const2),
                  pl.BlockSpec((1, RW), const2),
                  pl.BlockSpec((RW, RW), const2),
                  pl.BlockSpec((RW, D), const2),
                  pl.BlockSpec((RW, D), const2),
                  pl.BlockSpec((D, D), const2),
                  pl.BlockSpec((1, D), const2),
                  pl.BlockSpec((N_EXPERTS, D), const2)],
        out_specs=[pl.BlockSpec((1, tm, D), row),
                   pl.BlockSpec((1, tm, D), row),
                   pl.BlockSpec((1, N_EXPERTS, tm), lambda b, i: (b, 0, i))],
        out_shape=[jax.ShapeDtypeStruct((B, T, D), F32),
                   jax.ShapeDtypeStruct((B, T, D), F32),
                   jax.ShapeDtypeStruct((B, N_EXPERTS, T), F32)],
        compiler_params=_params(("parallel", "parallel")),
        name="merge",
    )(x, mod3, yf, yb, bonus, g, yatt, gl, gnw, gnb, ones, p_r, p_a, w_o, n2g, w_rt)


def _excl_prefix(mask, tri):
    e, t = mask.shape
    mb = mask.astype(BF16)
    carry = jnp.zeros((e, 1), F32)
    outs = []
    for j in range(t // 128):
        tile = mb[:, j * 128:(j + 1) * 128]
        outs.append(_dot(tile, tri) + carry)
        carry = carry + jnp.sum(tile.astype(F32), axis=1, keepdims=True)
    return jnp.concatenate(outs, axis=-1)


def _route_kernel(aff_ref, tri_ref, dig_ref, slot_ref, idx_ref, *, cap):
    nb, ne, t = aff_ref.shape
    a = aff_ref[...].reshape(nb * ne, t)
    e = a.shape[0]
    capf = jnp.float32(cap)

    def cond(s):
        it, lo, hi = s
        mid = 0.5 * (lo + hi)
        still_open = jnp.max(jnp.where((mid > lo) & (mid < hi), 1.0, 0.0))
        return (it < BISECT_ITERS) & (still_open > 0.0)

    def body(s):
        it, lo, hi = s
        mid = 0.5 * (lo + hi)
        ge = jnp.sum((a >= mid).astype(F32), axis=1, keepdims=True) >= capf
        return it + 1, jnp.where(ge, mid, lo), jnp.where(ge, hi, mid)

    lo0 = jnp.zeros((e, 1), F32)
    hi0 = jnp.full((e, 1), 2.0, F32)
    _, lo, hi = lax.while_loop(cond, body, (jnp.int32(0), lo0, hi0))
    gt = a >= hi
    eq = (a >= lo) & (a < hi)
    need = capf - jnp.sum(gt.astype(F32), axis=1, keepdims=True)
    tri = tri_ref[...]
    sel = gt | (eq & (_excl_prefix(eq, tri) < need))
    pos = _excl_prefix(sel, tri)
    slot_ref[...] = jnp.where(sel, pos.astype(jnp.int32), -1).reshape(nb, ne, t)

    digits = dig_ref[...]
    assert cap <= 256
    slots = lax.broadcasted_iota(jnp.int32, (cap, t), 0).astype(BF16)
    one, zero = jnp.ones((cap, t), BF16), jnp.zeros((cap, t), BF16)

    def row_body(r, carry):
        srow = slot_ref[r // ne, pl.ds(r % ne, 1), :].astype(BF16)
        d = _dot_nt(digits, jnp.where(slots == srow, one, zero))
        idx_ref[r] = (d[0:1] * TOKEN_RADIX + d[1:2]).astype(jnp.int32)
        return carry

    lax.fori_loop(0, nb * ne, row_body, 0)


def _route_call(aff, tri, digits, cap):
    B, E, T = aff.shape
    return pl.pallas_call(
        functools.partial(_route_kernel, cap=cap),
        grid=(1,),
        in_specs=[pl.BlockSpec((B, E, T), lambda i: (0, 0, 0)),
                  pl.BlockSpec((128, 128), lambda i: (0, 0)),
                  pl.BlockSpec((8, T), lambda i: (0, 0))],
        out_specs=[pl.BlockSpec((B, E, T), lambda i: (0, 0, 0)),
                   pl.BlockSpec((B * E, 1, cap), lambda i: (0, 0, 0))],
        out_shape=[jax.ShapeDtypeStruct((B, E, T), jnp.int32),
                   jax.ShapeDtypeStruct((B * E, 1, cap), jnp.int32)],
        compiler_params=_params(("arbitrary",)),
        name="route",
    )(aff, tri, digits)


def _moe_kernel(idx0_ref, idx1_ref, u2_ref, slot_ref, aff_ref, wg_ref, wu_ref, wd_ref, o_ref,
                acc_ref, xa_ref, xb_ref, *, cap, tt):
    e = pl.program_id(1)
    t = u2_ref.shape[1]

    def gather(idx_ref, dst_ref):
        for i in range(cap):
            dst_ref[pl.ds(i, 1), :] = u2_ref[0, pl.ds(idx_ref[0, 0, i], 1), :]

    def expert(x_ref):
        xe = x_ref[...].astype(BF16)
        hg = _dot(xe, wg_ref[0])
        hu = _dot(xe, wu_ref[0])
        h = (hg * jax.nn.sigmoid(hg) * hu).astype(BF16)
        ye = _dot(h, wd_ref[0]).astype(BF16)
        hit = lax.broadcasted_iota(jnp.int32, (cap, t), 0) == slot_ref[0, pl.ds(e, 1), :]
        wsc = jnp.where(hit, aff_ref[0, pl.ds(e, 1), :], 0.0).astype(BF16)
        for j in range(t // tt):
            acc_ref[j * tt:(j + 1) * tt, :] += _dot_tn(wsc[:, j * tt:(j + 1) * tt], ye)

    @pl.when(e == 0)
    def _():
        acc_ref[...] = jnp.zeros_like(acc_ref)
        gather(idx0_ref, xa_ref)

    @pl.when(e % 2 == 0)
    def _():
        gather(idx1_ref, xb_ref)
        expert(xa_ref)

    @pl.when(e % 2 == 1)
    def _():
        gather(idx1_ref, xa_ref)
        expert(xb_ref)

    @pl.when(e == pl.num_programs(1) - 1)
    def _():
        o_ref[0] = acc_ref[...].astype(BF16)


def _moe_call(u2, slot, aff, idx, wg, wu, wd, cap):
    B, T, D = u2.shape
    E = wg.shape[0]
    F = wg.shape[2]
    assert E % 2 == 0
    tt = min(T, 512)
    smem_row = lambda im: pl.BlockSpec((1, 1, cap), im, memory_space=pltpu.SMEM)
    return pl.pallas_call(
        functools.partial(_moe_kernel, cap=cap, tt=tt),
        grid=(B, E),
        in_specs=[smem_row(lambda b, e: (b * E + e, 0, 0)),
                  smem_row(lambda b, e: (b * E + jnp.minimum(e + 1, E - 1), 0, 0)),
                  pl.BlockSpec((1, T, D), lambda b, e: (b, 0, 0)),
                  pl.BlockSpec((1, E, T), lambda b, e: (b, 0, 0)),
                  pl.BlockSpec((1, E, T), lambda b, e: (b, 0, 0)),
                  pl.BlockSpec((1, D, F), lambda b, e: (e, 0, 0)),
                  pl.BlockSpec((1, D, F), lambda b, e: (e, 0, 0)),
                  pl.BlockSpec((1, F, D), lambda b, e: (e, 0, 0))],
        out_specs=pl.BlockSpec((1, T, D), lambda b, e: (b, 0, 0)),
        out_shape=jax.ShapeDtypeStruct((B, T, D), BF16),
        scratch_shapes=[pltpu.VMEM((T, D), F32), pltpu.VMEM((cap, D), F32), pltpu.VMEM((cap, D), F32)],
        compiler_params=pltpu.CompilerParams(dimension_semantics=("parallel", "arbitrary"),
                                             vmem_limit_bytes=MOE_VMEM_LIMIT),
        name="moe",
    )(idx, idx, u2, slot, aff, wg, wu, wd)


def _final_kernel(x1_ref, mod_ref, moe_ref, o_ref):
    o_ref[0] = x1_ref[0] + mod_ref[0, 5:6, :] * moe_ref[0].astype(F32)


def _final_call(x1, mod3, moe, tm):
    B, T, D = x1.shape
    row = lambda b, i: (b, i, 0)
    return pl.pallas_call(
        _final_kernel,
        grid=(B, T // tm),
        in_specs=[pl.BlockSpec((1, tm, D), row),
                  pl.BlockSpec((1, 6, D), lambda b, i: (b, 0, 0)),
                  pl.BlockSpec((1, tm, D), row)],
        out_specs=pl.BlockSpec((1, tm, D), row),
        out_shape=jax.ShapeDtypeStruct((B, T, D), F32),
        compiler_params=_params(("parallel", "parallel")),
        name="final",
    )(x1, mod3, moe)


def _blockdiag2(w):
    z = jnp.zeros_like(w[0])
    return jnp.concatenate([jnp.concatenate([w[0], z], axis=1), jnp.concatenate([z, w[1]], axis=1)], axis=0)


def _layer(x, mod3, cos_t, sin_t, cos_q, sin_q, ones, tri, norm1_g, w_in, mu_prev, mu_next, rwkv_w0, rwkv_w2, rwkv_a0,
           rwkv_a2, rwkv_g2, rwkv_k_k, rwkv_k_a, rwkv_r_k, rwkv_gn_w, rwkv_gn_b, q_norm_g, k_norm_g,
           attn_sink, p_rwkv, p_attn, w_out, norm2_g, w_router, w_gate, w_up, w_down):
    B, T, D = x.shape
    tm = TM_ROWS if T % TM_ROWS == 0 else min(T, 256)
    cap = CAP_FACTOR * T // N_EXPERTS
    row = lambda a: a.reshape(1, -1)
    w_b = w_in.astype(BF16)
    w_rest = jnp.concatenate([w_b[:, :Q0], w_b[:, K0:V0], w_b[:, V0:G0], w_b[:, G0:]], axis=1)
    qg = jnp.broadcast_to(jnp.tile(q_norm_g, HEADS)[:, None], (RW, 128))
    qt, k, v, gl, r, vv, kkn, g, bonus, lw0, lw1, k0, k1, b0, b1 = _inproj_call(
        x, mod3, row(norm1_g), w_rest, w_b[:, Q0:K0].T, cos_t, sin_t, cos_q, sin_q,
        qg, row(jnp.tile(k_norm_g, KV_HEADS)), ones,
        row(mu_prev), row(mu_next), row(rwkv_w0), _blockdiag2(rwkv_w2).astype(BF16),
        row(rwkv_a0), _blockdiag2(rwkv_a2).astype(BF16), rwkv_g2.astype(BF16),
        row(rwkv_k_k), row(rwkv_k_a), row(rwkv_r_k), tm)
    yf, yb = _scan_call(r, vv, kkn, lw0, lw1, k0, k1, b0, b1, SCAN_CHUNKS)
    yatt = _attn_call(qt, k, v, attn_sink, ATT_QB)
    x1, u2, aff = _merge_call(x, mod3, yf, yb, bonus, g, yatt, gl, row(rwkv_gn_w), row(rwkv_gn_b), ones,
                              p_rwkv.astype(BF16), p_attn.astype(BF16), w_out.astype(BF16),
                              row(norm2_g), w_router.T, tm)
    assert T <= 256 * TOKEN_RADIX
    tok = jnp.arange(T)
    digits = jnp.zeros((8, T), BF16).at[0].set((tok // TOKEN_RADIX).astype(BF16)).at[1].set(
        (tok % TOKEN_RADIX).astype(BF16))
    slot, idx = _route_call(aff, tri, digits, cap)
    moe = _moe_call(u2, slot, aff, idx, w_gate.astype(BF16), w_up.astype(BF16), w_down.astype(BF16), cap)
    return _final_call(x1, mod3, moe, tm)


def kernel(x, c, positions, w_ada, b_ada, norm1_g, w_in, mu_prev, mu_next, rwkv_w0, rwkv_w2, rwkv_a0, rwkv_a2, rwkv_g2, rwkv_k_k, rwkv_k_a, rwkv_r_k, rwkv_gn_w, rwkv_gn_b, q_norm_g, k_norm_g, attn_sink, p_rwkv, p_attn, w_out, norm2_g, w_router, w_gate, w_up, w_down):
    B, T, D = x.shape
    depth = w_ada.shape[0]
    half = ROT // 2
    inv_freq = ROPE_THETA ** (-jnp.arange(0, ROT, 2, dtype=F32) / ROT)
    ang = positions.astype(F32)[..., None] * inv_freq
    cos8, sin8 = jnp.cos(ang), jnp.sin(ang)
    pad1 = jnp.ones((B, T, HD - ROT), F32)
    pad0 = jnp.zeros((B, T, HD - ROT), F32)
    cos_t = jnp.tile(jnp.concatenate([cos8, cos8, pad1], axis=-1), (1, 1, KV_HEADS))
    sin_t = jnp.tile(jnp.concatenate([-sin8, sin8, pad0], axis=-1), (1, 1, KV_HEADS))
    seg = jnp.arange(RW) // HD
    ones = (seg[:, None] == seg[None, :]).astype(BF16)
    idx = jnp.arange(128)
    tri = (idx[:, None] < idx[None, :]).astype(BF16)
    cos_q, sin_q = jnp.swapaxes(cos8, 1, 2), jnp.swapaxes(sin8, 1, 2)
    for l in range(depth):
        mod3 = _mod_call(c, w_ada[l], b_ada[l]).reshape(B, 6, D)
        x = _layer(x, mod3, cos_t, sin_t, cos_q, sin_q, ones, tri, norm1_g[l], w_in[l], mu_prev[l], mu_next[l],
                   rwkv_w0[l], rwkv_w2[l], rwkv_a0[l], rwkv_a2[l], rwkv_g2[l], rwkv_k_k[l], rwkv_k_a[l],
                   rwkv_r_k[l], rwkv_gn_w[l], rwkv_gn_b[l], q_norm_g[l], k_norm_g[l], attn_sink[l],
                   p_rwkv[l], p_attn[l], w_out[l], norm2_g[l], w_router[l], w_gate[l], w_up[l], w_down[l])
    return x
```

```python
import functools
import math

import jax
import jax.numpy as jnp
from jax import lax
from jax.experimental import pallas as pl
from jax.experimental.pallas import tpu as pltpu

F32 = jnp.float32
BF16 = jnp.bfloat16

D_MODEL = 1024
RW = 512
HEADS = 8
HD = 64
LORA_W = 64
LORA_A = 64
LORA_G = 128
GN_EPS = HD * 1e-5
KV_HEADS = 2
GROUP = HEADS // KV_HEADS
KV_W = KV_HEADS * HD
BLOCK = 128
ROPE_THETA = 500000.0
ROT = HD // 4
N_EXPERTS = 16
CAP_FACTOR = 2
NORM_EPS = 1e-6
RWKV_COLS = 3 * RW + 2 * LORA_W + 2 * LORA_A + LORA_G
Q0 = RWKV_COLS
K0 = Q0 + RW
V0 = K0 + KV_W
G0 = V0 + KV_W
IN_COLS = G0 + 2 * D_MODEL
CHUNK = 64
SUB = 16
SCAN_CHUNKS = 4
ATT_QB = 4
TM_ROWS = 512
TOKEN_RADIX = 64
BISECT_ITERS = 160
NEG = -1e30
LOG2E = math.log2(math.e)
QK_SCALE = HD ** -0.5 * LOG2E
VMEM_LIMIT = 56 * 1024 * 1024
MOE_VMEM_LIMIT = 60 * 1024 * 1024


def _dot(a, b):
    return jnp.dot(a, b, preferred_element_type=F32)


def _dot_nt(a, b):
    return lax.dot_general(a, b, (((1,), (1,)), ((), ())), preferred_element_type=F32)


def _dot_tn(a, b):
    return lax.dot_general(a, b, (((0,), (0,)), ((), ())), preferred_element_type=F32)


def _segsum(x, ones):
    xh = x.astype(BF16)
    xl = (x - xh.astype(F32)).astype(BF16)
    return _dot(xh, ones) + _dot(xl, ones)


def _params(sem):
    return pltpu.CompilerParams(dimension_semantics=sem, vmem_limit_bytes=VMEM_LIMIT)


def _mod_kernel(c_ref, w_ref, b_ref, o_ref):
    c = c_ref[...]
    ca = c * jax.nn.sigmoid(c)
    o_ref[...] = jnp.dot(ca, w_ref[...], preferred_element_type=F32,
                         precision=lax.Precision.HIGHEST) + b_ref[...]


def _mod_call(c, w_ada, b_ada):
    B, D = c.shape
    n = w_ada.shape[1] // D
    return pl.pallas_call(
        _mod_kernel,
        grid=(n,),
        in_specs=[pl.BlockSpec((B, D), lambda j: (0, 0)),
                  pl.BlockSpec((D, D), lambda j: (0, j)),
                  pl.BlockSpec((1, D), lambda j: (0, j))],
        out_specs=pl.BlockSpec((B, D), lambda j: (0, j)),
        out_shape=jax.ShapeDtypeStruct((B, n * D), F32),
        compiler_params=_params(("arbitrary",)),
        name="mod",
    )(c, w_ada, b_ada.reshape(1, -1))


def _rope(xn, cos, sin, width):
    lane = lax.broadcasted_iota(jnp.int32, xn.shape, 1) & (HD - 1)
    rot = jnp.where(lane < ROT // 2, pltpu.roll(xn, width - ROT // 2, 1), pltpu.roll(xn, ROT // 2, 1))
    return xn * cos + rot * sin


HALO = 8


def _inproj_kernel(x_ref, xp_ref, xn_ref, mod_ref, g_ref, w_ref, wq_ref, cos_ref, sin_ref, cosq_ref, sinq_ref,
                   qg_ref, kg_ref, ones8_ref,
                   mup_ref, mun_ref, w0_ref, w2_ref, a0_ref, a2_ref, g2_ref, kk_ref, ka_ref, rk_ref,
                   qt_ref, k_ref, v_ref, gl_ref,
                   r_o, v_o, kkn_o, g_o, bonus_o, lw0_o, lw1_o, k0_o, k1_o, b0_o, b1_o):
    i = pl.program_id(1)
    last = pl.num_programs(1) - 1
    tm = x_ref.shape[1]
    xx = jnp.concatenate([xp_ref[0], x_ref[0], xn_ref[0]], axis=0)
    ms = jnp.mean(xx * xx, axis=-1, keepdims=True)
    y = xx * lax.rsqrt(ms + NORM_EPS) * g_ref[...]
    u_all = y * (1.0 + mod_ref[0, 1:2, :]) + mod_ref[0, 0:1, :]
    ub = u_all[HALO:HALO + tm].astype(BF16)
    z_all = _dot(u_all.astype(BF16), w_ref[:, 0:RWKV_COLS])
    k = _dot(ub, w_ref[:, K0:V0])
    v_ref[0] = _dot(ub, w_ref[:, V0:G0]).astype(BF16)
    gl_ref[0] = _dot(ub, w_ref[:, G0:IN_COLS]).astype(BF16)
    ones8 = ones8_ref[...]
    kn = k * lax.rsqrt(_segsum(k * k, ones8[:KV_W, :KV_W]) * (1.0 / HD) + NORM_EPS) * kg_ref[...]
    k_ref[0] = _rope(kn, cos_ref[0], sin_ref[0], KV_W).astype(BF16)

    z = z_all[HALO:HALO + tm]
    row = lax.broadcasted_iota(jnp.int32, z.shape, 0)
    zp = jnp.where((row == 0) & (i == 0), 0.0, z_all[HALO - 1:HALO - 1 + tm])
    zn = jnp.where((row == tm - 1) & (i == last), 0.0, z_all[HALO + 1:HALO + 1 + tm])
    zs = z + mup_ref[...] * (zp - z) + mun_ref[...] * (zn - z)
    r = zs[:, 0:RW]
    kr = zs[:, RW:2 * RW]
    vr = zs[:, 2 * RW:3 * RW]
    c0 = 3 * RW
    wd = zs[:, c0:c0 + 2 * LORA_W]
    ad = zs[:, c0 + 2 * LORA_W:c0 + 2 * LORA_W + 2 * LORA_A]
    gd = zs[:, c0 + 2 * LORA_W + 2 * LORA_A:RWKV_COLS]
    wl = _dot(jnp.tanh(wd).astype(BF16), w2_ref[...]) + w0_ref[...]
    lw = -math.exp(-0.5) * jax.nn.sigmoid(wl)
    al = jax.nn.sigmoid(_dot(ad.astype(BF16), a2_ref[...]) + a0_ref[...])
    g = _dot(jax.nn.sigmoid(gd).astype(BF16), g2_ref[...])
    kk = kr * kk_ref[...]
    kkn = kk / jnp.maximum(jnp.sqrt(_segsum(kk * kk, ones8)), 1e-12)
    ka = ka_ref[...]
    a_0 = al[:, 0:RW]
    a_1 = al[:, RW:2 * RW]
    k_0 = kr * (1.0 + (a_0 - 1.0) * ka)
    k_1 = kr * (1.0 + (a_1 - 1.0) * ka)
    bonus = _segsum(r * (k_0 + k_1) * rk_ref[...], ones8) * vr
    r_o[0] = r.astype(BF16)
    v_o[0] = vr.astype(BF16)
    kkn_o[0] = kkn.astype(BF16)
    g_o[0] = g.astype(BF16)
    bonus_o[0] = bonus.astype(BF16)
    lw0_o[0] = lw[:, 0:RW]
    lw1_o[0] = lw[:, RW:2 * RW]
    k0_o[0] = k_0.astype(BF16)
    k1_o[0] = k_1.astype(BF16)
    b0_o[0] = (kkn * a_0).astype(BF16)
    b1_o[0] = (kkn * a_1).astype(BF16)

    qt = _dot_nt(wq_ref[...], ub)
    qg = jnp.concatenate([qg_ref[...]] * (tm // 128), axis=1)
    cq = cosq_ref[0]
    sq = sinq_ref[0]
    half = ROT // 2
    pieces = []
    for h in range(HEADS):
        xh = qt[h * HD:(h + 1) * HD]
        xn = (xh * lax.rsqrt(jnp.mean(xh * xh, axis=0, keepdims=True) + NORM_EPS)
              * qg[h * HD:(h + 1) * HD] * QK_SCALE)
        x1, x2 = xn[0:half], xn[half:ROT]
        pieces += [x1 * cq - x2 * sq, x2 * cq + x1 * sq, xn[ROT:]]
    qt_ref[0] = jnp.concatenate(pieces, axis=0).astype(BF16)


def _inproj_call(x, mod3, norm1_g, w_rest, wq_t, cos_t, sin_t, cos_q, sin_q, qg, kg, ones,
                 mup, mun, w0c, w2c, a0c, a2c, g2b, k_k, k_a, r_k, tm):
    B, T, D = x.shape
    nh = tm // HALO
    nth = T // HALO
    row = lambda b, i: (b, i, 0)
    col = lambda b, i: (b, 0, i)
    const2 = lambda b, i: (0, 0)
    half = ROT // 2
    rw = pl.BlockSpec((1, tm, RW), row)
    vec = lambda n: pl.BlockSpec((1, n), const2)
    return pl.pallas_call(
        _inproj_kernel,
        grid=(B, T // tm),
        in_specs=[pl.BlockSpec((1, tm, D), row),
                  pl.BlockSpec((1, HALO, D), lambda b, i: (b, jnp.maximum(i * nh - 1, 0), 0)),
                  pl.BlockSpec((1, HALO, D), lambda b, i: (b, jnp.minimum((i + 1) * nh, nth - 1), 0)),
                  pl.BlockSpec((1, 6, D), lambda b, i: (b, 0, 0)),
                  vec(D),
                  pl.BlockSpec(w_rest.shape, const2),
                  pl.BlockSpec((RW, D), const2),
                  pl.BlockSpec((1, tm, KV_W), row),
                  pl.BlockSpec((1, tm, KV_W), row),
                  pl.BlockSpec((1, half, tm), col),
                  pl.BlockSpec((1, half, tm), col),
                  pl.BlockSpec((RW, 128), const2),
                  vec(KV_W),
                  pl.BlockSpec((RW, RW), const2),
                  vec(RWKV_COLS), vec(RWKV_COLS),
                  vec(2 * RW), pl.BlockSpec((2 * LORA_W, 2 * RW), const2),
                  vec(2 * RW), pl.BlockSpec((2 * LORA_A, 2 * RW), const2),
                  pl.BlockSpec((LORA_G, RW), const2),
                  vec(RW), vec(RW), vec(RW)],
        out_specs=[pl.BlockSpec((1, RW, tm), col),
                   pl.BlockSpec((1, tm, KV_W), row),
                   pl.BlockSpec((1, tm, KV_W), row),
                   pl.BlockSpec((1, tm, 2 * D), row)] + [rw] * 11,
        out_shape=[jax.ShapeDtypeStruct((B, RW, T), BF16),
                   jax.ShapeDtypeStruct((B, T, KV_W), BF16),
                   jax.ShapeDtypeStruct((B, T, KV_W), BF16),
                   jax.ShapeDtypeStruct((B, T, 2 * D), BF16)]
        + [jax.ShapeDtypeStruct((B, T, RW), F32 if i in (5, 6) else BF16) for i in range(11)],
        compiler_params=_params(("parallel", "parallel")),
        name="inproj",
    )(x, x, x, mod3, norm1_g, w_rest, wq_t, cos_t, sin_t, cos_q, sin_q, qg, kg, ones,
      mup, mun, w0c, w2c, a0c, a2c, g2b, k_k, k_a, r_k)


def _dir_operands(rev, nch, r, v, kk, lw, kd, bd):
    n = r.shape[0]
    ti = lax.broadcasted_iota(jnp.int32, (n, n), 0)
    si = lax.broadcasted_iota(jnp.int32, (n, n), 1)
    same = (ti // CHUNK) == (si // CHUNK)
    incl = ((si >= ti) if rev else (si <= ti)) & same
    mi = incl.astype(BF16)
    l1 = lw.astype(BF16)
    r1 = lw - l1.astype(F32)
    l2 = r1.astype(BF16)
    l3 = (r1 - l2.astype(F32)).astype(BF16)
    cum = _dot(mi, l1) + _dot(mi, l2) + _dot(mi, l3)
    wt = jnp.exp(cum)
    winv = jnp.exp(-cum)
    ops = dict(ah=-(kk * jnp.exp(cum - lw)), rt=r * wt, bt=bd * winv, kt=kd * winv, v=v)
    wcs, bhs, khs = [], [], []
    for j in range(nch):
        last = j * CHUNK if rev else (j + 1) * CHUNK - 1
        wc = wt[last:last + 1, :]
        rows = slice(j * CHUNK, (j + 1) * CHUNK)
        wcs.append(wc)
        bhs.append(ops["bt"][rows] * wc)
        khs.append(ops["kt"][rows] * wc)
    return ops, wcs, bhs, khs


def _scan_kernel(rf, vf, kkf, lwf, kf, bf, rb, vb, kkb, lwb, kb, bb, yf_ref, yb_ref, st_ref, *, nch):
    @pl.when(pl.program_id(1) == 0)
    def _():
        st_ref[...] = jnp.zeros_like(st_ref)

    c = CHUNK
    pw = 2 * HD
    npair = HEADS // 2
    ti = lax.broadcasted_iota(jnp.int32, (c, pw), 0)
    li = lax.broadcasted_iota(jnp.int32, (c, pw), 1)
    si = li & (HD - 1)
    head0 = li < HD
    blk = (ti // SUB) == (si // SUB)
    eye = ti == si
    two = lambda m: jnp.concatenate([m, m], axis=1)
    masks = {False: (two(si < ti), two(si <= ti)), True: (two(si > ti), two(si >= ti))}

    def bd(y):
        yb = y.astype(BF16)
        zero = jnp.zeros_like(yb)
        return jnp.concatenate([jnp.where(head0, yb, zero), jnp.where(head0, zero, yb)], axis=0)

    def bdcat(*ys):
        return jnp.concatenate([bd(y) for y in ys], axis=1)

    def diag_blocks(full):
        return jnp.where(head0, full[:c], full[c:])

    dirs = [(False, rf, vf, kkf, lwf, kf, bf), (True, rb, vb, kkb, lwb, kb, bb)]
    units = []
    for d, (rev, r_, v_, kk_, lw_, k_, b_) in enumerate(dirs):
        f32 = lambda ref: ref[0].astype(F32)
        ops, wcs, bhs, khs = _dir_operands(rev, nch, f32(r_), f32(v_), f32(kk_), lw_[0], f32(k_), f32(b_))
        for j in range(nch):
            rows = slice(j * c, (j + 1) * c)
            for p in range(npair):
                sl = slice(p * pw, (p + 1) * pw)
                u = dict(d=d, j=j, p=p, rev=rev, wc=wcs[j][:, sl],
                         bkh=jnp.concatenate([bhs[j][:, sl], khs[j][:, sl]], axis=0).astype(BF16))
                for name in ("ah", "rt", "bt", "kt", "v"):
                    u[name] = ops[name][rows, sl]
                units.append(u)

    for u in units:
        x1 = jnp.concatenate([u["ah"], u["rt"]], axis=0).astype(BF16)
        u["lbk"] = _dot_nt(x1, jnp.concatenate([bd(u["bt"]), bd(u["kt"])], axis=0))
    for u in units:
        strict, incl = masks[u["rev"]]
        top = jnp.where(strict, u["lbk"][:c], 0.0)
        bot = jnp.where(incl, u["lbk"][c:], 0.0)
        lab = top[:, :pw]
        u["mrb"] = bot[:, :pw].astype(BF16)
        u["ld"] = jnp.where(blk, lab, 0.0)
        u["lo"] = lab - u["ld"]
        u["lm"] = jnp.concatenate([top[:, pw:], bot[:, pw:]], axis=0).astype(BF16)
    for u in units:
        u["a2"] = _dot(u["ld"].astype(BF16), bd(u["ld"]))
        u["lkv"] = _dot(u["lm"], bd(u["v"]))
    for u in units:
        s = _dot(jnp.concatenate([u["a2"], u["ld"]], axis=0).astype(BF16), bd(u["a2"]))
        u["a4"] = s[:c]
        u["t"] = jnp.where(eye, 1.0, 0.0) + u["ld"] + u["a2"] + s[c:]
    for u in units:
        s = _dot(jnp.concatenate([u["a4"], u["t"]], axis=0).astype(BF16), bd(u["a4"]))
        u["a8"] = s[:c]
        u["t"] = u["t"] + s[c:]
    for u in units:
        u["td"] = u["t"] + _dot(u["t"].astype(BF16), bd(u["a8"]))
    for u in units:
        x = _dot(u["td"].astype(BF16), bdcat(u["lo"], u["ah"], u["lkv"][:c]))
        u["n"] = x[:, :pw]
        u["z"] = x[:, pw:]
    for u in units:
        s = _dot(u["n"].astype(BF16), bdcat(u["n"], u["z"][:, :pw], u["z"][:, pw:]))
        u["n2"] = s[:, :pw]
        u["z"] = u["z"] + s[:, pw:]
    for u in units:
        u["z"] = u["z"] + _dot(u["n2"].astype(BF16), bdcat(u["z"][:, :pw], u["z"][:, pw:]))
    for u in units:
        z = u["z"]
        y10 = (jnp.concatenate([u["rt"], u["lkv"][c:]], axis=1)
               + _dot(u["mrb"], bdcat(z[:, :pw], z[:, pw:])))
        vpad = jnp.concatenate([jnp.zeros_like(u["v"]), u["v"]], axis=1)
        full = _dot_tn(u["bkh"], jnp.concatenate([z, vpad], axis=0).astype(BF16))
        g = diag_blocks(full[:, :pw]) + jnp.where(eye, u["wc"], 0.0)
        u["gy"] = jnp.concatenate([g, y10[:, :pw]], axis=0).astype(BF16)
        u["h0"] = diag_blocks(full[:, pw:])
        u["y0"] = y10[:, pw:]

    ys = {}
    for d in range(2):
        order = range(nch - 1, -1, -1) if d == 1 else range(nch)
        zs = [st_ref[d, p] for p in range(npair)]
        for j in order:
            for p in range(npair):
                u = units[(d * nch + j) * npair + p]
                m = _dot(u["gy"], bd(zs[p]))
                zs[p] = m[:c] + u["h0"]
                ys[(d, j, p)] = m[c:] + u["y0"]
        for p in range(npair):
            st_ref[d, p] = zs[p]
    for d, y_ref in enumerate((yf_ref, yb_ref)):
        y_ref[0] = jnp.concatenate(
            [jnp.concatenate([ys[(d, j, p)] for p in range(npair)], axis=-1) for j in range(nch)],
            axis=0).astype(BF16)


def _scan_call(r, v, kkn, lw0, lw1, k0, k1, b0, b1, nch):
    B, T, _ = r.shape
    rows = nch * CHUNK
    ns = T // rows
    fwd = pl.BlockSpec((1, rows, RW), lambda b, c: (b, c, 0))
    bwd = pl.BlockSpec((1, rows, RW), lambda b, c: (b, ns - 1 - c, 0))
    return pl.pallas_call(
        functools.partial(_scan_kernel, nch=nch),
        grid=(B, ns),
        in_specs=[fwd] * 6 + [bwd] * 6,
        out_specs=[fwd, bwd],
        out_shape=[jax.ShapeDtypeStruct((B, T, RW), BF16)] * 2,
        scratch_shapes=[pltpu.VMEM((2, HEADS // 2, HD, 2 * HD), F32)],
        compiler_params=_params(("parallel", "arbitrary")),
        name="scan",
    )(r, v, kkn, lw0, k0, b0, r, v, kkn, lw1, k1, b1)


def _attn_kernel(sink_ref, qt_ref, k_ref, v_ref, o_ref, *, qb):
    n = pl.program_id(1)
    t = k_ref.shape[1]
    band = 3 * BLOCK
    cols = GROUP * BLOCK
    ki = lax.broadcasted_iota(jnp.int32, (band, BLOCK), 0)
    qi = lax.broadcasted_iota(jnp.int32, (band, BLOCK), 1)
    lane_head = lax.broadcasted_iota(jnp.int32, (1, cols), 1) // BLOCK
    zeros = jnp.zeros((HD, cols), BF16)
    units = []
    for j in range(qb):
        blk = n * qb + j
        start = pl.multiple_of(jnp.clip((blk - 1) * BLOCK, 0, t - band), BLOCK)
        bias = jnp.where(jnp.abs(start + ki - (blk * BLOCK + qi)) <= BLOCK, 0.0, NEG)
        bias = jnp.concatenate([bias] * GROUP, axis=1)
        kb = k_ref[0, pl.ds(start, band), :]
        vb = v_ref[0, pl.ds(start, band), :]
        for g in range(KV_HEADS):
            qg = jnp.concatenate([qt_ref[0, (g * GROUP + i) * HD:(g * GROUP + i + 1) * HD,
                                         j * BLOCK:(j + 1) * BLOCK] for i in range(GROUP)], axis=1)
            rhs = jnp.concatenate([qg if gg == g else zeros for gg in range(KV_HEADS)], axis=0)
            sk = jnp.zeros((1, cols), F32)
            for i in range(GROUP):
                sk = jnp.where(lane_head == i, sink_ref[g * GROUP + i] * LOG2E, sk)
            units.append(dict(j=j, g=g, kb=kb, vb=vb, rhs=rhs, sk=sk, bias=bias))
    for u in units:
        u["s"] = _dot(u["kb"], u["rhs"]) + u["bias"]
    for u in units:
        u["m"] = jnp.maximum(jnp.max(u["s"], axis=0, keepdims=True), u["sk"])
    for u in units:
        p = jnp.exp2(u["s"] - u["m"])
        u["r"] = 1.0 / (jnp.sum(p, axis=0, keepdims=True) + jnp.exp2(u["sk"] - u["m"]))
        u["p"] = p.astype(BF16)
    for u in units:
        g = u["g"]
        u["o"] = _dot_tn(u["vb"], u["p"])[g * HD:(g + 1) * HD] * u["r"]
    for u in units:
        j, g = u["j"], u["g"]
        for i in range(GROUP):
            o_ref[0, (g * GROUP + i) * HD:(g * GROUP + i + 1) * HD, j * BLOCK:(j + 1) * BLOCK] = (
                u["o"][:, i * BLOCK:(i + 1) * BLOCK].astype(BF16))


def _attn_call(qt, k, v, sink, qb):
    B, _, T = qt.shape
    assert T >= 3 * BLOCK and T % (qb * BLOCK) == 0
    qspec = pl.BlockSpec((1, RW, qb * BLOCK), lambda b, n: (b, 0, n))
    kvspec = pl.BlockSpec((1, T, KV_W), lambda b, n: (b, 0, 0))
    return pl.pallas_call(
        functools.partial(_attn_kernel, qb=qb),
        grid=(B, T // (qb * BLOCK)),
        in_specs=[pl.BlockSpec(memory_space=pltpu.SMEM), qspec, kvspec, kvspec],
        out_specs=qspec,
        out_shape=jax.ShapeDtypeStruct((B, RW, T), BF16),
        compiler_params=_params(("parallel", "parallel")),
        name="attn",
    )(sink, qt, k, v)


def _merge_kernel(x_ref, mod_ref, yf_ref, yb_ref, bonus_ref, g_ref, yatt_ref, gl_ref,
                  gnw_ref, gnb_ref, ones_ref, pr_ref, pa_ref, wo_ref, n2g_ref, wr_ref,
                  x1_ref, u2_ref, aff_ref):
    ones = ones_ref[...]
    y = yf_ref[0].astype(F32) + yb_ref[0].astype(F32)
    mu = _segsum(y, ones) * (1.0 / HD)
    yc = y - mu
    var = _segsum(yc * yc, ones) * (1.0 / HD)
    yn = yc * lax.rsqrt(var + GN_EPS) * gnw_ref[...] + gnb_ref[...]
    ya = ((yn + bonus_ref[0].astype(F32)) * g_ref[0].astype(F32)).astype(BF16)
    pa = _dot(ya, pr_ref[...])
    pb = _dot_tn(yatt_ref[0], pa_ref[...])
    gates = jax.nn.sigmoid(gl_ref[0].astype(F32))
    m = gates[:, :D_MODEL] * pa + gates[:, D_MODEL:] * pb
    x1 = x_ref[0] + mod_ref[0, 2:3, :] * _dot(m.astype(BF16), wo_ref[...])
    x1_ref[0] = x1
    ms = jnp.mean(x1 * x1, axis=-1, keepdims=True)
    u2 = x1 * lax.rsqrt(ms + NORM_EPS) * n2g_ref[...] * (1.0 + mod_ref[0, 4:5, :]) + mod_ref[0, 3:4, :]
    u2_ref[0] = u2
    u2h = u2.astype(BF16)
    u2l = (u2 - u2h.astype(F32)).astype(BF16)
    wr = wr_ref[...]
    wrh = wr.astype(BF16)
    wrl = (wr - wrh.astype(F32)).astype(BF16)
    t1 = _dot_nt(jnp.concatenate([wrh, wrl], axis=0), u2h)
    logits = t1[:N_EXPERTS] + t1[N_EXPERTS:] + _dot_nt(wrh, u2l)
    e = jnp.exp(logits - jnp.max(logits, axis=0, keepdims=True))
    aff_ref[0] = e / jnp.sum(e, axis=0, keepdims=True)


def _merge_call(x, mod3, yf, yb, bonus, g, yatt, gl, gnw, gnb, ones, p_r, p_a, w_o, n2g, w_rt, tm):
    B, T, D = x.shape
    row = lambda b, i: (b, i, 0)
    const2 = lambda b, i: (0, 0)
    rw = pl.BlockSpec((1, tm, RW), row)
    return pl.pallas_call(
        _merge_kernel,
        grid=(B, T // tm),
        in_specs=[pl.BlockSpec((1, tm, D), row),
                  pl.BlockSpec((1, 6, D), lambda b, i: (b, 0, 0)),
                  rw, rw, rw, rw,
                  pl.BlockSpec((1, RW, tm), lambda b, i: (b, 0, i)),
                  pl.BlockSpec((1, tm, 2 * D), row),
                  pl.BlockSpec((1, RW), const2),
                  pl.BlockSpec((1, RW), const2),
                  pl.BlockSpec((RW, RW), const2),
                  pl.BlockSpec((RW, D), const2),
                  pl.BlockSpec((RW, D), const2),
                  pl.BlockSpec((D, D), const2),
                  pl.BlockSpec((1, D), const2),
                  pl.BlockSpec((N_EXPERTS, D), const2)],
        out_specs=[pl.BlockSpec((1, tm, D), row),
                   pl.BlockSpec((1, tm, D), row),
                   pl.BlockSpec((1, N_EXPERTS, tm), lambda b, i: (b, 0, i))],
        out_shape=[jax.ShapeDtypeStruct((B, T, D), F32),
                   jax.ShapeDtypeStruct((B, T, D), F32),
                   jax.ShapeDtypeStruct((B, N_EXPERTS, T), F32)],
        compiler_params=_params(("parallel", "parallel")),
        name="merge",
    )(x, mod3, yf, yb, bonus, g, yatt, gl, gnw, gnb, ones, p_r, p_a, w_o, n2g, w_rt)


def _excl_prefix(mask, tri):
    e, t = mask.shape
    mb = mask.astype(BF16)
    carry = jnp.zeros((e, 1), F32)
    outs = []
    for j in range(t // 128):
        tile = mb[:, j * 128:(j + 1) * 128]
        outs.append(_dot(tile, tri) + carry)
        carry = carry + jnp.sum(tile.astype(F32), axis=1, keepdims=True)
    return jnp.concatenate(outs, axis=-1)


def _route_kernel(aff_ref, tri_ref, dig_ref, slot_ref, idx_ref, *, cap):
    nb, ne, t = aff_ref.shape
    a = aff_ref[...].reshape(nb * ne, t)
    e = a.shape[0]
    capf = jnp.float32(cap)

    def cond(s):
        it, lo, hi = s
        mid = 0.5 * (lo + hi)
        still_open = jnp.max(jnp.where((mid > lo) & (mid < hi), 1.0, 0.0))
        return (it < BISECT_ITERS) & (still_open > 0.0)

    def body(s):
        it, lo, hi = s
        mid = 0.5 * (lo + hi)
        ge = jnp.sum((a >= mid).astype(F32), axis=1, keepdims=True) >= capf
        return it + 1, jnp.where(ge, mid, lo), jnp.where(ge, hi, mid)

    lo0 = jnp.zeros((e, 1), F32)
    hi0 = jnp.full((e, 1), 2.0, F32)
    _, lo, hi = lax.while_loop(cond, body, (jnp.int32(0), lo0, hi0))
    gt = a >= hi
    eq = (a >= lo) & (a < hi)
    need = capf - jnp.sum(gt.astype(F32), axis=1, keepdims=True)
    tri = tri_ref[...]
    sel = gt | (eq & (_excl_prefix(eq, tri) < need))
    pos = _excl_prefix(sel, tri)
    slot_ref[...] = jnp.where(sel, pos.astype(jnp.int32), -1).reshape(nb, ne, t)

    digits = dig_ref[...]
    assert cap <= 256
    slots = lax.broadcasted_iota(jnp.int32, (cap, t), 0).astype(BF16)
    one, zero = jnp.ones((cap, t), BF16), jnp.zeros((cap, t), BF16)

    def row_body(r, carry):
        srow = slot_ref[r // ne, pl.ds(r % ne, 1), :].astype(BF16)
        d = _dot_nt(digits, jnp.where(slots == srow, one, zero))
        idx_ref[r] = (d[0:1] * TOKEN_RADIX + d[1:2]).astype(jnp.int32)
        return carry

    lax.fori_loop(0, nb * ne, row_body, 0, unroll=8)


def _route_call(aff, tri, digits, cap):
    B, E, T = aff.shape
    return pl.pallas_call(
        functools.partial(_route_kernel, cap=cap),
        grid=(1,),
        in_specs=[pl.BlockSpec((B, E, T), lambda i: (0, 0, 0)),
                  pl.BlockSpec((128, 128), lambda i: (0, 0)),
                  pl.BlockSpec((8, T), lambda i: (0, 0))],
        out_specs=[pl.BlockSpec((B, E, T), lambda i: (0, 0, 0)),
                   pl.BlockSpec((B * E, 1, cap), lambda i: (0, 0, 0))],
        out_shape=[jax.ShapeDtypeStruct((B, E, T), jnp.int32),
                   jax.ShapeDtypeStruct((B * E, 1, cap), jnp.int32)],
        compiler_params=_params(("arbitrary",)),
        name="route",
    )(aff, tri, digits)


def _moe_kernel(idx0_ref, idx1_ref, u2_ref, slot_ref, aff_ref, wg_ref, wu_ref, wd_ref, o_ref,
                acc_ref, xa_ref, xb_ref, *, cap, tt):
    e = pl.program_id(1)
    t = u2_ref.shape[1]

    def gather(idx_ref, dst_ref):
        for i in range(cap):
            dst_ref[pl.ds(i, 1), :] = u2_ref[0, pl.ds(idx_ref[0, 0, i], 1), :]

    def expert(x_ref):
        xe = x_ref[...].astype(BF16)
        hg = _dot(xe, wg_ref[0])
        hu = _dot(xe, wu_ref[0])
        h = (hg * jax.nn.sigmoid(hg) * hu).astype(BF16)
        ye = _dot(h, wd_ref[0]).astype(BF16)
        hit = lax.broadcasted_iota(jnp.int32, (cap, t), 0) == slot_ref[0, pl.ds(e, 1), :]
        wsc = jnp.where(hit, aff_ref[0, pl.ds(e, 1), :], 0.0).astype(BF16)
        for j in range(t // tt):
            acc_ref[j * tt:(j + 1) * tt, :] += _dot_tn(wsc[:, j * tt:(j + 1) * tt], ye)

    @pl.when(e == 0)
    def _():
        acc_ref[...] = jnp.zeros_like(acc_ref)
        gather(idx0_ref, xa_ref)

    @pl.when(e % 2 == 0)
    def _():
        gather(idx1_ref, xb_ref)
        expert(xa_ref)

    @pl.when(e % 2 == 1)
    def _():
        gather(idx1_ref, xa_ref)
        expert(xb_ref)

    @pl.when(e == pl.num_programs(1) - 1)
    def _():
        o_ref[0] = acc_ref[...].astype(BF16)


def _moe_call(u2, slot, aff, idx, wg, wu, wd, cap):
    B, T, D = u2.shape
    E = wg.shape[0]
    F = wg.shape[2]
    assert E % 2 == 0
    tt = min(T, 512)
    smem_row = lambda im: pl.BlockSpec((1, 1, cap), im, memory_space=pltpu.SMEM)
    return pl.pallas_call(
        functools.partial(_moe_kernel, cap=cap, tt=tt),
        grid=(B, E),
        in_specs=[smem_row(lambda b, e: (b * E + e, 0, 0)),
                  smem_row(lambda b, e: (b * E + jnp.minimum(e + 1, E - 1), 0, 0)),
                  pl.BlockSpec((1, T, D), lambda b, e: (b, 0, 0)),
                  pl.BlockSpec((1, E, T), lambda b, e: (b, 0, 0)),
                  pl.BlockSpec((1, E, T), lambda b, e: (b, 0, 0)),
                  pl.BlockSpec((1, D, F), lambda b, e: (e, 0, 0)),
                  pl.BlockSpec((1, D, F), lambda b, e: (e, 0, 0)),
                  pl.BlockSpec((1, F, D), lambda b, e: (e, 0, 0))],
        out_specs=pl.BlockSpec((1, T, D), lambda b, e: (b, 0, 0)),
        out_shape=jax.ShapeDtypeStruct((B, T, D), BF16),
        scratch_shapes=[pltpu.VMEM((T, D), F32), pltpu.VMEM((cap, D), F32), pltpu.VMEM((cap, D), F32)],
        compiler_params=pltpu.CompilerParams(dimension_semantics=("parallel", "arbitrary"),
                                             vmem_limit_bytes=MOE_VMEM_LIMIT),
        name="moe",
    )(idx, idx, u2, slot, aff, wg, wu, wd)


def _final_kernel(x1_ref, mod_ref, moe_ref, o_ref):
    o_ref[0] = x1_ref[0] + mod_ref[0, 5:6, :] * moe_ref[0].astype(F32)


def _final_call(x1, mod3, moe, tm):
    B, T, D = x1.shape
    row = lambda b, i: (b, i, 0)
    return pl.pallas_call(
        _final_kernel,
        grid=(B, T // tm),
        in_specs=[pl.BlockSpec((1, tm, D), row),
                  pl.BlockSpec((1, 6, D), lambda b, i: (b, 0, 0)),
                  pl.BlockSpec((1, tm, D), row)],
        out_specs=pl.BlockSpec((1, tm, D), row),
        out_shape=jax.ShapeDtypeStruct((B, T, D), F32),
        compiler_params=_params(("parallel", "parallel")),
        name="final",
    )(x1, mod3, moe)


def _blockdiag2(w):
    z = jnp.zeros_like(w[0])
    return jnp.concatenate([jnp.concatenate([w[0], z], axis=1), jnp.concatenate([z, w[1]], axis=1)], axis=0)


def _layer(x, mod3, cos_t, sin_t, cos_q, sin_q, ones, tri, norm1_g, w_in, mu_prev, mu_next, rwkv_w0, rwkv_w2, rwkv_a0,
           rwkv_a2, rwkv_g2, rwkv_k_k, rwkv_k_a, rwkv_r_k, rwkv_gn_w, rwkv_gn_b, q_norm_g, k_norm_g,
           attn_sink, p_rwkv, p_attn, w_out, norm2_g, w_router, w_gate, w_up, w_down):
    B, T, D = x.shape
    tm = TM_ROWS if T % TM_ROWS == 0 else min(T, 256)
    cap = CAP_FACTOR * T // N_EXPERTS
    row = lambda a: a.reshape(1, -1)
    w_b = w_in.astype(BF16)
    qg = jnp.broadcast_to(jnp.tile(q_norm_g, HEADS)[:, None], (RW, 128))
    qt, k, v, gl, r, vv, kkn, g, bonus, lw0, lw1, k0, k1, b0, b1 = _inproj_call(
        x, mod3, row(norm1_g), w_b, w_b[:, Q0:K0].T, cos_t, sin_t, cos_q, sin_q,
        qg, row(jnp.tile(k_norm_g, KV_HEADS)), ones,
        row(mu_prev), row(mu_next), row(rwkv_w0), _blockdiag2(rwkv_w2).astype(BF16),
        row(rwkv_a0), _blockdiag2(rwkv_a2).astype(BF16), rwkv_g2.astype(BF16),
        row(rwkv_k_k), row(rwkv_k_a), row(rwkv_r_k), tm)
    yf, yb = _scan_call(r, vv, kkn, lw0, lw1, k0, k1, b0, b1, SCAN_CHUNKS)
    yatt = _attn_call(qt, k, v, attn_sink, ATT_QB)
    x1, u2, aff = _merge_call(x, mod3, yf, yb, bonus, g, yatt, gl, row(rwkv_gn_w), row(rwkv_gn_b), ones,
                              p_rwkv.astype(BF16), p_attn.astype(BF16), w_out.astype(BF16),
                              row(norm2_g), w_router.T, tm)
    assert T <= 256 * TOKEN_RADIX
    tok = jnp.arange(T)
    digits = jnp.zeros((8, T), BF16).at[0].set((tok // TOKEN_RADIX).astype(BF16)).at[1].set(
        (tok % TOKEN_RADIX).astype(BF16))
    slot, idx = _route_call(aff, tri, digits, cap)
    moe = _moe_call(u2, slot, aff, idx, w_gate.astype(BF16), w_up.astype(BF16), w_down.astype(BF16), cap)
    return _final_call(x1, mod3, moe, tm)


def kernel(x, c, positions, w_ada, b_ada, norm1_g, w_in, mu_prev, mu_next, rwkv_w0, rwkv_w2, rwkv_a0, rwkv_a2, rwkv_g2, rwkv_k_k, rwkv_k_a, rwkv_r_k, rwkv_gn_w, rwkv_gn_b, q_norm_g, k_norm_g, attn_sink, p_rwkv, p_attn, w_out, norm2_g, w_router, w_gate, w_up, w_down):
    B, T, D = x.shape
    depth = w_ada.shape[0]
    half = ROT // 2
    inv_freq = ROPE_THETA ** (-jnp.arange(0, ROT, 2, dtype=F32) / ROT)
    ang = positions.astype(F32)[..., None] * inv_freq
    cos8, sin8 = jnp.cos(ang), jnp.sin(ang)
    pad1 = jnp.ones((B, T, HD - ROT), F32)
    pad0 = jnp.zeros((B, T, HD - ROT), F32)
    cos_t = jnp.tile(jnp.concatenate([cos8, cos8, pad1], axis=-1), (1, 1, KV_HEADS))
    sin_t = jnp.tile(jnp.concatenate([-sin8, sin8, pad0], axis=-1), (1, 1, KV_HEADS))
    seg = jnp.arange(RW) // HD
    ones = (seg[:, None] == seg[None, :]).astype(BF16)
    idx = jnp.arange(128)
    tri = (idx[:, None] < idx[None, :]).astype(BF16)
    cos_q, sin_q = jnp.swapaxes(cos8, 1, 2), jnp.swapaxes(sin8, 1, 2)
    for l in range(depth):
        mod3 = _mod_call(c, w_ada[l], b_ada[l]).reshape(B, 6, D)
        x = _layer(x, mod3, cos_t, sin_t, cos_q, sin_q, ones, tri, norm1_g[l], w_in[l], mu_prev[l], mu_next[l],
                   rwkv_w0[l], rwkv_w2[l], rwkv_a0[l], rwkv_a2[l], rwkv_g2[l], rwkv_k_k[l], rwkv_k_a[l],
                   rwkv_r_k[l], rwkv_gn_w[l], rwkv_gn_b[l], q_norm_g[l], k_norm_g[l], attn_sink[l],
                   p_rwkv[l], p_attn[l], w_out[l], norm2_g[l], w_router[l], w_gate[l], w_up[l], w_down[l])
    return x
```

```python
import functools
import math

import jax
import jax.numpy as jnp
from jax import lax
from jax.experimental import pallas as pl
from jax.experimental.pallas import tpu as pltpu

F32 = jnp.float32
BF16 = jnp.bfloat16

D_MODEL = 1024
RW = 512
HEADS = 8
HD = 64
LORA_W = 64
LORA_A = 64
LORA_G = 128
GN_EPS = HD * 1e-5
KV_HEADS = 2
GROUP = HEADS // KV_HEADS
KV_W = KV_HEADS * HD
BLOCK = 128
ROPE_THETA = 500000.0
ROT = HD // 4
N_EXPERTS = 16
CAP_FACTOR = 2
NORM_EPS = 1e-6
RWKV_COLS = 3 * RW + 2 * LORA_W + 2 * LORA_A + LORA_G
Q0 = RWKV_COLS
K0 = Q0 + RW
V0 = K0 + KV_W
G0 = V0 + KV_W
IN_COLS = G0 + 2 * D_MODEL
CHUNK = 64
SUB = 16
SCAN_CHUNKS = 4
TM_ROWS = 512
TOKEN_RADIX = 64
BISECT_ITERS = 160
NEG = -1e30
LOG2E = math.log2(math.e)
QK_SCALE = HD ** -0.5 * LOG2E
VMEM_LIMIT = 56 * 1024 * 1024
MOE_VMEM_LIMIT = 60 * 1024 * 1024


def _dot(a, b):
    return jnp.dot(a, b, preferred_element_type=F32)


def _dot_nt(a, b):
    return lax.dot_general(a, b, (((1,), (1,)), ((), ())), preferred_element_type=F32)


def _dot_tn(a, b):
    return lax.dot_general(a, b, (((0,), (0,)), ((), ())), preferred_element_type=F32)


def _segsum(x, ones):
    xh = x.astype(BF16)
    xl = (x - xh.astype(F32)).astype(BF16)
    return _dot(xh, ones) + _dot(xl, ones)


def _params(sem):
    return pltpu.CompilerParams(dimension_semantics=sem, vmem_limit_bytes=VMEM_LIMIT)


def _mod_kernel(c_ref, w_ref, b_ref, o_ref):
    c = c_ref[...]
    ca = c * jax.nn.sigmoid(c)
    o_ref[...] = jnp.dot(ca, w_ref[...], preferred_element_type=F32,
                         precision=lax.Precision.HIGHEST) + b_ref[...]


def _mod_call(c, w_ada, b_ada):
    B, D = c.shape
    n = w_ada.shape[1] // D
    return pl.pallas_call(
        _mod_kernel,
        grid=(n,),
        in_specs=[pl.BlockSpec((B, D), lambda j: (0, 0)),
                  pl.BlockSpec((D, D), lambda j: (0, j)),
                  pl.BlockSpec((1, D), lambda j: (0, j))],
        out_specs=pl.BlockSpec((B, D), lambda j: (0, j)),
        out_shape=jax.ShapeDtypeStruct((B, n * D), F32),
        compiler_params=_params(("arbitrary",)),
        name="mod",
    )(c, w_ada, b_ada.reshape(1, -1))


def _rope(xn, cos, sin, width):
    lane = lax.broadcasted_iota(jnp.int32, xn.shape, 1) & (HD - 1)
    rot = jnp.where(lane < ROT // 2, pltpu.roll(xn, width - ROT // 2, 1), pltpu.roll(xn, ROT // 2, 1))
    return xn * cos + rot * sin


HALO = 8


def _inproj_kernel(x_ref, xp_ref, xn_ref, mod_ref, g_ref, w_ref, wq_ref, cos_ref, sin_ref, cosq_ref, sinq_ref,
                   qg_ref, kg_ref, ones8_ref,
                   mup_ref, mun_ref, w0_ref, w2_ref, a0_ref, a2_ref, g2_ref, kk_ref, ka_ref, rk_ref,
                   qt_ref, k_ref, v_ref, gl_ref,
                   r_o, v_o, kkn_o, g_o, bonus_o, lw0_o, lw1_o, k0_o, k1_o, b0_o, b1_o):
    i = pl.program_id(1)
    last = pl.num_programs(1) - 1
    tm = x_ref.shape[1]
    xx = jnp.concatenate([xp_ref[0], x_ref[0], xn_ref[0]], axis=0)
    ms = jnp.mean(xx * xx, axis=-1, keepdims=True)
    y = xx * lax.rsqrt(ms + NORM_EPS) * g_ref[...]
    u_all = y * (1.0 + mod_ref[0, 1:2, :]) + mod_ref[0, 0:1, :]
    ub = u_all[HALO:HALO + tm].astype(BF16)
    z_all = _dot(u_all.astype(BF16), w_ref[:, 0:RWKV_COLS])
    k = _dot(ub, w_ref[:, K0:V0])
    v_ref[0] = _dot(ub, w_ref[:, V0:G0]).astype(BF16)
    gl_ref[0] = _dot(ub, w_ref[:, G0:IN_COLS]).astype(BF16)
    ones8 = ones8_ref[...]
    kn = k * lax.rsqrt(_segsum(k * k, ones8[:KV_W, :KV_W]) * (1.0 / HD) + NORM_EPS) * kg_ref[...]
    k_ref[0] = _rope(kn, cos_ref[0], sin_ref[0], KV_W).astype(BF16)

    z = z_all[HALO:HALO + tm]
    row = lax.broadcasted_iota(jnp.int32, z.shape, 0)
    zp = jnp.where((row == 0) & (i == 0), 0.0, z_all[HALO - 1:HALO - 1 + tm])
    zn = jnp.where((row == tm - 1) & (i == last), 0.0, z_all[HALO + 1:HALO + 1 + tm])
    zs = z + mup_ref[...] * (zp - z) + mun_ref[...] * (zn - z)
    r = zs[:, 0:RW]
    kr = zs[:, RW:2 * RW]
    vr = zs[:, 2 * RW:3 * RW]
    c0 = 3 * RW
    wd = zs[:, c0:c0 + 2 * LORA_W]
    ad = zs[:, c0 + 2 * LORA_W:c0 + 2 * LORA_W + 2 * LORA_A]
    gd = zs[:, c0 + 2 * LORA_W + 2 * LORA_A:RWKV_COLS]
    wl = _dot(jnp.tanh(wd).astype(BF16), w2_ref[...]) + w0_ref[...]
    lw = -math.exp(-0.5) * jax.nn.sigmoid(wl)
    al = jax.nn.sigmoid(_dot(ad.astype(BF16), a2_ref[...]) + a0_ref[...])
    g = _dot(jax.nn.sigmoid(gd).astype(BF16), g2_ref[...])
    kk = kr * kk_ref[...]
    kkn = kk / jnp.maximum(jnp.sqrt(_segsum(kk * kk, ones8)), 1e-12)
    ka = ka_ref[...]
    a_0 = al[:, 0:RW]
    a_1 = al[:, RW:2 * RW]
    k_0 = kr * (1.0 + (a_0 - 1.0) * ka)
    k_1 = kr * (1.0 + (a_1 - 1.0) * ka)
    bonus = _segsum(r * (k_0 + k_1) * rk_ref[...], ones8) * vr
    r_o[0] = r.astype(BF16)
    v_o[0] = vr.astype(BF16)
    kkn_o[0] = kkn.astype(BF16)
    g_o[0] = g.astype(BF16)
    bonus_o[0] = bonus.astype(BF16)
    lw0_o[0] = lw[:, 0:RW]
    lw1_o[0] = lw[:, RW:2 * RW]
    k0_o[0] = k_0.astype(BF16)
    k1_o[0] = k_1.astype(BF16)
    b0_o[0] = (kkn * a_0).astype(BF16)
    b1_o[0] = (kkn * a_1).astype(BF16)

    qt = _dot_nt(wq_ref[...], ub)
    qg = jnp.concatenate([qg_ref[...]] * (tm // 128), axis=1)
    cq = cosq_ref[0]
    sq = sinq_ref[0]
    half = ROT // 2
    pieces = []
    for h in range(HEADS):
        xh = qt[h * HD:(h + 1) * HD]
        xn = (xh * lax.rsqrt(jnp.mean(xh * xh, axis=0, keepdims=True) + NORM_EPS)
              * qg[h * HD:(h + 1) * HD] * QK_SCALE)
        x1, x2 = xn[0:half], xn[half:ROT]
        pieces += [x1 * cq - x2 * sq, x2 * cq + x1 * sq, xn[ROT:]]
    qt_ref[0] = jnp.concatenate(pieces, axis=0).astype(BF16)


def _inproj_call(x, mod3, norm1_g, w_rest, wq_t, cos_t, sin_t, cos_q, sin_q, qg, kg, ones,
                 mup, mun, w0c, w2c, a0c, a2c, g2b, k_k, k_a, r_k, tm):
    B, T, D = x.shape
    nh = tm // HALO
    nth = T // HALO
    row = lambda b, i: (b, i, 0)
    col = lambda b, i: (b, 0, i)
    const2 = lambda b, i: (0, 0)
    half = ROT // 2
    rw = pl.BlockSpec((1, tm, RW), row)
    vec = lambda n: pl.BlockSpec((1, n), const2)
    return pl.pallas_call(
        _inproj_kernel,
        grid=(B, T // tm),
        in_specs=[pl.BlockSpec((1, tm, D), row),
                  pl.BlockSpec((1, HALO, D), lambda b, i: (b, jnp.maximum(i * nh - 1, 0), 0)),
                  pl.BlockSpec((1, HALO, D), lambda b, i: (b, jnp.minimum((i + 1) * nh, nth - 1), 0)),
                  pl.BlockSpec((1, 6, D), lambda b, i: (b, 0, 0)),
                  vec(D),
                  pl.BlockSpec(w_rest.shape, const2),
                  pl.BlockSpec((RW, D), const2),
                  pl.BlockSpec((1, tm, KV_W), row),
                  pl.BlockSpec((1, tm, KV_W), row),
                  pl.BlockSpec((1, half, tm), col),
                  pl.BlockSpec((1, half, tm), col),
                  pl.BlockSpec((RW, 128), const2),
                  vec(KV_W),
                  pl.BlockSpec((RW, RW), const2),
                  vec(RWKV_COLS), vec(RWKV_COLS),
                  vec(2 * RW), pl.BlockSpec((2 * LORA_W, 2 * RW), const2),
                  vec(2 * RW), pl.BlockSpec((2 * LORA_A, 2 * RW), const2),
                  pl.BlockSpec((LORA_G, RW), const2),
                  vec(RW), vec(RW), vec(RW)],
        out_specs=[pl.BlockSpec((1, RW, tm), col),
                   pl.BlockSpec((1, tm, KV_W), row),
                   pl.BlockSpec((1, tm, KV_W), row),
                   pl.BlockSpec((1, tm, 2 * D), row)] + [rw] * 11,
        out_shape=[jax.ShapeDtypeStruct((B, RW, T), BF16),
                   jax.ShapeDtypeStruct((B, T, KV_W), BF16),
                   jax.ShapeDtypeStruct((B, T, KV_W), BF16),
                   jax.ShapeDtypeStruct((B, T, 2 * D), BF16)]
        + [jax.ShapeDtypeStruct((B, T, RW), F32 if i in (5, 6) else BF16) for i in range(11)],
        compiler_params=_params(("parallel", "parallel")),
        name="inproj",
    )(x, x, x, mod3, norm1_g, w_rest, wq_t, cos_t, sin_t, cos_q, sin_q, qg, kg, ones,
      mup, mun, w0c, w2c, a0c, a2c, g2b, k_k, k_a, r_k)


def _dir_operands(rev, nch, r, v, kk, lw, kd, bd):
    n = r.shape[0]
    ti = lax.broadcasted_iota(jnp.int32, (n, n), 0)
    si = lax.broadcasted_iota(jnp.int32, (n, n), 1)
    same = (ti // CHUNK) == (si // CHUNK)
    incl = ((si >= ti) if rev else (si <= ti)) & same
    mi = incl.astype(BF16)
    l1 = lw.astype(BF16)
    r1 = lw - l1.astype(F32)
    l2 = r1.astype(BF16)
    l3 = (r1 - l2.astype(F32)).astype(BF16)
    cum = _dot(mi, l1) + _dot(mi, l2) + _dot(mi, l3)
    wt = jnp.exp(cum)
    winv = jnp.exp(-cum)
    ops = dict(ah=-(kk * jnp.exp(cum - lw)), rt=r * wt, bt=bd * winv, kt=kd * winv, v=v)
    wcs, bhs, khs = [], [], []
    for j in range(nch):
        last = j * CHUNK if rev else (j + 1) * CHUNK - 1
        wc = wt[last:last + 1, :]
        rows = slice(j * CHUNK, (j + 1) * CHUNK)
        wcs.append(wc)
        bhs.append(ops["bt"][rows] * wc)
        khs.append(ops["kt"][rows] * wc)
    return ops, wcs, bhs, khs


def _scan_kernel(rf, vf, kkf, lwf, kf, bf, rb, vb, kkb, lwb, kb, bb, yf_ref, yb_ref, st_ref, *, nch):
    @pl.when(pl.program_id(1) == 0)
    def _():
        st_ref[...] = jnp.zeros_like(st_ref)

    c = CHUNK
    pw = 2 * HD
    npair = HEADS // 2
    ti = lax.broadcasted_iota(jnp.int32, (c, pw), 0)
    li = lax.broadcasted_iota(jnp.int32, (c, pw), 1)
    si = li & (HD - 1)
    head0 = li < HD
    blk = (ti // SUB) == (si // SUB)
    eye = ti == si
    two = lambda m: jnp.concatenate([m, m], axis=1)
    masks = {False: (two(si < ti), two(si <= ti)), True: (two(si > ti), two(si >= ti))}

    def bd(y):
        yb = y.astype(BF16)
        zero = jnp.zeros_like(yb)
        return jnp.concatenate([jnp.where(head0, yb, zero), jnp.where(head0, zero, yb)], axis=0)

    def bdcat(*ys):
        return jnp.concatenate([bd(y) for y in ys], axis=1)

    def diag_blocks(full):
        return jnp.where(head0, full[:c], full[c:])

    dirs = [(False, rf, vf, kkf, lwf, kf, bf), (True, rb, vb, kkb, lwb, kb, bb)]
    units = []
    for d, (rev, r_, v_, kk_, lw_, k_, b_) in enumerate(dirs):
        f32 = lambda ref: ref[0].astype(F32)
        ops, wcs, bhs, khs = _dir_operands(rev, nch, f32(r_), f32(v_), f32(kk_), lw_[0], f32(k_), f32(b_))
        for j in range(nch):
            rows = slice(j * c, (j + 1) * c)
            for p in range(npair):
                sl = slice(p * pw, (p + 1) * pw)
                u = dict(d=d, j=j, p=p, rev=rev, wc=wcs[j][:, sl],
                         bkh=jnp.concatenate([bhs[j][:, sl], khs[j][:, sl]], axis=0).astype(BF16))
                for name in ("ah", "rt", "bt", "kt", "v"):
                    u[name] = ops[name][rows, sl]
                units.append(u)

    for u in units:
        x1 = jnp.concatenate([u["ah"], u["rt"]], axis=0).astype(BF16)
        u["lbk"] = _dot_nt(x1, jnp.concatenate([bd(u["bt"]), bd(u["kt"])], axis=0))
    for u in units:
        strict, incl = masks[u["rev"]]
        top = jnp.where(strict, u["lbk"][:c], 0.0)
        bot = jnp.where(incl, u["lbk"][c:], 0.0)
        lab = top[:, :pw]
        u["mrb"] = bot[:, :pw].astype(BF16)
        u["ld"] = jnp.where(blk, lab, 0.0)
        u["lo"] = lab - u["ld"]
        u["lm"] = jnp.concatenate([top[:, pw:], bot[:, pw:]], axis=0).astype(BF16)
    for u in units:
        u["a2"] = _dot(u["ld"].astype(BF16), bd(u["ld"]))
        u["lkv"] = _dot(u["lm"], bd(u["v"]))
    for u in units:
        s = _dot(jnp.concatenate([u["a2"], u["ld"]], axis=0).astype(BF16), bd(u["a2"]))
        u["a4"] = s[:c]
        u["t"] = jnp.where(eye, 1.0, 0.0) + u["ld"] + u["a2"] + s[c:]
    for u in units:
        s = _dot(jnp.concatenate([u["a4"], u["t"]], axis=0).astype(BF16), bd(u["a4"]))
        u["a8"] = s[:c]
        u["t"] = u["t"] + s[c:]
    for u in units:
        u["td"] = u["t"] + _dot(u["t"].astype(BF16), bd(u["a8"]))
    for u in units:
        x = _dot(u["td"].astype(BF16), bdcat(u["lo"], u["ah"], u["lkv"][:c]))
        u["n"] = x[:, :pw]
        u["z"] = x[:, pw:]
    for u in units:
        s = _dot(u["n"].astype(BF16), bdcat(u["n"], u["z"][:, :pw], u["z"][:, pw:]))
        u["n2"] = s[:, :pw]
        u["z"] = u["z"] + s[:, pw:]
    for u in units:
        u["z"] = u["z"] + _dot(u["n2"].astype(BF16), bdcat(u["z"][:, :pw], u["z"][:, pw:]))
    for u in units:
        z = u["z"]
        y10 = (jnp.concatenate([u["rt"], u["lkv"][c:]], axis=1)
               + _dot(u["mrb"], bdcat(z[:, :pw], z[:, pw:])))
        vpad = jnp.concatenate([jnp.zeros_like(u["v"]), u["v"]], axis=1)
        full = _dot_tn(u["bkh"], jnp.concatenate([z, vpad], axis=0).astype(BF16))
        g = diag_blocks(full[:, :pw]) + jnp.where(eye, u["wc"], 0.0)
        u["gy"] = jnp.concatenate([g, y10[:, :pw]], axis=0).astype(BF16)
        u["h0"] = diag_blocks(full[:, pw:])
        u["y0"] = y10[:, pw:]

    ys = {}
    for d in range(2):
        order = range(nch - 1, -1, -1) if d == 1 else range(nch)
        zs = [st_ref[d, p] for p in range(npair)]
        for j in order:
            for p in range(npair):
                u = units[(d * nch + j) * npair + p]
                m = _dot(u["gy"], bd(zs[p]))
                zs[p] = m[:c] + u["h0"]
                ys[(d, j, p)] = m[c:] + u["y0"]
        for p in range(npair):
            st_ref[d, p] = zs[p]
    for d, y_ref in enumerate((yf_ref, yb_ref)):
        y_ref[0] = jnp.concatenate(
            [jnp.concatenate([ys[(d, j, p)] for p in range(npair)], axis=-1) for j in range(nch)],
            axis=0).astype(BF16)


def _scan_call(r, v, kkn, lw0, lw1, k0, k1, b0, b1, nch):
    B, T, _ = r.shape
    rows = nch * CHUNK
    ns = T // rows
    fwd = pl.BlockSpec((1, rows, RW), lambda b, c: (b, c, 0))
    bwd = pl.BlockSpec((1, rows, RW), lambda b, c: (b, ns - 1 - c, 0))
    return pl.pallas_call(
        functools.partial(_scan_kernel, nch=nch),
        grid=(B, ns),
        in_specs=[fwd] * 6 + [bwd] * 6,
        out_specs=[fwd, bwd],
        out_shape=[jax.ShapeDtypeStruct((B, T, RW), BF16)] * 2,
        scratch_shapes=[pltpu.VMEM((2, HEADS // 2, HD, 2 * HD), F32)],
        compiler_params=_params(("parallel", "arbitrary")),
        name="scan",
    )(r, v, kkn, lw0, k0, b0, r, v, kkn, lw1, k1, b1)


def _attention_scores(sink_ref, qt_ref, k_ref, v_ref, n, qb):
    t = k_ref.shape[1]
    band = 3 * BLOCK
    cols = GROUP * BLOCK
    ki = lax.broadcasted_iota(jnp.int32, (band, BLOCK), 0)
    qi = lax.broadcasted_iota(jnp.int32, (band, BLOCK), 1)
    lane_head = lax.broadcasted_iota(jnp.int32, (1, cols), 1) // BLOCK
    zeros = jnp.zeros((HD, cols), BF16)
    units = []
    for j in range(qb):
        blk = n * qb + j
        start = pl.multiple_of(jnp.clip((blk - 1) * BLOCK, 0, t - band), BLOCK)
        bias = jnp.where(jnp.abs(start + ki - (blk * BLOCK + qi)) <= BLOCK, 0.0, NEG)
        bias = jnp.concatenate([bias] * GROUP, axis=1)
        kb = k_ref[0, pl.ds(start, band), :]
        vb = v_ref[0, pl.ds(start, band), :]
        for g in range(KV_HEADS):
            qg = jnp.concatenate([qt_ref[0, (g * GROUP + i) * HD:(g * GROUP + i + 1) * HD,
                                         j * BLOCK:(j + 1) * BLOCK] for i in range(GROUP)], axis=1)
            rhs = jnp.concatenate([qg if gg == g else zeros for gg in range(KV_HEADS)], axis=0)
            sk = jnp.zeros((1, cols), F32)
            for i in range(GROUP):
                sk = jnp.where(lane_head == i, sink_ref[g * GROUP + i] * LOG2E, sk)
            units.append(dict(j=j, g=g, kb=kb, vb=vb, rhs=rhs, sk=sk, bias=bias))
    for u in units:
        u["s"] = _dot(u["kb"], u["rhs"]) + u["bias"]
    return units


def _attention_finish(units, qb):
    for u in units:
        u["m"] = jnp.maximum(jnp.max(u["s"], axis=0, keepdims=True), u["sk"])
    for u in units:
        p = jnp.exp2(u["s"] - u["m"])
        u["r"] = 1.0 / (jnp.sum(p, axis=0, keepdims=True) + jnp.exp2(u["sk"] - u["m"]))
        u["p"] = p.astype(BF16)
    for u in units:
        g = u["g"]
        u["o"] = _dot_tn(u["vb"], u["p"])[g * HD:(g + 1) * HD] * u["r"]
    out = {(u["j"], u["g"]): u["o"] for u in units}
    return jnp.concatenate(
        [jnp.concatenate([out[(j, h // GROUP)][:, (h % GROUP) * BLOCK:(h % GROUP + 1) * BLOCK]
                          for j in range(qb)], axis=1) for h in range(HEADS)], axis=0)


def _merge_kernel(sink_ref, x_ref, mod_ref, yf_ref, yb_ref, bonus_ref, g_ref, qt_ref, k_ref, v_ref, gl_ref,
                  gnw_ref, gnb_ref, ones_ref, pr_ref, pa_ref, wo_ref, n2g_ref, wr_ref,
                  x1_ref, u2_ref, aff_ref):
    qb = x_ref.shape[1] // BLOCK
    att = _attention_scores(sink_ref, qt_ref, k_ref, v_ref, pl.program_id(1), qb)
    ones = ones_ref[...]
    y = yf_ref[0].astype(F32) + yb_ref[0].astype(F32)
    mu = _segsum(y, ones) * (1.0 / HD)
    yc = y - mu
    var = _segsum(yc * yc, ones) * (1.0 / HD)
    yn = yc * lax.rsqrt(var + GN_EPS) * gnw_ref[...] + gnb_ref[...]
    ya = ((yn + bonus_ref[0].astype(F32)) * g_ref[0].astype(F32)).astype(BF16)
    pa = _dot(ya, pr_ref[...])
    gates = jax.nn.sigmoid(gl_ref[0].astype(F32))
    yatt_t = _attention_finish(att, qb).astype(BF16)
    pb = _dot_tn(yatt_t, pa_ref[...])
    m = gates[:, :D_MODEL] * pa + gates[:, D_MODEL:] * pb
    x1 = x_ref[0] + mod_ref[0, 2:3, :] * _dot(m.astype(BF16), wo_ref[...])
    x1_ref[0] = x1
    ms = jnp.mean(x1 * x1, axis=-1, keepdims=True)
    u2 = x1 * lax.rsqrt(ms + NORM_EPS) * n2g_ref[...] * (1.0 + mod_ref[0, 4:5, :]) + mod_ref[0, 3:4, :]
    u2_ref[0] = u2
    u2h = u2.astype(BF16)
    u2l = (u2 - u2h.astype(F32)).astype(BF16)
    wr = wr_ref[...]
    wrh = wr.astype(BF16)
    wrl = (wr - wrh.astype(F32)).astype(BF16)
    t1 = _dot_nt(jnp.concatenate([wrh, wrl], axis=0), u2h)
    logits = t1[:N_EXPERTS] + t1[N_EXPERTS:] + _dot_nt(wrh, u2l)
    e = jnp.exp(logits - jnp.max(logits, axis=0, keepdims=True))
    aff_ref[0] = e / jnp.sum(e, axis=0, keepdims=True)


def _merge_call(sink, x, mod3, yf, yb, bonus, g, qt, k, v, gl, gnw, gnb, ones, p_r, p_a, w_o, n2g, w_rt, tm):
    B, T, D = x.shape
    assert T >= 3 * BLOCK and tm % BLOCK == 0
    row = lambda b, i: (b, i, 0)
    const2 = lambda b, i: (0, 0)
    rw = pl.BlockSpec((1, tm, RW), row)
    kv = pl.BlockSpec((1, T, KV_W), lambda b, i: (b, 0, 0))
    return pl.pallas_call(
        _merge_kernel,
        grid=(B, T // tm),
        in_specs=[pl.BlockSpec(memory_space=pltpu.SMEM),
                  pl.BlockSpec((1, tm, D), row),
                  pl.BlockSpec((1, 6, D), lambda b, i: (b, 0, 0)),
                  rw, rw, rw, rw,
                  pl.BlockSpec((1, RW, tm), lambda b, i: (b, 0, i)), kv, kv,
                  pl.BlockSpec((1, tm, 2 * D), row),
                  pl.BlockSpec((1, RW), const2),
                  pl.BlockSpec((1, RW), const2),
                  pl.BlockSpec((RW, RW), const2),
                  pl.BlockSpec((RW, D), const2),
                  pl.BlockSpec((RW, D), const2),
                  pl.BlockSpec((D, D), const2),
                  pl.BlockSpec((1, D), const2),
                  pl.BlockSpec((N_EXPERTS, D), const2)],
        out_specs=[pl.BlockSpec((1, tm, D), row),
                   pl.BlockSpec((1, tm, D), row),
                   pl.BlockSpec((1, N_EXPERTS, tm), lambda b, i: (b, 0, i))],
        out_shape=[jax.ShapeDtypeStruct((B, T, D), F32),
                   jax.ShapeDtypeStruct((B, T, D), F32),
                   jax.ShapeDtypeStruct((B, N_EXPERTS, T), F32)],
        compiler_params=_params(("parallel", "parallel")),
        name="merge",
    )(sink, x, mod3, yf, yb, bonus, g, qt, k, v, gl, gnw, gnb, ones, p_r, p_a, w_o, n2g, w_rt)


def _excl_prefix(mask, tri):
    e, t = mask.shape
    mb = mask.astype(BF16)
    carry = jnp.zeros((e, 1), F32)
    outs = []
    for j in range(t // 128):
        tile = mb[:, j * 128:(j + 1) * 128]
        outs.append(_dot(tile, tri) + carry)
        carry = carry + jnp.sum(tile.astype(F32), axis=1, keepdims=True)
    return jnp.concatenate(outs, axis=-1)


def _route_kernel(aff_ref, tri_ref, dig_ref, slot_ref, idx_ref, *, cap):
    nb, ne, t = aff_ref.shape
    a = aff_ref[...].reshape(nb * ne, t)
    e = a.shape[0]
    capf = jnp.float32(cap)

    def cond(s):
        it, lo, hi = s
        mid = 0.5 * (lo + hi)
        still_open = jnp.max(jnp.where((mid > lo) & (mid < hi), 1.0, 0.0))
        return (it < BISECT_ITERS) & (still_open > 0.0)

    def body(s):
        it, lo, hi = s
        mid = 0.5 * (lo + hi)
        ge = jnp.sum((a >= mid).astype(F32), axis=1, keepdims=True) >= capf
        return it + 1, jnp.where(ge, mid, lo), jnp.where(ge, hi, mid)

    lo0 = jnp.zeros((e, 1), F32)
    hi0 = jnp.full((e, 1), 2.0, F32)
    _, lo, hi = lax.while_loop(cond, body, (jnp.int32(0), lo0, hi0))
    gt = a >= hi
    eq = (a >= lo) & (a < hi)
    need = capf - jnp.sum(gt.astype(F32), axis=1, keepdims=True)
    tri = tri_ref[...]
    sel = gt | (eq & (_excl_prefix(eq, tri) < need))
    pos = _excl_prefix(sel, tri)
    slot_ref[...] = jnp.where(sel, pos.astype(jnp.int32), -1).reshape(nb, ne, t)

    digits = dig_ref[...]
    assert cap <= 256
    slots = lax.broadcasted_iota(jnp.int32, (cap, t), 0).astype(BF16)
    one, zero = jnp.ones((cap, t), BF16), jnp.zeros((cap, t), BF16)

    def row_body(r, carry):
        srow = slot_ref[r // ne, pl.ds(r % ne, 1), :].astype(BF16)
        d = _dot_nt(digits, jnp.where(slots == srow, one, zero))
        idx_ref[r] = (d[0:1] * TOKEN_RADIX + d[1:2]).astype(jnp.int32)
        return carry

    lax.fori_loop(0, nb * ne, row_body, 0, unroll=8)


def _route_call(aff, tri, digits, cap):
    B, E, T = aff.shape
    return pl.pallas_call(
        functools.partial(_route_kernel, cap=cap),
        grid=(1,),
        in_specs=[pl.BlockSpec((B, E, T), lambda i: (0, 0, 0)),
                  pl.BlockSpec((128, 128), lambda i: (0, 0)),
                  pl.BlockSpec((8, T), lambda i: (0, 0))],
        out_specs=[pl.BlockSpec((B, E, T), lambda i: (0, 0, 0)),
                   pl.BlockSpec((B * E, 1, cap), lambda i: (0, 0, 0))],
        out_shape=[jax.ShapeDtypeStruct((B, E, T), jnp.int32),
                   jax.ShapeDtypeStruct((B * E, 1, cap), jnp.int32)],
        compiler_params=_params(("arbitrary",)),
        name="route",
    )(aff, tri, digits)


def _moe_kernel(idx0_ref, idx1_ref, u2_ref, slot_ref, aff_ref, wg_ref, wu_ref, wd_ref, o_ref,
                acc_ref, xa_ref, xb_ref, *, cap, tt):
    e = pl.program_id(1)
    t = u2_ref.shape[1]

    def gather(idx_ref, dst_ref):
        for i in range(cap):
            dst_ref[pl.ds(i, 1), :] = u2_ref[0, pl.ds(idx_ref[0, 0, i], 1), :]

    def expert(x_ref):
        xe = x_ref[...].astype(BF16)
        hg = _dot(xe, wg_ref[0])
        hu = _dot(xe, wu_ref[0])
        h = (hg * jax.nn.sigmoid(hg) * hu).astype(BF16)
        ye = _dot(h, wd_ref[0]).astype(BF16)
        hit = lax.broadcasted_iota(jnp.int32, (cap, t), 0) == slot_ref[0, pl.ds(e, 1), :]
        wsc = jnp.where(hit, aff_ref[0, pl.ds(e, 1), :], 0.0).astype(BF16)
        for j in range(t // tt):
            acc_ref[j * tt:(j + 1) * tt, :] += _dot_tn(wsc[:, j * tt:(j + 1) * tt], ye)

    @pl.when(e == 0)
    def _():
        acc_ref[...] = jnp.zeros_like(acc_ref)
        gather(idx0_ref, xa_ref)

    @pl.when(e % 2 == 0)
    def _():
        gather(idx1_ref, xb_ref)
        expert(xa_ref)

    @pl.when(e % 2 == 1)
    def _():
        gather(idx1_ref, xa_ref)
        expert(xb_ref)

    @pl.when(e == pl.num_programs(1) - 1)
    def _():
        o_ref[0] = acc_ref[...].astype(BF16)


def _moe_call(u2, slot, aff, idx, wg, wu, wd, cap):
    B, T, D = u2.shape
    E = wg.shape[0]
    F = wg.shape[2]
    assert E % 2 == 0
    tt = min(T, 512)
    smem_row = lambda im: pl.BlockSpec((1, 1, cap), im, memory_space=pltpu.SMEM)
    return pl.pallas_call(
        functools.partial(_moe_kernel, cap=cap, tt=tt),
        grid=(B, E),
        in_specs=[smem_row(lambda b, e: (b * E + e, 0, 0)),
                  smem_row(lambda b, e: (b * E + jnp.minimum(e + 1, E - 1), 0, 0)),
                  pl.BlockSpec((1, T, D), lambda b, e: (b, 0, 0)),
                  pl.BlockSpec((1, E, T), lambda b, e: (b, 0, 0)),
                  pl.BlockSpec((1, E, T), lambda b, e: (b, 0, 0)),
                  pl.BlockSpec((1, D, F), lambda b, e: (e, 0, 0)),
                  pl.BlockSpec((1, D, F), lambda b, e: (e, 0, 0)),
                  pl.BlockSpec((1, F, D), lambda b, e: (e, 0, 0))],
        out_specs=pl.BlockSpec((1, T, D), lambda b, e: (b, 0, 0)),
        out_shape=jax.ShapeDtypeStruct((B, T, D), BF16),
        scratch_shapes=[pltpu.VMEM((T, D), F32), pltpu.VMEM((cap, D), F32), pltpu.VMEM((cap, D), F32)],
        compiler_params=pltpu.CompilerParams(dimension_semantics=("parallel", "arbitrary"),
                                             vmem_limit_bytes=MOE_VMEM_LIMIT),
        name="moe",
    )(idx, idx, u2, slot, aff, wg, wu, wd)


def _final_kernel(x1_ref, mod_ref, moe_ref, o_ref):
    o_ref[0] = x1_ref[0] + mod_ref[0, 5:6, :] * moe_ref[0].astype(F32)


def _final_call(x1, mod3, moe, tm):
    B, T, D = x1.shape
    row = lambda b, i: (b, i, 0)
    return pl.pallas_call(
        _final_kernel,
        grid=(B, T // tm),
        in_specs=[pl.BlockSpec((1, tm, D), row),
                  pl.BlockSpec((1, 6, D), lambda b, i: (b, 0, 0)),
                  pl.BlockSpec((1, tm, D), row)],
        out_specs=pl.BlockSpec((1, tm, D), row),
        out_shape=jax.ShapeDtypeStruct((B, T, D), F32),
        compiler_params=_params(("parallel", "parallel")),
        name="final",
    )(x1, mod3, moe)


def _blockdiag2(w):
    z = jnp.zeros_like(w[0])
    return jnp.concatenate([jnp.concatenate([w[0], z], axis=1), jnp.concatenate([z, w[1]], axis=1)], axis=0)


def _layer(x, mod3, cos_t, sin_t, cos_q, sin_q, ones, tri, norm1_g, w_in, mu_prev, mu_next, rwkv_w0, rwkv_w2, rwkv_a0,
           rwkv_a2, rwkv_g2, rwkv_k_k, rwkv_k_a, rwkv_r_k, rwkv_gn_w, rwkv_gn_b, q_norm_g, k_norm_g,
           attn_sink, p_rwkv, p_attn, w_out, norm2_g, w_router, w_gate, w_up, w_down):
    B, T, D = x.shape
    tm = TM_ROWS if T % TM_ROWS == 0 else min(T, 256)
    cap = CAP_FACTOR * T // N_EXPERTS
    row = lambda a: a.reshape(1, -1)
    w_b = w_in.astype(BF16)
    qg = jnp.broadcast_to(jnp.tile(q_norm_g, HEADS)[:, None], (RW, 128))
    qt, k, v, gl, r, vv, kkn, g, bonus, lw0, lw1, k0, k1, b0, b1 = _inproj_call(
        x, mod3, row(norm1_g), w_b, w_b[:, Q0:K0].T, cos_t, sin_t, cos_q, sin_q,
        qg, row(jnp.tile(k_norm_g, KV_HEADS)), ones,
        row(mu_prev), row(mu_next), row(rwkv_w0), _blockdiag2(rwkv_w2).astype(BF16),
        row(rwkv_a0), _blockdiag2(rwkv_a2).astype(BF16), rwkv_g2.astype(BF16),
        row(rwkv_k_k), row(rwkv_k_a), row(rwkv_r_k), tm)
    yf, yb = _scan_call(r, vv, kkn, lw0, lw1, k0, k1, b0, b1, SCAN_CHUNKS)
    x1, u2, aff = _merge_call(attn_sink, x, mod3, yf, yb, bonus, g, qt, k, v, gl, row(rwkv_gn_w), row(rwkv_gn_b), ones,
                              p_rwkv.astype(BF16), p_attn.astype(BF16), w_out.astype(BF16),
                              row(norm2_g), w_router.T, tm)
    assert T <= 256 * TOKEN_RADIX
    tok = jnp.arange(T)
    digits = jnp.zeros((8, T), BF16).at[0].set((tok // TOKEN_RADIX).astype(BF16)).at[1].set(
        (tok % TOKEN_RADIX).astype(BF16))
    slot, idx = _route_call(aff, tri, digits, cap)
    moe = _moe_call(u2, slot, aff, idx, w_gate.astype(BF16), w_up.astype(BF16), w_down.astype(BF16), cap)
    return _final_call(x1, mod3, moe, tm)


def kernel(x, c, positions, w_ada, b_ada, norm1_g, w_in, mu_prev, mu_next, rwkv_w0, rwkv_w2, rwkv_a0, rwkv_a2, rwkv_g2, rwkv_k_k, rwkv_k_a, rwkv_r_k, rwkv_gn_w, rwkv_gn_b, q_norm_g, k_norm_g, attn_sink, p_rwkv, p_attn, w_out, norm2_g, w_router, w_gate, w_up, w_down):
    B, T, D = x.shape
    depth = w_ada.shape[0]
    half = ROT // 2
    inv_freq = ROPE_THETA ** (-jnp.arange(0, ROT, 2, dtype=F32) / ROT)
    ang = positions.astype(F32)[..., None] * inv_freq
    cos8, sin8 = jnp.cos(ang), jnp.sin(ang)
    pad1 = jnp.ones((B, T, HD - ROT), F32)
    pad0 = jnp.zeros((B, T, HD - ROT), F32)
    cos_t = jnp.tile(jnp.concatenate([cos8, cos8, pad1], axis=-1), (1, 1, KV_HEADS))
    sin_t = jnp.tile(jnp.concatenate([-sin8, sin8, pad0], axis=-1), (1, 1, KV_HEADS))
    seg = jnp.arange(RW) // HD
    ones = (seg[:, None] == seg[None, :]).astype(BF16)
    idx = jnp.arange(128)
    tri = (idx[:, None] < idx[None, :]).astype(BF16)
    cos_q, sin_q = jnp.swapaxes(cos8, 1, 2), jnp.swapaxes(sin8, 1, 2)
    for l in range(depth):
        mod3 = _mod_call(c, w_ada[l], b_ada[l]).reshape(B, 6, D)
        x = _layer(x, mod3, cos_t, sin_t, cos_q, sin_q, ones, tri, norm1_g[l], w_in[l], mu_prev[l], mu_next[l],
                   rwkv_w0[l], rwkv_w2[l], rwkv_a0[l], rwkv_a2[l], rwkv_g2[l], rwkv_k_k[l], rwkv_k_a[l],
                   rwkv_r_k[l], rwkv_gn_w[l], rwkv_gn_b[l], q_norm_g[l], k_norm_g[l], attn_sink[l],
                   p_rwkv[l], p_attn[l], w_out[l], norm2_g[l], w_router[l], w_gate[l], w_up[l], w_down[l])
    return x
```

```python
import functools
import math

import jax
import jax.numpy as jnp
from jax import lax
from jax.experimental import pallas as pl
from jax.experimental.pallas import tpu as pltpu

F32 = jnp.float32
BF16 = jnp.bfloat16

D_MODEL = 1024
RW = 512
HEADS = 8
HD = 64
LORA_W = 64
LORA_A = 64
LORA_G = 128
GN_EPS = HD * 1e-5
KV_HEADS = 2
GROUP = HEADS // KV_HEADS
KV_W = KV_HEADS * HD
BLOCK = 128
ROPE_THETA = 500000.0
ROT = HD // 4
N_EXPERTS = 16
CAP_FACTOR = 2
NORM_EPS = 1e-6
RWKV_COLS = 3 * RW + 2 * LORA_W + 2 * LORA_A + LORA_G
Q0 = RWKV_COLS
K0 = Q0 + RW
V0 = K0 + KV_W
G0 = V0 + KV_W
IN_COLS = G0 + 2 * D_MODEL
CHUNK = 64
SUB = 16
SCAN_CHUNKS = 4
TM_FINAL = 2048
TM_ROWS = 512
TOKEN_RADIX = 64
BISECT_ITERS = 160
NEG = -1e30
LOG2E = math.log2(math.e)
QK_SCALE = HD ** -0.5 * LOG2E
VMEM_LIMIT = 56 * 1024 * 1024
MOE_VMEM_LIMIT = 60 * 1024 * 1024


def _dot(a, b):
    return jnp.dot(a, b, preferred_element_type=F32)


def _dot_nt(a, b):
    return lax.dot_general(a, b, (((1,), (1,)), ((), ())), preferred_element_type=F32)


def _dot_tn(a, b):
    return lax.dot_general(a, b, (((0,), (0,)), ((), ())), preferred_element_type=F32)


def _segsum(x, ones):
    xh = x.astype(BF16)
    xl = (x - xh.astype(F32)).astype(BF16)
    return _dot(xh, ones) + _dot(xl, ones)


def _segsum_pos(x, ones):
    return _dot(x.astype(BF16), ones)


def _params(sem):
    return pltpu.CompilerParams(dimension_semantics=sem, vmem_limit_bytes=VMEM_LIMIT)


def _mod_kernel(c_ref, w_ref, b_ref, o_ref):
    c = c_ref[...]
    ca = c * jax.nn.sigmoid(c)
    o_ref[...] = jnp.dot(ca, w_ref[...], preferred_element_type=F32,
                         precision=lax.Precision.HIGHEST) + b_ref[...]


def _mod_call(c, w_ada, b_ada):
    B, D = c.shape
    n = w_ada.shape[1] // D
    return pl.pallas_call(
        _mod_kernel,
        grid=(n,),
        in_specs=[pl.BlockSpec((B, D), lambda j: (0, 0)),
                  pl.BlockSpec((D, D), lambda j: (0, j)),
                  pl.BlockSpec((1, D), lambda j: (0, j))],
        out_specs=pl.BlockSpec((B, D), lambda j: (0, j)),
        out_shape=jax.ShapeDtypeStruct((B, n * D), F32),
        compiler_params=_params(("arbitrary",)),
        name="mod",
    )(c, w_ada, b_ada.reshape(1, -1))


def _rope(xn, cos, sin, width):
    lane = lax.broadcasted_iota(jnp.int32, xn.shape, 1) & (HD - 1)
    rot = jnp.where(lane < ROT // 2, pltpu.roll(xn, width - ROT // 2, 1), pltpu.roll(xn, ROT // 2, 1))
    return xn * cos + rot * sin


HALO = 8
INPROJ_SUBTILES = 1


def _inproj_kernel(x_ref, xp_ref, xn_ref, mod_ref, g_ref, w_ref, wq_ref, cos_ref, sin_ref, cosq_ref, sinq_ref,
                   qg_ref, kg_ref, ones8_ref,
                   mup_ref, mun_ref, w0_ref, w2_ref, a0_ref, a2_ref, g2_ref, kk_ref, ka_ref, rk_ref,
                   qt_ref, k_ref, v_ref, gl_ref,
                   r_o, v_o, kkn_o, g_o, bonus_o, lw0_o, lw1_o, k0_o, k1_o, b0_o, b1_o):
    tm = x_ref.shape[1]
    sub = tm // INPROJ_SUBTILES
    xx = jnp.concatenate([xp_ref[0], x_ref[0], xn_ref[0]], axis=0)
    for lo in range(0, tm, sub):
        _inproj_rows(lo, sub, xx[lo:lo + sub + 2 * HALO], tm, mod_ref, g_ref, w_ref, wq_ref, cos_ref, sin_ref,
                     cosq_ref, sinq_ref, qg_ref, kg_ref, ones8_ref, mup_ref, mun_ref, w0_ref, w2_ref, a0_ref,
                     a2_ref, g2_ref, kk_ref, ka_ref, rk_ref, qt_ref, k_ref, v_ref, gl_ref,
                     (r_o, v_o, kkn_o, g_o, bonus_o, lw0_o, lw1_o, k0_o, k1_o, b0_o, b1_o))


def _inproj_rows(lo, n, xx, tm, mod_ref, g_ref, w_ref, wq_ref, cos_ref, sin_ref, cosq_ref, sinq_ref,
                 qg_ref, kg_ref, ones8_ref, mup_ref, mun_ref, w0_ref, w2_ref, a0_ref, a2_ref, g2_ref,
                 kk_ref, ka_ref, rk_ref, qt_ref, k_ref, v_ref, gl_ref, rwkv_outs):
    r_o, v_o, kkn_o, g_o, bonus_o, lw0_o, lw1_o, k0_o, k1_o, b0_o, b1_o = rwkv_outs
    i = pl.program_id(1)
    last = pl.num_programs(1) - 1
    rows = slice(lo, lo + n)
    ms = jnp.mean(xx * xx, axis=-1, keepdims=True)
    y = xx * lax.rsqrt(ms + NORM_EPS) * g_ref[...]
    u_all = y * (1.0 + mod_ref[0, 1:2, :]) + mod_ref[0, 0:1, :]
    ub = u_all[HALO:HALO + n].astype(BF16)
    z_all = _dot(u_all.astype(BF16), w_ref[:, 0:RWKV_COLS])
    k = _dot(ub, w_ref[:, K0:V0])
    v_ref[0, rows, :] = _dot(ub, w_ref[:, V0:G0]).astype(BF16)
    gl_ref[0, rows, :] = _dot(ub, w_ref[:, G0:IN_COLS]).astype(BF16)
    ones8 = ones8_ref[...]
    kn = k * lax.rsqrt(_segsum_pos(k * k, ones8[:KV_W, :KV_W]) * (1.0 / HD) + NORM_EPS) * kg_ref[...]
    k_ref[0, rows, :] = _rope(kn, cos_ref[0, rows, :], sin_ref[0, rows, :], KV_W).astype(BF16)

    z = z_all[HALO:HALO + n]
    row = lax.broadcasted_iota(jnp.int32, z.shape, 0) + lo
    zp = jnp.where((row == 0) & (i == 0), 0.0, z_all[HALO - 1:HALO - 1 + n])
    zn = jnp.where((row == tm - 1) & (i == last), 0.0, z_all[HALO + 1:HALO + 1 + n])
    zs = z + mup_ref[...] * (zp - z) + mun_ref[...] * (zn - z)
    r = zs[:, 0:RW]
    kr = zs[:, RW:2 * RW]
    vr = zs[:, 2 * RW:3 * RW]
    c0 = 3 * RW
    wd = zs[:, c0:c0 + 2 * LORA_W]
    ad = zs[:, c0 + 2 * LORA_W:c0 + 2 * LORA_W + 2 * LORA_A]
    gd = zs[:, c0 + 2 * LORA_W + 2 * LORA_A:RWKV_COLS]
    wl = _dot(jnp.tanh(wd).astype(BF16), w2_ref[...]) + w0_ref[...]
    lw = -math.exp(-0.5) * jax.nn.sigmoid(wl)
    al = jax.nn.sigmoid(_dot(ad.astype(BF16), a2_ref[...]) + a0_ref[...])
    g = _dot(jax.nn.sigmoid(gd).astype(BF16), g2_ref[...])
    kk = kr * kk_ref[...]
    kkn = kk / jnp.maximum(jnp.sqrt(_segsum_pos(kk * kk, ones8)), 1e-12)
    ka = ka_ref[...]
    a_0 = al[:, 0:RW]
    a_1 = al[:, RW:2 * RW]
    k_0 = kr * (1.0 + (a_0 - 1.0) * ka)
    k_1 = kr * (1.0 + (a_1 - 1.0) * ka)
    bonus = _segsum(r * (k_0 + k_1) * rk_ref[...], ones8) * vr
    r_o[0, rows, :] = r.astype(BF16)
    v_o[0, rows, :] = vr.astype(BF16)
    kkn_o[0, rows, :] = kkn.astype(BF16)
    g_o[0, rows, :] = g.astype(BF16)
    bonus_o[0, rows, :] = bonus.astype(BF16)
    lw0_o[0, rows, :] = lw[:, 0:RW]
    lw1_o[0, rows, :] = lw[:, RW:2 * RW]
    k0_o[0, rows, :] = k_0.astype(BF16)
    k1_o[0, rows, :] = k_1.astype(BF16)
    b0_o[0, rows, :] = (kkn * a_0).astype(BF16)
    b1_o[0, rows, :] = (kkn * a_1).astype(BF16)

    qt = _dot_nt(wq_ref[...], ub)
    qg = jnp.concatenate([qg_ref[...]] * (n // 128), axis=1)
    cq = cosq_ref[0, :, rows]
    sq = sinq_ref[0, :, rows]
    half = ROT // 2
    pieces = []
    for h in range(HEADS):
        xh = qt[h * HD:(h + 1) * HD]
        xn = (xh * lax.rsqrt(jnp.mean(xh * xh, axis=0, keepdims=True) + NORM_EPS)
              * qg[h * HD:(h + 1) * HD] * QK_SCALE)
        x1, x2 = xn[0:half], xn[half:ROT]
        pieces += [x1 * cq - x2 * sq, x2 * cq + x1 * sq, xn[ROT:]]
    qt_ref[0, :, rows] = jnp.concatenate(pieces, axis=0).astype(BF16)


def _inproj_call(x, mod3, norm1_g, w_rest, wq_t, cos_t, sin_t, cos_q, sin_q, qg, kg, ones,
                 mup, mun, w0c, w2c, a0c, a2c, g2b, k_k, k_a, r_k, tm):
    B, T, D = x.shape
    nh = tm // HALO
    nth = T // HALO
    row = lambda b, i: (b, i, 0)
    col = lambda b, i: (b, 0, i)
    const2 = lambda b, i: (0, 0)
    half = ROT // 2
    rw = pl.BlockSpec((1, tm, RW), row)
    vec = lambda n: pl.BlockSpec((1, n), const2)
    return pl.pallas_call(
        _inproj_kernel,
        grid=(B, T // tm),
        in_specs=[pl.BlockSpec((1, tm, D), row),
                  pl.BlockSpec((1, HALO, D), lambda b, i: (b, jnp.maximum(i * nh - 1, 0), 0)),
                  pl.BlockSpec((1, HALO, D), lambda b, i: (b, jnp.minimum((i + 1) * nh, nth - 1), 0)),
                  pl.BlockSpec((1, 6, D), lambda b, i: (b, 0, 0)),
                  vec(D),
                  pl.BlockSpec(w_rest.shape, const2),
                  pl.BlockSpec((RW, D), const2),
                  pl.BlockSpec((1, tm, KV_W), row),
                  pl.BlockSpec((1, tm, KV_W), row),
                  pl.BlockSpec((1, half, tm), col),
                  pl.BlockSpec((1, half, tm), col),
                  pl.BlockSpec((RW, 128), const2),
                  vec(KV_W),
                  pl.BlockSpec((RW, RW), const2),
                  vec(RWKV_COLS), vec(RWKV_COLS),
                  vec(2 * RW), pl.BlockSpec((2 * LORA_W, 2 * RW), const2),
                  vec(2 * RW), pl.BlockSpec((2 * LORA_A, 2 * RW), const2),
                  pl.BlockSpec((LORA_G, RW), const2),
                  vec(RW), vec(RW), vec(RW)],
        out_specs=[pl.BlockSpec((1, RW, tm), col),
                   pl.BlockSpec((1, tm, KV_W), row),
                   pl.BlockSpec((1, tm, KV_W), row),
                   pl.BlockSpec((1, tm, 2 * D), row)] + [rw] * 11,
        out_shape=[jax.ShapeDtypeStruct((B, RW, T), BF16),
                   jax.ShapeDtypeStruct((B, T, KV_W), BF16),
                   jax.ShapeDtypeStruct((B, T, KV_W), BF16),
                   jax.ShapeDtypeStruct((B, T, 2 * D), BF16)]
        + [jax.ShapeDtypeStruct((B, T, RW), F32 if i in (5, 6) else BF16) for i in range(11)],
        compiler_params=_params(("parallel", "parallel")),
        name="inproj",
    )(x, x, x, mod3, norm1_g, w_rest, wq_t, cos_t, sin_t, cos_q, sin_q, qg, kg, ones,
      mup, mun, w0c, w2c, a0c, a2c, g2b, k_k, k_a, r_k)


def _dir_operands(rev, nch, r, v, kk, lw, kd, bd):
    n = r.shape[0]
    ti = lax.broadcasted_iota(jnp.int32, (n, n), 0)
    si = lax.broadcasted_iota(jnp.int32, (n, n), 1)
    same = (ti // CHUNK) == (si // CHUNK)
    incl = ((si >= ti) if rev else (si <= ti)) & same
    mi = incl.astype(BF16)
    l1 = lw.astype(BF16)
    r1 = lw - l1.astype(F32)
    l2 = r1.astype(BF16)
    l3 = (r1 - l2.astype(F32)).astype(BF16)
    cum = _dot(mi, l1) + _dot(mi, l2) + _dot(mi, l3)
    wt = jnp.exp(cum)
    winv = jnp.exp(-cum)
    ops = dict(ah=-(kk * jnp.exp(cum - lw)), rt=r * wt, bt=bd * winv, kt=kd * winv, v=v)
    wcs, bhs, khs = [], [], []
    for j in range(nch):
        last = j * CHUNK if rev else (j + 1) * CHUNK - 1
        wc = wt[last:last + 1, :]
        rows = slice(j * CHUNK, (j + 1) * CHUNK)
        wcs.append(wc)
        bhs.append(ops["bt"][rows] * wc)
        khs.append(ops["kt"][rows] * wc)
    return ops, wcs, bhs, khs


def _scan_kernel(rf, vf, kkf, lwf, kf, bf, rb, vb, kkb, lwb, kb, bb, yf_ref, yb_ref, st_ref, *, nch):
    @pl.when(pl.program_id(1) == 0)
    def _():
        st_ref[...] = jnp.zeros_like(st_ref)

    c = CHUNK
    pw = 2 * HD
    npair = HEADS // 2
    ti = lax.broadcasted_iota(jnp.int32, (c, pw), 0)
    li = lax.broadcasted_iota(jnp.int32, (c, pw), 1)
    si = li & (HD - 1)
    head0 = li < HD
    blk = (ti // SUB) == (si // SUB)
    eye = ti == si
    two = lambda m: jnp.concatenate([m, m], axis=1)
    masks = {False: (two(si < ti), two(si <= ti)), True: (two(si > ti), two(si >= ti))}

    def bd(y):
        yb = y.astype(BF16)
        zero = jnp.zeros_like(yb)
        return jnp.concatenate([jnp.where(head0, yb, zero), jnp.where(head0, zero, yb)], axis=0)

    def bdcat(*ys):
        return jnp.concatenate([bd(y) for y in ys], axis=1)

    def diag_blocks(full):
        return jnp.where(head0, full[:c], full[c:])

    dirs = [(False, rf, vf, kkf, lwf, kf, bf), (True, rb, vb, kkb, lwb, kb, bb)]
    units = []
    for d, (rev, r_, v_, kk_, lw_, k_, b_) in enumerate(dirs):
        f32 = lambda ref: ref[0].astype(F32)
        ops, wcs, bhs, khs = _dir_operands(rev, nch, f32(r_), f32(v_), f32(kk_), lw_[0], f32(k_), f32(b_))
        for j in range(nch):
            rows = slice(j * c, (j + 1) * c)
            for p in range(npair):
                sl = slice(p * pw, (p + 1) * pw)
                u = dict(d=d, j=j, p=p, rev=rev, wc=wcs[j][:, sl],
                         bkh=jnp.concatenate([bhs[j][:, sl], khs[j][:, sl]], axis=0).astype(BF16))
                for name in ("ah", "rt", "bt", "kt", "v"):
                    u[name] = ops[name][rows, sl]
                units.append(u)

    for u in units:
        x1 = jnp.concatenate([u["ah"], u["rt"]], axis=0).astype(BF16)
        u["lbk"] = _dot_nt(x1, jnp.concatenate([bd(u["bt"]), bd(u["kt"])], axis=0))
    for u in units:
        strict, incl = masks[u["rev"]]
        top = jnp.where(strict, u["lbk"][:c], 0.0)
        bot = jnp.where(incl, u["lbk"][c:], 0.0)
        lab = top[:, :pw]
        u["mrb"] = bot[:, :pw].astype(BF16)
        u["ld"] = jnp.where(blk, lab, 0.0)
        u["lo"] = lab - u["ld"]
        u["lm"] = jnp.concatenate([top[:, pw:], bot[:, pw:]], axis=0).astype(BF16)
    for u in units:
        u["a2"] = _dot(u["ld"].astype(BF16), bd(u["ld"]))
        u["lkv"] = _dot(u["lm"], bd(u["v"]))
    for u in units:
        s = _dot(jnp.concatenate([u["a2"], u["ld"]], axis=0).astype(BF16), bd(u["a2"]))
        u["a4"] = s[:c]
        u["t"] = jnp.where(eye, 1.0, 0.0) + u["ld"] + u["a2"] + s[c:]
    for u in units:
        s = _dot(jnp.concatenate([u["a4"], u["t"]], axis=0).astype(BF16), bd(u["a4"]))
        u["a8"] = s[:c]
        u["t"] = u["t"] + s[c:]
    for u in units:
        u["td"] = u["t"] + _dot(u["t"].astype(BF16), bd(u["a8"]))
    for u in units:
        x = _dot(u["td"].astype(BF16), bdcat(u["lo"], u["ah"], u["lkv"][:c]))
        u["n"] = x[:, :pw]
        u["z"] = x[:, pw:]
    for u in units:
        s = _dot(u["n"].astype(BF16), bdcat(u["n"], u["z"][:, :pw], u["z"][:, pw:]))
        u["n2"] = s[:, :pw]
        u["z"] = u["z"] + s[:, pw:]
    for u in units:
        u["z"] = u["z"] + _dot(u["n2"].astype(BF16), bdcat(u["z"][:, :pw], u["z"][:, pw:]))
    for u in units:
        z = u["z"]
        y10 = (jnp.concatenate([u["rt"], u["lkv"][c:]], axis=1)
               + _dot(u["mrb"], bdcat(z[:, :pw], z[:, pw:])))
        vpad = jnp.concatenate([jnp.zeros_like(u["v"]), u["v"]], axis=1)
        full = _dot_tn(u["bkh"], jnp.concatenate([z, vpad], axis=0).astype(BF16))
        g = diag_blocks(full[:, :pw]) + jnp.where(eye, u["wc"], 0.0)
        u["gy"] = jnp.concatenate([g, y10[:, :pw]], axis=0).astype(BF16)
        u["h0"] = diag_blocks(full[:, pw:])
        u["y0"] = y10[:, pw:]

    ys = {}
    for d in range(2):
        order = range(nch - 1, -1, -1) if d == 1 else range(nch)
        zs = [st_ref[d, p] for p in range(npair)]
        for j in order:
            for p in range(npair):
                u = units[(d * nch + j) * npair + p]
                m = _dot(u["gy"], bd(zs[p]))
                zs[p] = m[:c] + u["h0"]
                ys[(d, j, p)] = m[c:] + u["y0"]
        for p in range(npair):
            st_ref[d, p] = zs[p]
    for d, y_ref in enumerate((yf_ref, yb_ref)):
        y_ref[0] = jnp.concatenate(
            [jnp.concatenate([ys[(d, j, p)] for p in range(npair)], axis=-1) for j in range(nch)],
            axis=0).astype(BF16)


def _scan_call(r, v, kkn, lw0, lw1, k0, k1, b0, b1, nch):
    B, T, _ = r.shape
    rows = nch * CHUNK
    ns = T // rows
    fwd = pl.BlockSpec((1, rows, RW), lambda b, c: (b, c, 0))
    bwd = pl.BlockSpec((1, rows, RW), lambda b, c: (b, ns - 1 - c, 0))
    return pl.pallas_call(
        functools.partial(_scan_kernel, nch=nch),
        grid=(B, ns),
        in_specs=[fwd] * 6 + [bwd] * 6,
        out_specs=[fwd, bwd],
        out_shape=[jax.ShapeDtypeStruct((B, T, RW), BF16)] * 2,
        scratch_shapes=[pltpu.VMEM((2, HEADS // 2, HD, 2 * HD), F32)],
        compiler_params=_params(("parallel", "arbitrary")),
        name="scan",
    )(r, v, kkn, lw0, k0, b0, r, v, kkn, lw1, k1, b1)


def _attention_scores(sink_ref, qt_ref, k_ref, v_ref, n, qb):
    t = k_ref.shape[1]
    band = 3 * BLOCK
    cols = GROUP * BLOCK
    ki = lax.broadcasted_iota(jnp.int32, (band, BLOCK), 0)
    qi = lax.broadcasted_iota(jnp.int32, (band, BLOCK), 1)
    lane_head = lax.broadcasted_iota(jnp.int32, (1, cols), 1) // BLOCK
    zeros = jnp.zeros((HD, cols), BF16)
    units = []
    for j in range(qb):
        blk = n * qb + j
        start = pl.multiple_of(jnp.clip((blk - 1) * BLOCK, 0, t - band), BLOCK)
        bias = jnp.where(jnp.abs(start + ki - (blk * BLOCK + qi)) <= BLOCK, 0.0, NEG)
        bias = jnp.concatenate([bias] * GROUP, axis=1)
        kb = k_ref[0, pl.ds(start, band), :]
        vb = v_ref[0, pl.ds(start, band), :]
        for g in range(KV_HEADS):
            qg = jnp.concatenate([qt_ref[0, (g * GROUP + i) * HD:(g * GROUP + i + 1) * HD,
                                         j * BLOCK:(j + 1) * BLOCK] for i in range(GROUP)], axis=1)
            rhs = jnp.concatenate([qg if gg == g else zeros for gg in range(KV_HEADS)], axis=0)
            sk = jnp.zeros((1, cols), F32)
            for i in range(GROUP):
                sk = jnp.where(lane_head == i, sink_ref[g * GROUP + i] * LOG2E, sk)
            units.append(dict(j=j, g=g, kb=kb, vb=vb, rhs=rhs, sk=sk, bias=bias))
    for u in units:
        u["s"] = _dot(u["kb"], u["rhs"]) + u["bias"]
    return units


def _attention_finish(units, qb):
    for u in units:
        u["m"] = jnp.maximum(jnp.max(u["s"], axis=0, keepdims=True), u["sk"])
    for u in units:
        p = jnp.exp2(u["s"] - u["m"])
        u["r"] = 1.0 / (jnp.sum(p, axis=0, keepdims=True) + jnp.exp2(u["sk"] - u["m"]))
        u["p"] = p.astype(BF16)
    for u in units:
        g = u["g"]
        u["o"] = _dot_tn(u["vb"], u["p"])[g * HD:(g + 1) * HD] * u["r"]
    out = {(u["j"], u["g"]): u["o"] for u in units}
    return jnp.concatenate(
        [jnp.concatenate([out[(j, h // GROUP)][:, (h % GROUP) * BLOCK:(h % GROUP + 1) * BLOCK]
                          for j in range(qb)], axis=1) for h in range(HEADS)], axis=0)


def _merge_kernel(sink_ref, x_ref, mod_ref, yf_ref, yb_ref, bonus_ref, g_ref, qt_ref, k_ref, v_ref, gl_ref,
                  gnw_ref, gnb_ref, ones_ref, pr_ref, pa_ref, wo_ref, n2g_ref, wr_ref,
                  x1_ref, u2_ref, aff_ref):
    qb = x_ref.shape[1] // BLOCK
    att = _attention_scores(sink_ref, qt_ref, k_ref, v_ref, pl.program_id(1), qb)
    ones = ones_ref[...]
    y = yf_ref[0].astype(F32) + yb_ref[0].astype(F32)
    mu = _segsum(y, ones) * (1.0 / HD)
    yc = y - mu
    var = _segsum_pos(yc * yc, ones) * (1.0 / HD)
    yn = yc * lax.rsqrt(var + GN_EPS) * gnw_ref[...] + gnb_ref[...]
    ya = ((yn + bonus_ref[0].astype(F32)) * g_ref[0].astype(F32)).astype(BF16)
    pa = _dot(ya, pr_ref[...])
    gates = jax.nn.sigmoid(gl_ref[0].astype(F32))
    yatt_t = _attention_finish(att, qb).astype(BF16)
    pb = _dot_tn(yatt_t, pa_ref[...])
    m = gates[:, :D_MODEL] * pa + gates[:, D_MODEL:] * pb
    x1 = x_ref[0] + mod_ref[0, 2:3, :] * _dot(m.astype(BF16), wo_ref[...])
    x1_ref[0] = x1
    ms = jnp.mean(x1 * x1, axis=-1, keepdims=True)
    u2 = x1 * lax.rsqrt(ms + NORM_EPS) * n2g_ref[...] * (1.0 + mod_ref[0, 4:5, :]) + mod_ref[0, 3:4, :]
    u2_ref[0] = u2
    u2h = u2.astype(BF16)
    u2l = (u2 - u2h.astype(F32)).astype(BF16)
    wr = wr_ref[...]
    wrh = wr.astype(BF16)
    wrl = (wr - wrh.astype(F32)).astype(BF16)
    t1 = _dot_nt(jnp.concatenate([wrh, wrl], axis=0), u2h)
    logits = t1[:N_EXPERTS] + t1[N_EXPERTS:] + _dot_nt(wrh, u2l)
    e = jnp.exp(logits - jnp.max(logits, axis=0, keepdims=True))
    aff_ref[0] = e / jnp.sum(e, axis=0, keepdims=True)


def _merge_call(sink, x, mod3, yf, yb, bonus, g, qt, k, v, gl, gnw, gnb, ones, p_r, p_a, w_o, n2g, w_rt, tm):
    B, T, D = x.shape
    assert T >= 3 * BLOCK and tm % BLOCK == 0
    row = lambda b, i: (b, i, 0)
    const2 = lambda b, i: (0, 0)
    rw = pl.BlockSpec((1, tm, RW), row)
    kv = pl.BlockSpec((1, T, KV_W), lambda b, i: (b, 0, 0))
    return pl.pallas_call(
        _merge_kernel,
        grid=(B, T // tm),
        in_specs=[pl.BlockSpec(memory_space=pltpu.SMEM),
                  pl.BlockSpec((1, tm, D), row),
                  pl.BlockSpec((1, 6, D), lambda b, i: (b, 0, 0)),
                  rw, rw, rw, rw,
                  pl.BlockSpec((1, RW, tm), lambda b, i: (b, 0, i)), kv, kv,
                  pl.BlockSpec((1, tm, 2 * D), row),
                  pl.BlockSpec((1, RW), const2),
                  pl.BlockSpec((1, RW), const2),
                  pl.BlockSpec((RW, RW), const2),
                  pl.BlockSpec((RW, D), const2),
                  pl.BlockSpec((RW, D), const2),
                  pl.BlockSpec((D, D), const2),
                  pl.BlockSpec((1, D), const2),
                  pl.BlockSpec((N_EXPERTS, D), const2)],
        out_specs=[pl.BlockSpec((1, tm, D), row),
                   pl.BlockSpec((1, tm, D), row),
                   pl.BlockSpec((1, N_EXPERTS, tm), lambda b, i: (b, 0, i))],
        out_shape=[jax.ShapeDtypeStruct((B, T, D), F32),
                   jax.ShapeDtypeStruct((B, T, D), F32),
                   jax.ShapeDtypeStruct((B, N_EXPERTS, T), F32)],
        compiler_params=_params(("parallel", "parallel")),
        name="merge",
    )(sink, x, mod3, yf, yb, bonus, g, qt, k, v, gl, gnw, gnb, ones, p_r, p_a, w_o, n2g, w_rt)


def _excl_prefix(mask, tri):
    e, t = mask.shape
    mb = mask.astype(BF16)
    carry = jnp.zeros((e, 1), F32)
    outs = []
    for j in range(t // 128):
        tile = mb[:, j * 128:(j + 1) * 128]
        outs.append(_dot(tile, tri) + carry)
        carry = carry + jnp.sum(tile.astype(F32), axis=1, keepdims=True)
    return jnp.concatenate(outs, axis=-1)


def _route_kernel(aff_ref, tri_ref, dig_ref, slot_ref, idx_ref, *, cap):
    nb, ne, t = aff_ref.shape
    a = aff_ref[...].reshape(nb * ne, t)
    e = a.shape[0]
    capf = jnp.float32(cap)

    def cond(s):
        it, lo, hi = s
        mid = 0.5 * (lo + hi)
        still_open = jnp.max(jnp.where((mid > lo) & (mid < hi), 1.0, 0.0))
        return (it < BISECT_ITERS) & (still_open > 0.0)

    def body(s):
        it, lo, hi = s
        mid = 0.5 * (lo + hi)
        ge = jnp.sum((a >= mid).astype(F32), axis=1, keepdims=True) >= capf
        return it + 1, jnp.where(ge, mid, lo), jnp.where(ge, hi, mid)

    lo0 = jnp.zeros((e, 1), F32)
    hi0 = jnp.full((e, 1), 2.0, F32)
    _, lo, hi = lax.while_loop(cond, body, (jnp.int32(0), lo0, hi0))
    gt = a >= hi
    eq = (a >= lo) & (a < hi)
    need = capf - jnp.sum(gt.astype(F32), axis=1, keepdims=True)
    tri = tri_ref[...]
    sel = gt | (eq & (_excl_prefix(eq, tri) < need))
    pos = _excl_prefix(sel, tri)
    slot_ref[...] = jnp.where(sel, pos.astype(jnp.int32), -1).reshape(nb, ne, t)

    digits = dig_ref[...]
    assert cap <= 256
    slots = lax.broadcasted_iota(jnp.int32, (cap, t), 0).astype(BF16)
    one, zero = jnp.ones((cap, t), BF16), jnp.zeros((cap, t), BF16)

    def row_body(r, carry):
        srow = slot_ref[r // ne, pl.ds(r % ne, 1), :].astype(BF16)
        d = _dot_nt(digits, jnp.where(slots == srow, one, zero))
        idx_ref[r] = (d[0:1] * TOKEN_RADIX + d[1:2]).astype(jnp.int32)
        return carry

    lax.fori_loop(0, nb * ne, row_body, 0, unroll=8)


def _route_call(aff, tri, digits, cap):
    B, E, T = aff.shape
    return pl.pallas_call(
        functools.partial(_route_kernel, cap=cap),
        grid=(1,),
        in_specs=[pl.BlockSpec((B, E, T), lambda i: (0, 0, 0)),
                  pl.BlockSpec((128, 128), lambda i: (0, 0)),
                  pl.BlockSpec((8, T), lambda i: (0, 0))],
        out_specs=[pl.BlockSpec((B, E, T), lambda i: (0, 0, 0)),
                   pl.BlockSpec((B * E, 1, cap), lambda i: (0, 0, 0))],
        out_shape=[jax.ShapeDtypeStruct((B, E, T), jnp.int32),
                   jax.ShapeDtypeStruct((B * E, 1, cap), jnp.int32)],
        compiler_params=_params(("arbitrary",)),
        name="route",
    )(aff, tri, digits)


def _moe_kernel(idx0_ref, idx1_ref, u2_ref, slot_ref, aff_ref, wg_ref, wu_ref, wd_ref, o_ref,
                acc_ref, xa_ref, xb_ref, *, cap, tt):
    e = pl.program_id(1)
    t = u2_ref.shape[1]

    def gather(idx_ref, dst_ref):
        for i in range(cap):
            dst_ref[pl.ds(i, 1), :] = u2_ref[0, pl.ds(idx_ref[0, 0, i], 1), :]

    def expert(x_ref):
        xe = x_ref[...].astype(BF16)
        hg = _dot(xe, wg_ref[0])
        hu = _dot(xe, wu_ref[0])
        h = (hg * jax.nn.sigmoid(hg) * hu).astype(BF16)
        ye = _dot(h, wd_ref[0]).astype(BF16)
        hit = lax.broadcasted_iota(jnp.int32, (cap, t), 0) == slot_ref[0, pl.ds(e, 1), :]
        wsc = jnp.where(hit, aff_ref[0, pl.ds(e, 1), :], 0.0).astype(BF16)
        for j in range(t // tt):
            acc_ref[j * tt:(j + 1) * tt, :] += _dot_tn(wsc[:, j * tt:(j + 1) * tt], ye)

    @pl.when(e == 0)
    def _():
        acc_ref[...] = jnp.zeros_like(acc_ref)
        gather(idx0_ref, xa_ref)

    @pl.when(e % 2 == 0)
    def _():
        gather(idx1_ref, xb_ref)
        expert(xa_ref)

    @pl.when(e % 2 == 1)
    def _():
        gather(idx1_ref, xa_ref)
        expert(xb_ref)

    @pl.when(e == pl.num_programs(1) - 1)
    def _():
        o_ref[0] = acc_ref[...].astype(BF16)


def _moe_call(u2, slot, aff, idx, wg, wu, wd, cap):
    B, T, D = u2.shape
    E = wg.shape[0]
    F = wg.shape[2]
    assert E % 2 == 0
    tt = min(T, 512)
    smem_row = lambda im: pl.BlockSpec((1, 1, cap), im, memory_space=pltpu.SMEM)
    return pl.pallas_call(
        functools.partial(_moe_kernel, cap=cap, tt=tt),
        grid=(B, E),
        in_specs=[smem_row(lambda b, e: (b * E + e, 0, 0)),
                  smem_row(lambda b, e: (b * E + jnp.minimum(e + 1, E - 1), 0, 0)),
                  pl.BlockSpec((1, T, D), lambda b, e: (b, 0, 0)),
                  pl.BlockSpec((1, E, T), lambda b, e: (b, 0, 0)),
                  pl.BlockSpec((1, E, T), lambda b, e: (b, 0, 0)),
                  pl.BlockSpec((1, D, F), lambda b, e: (e, 0, 0)),
                  pl.BlockSpec((1, D, F), lambda b, e: (e, 0, 0)),
                  pl.BlockSpec((1, F, D), lambda b, e: (e, 0, 0))],
        out_specs=pl.BlockSpec((1, T, D), lambda b, e: (b, 0, 0)),
        out_shape=jax.ShapeDtypeStruct((B, T, D), BF16),
        scratch_shapes=[pltpu.VMEM((T, D), F32), pltpu.VMEM((cap, D), F32), pltpu.VMEM((cap, D), F32)],
        compiler_params=pltpu.CompilerParams(dimension_semantics=("parallel", "arbitrary"),
                                             vmem_limit_bytes=MOE_VMEM_LIMIT),
        name="moe",
    )(idx, idx, u2, slot, aff, wg, wu, wd)


def _final_kernel(x1_ref, mod_ref, moe_ref, o_ref):
    o_ref[0] = x1_ref[0] + mod_ref[0, 5:6, :] * moe_ref[0].astype(F32)


def _final_call(x1, mod3, moe, tm):
    B, T, D = x1.shape
    row = lambda b, i: (b, i, 0)
    return pl.pallas_call(
        _final_kernel,
        grid=(B, T // tm),
        in_specs=[pl.BlockSpec((1, tm, D), row),
                  pl.BlockSpec((1, 6, D), lambda b, i: (b, 0, 0)),
                  pl.BlockSpec((1, tm, D), row)],
        out_specs=pl.BlockSpec((1, tm, D), row),
        out_shape=jax.ShapeDtypeStruct((B, T, D), F32),
        compiler_params=_params(("parallel", "parallel")),
        name="final",
    )(x1, mod3, moe)


def _blockdiag2(w):
    z = jnp.zeros_like(w[0])
    return jnp.concatenate([jnp.concatenate([w[0], z], axis=1), jnp.concatenate([z, w[1]], axis=1)], axis=0)


def _layer(x, mod3, cos_t, sin_t, cos_q, sin_q, ones, tri, norm1_g, w_in, mu_prev, mu_next, rwkv_w0, rwkv_w2, rwkv_a0,
           rwkv_a2, rwkv_g2, rwkv_k_k, rwkv_k_a, rwkv_r_k, rwkv_gn_w, rwkv_gn_b, q_norm_g, k_norm_g,
           attn_sink, p_rwkv, p_attn, w_out, norm2_g, w_router, w_gate, w_up, w_down):
    B, T, D = x.shape
    tm = TM_ROWS if T % TM_ROWS == 0 else min(T, 256)
    cap = CAP_FACTOR * T // N_EXPERTS
    row = lambda a: a.reshape(1, -1)
    w_b = w_in.astype(BF16)
    qg = jnp.broadcast_to(jnp.tile(q_norm_g, HEADS)[:, None], (RW, 128))
    qt, k, v, gl, r, vv, kkn, g, bonus, lw0, lw1, k0, k1, b0, b1 = _inproj_call(
        x, mod3, row(norm1_g), w_b, w_b[:, Q0:K0].T, cos_t, sin_t, cos_q, sin_q,
        qg, row(jnp.tile(k_norm_g, KV_HEADS)), ones,
        row(mu_prev), row(mu_next), row(rwkv_w0), _blockdiag2(rwkv_w2).astype(BF16),
        row(rwkv_a0), _blockdiag2(rwkv_a2).astype(BF16), rwkv_g2.astype(BF16),
        row(rwkv_k_k), row(rwkv_k_a), row(rwkv_r_k), tm)
    yf, yb = _scan_call(r, vv, kkn, lw0, lw1, k0, k1, b0, b1, SCAN_CHUNKS)
    x1, u2, aff = _merge_call(attn_sink, x, mod3, yf, yb, bonus, g, qt, k, v, gl, row(rwkv_gn_w), row(rwkv_gn_b), ones,
                              p_rwkv.astype(BF16), p_attn.astype(BF16), w_out.astype(BF16),
                              row(norm2_g), w_router.T, tm)
    assert T <= 256 * TOKEN_RADIX
    tok = jnp.arange(T)
    digits = jnp.zeros((8, T), BF16).at[0].set((tok // TOKEN_RADIX).astype(BF16)).at[1].set(
        (tok % TOKEN_RADIX).astype(BF16))
    slot, idx = _route_call(aff, tri, digits, cap)
    moe = _moe_call(u2, slot, aff, idx, w_gate.astype(BF16), w_up.astype(BF16), w_down.astype(BF16), cap)
    return _final_call(x1, mod3, moe, TM_FINAL if T % TM_FINAL == 0 else tm)


def kernel(x, c, positions, w_ada, b_ada, norm1_g, w_in, mu_prev, mu_next, rwkv_w0, rwkv_w2, rwkv_a0, rwkv_a2, rwkv_g2, rwkv_k_k, rwkv_k_a, rwkv_r_k, rwkv_gn_w, rwkv_gn_b, q_norm_g, k_norm_g, attn_sink, p_rwkv, p_attn, w_out, norm2_g, w_router, w_gate, w_up, w_down):
    B, T, D = x.shape
    depth = w_ada.shape[0]
    half = ROT // 2
    inv_freq = ROPE_THETA ** (-jnp.arange(0, ROT, 2, dtype=F32) / ROT)
    ang = positions.astype(F32)[..., None] * inv_freq
    cos8, sin8 = jnp.cos(ang), jnp.sin(ang)
    pad1 = jnp.ones((B, T, HD - ROT), F32)
    pad0 = jnp.zeros((B, T, HD - ROT), F32)
    cos_t = jnp.tile(jnp.concatenate([cos8, cos8, pad1], axis=-1), (1, 1, KV_HEADS))
    sin_t = jnp.tile(jnp.concatenate([-sin8, sin8, pad0], axis=-1), (1, 1, KV_HEADS))
    seg = jnp.arange(RW) // HD
    ones = (seg[:, None] == seg[None, :]).astype(BF16)
    idx = jnp.arange(128)
    tri = (idx[:, None] < idx[None, :]).astype(BF16)
    cos_q, sin_q = jnp.swapaxes(cos8, 1, 2), jnp.swapaxes(sin8, 1, 2)
    for l in range(depth):
        mod3 = _mod_call(c, w_ada[l], b_ada[l]).reshape(B, 6, D)
        x = _layer(x, mod3, cos_t, sin_t, cos_q, sin_q, ones, tri, norm1_g[l], w_in[l], mu_prev[l], mu_next[l],
                   rwkv_w0[l], rwkv_w2[l], rwkv_a0[l], rwkv_a2[l], rwkv_g2[l], rwkv_k_k[l], rwkv_k_a[l],
                   rwkv_r_k[l], rwkv_gn_w[l], rwkv_gn_b[l], q_norm_g[l], k_norm_g[l], attn_sink[l],
                   p_rwkv[l], p_attn[l], w_out[l], norm2_g[l], w_router[l], w_gate[l], w_up[l], w_down[l])
    return x
```

```python
import functools
import math

import jax
import jax.numpy as jnp
from jax import lax
from jax.experimental import pallas as pl
from jax.experimental.pallas import tpu as pltpu

F32 = jnp.float32
BF16 = jnp.bfloat16

D_MODEL = 1024
RW = 512
HEADS = 8
HD = 64
LORA_W = 64
LORA_A = 64
LORA_G = 128
GN_EPS = HD * 1e-5
KV_HEADS = 2
GROUP = HEADS // KV_HEADS
KV_W = KV_HEADS * HD
BLOCK = 128
ROPE_THETA = 500000.0
ROT = HD // 4
N_EXPERTS = 16
CAP_FACTOR = 2
NORM_EPS = 1e-6
RWKV_COLS = 3 * RW + 2 * LORA_W + 2 * LORA_A + LORA_G
Q0 = RWKV_COLS
K0 = Q0 + RW
V0 = K0 + KV_W
G0 = V0 + KV_W
IN_COLS = G0 + 2 * D_MODEL
CHUNK = 64
SUB = 16
SCAN_CHUNKS = 4
TM_FINAL = 2048
TM_ROWS = 512
TOKEN_RADIX = 64
BISECT_ITERS = 160
NEG = -1e30
LOG2E = math.log2(math.e)
QK_SCALE = HD ** -0.5 * LOG2E
VMEM_LIMIT = 56 * 1024 * 1024
MOE_VMEM_LIMIT = 60 * 1024 * 1024


def _dot(a, b):
    return jnp.dot(a, b, preferred_element_type=F32)


def _dot_nt(a, b):
    return lax.dot_general(a, b, (((1,), (1,)), ((), ())), preferred_element_type=F32)


def _dot_tn(a, b):
    return lax.dot_general(a, b, (((0,), (0,)), ((), ())), preferred_element_type=F32)


def _segsum(x, ones):
    xh = x.astype(BF16)
    xl = (x - xh.astype(F32)).astype(BF16)
    return _dot(xh, ones) + _dot(xl, ones)


def _segsum_pos(x, ones):
    return _dot(x.astype(BF16), ones)


def _params(sem):
    return pltpu.CompilerParams(dimension_semantics=sem, vmem_limit_bytes=VMEM_LIMIT)


def _mod_kernel(c_ref, w_ref, b_ref, o_ref):
    c = c_ref[...]
    ca = c * jax.nn.sigmoid(c)
    o_ref[...] = jnp.dot(ca, w_ref[...], preferred_element_type=F32,
                         precision=lax.Precision.HIGHEST) + b_ref[...]


def _mod_call(c, w_ada, b_ada):
    B, D = c.shape
    n = w_ada.shape[1] // D
    return pl.pallas_call(
        _mod_kernel,
        grid=(n,),
        in_specs=[pl.BlockSpec((B, D), lambda j: (0, 0)),
                  pl.BlockSpec((D, D), lambda j: (0, j)),
                  pl.BlockSpec((1, D), lambda j: (0, j))],
        out_specs=pl.BlockSpec((B, D), lambda j: (0, j)),
        out_shape=jax.ShapeDtypeStruct((B, n * D), F32),
        compiler_params=_params(("arbitrary",)),
        name="mod",
    )(c, w_ada, b_ada.reshape(1, -1))


def _rope(xn, cos, sin, width):
    lane = lax.broadcasted_iota(jnp.int32, xn.shape, 1) & (HD - 1)
    rot = jnp.where(lane < ROT // 2, pltpu.roll(xn, width - ROT // 2, 1), pltpu.roll(xn, ROT // 2, 1))
    return xn * cos + rot * sin


HALO = 8
INPROJ_SUBTILES = 1


def _inproj_kernel(x_ref, xp_ref, xn_ref, mod_ref, g_ref, w_ref, wq_ref, place_ref, pad_ref, cosq_ref, sinq_ref,
                   qg_ref, kg_ref, ones8_ref,
                   mup_ref, mun_ref, w0_ref, w2_ref, a0_ref, a2_ref, g2_ref, kk_ref, ka_ref, rk_ref,
                   qt_ref, k_ref, v_ref, gl_ref,
                   r_o, v_o, kkn_o, g_o, bonus_o, lw0_o, lw1_o, k0_o, k1_o, b0_o, b1_o):
    tm = x_ref.shape[1]
    sub = tm // INPROJ_SUBTILES
    xx = jnp.concatenate([xp_ref[0], x_ref[0], xn_ref[0]], axis=0)
    for lo in range(0, tm, sub):
        _inproj_rows(lo, sub, xx[lo:lo + sub + 2 * HALO], tm, mod_ref, g_ref, w_ref, wq_ref, place_ref, pad_ref,
                     cosq_ref, sinq_ref, qg_ref, kg_ref, ones8_ref, mup_ref, mun_ref, w0_ref, w2_ref, a0_ref,
                     a2_ref, g2_ref, kk_ref, ka_ref, rk_ref, qt_ref, k_ref, v_ref, gl_ref,
                     (r_o, v_o, kkn_o, g_o, bonus_o, lw0_o, lw1_o, k0_o, k1_o, b0_o, b1_o))


def _inproj_rows(lo, n, xx, tm, mod_ref, g_ref, w_ref, wq_ref, place_ref, pad_ref, cosq_ref, sinq_ref,
                 qg_ref, kg_ref, ones8_ref, mup_ref, mun_ref, w0_ref, w2_ref, a0_ref, a2_ref, g2_ref,
                 kk_ref, ka_ref, rk_ref, qt_ref, k_ref, v_ref, gl_ref, rwkv_outs):
    r_o, v_o, kkn_o, g_o, bonus_o, lw0_o, lw1_o, k0_o, k1_o, b0_o, b1_o = rwkv_outs
    i = pl.program_id(1)
    last = pl.num_programs(1) - 1
    rows = slice(lo, lo + n)
    ms = jnp.mean(xx * xx, axis=-1, keepdims=True)
    y = xx * lax.rsqrt(ms + NORM_EPS) * g_ref[...]
    u_all = y * (1.0 + mod_ref[0, 1:2, :]) + mod_ref[0, 0:1, :]
    ub = u_all[HALO:HALO + n].astype(BF16)
    z_all = _dot(u_all.astype(BF16), w_ref[:, 0:RWKV_COLS])
    k = _dot(ub, w_ref[:, K0:V0])
    v_ref[0, rows, :] = _dot(ub, w_ref[:, V0:G0]).astype(BF16)
    gl_ref[0, rows, :] = _dot(ub, w_ref[:, G0:IN_COLS]).astype(BF16)
    ones8 = ones8_ref[...]
    kn = k * lax.rsqrt(_segsum_pos(k * k, ones8[:KV_W, :KV_W]) * (1.0 / HD) + NORM_EPS) * kg_ref[...]
    def place(t8, e):
        hi = t8.astype(BF16).astype(F32)
        return _dot_tn(hi, e) + _dot_tn(t8 - hi, e)

    cos_k = place(cosq_ref[0, :, rows], place_ref[0]) + pad_ref[...]
    sin_k = place(sinq_ref[0, :, rows], place_ref[1])
    k_ref[0, rows, :] = _rope(kn, cos_k, sin_k, KV_W).astype(BF16)

    z = z_all[HALO:HALO + n]
    row = lax.broadcasted_iota(jnp.int32, z.shape, 0) + lo
    zp = jnp.where((row == 0) & (i == 0), 0.0, z_all[HALO - 1:HALO - 1 + n])
    zn = jnp.where((row == tm - 1) & (i == last), 0.0, z_all[HALO + 1:HALO + 1 + n])
    zs = z + mup_ref[...] * (zp - z) + mun_ref[...] * (zn - z)
    r = zs[:, 0:RW]
    kr = zs[:, RW:2 * RW]
    vr = zs[:, 2 * RW:3 * RW]
    c0 = 3 * RW
    wd = zs[:, c0:c0 + 2 * LORA_W]
    ad = zs[:, c0 + 2 * LORA_W:c0 + 2 * LORA_W + 2 * LORA_A]
    gd = zs[:, c0 + 2 * LORA_W + 2 * LORA_A:RWKV_COLS]
    wl = _dot(jnp.tanh(wd).astype(BF16), w2_ref[...]) + w0_ref[...]
    lw = -math.exp(-0.5) * jax.nn.sigmoid(wl)
    al = jax.nn.sigmoid(_dot(ad.astype(BF16), a2_ref[...]) + a0_ref[...])
    g = _dot(jax.nn.sigmoid(gd).astype(BF16), g2_ref[...])
    kk = kr * kk_ref[...]
    kkn = kk / jnp.maximum(jnp.sqrt(_segsum_pos(kk * kk, ones8)), 1e-12)
    ka = ka_ref[...]
    a_0 = al[:, 0:RW]
    a_1 = al[:, RW:2 * RW]
    k_0 = kr * (1.0 + (a_0 - 1.0) * ka)
    k_1 = kr * (1.0 + (a_1 - 1.0) * ka)
    bonus = _segsum(r * (k_0 + k_1) * rk_ref[...], ones8) * vr
    r_o[0, rows, :] = r.astype(BF16)
    v_o[0, rows, :] = vr.astype(BF16)
    kkn_o[0, rows, :] = kkn.astype(BF16)
    g_o[0, rows, :] = g.astype(BF16)
    bonus_o[0, rows, :] = bonus.astype(BF16)
    lw0_o[0, rows, :] = lw[:, 0:RW]
    lw1_o[0, rows, :] = lw[:, RW:2 * RW]
    k0_o[0, rows, :] = k_0.astype(BF16)
    k1_o[0, rows, :] = k_1.astype(BF16)
    b0_o[0, rows, :] = (kkn * a_0).astype(BF16)
    b1_o[0, rows, :] = (kkn * a_1).astype(BF16)

    qt = _dot_nt(wq_ref[...], ub)
    qg = jnp.concatenate([qg_ref[...]] * (n // 128), axis=1)
    cq = cosq_ref[0, :, rows]
    sq = sinq_ref[0, :, rows]
    half = ROT // 2
    pieces = []
    for h in range(HEADS):
        xh = qt[h * HD:(h + 1) * HD]
        xn = (xh * lax.rsqrt(jnp.mean(xh * xh, axis=0, keepdims=True) + NORM_EPS)
              * qg[h * HD:(h + 1) * HD] * QK_SCALE)
        x1, x2 = xn[0:half], xn[half:ROT]
        pieces += [x1 * cq - x2 * sq, x2 * cq + x1 * sq, xn[ROT:]]
    qt_ref[0, :, rows] = jnp.concatenate(pieces, axis=0).astype(BF16)


def _inproj_call(x, mod3, norm1_g, w_rest, wq_t, rope_place, rope_pad, cos_q, sin_q, qg, kg, ones,
                 mup, mun, w0c, w2c, a0c, a2c, g2b, k_k, k_a, r_k, tm):
    B, T, D = x.shape
    nh = tm // HALO
    nth = T // HALO
    row = lambda b, i: (b, i, 0)
    col = lambda b, i: (b, 0, i)
    const2 = lambda b, i: (0, 0)
    half = ROT // 2
    rw = pl.BlockSpec((1, tm, RW), row)
    vec = lambda n: pl.BlockSpec((1, n), const2)
    return pl.pallas_call(
        _inproj_kernel,
        grid=(B, T // tm),
        in_specs=[pl.BlockSpec((1, tm, D), row),
                  pl.BlockSpec((1, HALO, D), lambda b, i: (b, jnp.maximum(i * nh - 1, 0), 0)),
                  pl.BlockSpec((1, HALO, D), lambda b, i: (b, jnp.minimum((i + 1) * nh, nth - 1), 0)),
                  pl.BlockSpec((1, 6, D), lambda b, i: (b, 0, 0)),
                  vec(D),
                  pl.BlockSpec(w_rest.shape, const2),
                  pl.BlockSpec((RW, D), const2),
                  pl.BlockSpec((2, half, KV_W), lambda b, i: (0, 0, 0)),
                  vec(KV_W),
                  pl.BlockSpec((1, half, tm), col),
                  pl.BlockSpec((1, half, tm), col),
                  pl.BlockSpec((RW, 128), const2),
                  vec(KV_W),
                  pl.BlockSpec((RW, RW), const2),
                  vec(RWKV_COLS), vec(RWKV_COLS),
                  vec(2 * RW), pl.BlockSpec((2 * LORA_W, 2 * RW), const2),
                  vec(2 * RW), pl.BlockSpec((2 * LORA_A, 2 * RW), const2),
                  pl.BlockSpec((LORA_G, RW), const2),
                  vec(RW), vec(RW), vec(RW)],
        out_specs=[pl.BlockSpec((1, RW, tm), col),
                   pl.BlockSpec((1, tm, KV_W), row),
                   pl.BlockSpec((1, tm, KV_W), row),
                   pl.BlockSpec((1, tm, 2 * D), row)] + [rw] * 11,
        out_shape=[jax.ShapeDtypeStruct((B, RW, T), BF16),
                   jax.ShapeDtypeStruct((B, T, KV_W), BF16),
                   jax.ShapeDtypeStruct((B, T, KV_W), BF16),
                   jax.ShapeDtypeStruct((B, T, 2 * D), BF16)]
        + [jax.ShapeDtypeStruct((B, T, RW), F32 if i in (5, 6) else BF16) for i in range(11)],
        compiler_params=_params(("parallel", "parallel")),
        name="inproj",
    )(x, x, x, mod3, norm1_g, w_rest, wq_t, rope_place, rope_pad, cos_q, sin_q, qg, kg, ones,
      mup, mun, w0c, w2c, a0c, a2c, g2b, k_k, k_a, r_k)


def _dir_operands(rev, nch, r, v, kk, lw, kd, bd):
    n = r.shape[0]
    ti = lax.broadcasted_iota(jnp.int32, (n, n), 0)
    si = lax.broadcasted_iota(jnp.int32, (n, n), 1)
    same = (ti // CHUNK) == (si // CHUNK)
    incl = ((si >= ti) if rev else (si <= ti)) & same
    mi = incl.astype(BF16)
    l1 = lw.astype(BF16)
    r1 = lw - l1.astype(F32)
    l2 = r1.astype(BF16)
    l3 = (r1 - l2.astype(F32)).astype(BF16)
    cum = _dot(mi, l1) + _dot(mi, l2) + _dot(mi, l3)
    wt = jnp.exp(cum)
    winv = jnp.exp(-cum)
    ops = dict(ah=-(kk * jnp.exp(cum - lw)), rt=r * wt, bt=bd * winv, kt=kd * winv, v=v)
    wcs, bhs, khs = [], [], []
    for j in range(nch):
        last = j * CHUNK if rev else (j + 1) * CHUNK - 1
        wc = wt[last:last + 1, :]
        rows = slice(j * CHUNK, (j + 1) * CHUNK)
        wcs.append(wc)
        bhs.append(ops["bt"][rows] * wc)
        khs.append(ops["kt"][rows] * wc)
    return ops, wcs, bhs, khs


def _scan_kernel(rf, vf, kkf, lwf, kf, bf, rb, vb, kkb, lwb, kb, bb, yf_ref, yb_ref, st_ref, *, nch):
    @pl.when(pl.program_id(1) == 0)
    def _():
        st_ref[...] = jnp.zeros_like(st_ref)

    c = CHUNK
    pw = 2 * HD
    npair = HEADS // 2
    ti = lax.broadcasted_iota(jnp.int32, (c, pw), 0)
    li = lax.broadcasted_iota(jnp.int32, (c, pw), 1)
    si = li & (HD - 1)
    head0 = li < HD
    blk = (ti // SUB) == (si // SUB)
    eye = ti == si
    two = lambda m: jnp.concatenate([m, m], axis=1)
    masks = {False: (two(si < ti), two(si <= ti)), True: (two(si > ti), two(si >= ti))}

    def bd(y):
        yb = y.astype(BF16)
        zero = jnp.zeros_like(yb)
        return jnp.concatenate([jnp.where(head0, yb, zero), jnp.where(head0, zero, yb)], axis=0)

    def bdcat(*ys):
        return jnp.concatenate([bd(y) for y in ys], axis=1)

    def diag_blocks(full):
        return jnp.where(head0, full[:c], full[c:])

    dirs = [(False, rf, vf, kkf, lwf, kf, bf), (True, rb, vb, kkb, lwb, kb, bb)]
    units = []
    for d, (rev, r_, v_, kk_, lw_, k_, b_) in enumerate(dirs):
        f32 = lambda ref: ref[0].astype(F32)
        ops, wcs, bhs, khs = _dir_operands(rev, nch, f32(r_), f32(v_), f32(kk_), lw_[0], f32(k_), f32(b_))
        for j in range(nch):
            rows = slice(j * c, (j + 1) * c)
            for p in range(npair):
                sl = slice(p * pw, (p + 1) * pw)
                u = dict(d=d, j=j, p=p, rev=rev, wc=wcs[j][:, sl],
                         bkh=jnp.concatenate([bhs[j][:, sl], khs[j][:, sl]], axis=0).astype(BF16))
                for name in ("ah", "rt", "bt", "kt", "v"):
                    u[name] = ops[name][rows, sl]
                units.append(u)

    for u in units:
        x1 = jnp.concatenate([u["ah"], u["rt"]], axis=0).astype(BF16)
        u["lbk"] = _dot_nt(x1, jnp.concatenate([bd(u["bt"]), bd(u["kt"])], axis=0))
    for u in units:
        strict, incl = masks[u["rev"]]
        top = jnp.where(strict, u["lbk"][:c], 0.0)
        bot = jnp.where(incl, u["lbk"][c:], 0.0)
        lab = top[:, :pw]
        u["mrb"] = bot[:, :pw].astype(BF16)
        u["ld"] = jnp.where(blk, lab, 0.0)
        u["lo"] = lab - u["ld"]
        u["lm"] = jnp.concatenate([top[:, pw:], bot[:, pw:]], axis=0).astype(BF16)
    for u in units:
        u["a2"] = _dot(u["ld"].astype(BF16), bd(u["ld"]))
        u["lkv"] = _dot(u["lm"], bd(u["v"]))
    for u in units:
        s = _dot(jnp.concatenate([u["a2"], u["ld"]], axis=0).astype(BF16), bd(u["a2"]))
        u["a4"] = s[:c]
        u["t"] = jnp.where(eye, 1.0, 0.0) + u["ld"] + u["a2"] + s[c:]
    for u in units:
        s = _dot(jnp.concatenate([u["a4"], u["t"]], axis=0).astype(BF16), bd(u["a4"]))
        u["a8"] = s[:c]
        u["t"] = u["t"] + s[c:]
    for u in units:
        u["td"] = u["t"] + _dot(u["t"].astype(BF16), bd(u["a8"]))
    for u in units:
        x = _dot(u["td"].astype(BF16), bdcat(u["lo"], u["ah"], u["lkv"][:c]))
        u["n"] = x[:, :pw]
        u["z"] = x[:, pw:]
    for u in units:
        s = _dot(u["n"].astype(BF16), bdcat(u["n"], u["z"][:, :pw], u["z"][:, pw:]))
        u["n2"] = s[:, :pw]
        u["z"] = u["z"] + s[:, pw:]
    for u in units:
        u["z"] = u["z"] + _dot(u["n2"].astype(BF16), bdcat(u["z"][:, :pw], u["z"][:, pw:]))
    for u in units:
        z = u["z"]
        y10 = (jnp.concatenate([u["rt"], u["lkv"][c:]], axis=1)
               + _dot(u["mrb"], bdcat(z[:, :pw], z[:, pw:])))
        vpad = jnp.concatenate([jnp.zeros_like(u["v"]), u["v"]], axis=1)
        full = _dot_tn(u["bkh"], jnp.concatenate([z, vpad], axis=0).astype(BF16))
        g = diag_blocks(full[:, :pw]) + jnp.where(eye, u["wc"], 0.0)
        u["gy"] = jnp.concatenate([g, y10[:, :pw]], axis=0).astype(BF16)
        u["h0"] = diag_blocks(full[:, pw:])
        u["y0"] = y10[:, pw:]

    ys = {}
    for d in range(2):
        order = range(nch - 1, -1, -1) if d == 1 else range(nch)
        zs = [st_ref[d, p] for p in range(npair)]
        for j in order:
            for p in range(npair):
                u = units[(d * nch + j) * npair + p]
                m = _dot(u["gy"], bd(zs[p]))
                zs[p] = m[:c] + u["h0"]
                ys[(d, j, p)] = m[c:] + u["y0"]
        for p in range(npair):
            st_ref[d, p] = zs[p]
    for d, y_ref in enumerate((yf_ref, yb_ref)):
        y_ref[0] = jnp.concatenate(
            [jnp.concatenate([ys[(d, j, p)] for p in range(npair)], axis=-1) for j in range(nch)],
            axis=0).astype(BF16)


def _scan_call(r, v, kkn, lw0, lw1, k0, k1, b0, b1, nch):
    B, T, _ = r.shape
    rows = nch * CHUNK
    ns = T // rows
    fwd = pl.BlockSpec((1, rows, RW), lambda b, c: (b, c, 0))
    bwd = pl.BlockSpec((1, rows, RW), lambda b, c: (b, ns - 1 - c, 0))
    return pl.pallas_call(
        functools.partial(_scan_kernel, nch=nch),
        grid=(B, ns),
        in_specs=[fwd] * 6 + [bwd] * 6,
        out_specs=[fwd, bwd],
        out_shape=[jax.ShapeDtypeStruct((B, T, RW), BF16)] * 2,
        scratch_shapes=[pltpu.VMEM((2, HEADS // 2, HD, 2 * HD), F32)],
        compiler_params=_params(("parallel", "arbitrary")),
        name="scan",
    )(r, v, kkn, lw0, k0, b0, r, v, kkn, lw1, k1, b1)


def _attention_scores(sink_ref, qt_ref, k_ref, v_ref, n, qb):
    t = k_ref.shape[1]
    band = 3 * BLOCK
    cols = GROUP * BLOCK
    ki = lax.broadcasted_iota(jnp.int32, (band, BLOCK), 0)
    qi = lax.broadcasted_iota(jnp.int32, (band, BLOCK), 1)
    lane_head = lax.broadcasted_iota(jnp.int32, (1, cols), 1) // BLOCK
    zeros = jnp.zeros((HD, cols), BF16)
    units = []
    for j in range(qb):
        blk = n * qb + j
        start = pl.multiple_of(jnp.clip((blk - 1) * BLOCK, 0, t - band), BLOCK)
        bias = jnp.where(jnp.abs(start + ki - (blk * BLOCK + qi)) <= BLOCK, 0.0, NEG)
        bias = jnp.concatenate([bias] * GROUP, axis=1)
        kb = k_ref[0, pl.ds(start, band), :]
        vb = v_ref[0, pl.ds(start, band), :]
        for g in range(KV_HEADS):
            qg = jnp.concatenate([qt_ref[0, (g * GROUP + i) * HD:(g * GROUP + i + 1) * HD,
                                         j * BLOCK:(j + 1) * BLOCK] for i in range(GROUP)], axis=1)
            rhs = jnp.concatenate([qg if gg == g else zeros for gg in range(KV_HEADS)], axis=0)
            sk = jnp.zeros((1, cols), F32)
            for i in range(GROUP):
                sk = jnp.where(lane_head == i, sink_ref[g * GROUP + i] * LOG2E, sk)
            units.append(dict(j=j, g=g, kb=kb, vb=vb, rhs=rhs, sk=sk, bias=bias))
    for u in units:
        u["s"] = _dot(u["kb"], u["rhs"]) + u["bias"]
    return units


def _attention_finish(units, qb):
    for u in units:
        u["m"] = jnp.maximum(jnp.max(u["s"], axis=0, keepdims=True), u["sk"])
    for u in units:
        p = jnp.exp2(u["s"] - u["m"])
        u["r"] = 1.0 / (jnp.sum(p, axis=0, keepdims=True) + jnp.exp2(u["sk"] - u["m"]))
        u["p"] = p.astype(BF16)
    for u in units:
        g = u["g"]
        u["o"] = _dot_tn(u["vb"], u["p"])[g * HD:(g + 1) * HD] * u["r"]
    out = {(u["j"], u["g"]): u["o"] for u in units}
    return jnp.concatenate(
        [jnp.concatenate([out[(j, h // GROUP)][:, (h % GROUP) * BLOCK:(h % GROUP + 1) * BLOCK]
                          for j in range(qb)], axis=1) for h in range(HEADS)], axis=0)


def _merge_kernel(sink_ref, x_ref, mod_ref, yf_ref, yb_ref, bonus_ref, g_ref, qt_ref, k_ref, v_ref, gl_ref,
                  gnw_ref, gnb_ref, ones_ref, pr_ref, pa_ref, wo_ref, n2g_ref, wr_ref,
                  x1_ref, u2_ref, aff_ref):
    qb = x_ref.shape[1] // BLOCK
    att = _attention_scores(sink_ref, qt_ref, k_ref, v_ref, pl.program_id(1), qb)
    ones = ones_ref[...]
    y = yf_ref[0].astype(F32) + yb_ref[0].astype(F32)
    mu = _segsum(y, ones) * (1.0 / HD)
    yc = y - mu
    var = _segsum_pos(yc * yc, ones) * (1.0 / HD)
    yn = yc * lax.rsqrt(var + GN_EPS) * gnw_ref[...] + gnb_ref[...]
    ya = ((yn + bonus_ref[0].astype(F32)) * g_ref[0].astype(F32)).astype(BF16)
    pa = _dot(ya, pr_ref[...])
    gates = jax.nn.sigmoid(gl_ref[0].astype(F32))
    yatt_t = _attention_finish(att, qb).astype(BF16)
    pb = _dot_tn(yatt_t, pa_ref[...])
    m = gates[:, :D_MODEL] * pa + gates[:, D_MODEL:] * pb
    x1 = x_ref[0] + mod_ref[0, 2:3, :] * _dot(m.astype(BF16), wo_ref[...])
    x1_ref[0] = x1
    ms = jnp.mean(x1 * x1, axis=-1, keepdims=True)
    u2 = x1 * lax.rsqrt(ms + NORM_EPS) * n2g_ref[...] * (1.0 + mod_ref[0, 4:5, :]) + mod_ref[0, 3:4, :]
    u2_ref[0] = u2
    u2h = u2.astype(BF16)
    u2l = (u2 - u2h.astype(F32)).astype(BF16)
    wr = wr_ref[...]
    wrh = wr.astype(BF16)
    wrl = (wr - wrh.astype(F32)).astype(BF16)
    t1 = _dot_nt(jnp.concatenate([wrh, wrl], axis=0), u2h)
    logits = t1[:N_EXPERTS] + t1[N_EXPERTS:] + _dot_nt(wrh, u2l)
    e = jnp.exp(logits - jnp.max(logits, axis=0, keepdims=True))
    aff_ref[0] = e / jnp.sum(e, axis=0, keepdims=True)


def _merge_call(sink, x, mod3, yf, yb, bonus, g, qt, k, v, gl, gnw, gnb, ones, p_r, p_a, w_o, n2g, w_rt, tm):
    B, T, D = x.shape
    assert T >= 3 * BLOCK and tm % BLOCK == 0
    row = lambda b, i: (b, i, 0)
    const2 = lambda b, i: (0, 0)
    rw = pl.BlockSpec((1, tm, RW), row)
    kv = pl.BlockSpec((1, T, KV_W), lambda b, i: (b, 0, 0))
    return pl.pallas_call(
        _merge_kernel,
        grid=(B, T // tm),
        in_specs=[pl.BlockSpec(memory_space=pltpu.SMEM),
                  pl.BlockSpec((1, tm, D), row),
                  pl.BlockSpec((1, 6, D), lambda b, i: (b, 0, 0)),
                  rw, rw, rw, rw,
                  pl.BlockSpec((1, RW, tm), lambda b, i: (b, 0, i)), kv, kv,
                  pl.BlockSpec((1, tm, 2 * D), row),
                  pl.BlockSpec((1, RW), const2),
                  pl.BlockSpec((1, RW), const2),
                  pl.BlockSpec((RW, RW), const2),
                  pl.BlockSpec((RW, D), const2),
                  pl.BlockSpec((RW, D), const2),
                  pl.BlockSpec((D, D), const2),
                  pl.BlockSpec((1, D), const2),
                  pl.BlockSpec((N_EXPERTS, D), const2)],
        out_specs=[pl.BlockSpec((1, tm, D), row),
                   pl.BlockSpec((1, tm, D), row),
                   pl.BlockSpec((1, N_EXPERTS, tm), lambda b, i: (b, 0, i))],
        out_shape=[jax.ShapeDtypeStruct((B, T, D), F32),
                   jax.ShapeDtypeStruct((B, T, D), F32),
                   jax.ShapeDtypeStruct((B, N_EXPERTS, T), F32)],
        compiler_params=_params(("parallel", "parallel")),
        name="merge",
    )(sink, x, mod3, yf, yb, bonus, g, qt, k, v, gl, gnw, gnb, ones, p_r, p_a, w_o, n2g, w_rt)


def _excl_prefix(mask, tri):
    e, t = mask.shape
    mb = mask.astype(BF16)
    carry = jnp.zeros((e, 1), F32)
    outs = []
    for j in range(t // 128):
        tile = mb[:, j * 128:(j + 1) * 128]
        outs.append(_dot(tile, tri) + carry)
        carry = carry + jnp.sum(tile.astype(F32), axis=1, keepdims=True)
    return jnp.concatenate(outs, axis=-1)


def _route_kernel(aff_ref, tri_ref, dig_ref, slot_ref, idx_ref, *, cap):
    nb, ne, t = aff_ref.shape
    a = aff_ref[...].reshape(nb * ne, t)
    e = a.shape[0]
    capf = jnp.float32(cap)

    def cond(s):
        it, lo, hi = s
        mid = 0.5 * (lo + hi)
        still_open = jnp.max(jnp.where((mid > lo) & (mid < hi), 1.0, 0.0))
        return (it < BISECT_ITERS) & (still_open > 0.0)

    def body(s):
        it, lo, hi = s
        mid = 0.5 * (lo + hi)
        ge = jnp.sum((a >= mid).astype(F32), axis=1, keepdims=True) >= capf
        return it + 1, jnp.where(ge, mid, lo), jnp.where(ge, hi, mid)

    lo0 = jnp.zeros((e, 1), F32)
    hi0 = jnp.full((e, 1), 2.0, F32)
    _, lo, hi = lax.while_loop(cond, body, (jnp.int32(0), lo0, hi0))
    gt = a >= hi
    eq = (a >= lo) & (a < hi)
    need = capf - jnp.sum(gt.astype(F32), axis=1, keepdims=True)
    tri = tri_ref[...]
    sel = gt | (eq & (_excl_prefix(eq, tri) < need))
    pos = _excl_prefix(sel, tri)
    slot_ref[...] = jnp.where(sel, pos.astype(jnp.int32), -1).reshape(nb, ne, t)

    digits = dig_ref[...]
    assert cap <= 256
    slots = lax.broadcasted_iota(jnp.int32, (cap, t), 0).astype(BF16)
    one, zero = jnp.ones((cap, t), BF16), jnp.zeros((cap, t), BF16)

    def row_body(r, carry):
        srow = slot_ref[r // ne, pl.ds(r % ne, 1), :].astype(BF16)
        d = _dot_nt(digits, jnp.where(slots == srow, one, zero))
        idx_ref[r] = (d[0:1] * TOKEN_RADIX + d[1:2]).astype(jnp.int32)
        return carry

    lax.fori_loop(0, nb * ne, row_body, 0, unroll=8)


def _route_call(aff, tri, digits, cap):
    B, E, T = aff.shape
    return pl.pallas_call(
        functools.partial(_route_kernel, cap=cap),
        grid=(1,),
        in_specs=[pl.BlockSpec((B, E, T), lambda i: (0, 0, 0)),
                  pl.BlockSpec((128, 128), lambda i: (0, 0)),
                  pl.BlockSpec((8, T), lambda i: (0, 0))],
        out_specs=[pl.BlockSpec((B, E, T), lambda i: (0, 0, 0)),
                   pl.BlockSpec((B * E, 1, cap), lambda i: (0, 0, 0))],
        out_shape=[jax.ShapeDtypeStruct((B, E, T), jnp.int32),
                   jax.ShapeDtypeStruct((B * E, 1, cap), jnp.int32)],
        compiler_params=_params(("arbitrary",)),
        name="route",
    )(aff, tri, digits)


def _moe_kernel(idx0_ref, idx1_ref, u2_ref, slot_ref, aff_ref, wg_ref, wu_ref, wd_ref, o_ref,
                acc_ref, xa_ref, xb_ref, *, cap, tt):
    e = pl.program_id(1)
    t = u2_ref.shape[1]

    def gather(idx_ref, dst_ref):
        for i in range(cap):
            dst_ref[pl.ds(i, 1), :] = u2_ref[0, pl.ds(idx_ref[0, 0, i], 1), :]

    def expert(x_ref):
        xe = x_ref[...].astype(BF16)
        hg = _dot(xe, wg_ref[0])
        hu = _dot(xe, wu_ref[0])
        h = (hg * jax.nn.sigmoid(hg) * hu).astype(BF16)
        ye = _dot(h, wd_ref[0]).astype(BF16)
        hit = lax.broadcasted_iota(jnp.int32, (cap, t), 0) == slot_ref[0, pl.ds(e, 1), :]
        wsc = jnp.where(hit, aff_ref[0, pl.ds(e, 1), :], 0.0).astype(BF16)
        for j in range(t // tt):
            acc_ref[j * tt:(j + 1) * tt, :] += _dot_tn(wsc[:, j * tt:(j + 1) * tt], ye)

    @pl.when(e == 0)
    def _():
        acc_ref[...] = jnp.zeros_like(acc_ref)
        gather(idx0_ref, xa_ref)

    @pl.when(e % 2 == 0)
    def _():
        gather(idx1_ref, xb_ref)
        expert(xa_ref)

    @pl.when(e % 2 == 1)
    def _():
        gather(idx1_ref, xa_ref)
        expert(xb_ref)

    @pl.when(e == pl.num_programs(1) - 1)
    def _():
        o_ref[0] = acc_ref[...].astype(BF16)


def _moe_call(u2, slot, aff, idx, wg, wu, wd, cap):
    B, T, D = u2.shape
    E = wg.shape[0]
    F = wg.shape[2]
    assert E % 2 == 0
    tt = min(T, 512)
    smem_row = lambda im: pl.BlockSpec((1, 1, cap), im, memory_space=pltpu.SMEM)
    return pl.pallas_call(
        functools.partial(_moe_kernel, cap=cap, tt=tt),
        grid=(B, E),
        in_specs=[smem_row(lambda b, e: (b * E + e, 0, 0)),
                  smem_row(lambda b, e: (b * E + jnp.minimum(e + 1, E - 1), 0, 0)),
                  pl.BlockSpec((1, T, D), lambda b, e: (b, 0, 0)),
                  pl.BlockSpec((1, E, T), lambda b, e: (b, 0, 0)),
                  pl.BlockSpec((1, E, T), lambda b, e: (b, 0, 0)),
                  pl.BlockSpec((1, D, F), lambda b, e: (e, 0, 0)),
                  pl.BlockSpec((1, D, F), lambda b, e: (e, 0, 0)),
                  pl.BlockSpec((1, F, D), lambda b, e: (e, 0, 0))],
        out_specs=pl.BlockSpec((1, T, D), lambda b, e: (b, 0, 0)),
        out_shape=jax.ShapeDtypeStruct((B, T, D), BF16),
        scratch_shapes=[pltpu.VMEM((T, D), F32), pltpu.VMEM((cap, D), F32), pltpu.VMEM((cap, D), F32)],
        compiler_params=pltpu.CompilerParams(dimension_semantics=("parallel", "arbitrary"),
                                             vmem_limit_bytes=MOE_VMEM_LIMIT),
        name="moe",
    )(idx, idx, u2, slot, aff, wg, wu, wd)


def _final_kernel(x1_ref, mod_ref, moe_ref, o_ref):
    o_ref[0] = x1_ref[0] + mod_ref[0, 5:6, :] * moe_ref[0].astype(F32)


def _final_call(x1, mod3, moe, tm):
    B, T, D = x1.shape
    row = lambda b, i: (b, i, 0)
    return pl.pallas_call(
        _final_kernel,
        grid=(B, T // tm),
        in_specs=[pl.BlockSpec((1, tm, D), row),
                  pl.BlockSpec((1, 6, D), lambda b, i: (b, 0, 0)),
                  pl.BlockSpec((1, tm, D), row)],
        out_specs=pl.BlockSpec((1, tm, D), row),
        out_shape=jax.ShapeDtypeStruct((B, T, D), F32),
        compiler_params=_params(("parallel", "parallel")),
        name="final",
    )(x1, mod3, moe)


def _blockdiag2(w):
    z = jnp.zeros_like(w[0])
    return jnp.concatenate([jnp.concatenate([w[0], z], axis=1), jnp.concatenate([z, w[1]], axis=1)], axis=0)


def _layer(x, mod3, rope_place, rope_pad, cos_q, sin_q, ones, tri, norm1_g, w_in, mu_prev, mu_next, rwkv_w0, rwkv_w2, rwkv_a0,
           rwkv_a2, rwkv_g2, rwkv_k_k, rwkv_k_a, rwkv_r_k, rwkv_gn_w, rwkv_gn_b, q_norm_g, k_norm_g,
           attn_sink, p_rwkv, p_attn, w_out, norm2_g, w_router, w_gate, w_up, w_down):
    B, T, D = x.shape
    tm = TM_ROWS if T % TM_ROWS == 0 else min(T, 256)
    cap = CAP_FACTOR * T // N_EXPERTS
    row = lambda a: a.reshape(1, -1)
    w_b = w_in.astype(BF16)
    qg = jnp.broadcast_to(jnp.tile(q_norm_g, HEADS)[:, None], (RW, 128))
    qt, k, v, gl, r, vv, kkn, g, bonus, lw0, lw1, k0, k1, b0, b1 = _inproj_call(
        x, mod3, row(norm1_g), w_b, w_b[:, Q0:K0].T, rope_place, rope_pad, cos_q, sin_q,
        qg, row(jnp.tile(k_norm_g, KV_HEADS)), ones,
        row(mu_prev), row(mu_next), row(rwkv_w0), _blockdiag2(rwkv_w2).astype(BF16),
        row(rwkv_a0), _blockdiag2(rwkv_a2).astype(BF16), rwkv_g2.astype(BF16),
        row(rwkv_k_k), row(rwkv_k_a), row(rwkv_r_k), tm)
    yf, yb = _scan_call(r, vv, kkn, lw0, lw1, k0, k1, b0, b1, SCAN_CHUNKS)
    x1, u2, aff = _merge_call(attn_sink, x, mod3, yf, yb, bonus, g, qt, k, v, gl, row(rwkv_gn_w), row(rwkv_gn_b), ones,
                              p_rwkv.astype(BF16), p_attn.astype(BF16), w_out.astype(BF16),
                              row(norm2_g), w_router.T, tm)
    assert T <= 256 * TOKEN_RADIX
    tok = jnp.arange(T)
    digits = jnp.zeros((8, T), BF16).at[0].set((tok // TOKEN_RADIX).astype(BF16)).at[1].set(
        (tok % TOKEN_RADIX).astype(BF16))
    slot, idx = _route_call(aff, tri, digits, cap)
    moe = _moe_call(u2, slot, aff, idx, w_gate.astype(BF16), w_up.astype(BF16), w_down.astype(BF16), cap)
    return _final_call(x1, mod3, moe, TM_FINAL if T % TM_FINAL == 0 else tm)


def kernel(x, c, positions, w_ada, b_ada, norm1_g, w_in, mu_prev, mu_next, rwkv_w0, rwkv_w2, rwkv_a0, rwkv_a2, rwkv_g2, rwkv_k_k, rwkv_k_a, rwkv_r_k, rwkv_gn_w, rwkv_gn_b, q_norm_g, k_norm_g, attn_sink, p_rwkv, p_attn, w_out, norm2_g, w_router, w_gate, w_up, w_down):
    B, T, D = x.shape
    depth = w_ada.shape[0]
    half = ROT // 2
    inv_freq = ROPE_THETA ** (-jnp.arange(0, ROT, 2, dtype=F32) / ROT)
    ang = positions.astype(F32)[..., None] * inv_freq
    cos8, sin8 = jnp.cos(ang), jnp.sin(ang)
    lane = jnp.arange(KV_W) % HD
    freq = jnp.arange(half)[:, None]
    first, second = (lane[None, :] == freq), (lane[None, :] == freq + half)
    rope_place = jnp.stack([(first | second).astype(F32), second.astype(F32) - first.astype(F32)])
    rope_pad = (lane >= ROT).astype(F32).reshape(1, KV_W)
    seg = jnp.arange(RW) // HD
    ones = (seg[:, None] == seg[None, :]).astype(BF16)
    idx = jnp.arange(128)
    tri = (idx[:, None] < idx[None, :]).astype(BF16)
    cos_q, sin_q = jnp.swapaxes(cos8, 1, 2), jnp.swapaxes(sin8, 1, 2)
    for l in range(depth):
        mod3 = _mod_call(c, w_ada[l], b_ada[l]).reshape(B, 6, D)
        x = _layer(x, mod3, rope_place, rope_pad, cos_q, sin_q, ones, tri, norm1_g[l], w_in[l], mu_prev[l], mu_next[l],
                   rwkv_w0[l], rwkv_w2[l], rwkv_a0[l], rwkv_a2[l], rwkv_g2[l], rwkv_k_k[l], rwkv_k_a[l],
                   rwkv_r_k[l], rwkv_gn_w[l], rwkv_gn_b[l], q_norm_g[l], k_norm_g[l], attn_sink[l],
                   p_rwkv[l], p_attn[l], w_out[l], norm2_g[l], w_router[l], w_gate[l], w_up[l], w_down[l])
    return x
```

```python
import functools
import math

import jax
import jax.numpy as jnp
from jax import lax
from jax.experimental import pallas as pl
from jax.experimental.pallas import tpu as pltpu

F32 = jnp.float32
BF16 = jnp.bfloat16

D_MODEL = 1024
RW = 512
HEADS = 8
HD = 64
LORA_W = 64
LORA_A = 64
LORA_G = 128
GN_EPS = HD * 1e-5
KV_HEADS = 2
GROUP = HEADS // KV_HEADS
KV_W = KV_HEADS * HD
BLOCK = 128
ROPE_THETA = 500000.0
ROT = HD // 4
N_EXPERTS = 16
CAP_FACTOR = 2
NORM_EPS = 1e-6
RWKV_COLS = 3 * RW + 2 * LORA_W + 2 * LORA_A + LORA_G
Q0 = RWKV_COLS
K0 = Q0 + RW
V0 = K0 + KV_W
G0 = V0 + KV_W
IN_COLS = G0 + 2 * D_MODEL
CHUNK = 64
SUB = 16
SCAN_CHUNKS = 4
TM_FINAL = 2048
TM_ROWS = 512
TOKEN_RADIX = 64
BISECT_ITERS = 160
NEG = -1e30
LOG2E = math.log2(math.e)
QK_SCALE = HD ** -0.5 * LOG2E
VMEM_LIMIT = 56 * 1024 * 1024
MOE_VMEM_LIMIT = 60 * 1024 * 1024


def _dot(a, b):
    return jnp.dot(a, b, preferred_element_type=F32)


def _dot_nt(a, b):
    return lax.dot_general(a, b, (((1,), (1,)), ((), ())), preferred_element_type=F32)


def _dot_tn(a, b):
    return lax.dot_general(a, b, (((0,), (0,)), ((), ())), preferred_element_type=F32)


def _segsum(x, ones):
    xh = x.astype(BF16)
    xl = (x - xh.astype(F32)).astype(BF16)
    return _dot(xh, ones) + _dot(xl, ones)


def _segsum_pos(x, ones):
    return _dot(x.astype(BF16), ones)


def _params(sem):
    return pltpu.CompilerParams(dimension_semantics=sem, vmem_limit_bytes=VMEM_LIMIT)


def _mod_kernel(c_ref, w_ref, b_ref, o_ref):
    c = c_ref[...]
    ca = c * jax.nn.sigmoid(c)
    o_ref[...] = jnp.dot(ca, w_ref[...], preferred_element_type=F32,
                         precision=lax.Precision.HIGHEST) + b_ref[...]


def _mod_call(c, w_ada, b_ada):
    B, D = c.shape
    n = w_ada.shape[1] // D
    return pl.pallas_call(
        _mod_kernel,
        grid=(n,),
        in_specs=[pl.BlockSpec((B, D), lambda j: (0, 0)),
                  pl.BlockSpec((D, D), lambda j: (0, j)),
                  pl.BlockSpec((1, D), lambda j: (0, j))],
        out_specs=pl.BlockSpec((B, D), lambda j: (0, j)),
        out_shape=jax.ShapeDtypeStruct((B, n * D), F32),
        compiler_params=_params(("arbitrary",)),
        name="mod",
    )(c, w_ada, b_ada.reshape(1, -1))


def _rope(xn, cos, sin, width):
    lane = lax.broadcasted_iota(jnp.int32, xn.shape, 1) & (HD - 1)
    rot = jnp.where(lane < ROT // 2, pltpu.roll(xn, width - ROT // 2, 1), pltpu.roll(xn, ROT // 2, 1))
    return xn * cos + rot * sin


HALO = 8
INPROJ_SUBTILES = 1


def _inproj_kernel(x_ref, xp_ref, xn_ref, mod_ref, g_ref, w_ref, wq_ref, place_ref, pad_ref, cosq_ref, sinq_ref,
                   qg_ref, kg_ref, ones8_ref,
                   mup_ref, mun_ref, w0_ref, w2_ref, a0_ref, a2_ref, g2_ref, kk_ref, ka_ref, rk_ref,
                   qt_ref, k_ref, v_ref, gl_ref,
                   r_o, v_o, kkn_o, g_o, bonus_o, lw0_o, lw1_o, k0_o, k1_o, b0_o, b1_o):
    tm = x_ref.shape[1]
    sub = tm // INPROJ_SUBTILES
    xx = jnp.concatenate([xp_ref[0], x_ref[0], xn_ref[0]], axis=0)
    for lo in range(0, tm, sub):
        _inproj_rows(lo, sub, xx[lo:lo + sub + 2 * HALO], tm, mod_ref, g_ref, w_ref, wq_ref, place_ref, pad_ref,
                     cosq_ref, sinq_ref, qg_ref, kg_ref, ones8_ref, mup_ref, mun_ref, w0_ref, w2_ref, a0_ref,
                     a2_ref, g2_ref, kk_ref, ka_ref, rk_ref, qt_ref, k_ref, v_ref, gl_ref,
                     (r_o, v_o, kkn_o, g_o, bonus_o, lw0_o, lw1_o, k0_o, k1_o, b0_o, b1_o))


def _inproj_rows(lo, n, xx, tm, mod_ref, g_ref, w_ref, wq_ref, place_ref, pad_ref, cosq_ref, sinq_ref,
                 qg_ref, kg_ref, ones8_ref, mup_ref, mun_ref, w0_ref, w2_ref, a0_ref, a2_ref, g2_ref,
                 kk_ref, ka_ref, rk_ref, qt_ref, k_ref, v_ref, gl_ref, rwkv_outs):
    r_o, v_o, kkn_o, g_o, bonus_o, lw0_o, lw1_o, k0_o, k1_o, b0_o, b1_o = rwkv_outs
    i = pl.program_id(1)
    last = pl.num_programs(1) - 1
    rows = slice(lo, lo + n)
    ms = jnp.mean(xx * xx, axis=-1, keepdims=True)
    y = xx * lax.rsqrt(ms + NORM_EPS) * g_ref[...]
    u_all = y * (1.0 + mod_ref[0, 1:2, :]) + mod_ref[0, 0:1, :]
    ub = u_all[HALO:HALO + n].astype(BF16)
    z_all = _dot(u_all.astype(BF16), w_ref[:, 0:RWKV_COLS])
    ones8 = ones8_ref[...]

    k = _dot(ub, w_ref[:, K0:V0])
    v_ref[0, rows, :] = _dot(ub, w_ref[:, V0:G0]).astype(BF16)
    gl_ref[0, rows, :] = _dot(ub, w_ref[:, G0:IN_COLS]).astype(BF16)
    kn = k * lax.rsqrt(_segsum_pos(k * k, ones8[:KV_W, :KV_W]) * (1.0 / HD) + NORM_EPS) * kg_ref[...]
    def place(t8, e):
        hi = t8.astype(BF16).astype(F32)
        return _dot_tn(hi, e) + _dot_tn(t8 - hi, e)

    cos_k = place(cosq_ref[0, :, rows], place_ref[0]) + pad_ref[...]
    sin_k = place(sinq_ref[0, :, rows], place_ref[1])
    k_ref[0, rows, :] = _rope(kn, cos_k, sin_k, KV_W).astype(BF16)

    z = z_all[HALO:HALO + n]
    row = lax.broadcasted_iota(jnp.int32, z.shape, 0) + lo
    zp = jnp.where((row == 0) & (i == 0), 0.0, z_all[HALO - 1:HALO - 1 + n])
    zn = jnp.where((row == tm - 1) & (i == last), 0.0, z_all[HALO + 1:HALO + 1 + n])
    zs = z + mup_ref[...] * (zp - z) + mun_ref[...] * (zn - z)
    r = zs[:, 0:RW]
    kr = zs[:, RW:2 * RW]
    vr = zs[:, 2 * RW:3 * RW]
    c0 = 3 * RW
    wd = zs[:, c0:c0 + 2 * LORA_W]
    ad = zs[:, c0 + 2 * LORA_W:c0 + 2 * LORA_W + 2 * LORA_A]
    gd = zs[:, c0 + 2 * LORA_W + 2 * LORA_A:RWKV_COLS]
    wl = _dot(jnp.tanh(wd).astype(BF16), w2_ref[...]) + w0_ref[...]
    lw = -math.exp(-0.5) * jax.nn.sigmoid(wl)
    al = jax.nn.sigmoid(_dot(ad.astype(BF16), a2_ref[...]) + a0_ref[...])
    g = _dot(jax.nn.sigmoid(gd).astype(BF16), g2_ref[...])
    kk = kr * kk_ref[...]
    kkn = kk / jnp.maximum(jnp.sqrt(_segsum_pos(kk * kk, ones8)), 1e-12)
    ka = ka_ref[...]
    a_0 = al[:, 0:RW]
    a_1 = al[:, RW:2 * RW]
    k_0 = kr * (1.0 + (a_0 - 1.0) * ka)
    k_1 = kr * (1.0 + (a_1 - 1.0) * ka)
    bonus = _segsum(r * (k_0 + k_1) * rk_ref[...], ones8) * vr
    r_o[0, rows, :] = r.astype(BF16)
    v_o[0, rows, :] = vr.astype(BF16)
    kkn_o[0, rows, :] = kkn.astype(BF16)
    g_o[0, rows, :] = g.astype(BF16)
    bonus_o[0, rows, :] = bonus.astype(BF16)
    lw0_o[0, rows, :] = lw[:, 0:RW]
    lw1_o[0, rows, :] = lw[:, RW:2 * RW]
    k0_o[0, rows, :] = k_0.astype(BF16)
    k1_o[0, rows, :] = k_1.astype(BF16)
    b0_o[0, rows, :] = (kkn * a_0).astype(BF16)
    b1_o[0, rows, :] = (kkn * a_1).astype(BF16)

    qt = _dot_nt(wq_ref[...], ub)
    qg = jnp.concatenate([qg_ref[...]] * (n // 128), axis=1)
    cq = cosq_ref[0, :, rows]
    sq = sinq_ref[0, :, rows]
    half = ROT // 2
    pieces = []
    for h in range(HEADS):
        xh = qt[h * HD:(h + 1) * HD]
        xn = (xh * lax.rsqrt(jnp.mean(xh * xh, axis=0, keepdims=True) + NORM_EPS)
              * qg[h * HD:(h + 1) * HD] * QK_SCALE)
        x1, x2 = xn[0:half], xn[half:ROT]
        pieces += [x1 * cq - x2 * sq, x2 * cq + x1 * sq, xn[ROT:]]
    qt_ref[0, :, rows] = jnp.concatenate(pieces, axis=0).astype(BF16)


def _inproj_call(x, mod3, norm1_g, w_rest, wq_t, rope_place, rope_pad, cos_q, sin_q, qg, kg, ones,
                 mup, mun, w0c, w2c, a0c, a2c, g2b, k_k, k_a, r_k, tm):
    B, T, D = x.shape
    nh = tm // HALO
    nth = T // HALO
    row = lambda b, i: (b, i, 0)
    col = lambda b, i: (b, 0, i)
    const2 = lambda b, i: (0, 0)
    half = ROT // 2
    rw = pl.BlockSpec((1, tm, RW), row)
    vec = lambda n: pl.BlockSpec((1, n), const2)
    return pl.pallas_call(
        _inproj_kernel,
        grid=(B, T // tm),
        in_specs=[pl.BlockSpec((1, tm, D), row),
                  pl.BlockSpec((1, HALO, D), lambda b, i: (b, jnp.maximum(i * nh - 1, 0), 0)),
                  pl.BlockSpec((1, HALO, D), lambda b, i: (b, jnp.minimum((i + 1) * nh, nth - 1), 0)),
                  pl.BlockSpec((1, 6, D), lambda b, i: (b, 0, 0)),
                  vec(D),
                  pl.BlockSpec(w_rest.shape, const2),
                  pl.BlockSpec((RW, D), const2),
                  pl.BlockSpec((2, half, KV_W), lambda b, i: (0, 0, 0)),
                  vec(KV_W),
                  pl.BlockSpec((1, half, tm), col),
                  pl.BlockSpec((1, half, tm), col),
                  pl.BlockSpec((RW, 128), const2),
                  vec(KV_W),
                  pl.BlockSpec((RW, RW), const2),
                  vec(RWKV_COLS), vec(RWKV_COLS),
                  vec(2 * RW), pl.BlockSpec((2 * LORA_W, 2 * RW), const2),
                  vec(2 * RW), pl.BlockSpec((2 * LORA_A, 2 * RW), const2),
                  pl.BlockSpec((LORA_G, RW), const2),
                  vec(RW), vec(RW), vec(RW)],
        out_specs=[pl.BlockSpec((1, RW, tm), col),
                   pl.BlockSpec((1, tm, KV_W), row),
                   pl.BlockSpec((1, tm, KV_W), row),
                   pl.BlockSpec((1, tm, 2 * D), row)] + [rw] * 11,
        out_shape=[jax.ShapeDtypeStruct((B, RW, T), BF16),
                   jax.ShapeDtypeStruct((B, T, KV_W), BF16),
                   jax.ShapeDtypeStruct((B, T, KV_W), BF16),
                   jax.ShapeDtypeStruct((B, T, 2 * D), BF16)]
        + [jax.ShapeDtypeStruct((B, T, RW), F32 if i in (5, 6) else BF16) for i in range(11)],
        compiler_params=_params(("parallel", "parallel")),
        name="inproj",
    )(x, x, x, mod3, norm1_g, w_rest, wq_t, rope_place, rope_pad, cos_q, sin_q, qg, kg, ones,
      mup, mun, w0c, w2c, a0c, a2c, g2b, k_k, k_a, r_k)


def _dir_operands(rev, nch, r, v, kk, lw, kd, bd):
    n = r.shape[0]
    ti = lax.broadcasted_iota(jnp.int32, (n, n), 0)
    si = lax.broadcasted_iota(jnp.int32, (n, n), 1)
    same = (ti // CHUNK) == (si // CHUNK)
    incl = ((si >= ti) if rev else (si <= ti)) & same
    mi = incl.astype(BF16)
    l1 = lw.astype(BF16)
    l2 = (lw - l1.astype(F32)).astype(BF16)
    cum = _dot(mi, l1) + _dot(mi, l2)
    wt = jnp.exp(cum)
    winv = jnp.exp(-cum)
    ops = dict(ah=-(kk * jnp.exp(cum - lw)), rt=r * wt, bt=bd * winv, kt=kd * winv, v=v)
    wcs, bhs, khs = [], [], []
    for j in range(nch):
        last = j * CHUNK if rev else (j + 1) * CHUNK - 1
        wc = wt[last:last + 1, :]
        rows = slice(j * CHUNK, (j + 1) * CHUNK)
        wcs.append(wc)
        bhs.append(ops["bt"][rows] * wc)
        khs.append(ops["kt"][rows] * wc)
    return ops, wcs, bhs, khs


def _scan_kernel(rf, vf, kkf, lwf, kf, bf, rb, vb, kkb, lwb, kb, bb, yf_ref, yb_ref, st_ref, *, nch):
    @pl.when(pl.program_id(1) == 0)
    def _():
        st_ref[...] = jnp.zeros_like(st_ref)

    c = CHUNK
    pw = 2 * HD
    npair = HEADS // 2
    ti = lax.broadcasted_iota(jnp.int32, (c, pw), 0)
    li = lax.broadcasted_iota(jnp.int32, (c, pw), 1)
    si = li & (HD - 1)
    head0 = li < HD
    blk = (ti // SUB) == (si // SUB)
    eye = ti == si
    two = lambda m: jnp.concatenate([m, m], axis=1)
    masks = {False: (two(si < ti), two(si <= ti)), True: (two(si > ti), two(si >= ti))}

    def bd(y):
        yb = y.astype(BF16)
        zero = jnp.zeros_like(yb)
        return jnp.concatenate([jnp.where(head0, yb, zero), jnp.where(head0, zero, yb)], axis=0)

    def bdcat(*ys):
        return jnp.concatenate([bd(y) for y in ys], axis=1)

    def diag_blocks(full):
        return jnp.where(head0, full[:c], full[c:])

    dirs = [(False, rf, vf, kkf, lwf, kf, bf), (True, rb, vb, kkb, lwb, kb, bb)]
    units = []
    for d, (rev, r_, v_, kk_, lw_, k_, b_) in enumerate(dirs):
        f32 = lambda ref: ref[0].astype(F32)
        ops, wcs, bhs, khs = _dir_operands(rev, nch, f32(r_), f32(v_), f32(kk_), lw_[0], f32(k_), f32(b_))
        for j in range(nch):
            rows = slice(j * c, (j + 1) * c)
            for p in range(npair):
                sl = slice(p * pw, (p + 1) * pw)
                u = dict(d=d, j=j, p=p, rev=rev, wc=wcs[j][:, sl],
                         bkh=jnp.concatenate([bhs[j][:, sl], khs[j][:, sl]], axis=0).astype(BF16))
                for name in ("ah", "rt", "bt", "kt", "v"):
                    u[name] = ops[name][rows, sl]
                units.append(u)

    for u in units:
        x1 = jnp.concatenate([u["ah"], u["rt"]], axis=0).astype(BF16)
        u["lbk"] = _dot_nt(x1, jnp.concatenate([bd(u["bt"]), bd(u["kt"])], axis=0))
    for u in units:
        strict, incl = masks[u["rev"]]
        top = jnp.where(strict, u["lbk"][:c], 0.0)
        bot = jnp.where(incl, u["lbk"][c:], 0.0)
        lab = top[:, :pw]
        u["mrb"] = bot[:, :pw].astype(BF16)
        u["ld"] = jnp.where(blk, lab, 0.0)
        u["lo"] = lab - u["ld"]
        u["lm"] = jnp.concatenate([top[:, pw:], bot[:, pw:]], axis=0).astype(BF16)
    for u in units:
        u["a2"] = _dot(u["ld"].astype(BF16), bd(u["ld"]))
        u["lkv"] = _dot(u["lm"], bd(u["v"]))
    for u in units:
        s = _dot(jnp.concatenate([u["a2"], u["ld"]], axis=0).astype(BF16), bd(u["a2"]))
        u["a4"] = s[:c]
        u["t"] = jnp.where(eye, 1.0, 0.0) + u["ld"] + u["a2"] + s[c:]
    for u in units:
        s = _dot(jnp.concatenate([u["a4"], u["t"]], axis=0).astype(BF16), bd(u["a4"]))
        u["a8"] = s[:c]
        u["t"] = u["t"] + s[c:]
    for u in units:
        u["td"] = u["t"] + _dot(u["t"].astype(BF16), bd(u["a8"]))
    for u in units:
        x = _dot(u["td"].astype(BF16), bdcat(u["lo"], u["ah"], u["lkv"][:c]))
        u["n"] = x[:, :pw]
        u["z"] = x[:, pw:]
    for u in units:
        s = _dot(u["n"].astype(BF16), bdcat(u["n"], u["z"][:, :pw], u["z"][:, pw:]))
        u["n2"] = s[:, :pw]
        u["z"] = u["z"] + s[:, pw:]
    for u in units:
        u["z"] = u["z"] + _dot(u["n2"].astype(BF16), bdcat(u["z"][:, :pw], u["z"][:, pw:]))
    for u in units:
        z = u["z"]
        y10 = (jnp.concatenate([u["rt"], u["lkv"][c:]], axis=1)
               + _dot(u["mrb"], bdcat(z[:, :pw], z[:, pw:])))
        vpad = jnp.concatenate([jnp.zeros_like(u["v"]), u["v"]], axis=1)
        full = _dot_tn(u["bkh"], jnp.concatenate([z, vpad], axis=0).astype(BF16))
        g = diag_blocks(full[:, :pw]) + jnp.where(eye, u["wc"], 0.0)
        u["gy"] = jnp.concatenate([g, y10[:, :pw]], axis=0).astype(BF16)
        u["h0"] = diag_blocks(full[:, pw:])
        u["y0"] = y10[:, pw:]

    ys = {}
    for d in range(2):
        order = range(nch - 1, -1, -1) if d == 1 else range(nch)
        zs = [st_ref[d, p] for p in range(npair)]
        for j in order:
            for p in range(npair):
                u = units[(d * nch + j) * npair + p]
                m = _dot(u["gy"], bd(zs[p]))
                zs[p] = m[:c] + u["h0"]
                ys[(d, j, p)] = m[c:] + u["y0"]
        for p in range(npair):
            st_ref[d, p] = zs[p]
    for d, y_ref in enumerate((yf_ref, yb_ref)):
        y_ref[0] = jnp.concatenate(
            [jnp.concatenate([ys[(d, j, p)] for p in range(npair)], axis=-1) for j in range(nch)],
            axis=0).astype(BF16)


def _scan_call(r, v, kkn, lw0, lw1, k0, k1, b0, b1, nch):
    B, T, _ = r.shape
    rows = nch * CHUNK
    ns = T // rows
    fwd = pl.BlockSpec((1, rows, RW), lambda b, c: (b, c, 0))
    bwd = pl.BlockSpec((1, rows, RW), lambda b, c: (b, ns - 1 - c, 0))
    return pl.pallas_call(
        functools.partial(_scan_kernel, nch=nch),
        grid=(B, ns),
        in_specs=[fwd] * 6 + [bwd] * 6,
        out_specs=[fwd, bwd],
        out_shape=[jax.ShapeDtypeStruct((B, T, RW), BF16)] * 2,
        scratch_shapes=[pltpu.VMEM((2, HEADS // 2, HD, 2 * HD), F32)],
        compiler_params=_params(("parallel", "arbitrary")),
        name="scan",
    )(r, v, kkn, lw0, k0, b0, r, v, kkn, lw1, k1, b1)


def _attention_scores(sink_ref, qt_ref, k_ref, v_ref, n, qb):
    t = k_ref.shape[1]
    band = 3 * BLOCK
    cols = GROUP * BLOCK
    ki = lax.broadcasted_iota(jnp.int32, (band, BLOCK), 0)
    qi = lax.broadcasted_iota(jnp.int32, (band, BLOCK), 1)
    lane_head = lax.broadcasted_iota(jnp.int32, (1, cols), 1) // BLOCK
    zeros = jnp.zeros((HD, cols), BF16)
    units = []
    for j in range(qb):
        blk = n * qb + j
        start = pl.multiple_of(jnp.clip((blk - 1) * BLOCK, 0, t - band), BLOCK)
        bias = jnp.where(jnp.abs(start + ki - (blk * BLOCK + qi)) <= BLOCK, 0.0, NEG)
        bias = jnp.concatenate([bias] * GROUP, axis=1)
        kb = k_ref[0, pl.ds(start, band), :]
        vb = v_ref[0, pl.ds(start, band), :]
        for g in range(KV_HEADS):
            qg = jnp.concatenate([qt_ref[0, (g * GROUP + i) * HD:(g * GROUP + i + 1) * HD,
                                         j * BLOCK:(j + 1) * BLOCK] for i in range(GROUP)], axis=1)
            rhs = jnp.concatenate([qg if gg == g else zeros for gg in range(KV_HEADS)], axis=0)
            sk = jnp.zeros((1, cols), F32)
            for i in range(GROUP):
                sk = jnp.where(lane_head == i, sink_ref[g * GROUP + i] * LOG2E, sk)
            units.append(dict(j=j, g=g, kb=kb, vb=vb, rhs=rhs, sk=sk, bias=bias))
    for u in units:
        u["s"] = _dot(u["kb"], u["rhs"]) + u["bias"]
    return units


def _attention_finish(units, qb):
    for u in units:
        u["m"] = jnp.maximum(jnp.max(u["s"], axis=0, keepdims=True), u["sk"])
    for u in units:
        p = jnp.exp2(u["s"] - u["m"])
        u["r"] = 1.0 / (jnp.sum(p, axis=0, keepdims=True) + jnp.exp2(u["sk"] - u["m"]))
        u["p"] = p.astype(BF16)
    for u in units:
        g = u["g"]
        u["o"] = _dot_tn(u["vb"], u["p"])[g * HD:(g + 1) * HD] * u["r"]
    out = {(u["j"], u["g"]): u["o"] for u in units}
    return jnp.concatenate(
        [jnp.concatenate([out[(j, h // GROUP)][:, (h % GROUP) * BLOCK:(h % GROUP + 1) * BLOCK]
                          for j in range(qb)], axis=1) for h in range(HEADS)], axis=0)


def _merge_kernel(sink_ref, x_ref, mod_ref, yf_ref, yb_ref, bonus_ref, g_ref, qt_ref, k_ref, v_ref, gl_ref,
                  gnw_ref, gnb_ref, ones_ref, pr_ref, pa_ref, wo_ref, n2g_ref, wr_ref,
                  x1_ref, u2_ref, aff_ref):
    qb = x_ref.shape[1] // BLOCK
    att = _attention_scores(sink_ref, qt_ref, k_ref, v_ref, pl.program_id(1), qb)
    ones = ones_ref[...]
    y = yf_ref[0].astype(F32) + yb_ref[0].astype(F32)
    mu = _segsum(y, ones) * (1.0 / HD)
    yc = y - mu
    var = _segsum_pos(yc * yc, ones) * (1.0 / HD)
    yn = yc * lax.rsqrt(var + GN_EPS) * gnw_ref[...] + gnb_ref[...]
    ya = ((yn + bonus_ref[0].astype(F32)) * g_ref[0].astype(F32)).astype(BF16)
    pa = _dot(ya, pr_ref[...])
    gates = jax.nn.sigmoid(gl_ref[0].astype(F32))
    yatt_t = _attention_finish(att, qb).astype(BF16)
    pb = _dot_tn(yatt_t, pa_ref[...])
    m = gates[:, :D_MODEL] * pa + gates[:, D_MODEL:] * pb
    x1 = x_ref[0] + mod_ref[0, 2:3, :] * _dot(m.astype(BF16), wo_ref[...])
    x1_ref[0] = x1
    ms = jnp.mean(x1 * x1, axis=-1, keepdims=True)
    u2 = x1 * lax.rsqrt(ms + NORM_EPS) * n2g_ref[...] * (1.0 + mod_ref[0, 4:5, :]) + mod_ref[0, 3:4, :]
    u2_ref[0] = u2
    u2h = u2.astype(BF16)
    u2l = (u2 - u2h.astype(F32)).astype(BF16)
    wr = wr_ref[...]
    wrh = wr.astype(BF16)
    wrl = (wr - wrh.astype(F32)).astype(BF16)
    t1 = _dot_nt(jnp.concatenate([wrh, wrl], axis=0), u2h)
    logits = t1[:N_EXPERTS] + t1[N_EXPERTS:] + _dot_nt(wrh, u2l)
    e = jnp.exp(logits - jnp.max(logits, axis=0, keepdims=True))
    aff_ref[0] = e / jnp.sum(e, axis=0, keepdims=True)


def _merge_call(sink, x, mod3, yf, yb, bonus, g, qt, k, v, gl, gnw, gnb, ones, p_r, p_a, w_o, n2g, w_rt, tm):
    B, T, D = x.shape
    assert T >= 3 * BLOCK and tm % BLOCK == 0
    row = lambda b, i: (b, i, 0)
    const2 = lambda b, i: (0, 0)
    rw = pl.BlockSpec((1, tm, RW), row)
    kv = pl.BlockSpec((1, T, KV_W), lambda b, i: (b, 0, 0))
    return pl.pallas_call(
        _merge_kernel,
        grid=(B, T // tm),
        in_specs=[pl.BlockSpec(memory_space=pltpu.SMEM),
                  pl.BlockSpec((1, tm, D), row),
                  pl.BlockSpec((1, 6, D), lambda b, i: (b, 0, 0)),
                  rw, rw, rw, rw,
                  pl.BlockSpec((1, RW, tm), lambda b, i: (b, 0, i)), kv, kv,
                  pl.BlockSpec((1, tm, 2 * D), row),
                  pl.BlockSpec((1, RW), const2),
                  pl.BlockSpec((1, RW), const2),
                  pl.BlockSpec((RW, RW), const2),
                  pl.BlockSpec((RW, D), const2),
                  pl.BlockSpec((RW, D), const2),
                  pl.BlockSpec((D, D), const2),
                  pl.BlockSpec((1, D), const2),
                  pl.BlockSpec((N_EXPERTS, D), const2)],
        out_specs=[pl.BlockSpec((1, tm, D), row),
                   pl.BlockSpec((1, tm, D), row),
                   pl.BlockSpec((1, N_EXPERTS, tm), lambda b, i: (b, 0, i))],
        out_shape=[jax.ShapeDtypeStruct((B, T, D), F32),
                   jax.ShapeDtypeStruct((B, T, D), F32),
                   jax.ShapeDtypeStruct((B, N_EXPERTS, T), F32)],
        compiler_params=_params(("parallel", "parallel")),
        name="merge",
    )(sink, x, mod3, yf, yb, bonus, g, qt, k, v, gl, gnw, gnb, ones, p_r, p_a, w_o, n2g, w_rt)


def _excl_prefix(mask, tri):
    e, t = mask.shape
    mb = mask.astype(BF16)
    carry = jnp.zeros((e, 1), F32)
    outs = []
    for j in range(t // 128):
        tile = mb[:, j * 128:(j + 1) * 128]
        outs.append(_dot(tile, tri) + carry)
        carry = carry + jnp.sum(tile.astype(F32), axis=1, keepdims=True)
    return jnp.concatenate(outs, axis=-1)


def _route_kernel(aff_ref, tri_ref, dig_ref, idx_ref, affsel_ref, slot_ref, *, cap):
    nb, ne, t = aff_ref.shape
    a = aff_ref[...].reshape(nb * ne, t)
    e = a.shape[0]
    capf = jnp.float32(cap)

    def cond(s):
        it, lo, hi = s
        mid = 0.5 * (lo + hi)
        still_open = jnp.max(jnp.where((mid > lo) & (mid < hi), 1.0, 0.0))
        return (it < BISECT_ITERS) & (still_open > 0.0)

    def body(s):
        it, lo, hi = s
        mid = 0.5 * (lo + hi)
        ge = jnp.sum((a >= mid).astype(F32), axis=1, keepdims=True) >= capf
        return it + 1, jnp.where(ge, mid, lo), jnp.where(ge, hi, mid)

    lo0 = jnp.zeros((e, 1), F32)
    hi0 = jnp.full((e, 1), 2.0, F32)
    _, lo, hi = lax.while_loop(cond, body, (jnp.int32(0), lo0, hi0))
    gt = a >= hi
    eq = (a >= lo) & (a < hi)
    need = capf - jnp.sum(gt.astype(F32), axis=1, keepdims=True)
    tri = tri_ref[...]
    sel = gt | (eq & (_excl_prefix(eq, tri) < need))
    pos = _excl_prefix(sel, tri)
    slot_ref[...] = jnp.where(sel, pos.astype(jnp.int32), -1).reshape(nb, ne, t)

    digits = dig_ref[...]
    assert cap <= 256
    slots = lax.broadcasted_iota(jnp.int32, (cap, t), 0).astype(BF16)
    one, zero = jnp.ones((cap, t), BF16), jnp.zeros((cap, t), BF16)
    pad = jnp.zeros((3, t), F32)

    def row_body(r, carry):
        srow = slot_ref[r // ne, pl.ds(r % ne, 1), :].astype(BF16)
        arow = aff_ref[r // ne, pl.ds(r % ne, 1), :]
        a1 = arow.astype(BF16).astype(F32)
        a2 = (arow - a1).astype(BF16).astype(F32)
        a3 = arow - a1 - a2
        lhs = jnp.concatenate([digits, a1, a2, a3, pad], axis=0).astype(BF16)
        d = _dot_nt(lhs, jnp.where(slots == srow, one, zero))
        idx_ref[r] = (d[0:1] * TOKEN_RADIX + d[1:2]).astype(jnp.int32)
        affsel_ref[r] = d[2:3] + d[3:4] + d[4:5]
        return carry

    lax.fori_loop(0, nb * ne, row_body, 0, unroll=8)


def _route_call(aff, tri, digits, cap):
    B, E, T = aff.shape
    per_slot = pl.BlockSpec((B * E, 1, cap), lambda i: (0, 0, 0))
    return pl.pallas_call(
        functools.partial(_route_kernel, cap=cap),
        grid=(1,),
        in_specs=[pl.BlockSpec((B, E, T), lambda i: (0, 0, 0)),
                  pl.BlockSpec((128, 128), lambda i: (0, 0)),
                  pl.BlockSpec((2, T), lambda i: (0, 0))],
        out_specs=[per_slot, per_slot],
        out_shape=[jax.ShapeDtypeStruct((B * E, 1, cap), jnp.int32),
                   jax.ShapeDtypeStruct((B * E, 1, cap), F32)],
        scratch_shapes=[pltpu.VMEM((B, E, T), jnp.int32)],
        compiler_params=_params(("arbitrary",)),
        name="route",
    )(aff, tri, digits)


def _moe_kernel(idxp_ref, idx0_ref, idx1_ref, affp_ref, aff0_ref, u2_ref, wg_ref, wu_ref, wd_ref, o_ref,
                acc_ref, xa_ref, xb_ref, ya_ref, yb_ref, *, cap):
    e = pl.program_id(1)

    def gather(idx_ref, dst_ref):
        for i in range(cap):
            dst_ref[pl.ds(i, 1), :] = u2_ref[0, pl.ds(idx_ref[0, 0, i], 1), :]

    def scatter(idx_ref, w_ref, src_ref):
        for i in range(cap):
            acc_ref[pl.ds(idx_ref[0, 0, i], 1), :] += w_ref[0, 0, i] * src_ref[pl.ds(i, 1), :]

    def expert(x_ref, y_ref):
        xe = x_ref[...].astype(BF16)
        hg = _dot(xe, wg_ref[0])
        hu = _dot(xe, wu_ref[0])
        h = (hg * jax.nn.sigmoid(hg) * hu).astype(BF16)
        y_ref[...] = _dot(h, wd_ref[0])

    @pl.when(e == 0)
    def _():
        acc_ref[...] = jnp.zeros_like(acc_ref)
        yb_ref[...] = jnp.zeros_like(yb_ref)
        gather(idx0_ref, xa_ref)

    @pl.when(e % 2 == 0)
    def _():
        gather(idx1_ref, xb_ref)
        scatter(idxp_ref, affp_ref, yb_ref)
        expert(xa_ref, ya_ref)

    @pl.when(e % 2 == 1)
    def _():
        gather(idx1_ref, xa_ref)
        scatter(idxp_ref, affp_ref, ya_ref)
        expert(xb_ref, yb_ref)

    @pl.when(e == pl.num_programs(1) - 1)
    def _():
        scatter(idx0_ref, aff0_ref, yb_ref)
        o_ref[0] = acc_ref[...].astype(BF16)


def _moe_call(u2, idx, affsel, wg, wu, wd, cap):
    B, T, D = u2.shape
    E = wg.shape[0]
    F = wg.shape[2]
    assert E % 2 == 0
    smem_row = lambda im: pl.BlockSpec((1, 1, cap), im, memory_space=pltpu.SMEM)
    prev = lambda b, e: (b * E + jnp.maximum(e - 1, 0), 0, 0)
    cur = lambda b, e: (b * E + e, 0, 0)
    nxt = lambda b, e: (b * E + jnp.minimum(e + 1, E - 1), 0, 0)
    rows = pltpu.VMEM((cap, D), F32)
    return pl.pallas_call(
        functools.partial(_moe_kernel, cap=cap),
        grid=(B, E),
        in_specs=[smem_row(prev), smem_row(cur), smem_row(nxt), smem_row(prev), smem_row(cur),
                  pl.BlockSpec((1, T, D), lambda b, e: (b, 0, 0)),
                  pl.BlockSpec((1, D, F), lambda b, e: (e, 0, 0)),
                  pl.BlockSpec((1, D, F), lambda b, e: (e, 0, 0)),
                  pl.BlockSpec((1, F, D), lambda b, e: (e, 0, 0))],
        out_specs=pl.BlockSpec((1, T, D), lambda b, e: (b, 0, 0)),
        out_shape=jax.ShapeDtypeStruct((B, T, D), BF16),
        scratch_shapes=[pltpu.VMEM((T, D), F32), rows, rows, rows, rows],
        compiler_params=pltpu.CompilerParams(dimension_semantics=("parallel", "arbitrary"),
                                             vmem_limit_bytes=MOE_VMEM_LIMIT),
        name="moe",
    )(idx, idx, idx, affsel, affsel, u2, wg, wu, wd)


def _final_kernel(x1_ref, mod_ref, moe_ref, o_ref):
    o_ref[0] = x1_ref[0] + mod_ref[0, 5:6, :] * moe_ref[0].astype(F32)


def _final_call(x1, mod3, moe, tm):
    B, T, D = x1.shape
    row = lambda b, i: (b, i, 0)
    return pl.pallas_call(
        _final_kernel,
        grid=(B, T // tm),
        in_specs=[pl.BlockSpec((1, tm, D), row),
                  pl.BlockSpec((1, 6, D), lambda b, i: (b, 0, 0)),
                  pl.BlockSpec((1, tm, D), row)],
        out_specs=pl.BlockSpec((1, tm, D), row),
        out_shape=jax.ShapeDtypeStruct((B, T, D), F32),
        compiler_params=_params(("parallel", "parallel")),
        name="final",
    )(x1, mod3, moe)


def _blockdiag2(w):
    z = jnp.zeros_like(w[0])
    return jnp.concatenate([jnp.concatenate([w[0], z], axis=1), jnp.concatenate([z, w[1]], axis=1)], axis=0)


def _layer(x, mod3, rope_place, rope_pad, cos_q, sin_q, ones, tri, norm1_g, w_in, mu_prev, mu_next, rwkv_w0, rwkv_w2, rwkv_a0,
           rwkv_a2, rwkv_g2, rwkv_k_k, rwkv_k_a, rwkv_r_k, rwkv_gn_w, rwkv_gn_b, q_norm_g, k_norm_g,
           attn_sink, p_rwkv, p_attn, w_out, norm2_g, w_router, w_gate, w_up, w_down):
    B, T, D = x.shape
    tm = TM_ROWS if T % TM_ROWS == 0 else min(T, 256)
    cap = CAP_FACTOR * T // N_EXPERTS
    row = lambda a: a.reshape(1, -1)
    w_b = w_in.astype(BF16)
    qg = jnp.broadcast_to(jnp.tile(q_norm_g, HEADS)[:, None], (RW, 128))
    qt, k, v, gl, r, vv, kkn, g, bonus, lw0, lw1, k0, k1, b0, b1 = _inproj_call(
        x, mod3, row(norm1_g), w_b, w_in[:, Q0:K0].T.astype(BF16), rope_place, rope_pad, cos_q, sin_q,
        qg, row(jnp.tile(k_norm_g, KV_HEADS)), ones,
        row(mu_prev), row(mu_next), row(rwkv_w0), _blockdiag2(rwkv_w2).astype(BF16),
        row(rwkv_a0), _blockdiag2(rwkv_a2).astype(BF16), rwkv_g2.astype(BF16),
        row(rwkv_k_k), row(rwkv_k_a), row(rwkv_r_k), tm)
    yf, yb = _scan_call(r, vv, kkn, lw0, lw1, k0, k1, b0, b1, SCAN_CHUNKS)
    x1, u2, aff = _merge_call(attn_sink, x, mod3, yf, yb, bonus, g, qt, k, v, gl, row(rwkv_gn_w), row(rwkv_gn_b), ones,
                              p_rwkv.astype(BF16), p_attn.astype(BF16), w_out.astype(BF16),
                              row(norm2_g), w_router.T, tm)
    assert T <= 256 * TOKEN_RADIX
    tok = jnp.arange(T)
    digits = jnp.stack([tok // TOKEN_RADIX, tok % TOKEN_RADIX]).astype(F32)
    idx, affsel = _route_call(aff, tri, digits, cap)
    moe = _moe_call(u2, idx, affsel, w_gate.astype(BF16), w_up.astype(BF16), w_down.astype(BF16), cap)
    return _final_call(x1, mod3, moe, TM_FINAL if T % TM_FINAL == 0 else tm)


def kernel(x, c, positions, w_ada, b_ada, norm1_g, w_in, mu_prev, mu_next, rwkv_w0, rwkv_w2, rwkv_a0, rwkv_a2, rwkv_g2, rwkv_k_k, rwkv_k_a, rwkv_r_k, rwkv_gn_w, rwkv_gn_b, q_norm_g, k_norm_g, attn_sink, p_rwkv, p_attn, w_out, norm2_g, w_router, w_gate, w_up, w_down):
    B, T, D = x.shape
    depth = w_ada.shape[0]
    half = ROT // 2
    inv_freq = ROPE_THETA ** (-jnp.arange(0, ROT, 2, dtype=F32) / ROT)
    ang = positions.astype(F32)[..., None] * inv_freq
    cos8, sin8 = jnp.cos(ang), jnp.sin(ang)
    lane = jnp.arange(KV_W) % HD
    freq = jnp.arange(half)[:, None]
    first, second = (lane[None, :] == freq), (lane[None, :] == freq + half)
    rope_place = jnp.stack([(first | second).astype(F32), second.astype(F32) - first.astype(F32)])
    rope_pad = (lane >= ROT).astype(F32).reshape(1, KV_W)
    seg = jnp.arange(RW) // HD
    ones = (seg[:, None] == seg[None, :]).astype(BF16)
    idx = jnp.arange(128)
    tri = (idx[:, None] < idx[None, :]).astype(BF16)
    cos_q, sin_q = jnp.swapaxes(cos8, 1, 2), jnp.swapaxes(sin8, 1, 2)
    for l in range(depth):
        mod3 = _mod_call(c, w_ada[l], b_ada[l]).reshape(B, 6, D)
        x = _layer(x, mod3, rope_place, rope_pad, cos_q, sin_q, ones, tri, norm1_g[l], w_in[l], mu_prev[l], mu_next[l],
                   rwkv_w0[l], rwkv_w2[l], rwkv_a0[l], rwkv_a2[l], rwkv_g2[l], rwkv_k_k[l], rwkv_k_a[l],
                   rwkv_r_k[l], rwkv_gn_w[l], rwkv_gn_b[l], q_norm_g[l], k_norm_g[l], attn_sink[l],
                   p_rwkv[l], p_attn[l], w_out[l], norm2_g[l], w_router[l], w_gate[l], w_up[l], w_down[l])
    return x
```

```python
import functools
import math

import jax
import jax.numpy as jnp
from jax import lax
from jax.experimental import pallas as pl
from jax.experimental.pallas import tpu as pltpu

F32 = jnp.float32
BF16 = jnp.bfloat16

D_MODEL = 1024
RW = 512
HEADS = 8
HD = 64
LORA_W = 64
LORA_A = 64
LORA_G = 128
GN_EPS = HD * 1e-5
KV_HEADS = 2
GROUP = HEADS // KV_HEADS
KV_W = KV_HEADS * HD
BLOCK = 128
ROPE_THETA = 500000.0
ROT = HD // 4
N_EXPERTS = 16
CAP_FACTOR = 2
NORM_EPS = 1e-6
RWKV_COLS = 3 * RW + 2 * LORA_W + 2 * LORA_A + LORA_G
Q0 = RWKV_COLS
K0 = Q0 + RW
V0 = K0 + KV_W
G0 = V0 + KV_W
IN_COLS = G0 + 2 * D_MODEL
CHUNK = 64
SUB = 16
SCAN_CHUNKS = 4
TM_FINAL = 2048
TM_ROWS = 512
TOKEN_RADIX = 64
BISECT_ITERS = 160
NEG = -1e30
LOG2E = math.log2(math.e)
QK_SCALE = HD ** -0.5 * LOG2E
VMEM_LIMIT = 56 * 1024 * 1024
MOE_VMEM_LIMIT = 60 * 1024 * 1024
MOE_ACC_SPLIT = 4


def _dot(a, b):
    return jnp.dot(a, b, preferred_element_type=F32)


def _dot_nt(a, b):
    return lax.dot_general(a, b, (((1,), (1,)), ((), ())), preferred_element_type=F32)


def _dot_tn(a, b):
    return lax.dot_general(a, b, (((0,), (0,)), ((), ())), preferred_element_type=F32)


def _segsum(x, ones):
    xh = x.astype(BF16)
    xl = (x - xh.astype(F32)).astype(BF16)
    return _dot(xh, ones) + _dot(xl, ones)


def _segsum_pos(x, ones):
    return _dot(x.astype(BF16), ones)


def _params(sem):
    return pltpu.CompilerParams(dimension_semantics=sem, vmem_limit_bytes=VMEM_LIMIT)


def _mod_kernel(c_ref, w_ref, b_ref, o_ref):
    c = c_ref[...]
    ca = c * jax.nn.sigmoid(c)
    o_ref[...] = jnp.dot(ca, w_ref[...], preferred_element_type=F32,
                         precision=lax.Precision.HIGHEST) + b_ref[...]


def _mod_call(c, w_ada, b_ada):
    B, D = c.shape
    n = w_ada.shape[1] // D
    return pl.pallas_call(
        _mod_kernel,
        grid=(n,),
        in_specs=[pl.BlockSpec((B, D), lambda j: (0, 0)),
                  pl.BlockSpec((D, D), lambda j: (0, j)),
                  pl.BlockSpec((1, D), lambda j: (0, j))],
        out_specs=pl.BlockSpec((B, D), lambda j: (0, j)),
        out_shape=jax.ShapeDtypeStruct((B, n * D), F32),
        compiler_params=_params(("arbitrary",)),
        name="mod",
    )(c, w_ada, b_ada.reshape(1, -1))


def _rope(xn, cos, sin, width):
    lane = lax.broadcasted_iota(jnp.int32, xn.shape, 1) & (HD - 1)
    rot = jnp.where(lane < ROT // 2, pltpu.roll(xn, width - ROT // 2, 1), pltpu.roll(xn, ROT // 2, 1))
    return xn * cos + rot * sin


HALO = 8
INPROJ_SUBTILES = 1


def _inproj_kernel(x_ref, xp_ref, xn_ref, mod_ref, g_ref, w_ref, wq_ref, place_ref, pad_ref, cosq_ref, sinq_ref,
                   qg_ref, kg_ref, ones8_ref,
                   mup_ref, mun_ref, w0_ref, w2_ref, a0_ref, a2_ref, g2_ref, kk_ref, ka_ref, rk_ref,
                   qt_ref, k_ref, v_ref, gl_ref,
                   r_o, v_o, kkn_o, g_o, bonus_o, lw0_o, lw1_o, k0_o, k1_o, b0_o, b1_o):
    tm = x_ref.shape[1]
    sub = tm // INPROJ_SUBTILES
    xx = jnp.concatenate([xp_ref[0], x_ref[0], xn_ref[0]], axis=0)
    for lo in range(0, tm, sub):
        _inproj_rows(lo, sub, xx[lo:lo + sub + 2 * HALO], tm, mod_ref, g_ref, w_ref, wq_ref, place_ref, pad_ref,
                     cosq_ref, sinq_ref, qg_ref, kg_ref, ones8_ref, mup_ref, mun_ref, w0_ref, w2_ref, a0_ref,
                     a2_ref, g2_ref, kk_ref, ka_ref, rk_ref, qt_ref, k_ref, v_ref, gl_ref,
                     (r_o, v_o, kkn_o, g_o, bonus_o, lw0_o, lw1_o, k0_o, k1_o, b0_o, b1_o))


def _inproj_rows(lo, n, xx, tm, mod_ref, g_ref, w_ref, wq_ref, place_ref, pad_ref, cosq_ref, sinq_ref,
                 qg_ref, kg_ref, ones8_ref, mup_ref, mun_ref, w0_ref, w2_ref, a0_ref, a2_ref, g2_ref,
                 kk_ref, ka_ref, rk_ref, qt_ref, k_ref, v_ref, gl_ref, rwkv_outs):
    r_o, v_o, kkn_o, g_o, bonus_o, lw0_o, lw1_o, k0_o, k1_o, b0_o, b1_o = rwkv_outs
    i = pl.program_id(1)
    last = pl.num_programs(1) - 1
    rows = slice(lo, lo + n)
    ms = jnp.mean(xx * xx, axis=-1, keepdims=True)
    y = xx * lax.rsqrt(ms + NORM_EPS) * g_ref[...]
    u_all = y * (1.0 + mod_ref[0, 1:2, :]) + mod_ref[0, 0:1, :]
    ub = u_all[HALO:HALO + n].astype(BF16)
    z_all = _dot(u_all.astype(BF16), w_ref[:, 0:RWKV_COLS])
    ones8 = ones8_ref[...]

    k = _dot(ub, w_ref[:, K0:V0])
    v_ref[0, rows, :] = _dot(ub, w_ref[:, V0:G0]).astype(BF16)
    gl_ref[0, rows, :] = _dot(ub, w_ref[:, G0:IN_COLS]).astype(BF16)
    kn = k * lax.rsqrt(_segsum_pos(k * k, ones8[:KV_W, :KV_W]) * (1.0 / HD) + NORM_EPS) * kg_ref[...]
    def place(t8, e):
        hi = t8.astype(BF16).astype(F32)
        return _dot_tn(hi, e) + _dot_tn(t8 - hi, e)

    cos_k = place(cosq_ref[0, :, rows], place_ref[0]) + pad_ref[...]
    sin_k = place(sinq_ref[0, :, rows], place_ref[1])
    k_ref[0, rows, :] = _rope(kn, cos_k, sin_k, KV_W).astype(BF16)

    z = z_all[HALO:HALO + n]
    row = lax.broadcasted_iota(jnp.int32, z.shape, 0) + lo
    zp = jnp.where((row == 0) & (i == 0), 0.0, z_all[HALO - 1:HALO - 1 + n])
    zn = jnp.where((row == tm - 1) & (i == last), 0.0, z_all[HALO + 1:HALO + 1 + n])
    zs = z + mup_ref[...] * (zp - z) + mun_ref[...] * (zn - z)
    r = zs[:, 0:RW]
    kr = zs[:, RW:2 * RW]
    vr = zs[:, 2 * RW:3 * RW]
    c0 = 3 * RW
    wd = zs[:, c0:c0 + 2 * LORA_W]
    ad = zs[:, c0 + 2 * LORA_W:c0 + 2 * LORA_W + 2 * LORA_A]
    gd = zs[:, c0 + 2 * LORA_W + 2 * LORA_A:RWKV_COLS]
    wl = _dot(jnp.tanh(wd).astype(BF16), w2_ref[...]) + w0_ref[...]
    lw = -math.exp(-0.5) * jax.nn.sigmoid(wl)
    al = jax.nn.sigmoid(_dot(ad.astype(BF16), a2_ref[...]) + a0_ref[...])
    g = _dot(jax.nn.sigmoid(gd).astype(BF16), g2_ref[...])
    kk = kr * kk_ref[...]
    kkn = kk / jnp.maximum(jnp.sqrt(_segsum_pos(kk * kk, ones8)), 1e-12)
    ka = ka_ref[...]
    a_0 = al[:, 0:RW]
    a_1 = al[:, RW:2 * RW]
    k_0 = kr * (1.0 + (a_0 - 1.0) * ka)
    k_1 = kr * (1.0 + (a_1 - 1.0) * ka)
    bonus = _segsum(r * (k_0 + k_1) * rk_ref[...], ones8) * vr
    r_o[0, rows, :] = r.astype(BF16)
    v_o[0, rows, :] = vr.astype(BF16)
    kkn_o[0, rows, :] = kkn.astype(BF16)
    g_o[0, rows, :] = g.astype(BF16)
    bonus_o[0, rows, :] = bonus.astype(BF16)
    lw0_o[0, rows, :] = lw[:, 0:RW]
    lw1_o[0, rows, :] = lw[:, RW:2 * RW]
    k0_o[0, rows, :] = k_0.astype(BF16)
    k1_o[0, rows, :] = k_1.astype(BF16)
    b0_o[0, rows, :] = (kkn * a_0).astype(BF16)
    b1_o[0, rows, :] = (kkn * a_1).astype(BF16)

    qt = _dot_nt(wq_ref[...], ub)
    qg = jnp.concatenate([qg_ref[...]] * (n // 128), axis=1)
    cq = cosq_ref[0, :, rows]
    sq = sinq_ref[0, :, rows]
    half = ROT // 2
    pieces = []
    for h in range(HEADS):
        xh = qt[h * HD:(h + 1) * HD]
        xn = (xh * lax.rsqrt(jnp.mean(xh * xh, axis=0, keepdims=True) + NORM_EPS)
              * qg[h * HD:(h + 1) * HD] * QK_SCALE)
        x1, x2 = xn[0:half], xn[half:ROT]
        pieces += [x1 * cq - x2 * sq, x2 * cq + x1 * sq, xn[ROT:]]
    qt_ref[0, :, rows] = jnp.concatenate(pieces, axis=0).astype(BF16)


def _inproj_call(x, mod3, norm1_g, w_rest, wq_t, rope_place, rope_pad, cos_q, sin_q, qg, kg, ones,
                 mup, mun, w0c, w2c, a0c, a2c, g2b, k_k, k_a, r_k, tm):
    B, T, D = x.shape
    nh = tm // HALO
    nth = T // HALO
    row = lambda b, i: (b, i, 0)
    col = lambda b, i: (b, 0, i)
    const2 = lambda b, i: (0, 0)
    half = ROT // 2
    rw = pl.BlockSpec((1, tm, RW), row)
    vec = lambda n: pl.BlockSpec((1, n), const2)
    return pl.pallas_call(
        _inproj_kernel,
        grid=(B, T // tm),
        in_specs=[pl.BlockSpec((1, tm, D), row),
                  pl.BlockSpec((1, HALO, D), lambda b, i: (b, jnp.maximum(i * nh - 1, 0), 0)),
                  pl.BlockSpec((1, HALO, D), lambda b, i: (b, jnp.minimum((i + 1) * nh, nth - 1), 0)),
                  pl.BlockSpec((1, 6, D), lambda b, i: (b, 0, 0)),
                  vec(D),
                  pl.BlockSpec(w_rest.shape, const2),
                  pl.BlockSpec((RW, D), const2),
                  pl.BlockSpec((2, half, KV_W), lambda b, i: (0, 0, 0)),
                  vec(KV_W),
                  pl.BlockSpec((1, half, tm), col),
                  pl.BlockSpec((1, half, tm), col),
                  pl.BlockSpec((RW, 128), const2),
                  vec(KV_W),
                  pl.BlockSpec((RW, RW), const2),
                  vec(RWKV_COLS), vec(RWKV_COLS),
                  vec(2 * RW), pl.BlockSpec((2 * LORA_W, 2 * RW), const2),
                  vec(2 * RW), pl.BlockSpec((2 * LORA_A, 2 * RW), const2),
                  pl.BlockSpec((LORA_G, RW), const2),
                  vec(RW), vec(RW), vec(RW)],
        out_specs=[pl.BlockSpec((1, RW, tm), col),
                   pl.BlockSpec((1, tm, KV_W), row),
                   pl.BlockSpec((1, tm, KV_W), row),
                   pl.BlockSpec((1, tm, 2 * D), row)] + [rw] * 11,
        out_shape=[jax.ShapeDtypeStruct((B, RW, T), BF16),
                   jax.ShapeDtypeStruct((B, T, KV_W), BF16),
                   jax.ShapeDtypeStruct((B, T, KV_W), BF16),
                   jax.ShapeDtypeStruct((B, T, 2 * D), BF16)]
        + [jax.ShapeDtypeStruct((B, T, RW), F32 if i in (5, 6) else BF16) for i in range(11)],
        compiler_params=_params(("parallel", "parallel")),
        name="inproj",
    )(x, x, x, mod3, norm1_g, w_rest, wq_t, rope_place, rope_pad, cos_q, sin_q, qg, kg, ones,
      mup, mun, w0c, w2c, a0c, a2c, g2b, k_k, k_a, r_k)


def _dir_operands(rev, nch, r, v, kk, lw, kd, bd):
    n = r.shape[0]
    ti = lax.broadcasted_iota(jnp.int32, (n, n), 0)
    si = lax.broadcasted_iota(jnp.int32, (n, n), 1)
    same = (ti // CHUNK) == (si // CHUNK)
    incl = ((si >= ti) if rev else (si <= ti)) & same
    mi = incl.astype(BF16)
    l1 = lw.astype(BF16)
    l2 = (lw - l1.astype(F32)).astype(BF16)
    cum = _dot(mi, l1) + _dot(mi, l2)
    wt = jnp.exp(cum)
    winv = jnp.exp(-cum)
    ops = dict(ah=-(kk * jnp.exp(cum - lw)), rt=r * wt, bt=bd * winv, kt=kd * winv, v=v)
    wcs, bhs, khs = [], [], []
    for j in range(nch):
        last = j * CHUNK if rev else (j + 1) * CHUNK - 1
        wc = wt[last:last + 1, :]
        rows = slice(j * CHUNK, (j + 1) * CHUNK)
        wcs.append(wc)
        bhs.append(ops["bt"][rows] * wc)
        khs.append(ops["kt"][rows] * wc)
    return ops, wcs, bhs, khs


def _scan_kernel(rf, vf, kkf, lwf, kf, bf, rb, vb, kkb, lwb, kb, bb, yf_ref, yb_ref, st_ref, *, nch):
    @pl.when(pl.program_id(1) == 0)
    def _():
        st_ref[...] = jnp.zeros_like(st_ref)

    c = CHUNK
    pw = 2 * HD
    npair = HEADS // 2
    ti = lax.broadcasted_iota(jnp.int32, (c, pw), 0)
    li = lax.broadcasted_iota(jnp.int32, (c, pw), 1)
    si = li & (HD - 1)
    head0 = li < HD
    blk = (ti // SUB) == (si // SUB)
    eye = ti == si
    two = lambda m: jnp.concatenate([m, m], axis=1)
    masks = {False: (two(si < ti), two(si <= ti)), True: (two(si > ti), two(si >= ti))}

    def bd(y):
        yb = y.astype(BF16)
        zero = jnp.zeros_like(yb)
        return jnp.concatenate([jnp.where(head0, yb, zero), jnp.where(head0, zero, yb)], axis=0)

    def bdcat(*ys):
        return jnp.concatenate([bd(y) for y in ys], axis=1)

    def diag_blocks(full):
        return jnp.where(head0, full[:c], full[c:])

    dirs = [(False, rf, vf, kkf, lwf, kf, bf), (True, rb, vb, kkb, lwb, kb, bb)]
    units = []
    for d, (rev, r_, v_, kk_, lw_, k_, b_) in enumerate(dirs):
        f32 = lambda ref: ref[0].astype(F32)
        ops, wcs, bhs, khs = _dir_operands(rev, nch, f32(r_), f32(v_), f32(kk_), lw_[0], f32(k_), f32(b_))
        for j in range(nch):
            rows = slice(j * c, (j + 1) * c)
            for p in range(npair):
                sl = slice(p * pw, (p + 1) * pw)
                u = dict(d=d, j=j, p=p, rev=rev, wc=wcs[j][:, sl],
                         bkh=jnp.concatenate([bhs[j][:, sl], khs[j][:, sl]], axis=0).astype(BF16))
                for name in ("ah", "rt", "bt", "kt", "v"):
                    u[name] = ops[name][rows, sl]
                units.append(u)

    for u in units:
        x1 = jnp.concatenate([u["ah"], u["rt"]], axis=0).astype(BF16)
        u["lbk"] = _dot_nt(x1, jnp.concatenate([bd(u["bt"]), bd(u["kt"])], axis=0))
    for u in units:
        strict, incl = masks[u["rev"]]
        top = jnp.where(strict, u["lbk"][:c], 0.0)
        bot = jnp.where(incl, u["lbk"][c:], 0.0)
        lab = top[:, :pw]
        u["mrb"] = bot[:, :pw].astype(BF16)
        u["ld"] = jnp.where(blk, lab, 0.0)
        u["lo"] = lab - u["ld"]
        u["lm"] = jnp.concatenate([top[:, pw:], bot[:, pw:]], axis=0).astype(BF16)
    for u in units:
        u["a2"] = _dot(u["ld"].astype(BF16), bd(u["ld"]))
        u["lkv"] = _dot(u["lm"], bd(u["v"]))
    for u in units:
        s = _dot(jnp.concatenate([u["a2"], u["ld"]], axis=0).astype(BF16), bd(u["a2"]))
        u["a4"] = s[:c]
        u["t"] = jnp.where(eye, 1.0, 0.0) + u["ld"] + u["a2"] + s[c:]
    for u in units:
        s = _dot(jnp.concatenate([u["a4"], u["t"]], axis=0).astype(BF16), bd(u["a4"]))
        u["a8"] = s[:c]
        u["t"] = u["t"] + s[c:]
    for u in units:
        u["td"] = u["t"] + _dot(u["t"].astype(BF16), bd(u["a8"]))
    for u in units:
        x = _dot(u["td"].astype(BF16), bdcat(u["lo"], u["ah"], u["lkv"][:c]))
        u["n"] = x[:, :pw]
        u["z"] = x[:, pw:]
    for u in units:
        s = _dot(u["n"].astype(BF16), bdcat(u["n"], u["z"][:, :pw], u["z"][:, pw:]))
        u["n2"] = s[:, :pw]
        u["z"] = u["z"] + s[:, pw:]
    for u in units:
        u["z"] = u["z"] + _dot(u["n2"].astype(BF16), bdcat(u["z"][:, :pw], u["z"][:, pw:]))
    for u in units:
        z = u["z"]
        y10 = (jnp.concatenate([u["rt"], u["lkv"][c:]], axis=1)
               + _dot(u["mrb"], bdcat(z[:, :pw], z[:, pw:])))
        vpad = jnp.concatenate([jnp.zeros_like(u["v"]), u["v"]], axis=1)
        full = _dot_tn(u["bkh"], jnp.concatenate([z, vpad], axis=0).astype(BF16))
        g = diag_blocks(full[:, :pw]) + jnp.where(eye, u["wc"], 0.0)
        u["gy"] = jnp.concatenate([g, y10[:, :pw]], axis=0).astype(BF16)
        u["h0"] = diag_blocks(full[:, pw:])
        u["y0"] = y10[:, pw:]

    ys = {}
    for d in range(2):
        order = range(nch - 1, -1, -1) if d == 1 else range(nch)
        zs = [st_ref[d, p] for p in range(npair)]
        for j in order:
            for p in range(npair):
                u = units[(d * nch + j) * npair + p]
                m = _dot(u["gy"], bd(zs[p]))
                zs[p] = m[:c] + u["h0"]
                ys[(d, j, p)] = m[c:] + u["y0"]
        for p in range(npair):
            st_ref[d, p] = zs[p]
    for d, y_ref in enumerate((yf_ref, yb_ref)):
        y_ref[0] = jnp.concatenate(
            [jnp.concatenate([ys[(d, j, p)] for p in range(npair)], axis=-1) for j in range(nch)],
            axis=0).astype(BF16)


def _scan_call(r, v, kkn, lw0, lw1, k0, k1, b0, b1, nch):
    B, T, _ = r.shape
    rows = nch * CHUNK
    ns = T // rows
    fwd = pl.BlockSpec((1, rows, RW), lambda b, c: (b, c, 0))
    bwd = pl.BlockSpec((1, rows, RW), lambda b, c: (b, ns - 1 - c, 0))
    return pl.pallas_call(
        functools.partial(_scan_kernel, nch=nch),
        grid=(B, ns),
        in_specs=[fwd] * 6 + [bwd] * 6,
        out_specs=[fwd, bwd],
        out_shape=[jax.ShapeDtypeStruct((B, T, RW), BF16)] * 2,
        scratch_shapes=[pltpu.VMEM((2, HEADS // 2, HD, 2 * HD), F32)],
        compiler_params=_params(("parallel", "arbitrary")),
        name="scan",
    )(r, v, kkn, lw0, k0, b0, r, v, kkn, lw1, k1, b1)


def _attention_scores(sink_ref, qt_ref, k_ref, v_ref, n, qb):
    t = k_ref.shape[1]
    band = 3 * BLOCK
    cols = GROUP * BLOCK
    ki = lax.broadcasted_iota(jnp.int32, (band, BLOCK), 0)
    qi = lax.broadcasted_iota(jnp.int32, (band, BLOCK), 1)
    lane_head = lax.broadcasted_iota(jnp.int32, (1, cols), 1) // BLOCK
    zeros = jnp.zeros((HD, cols), BF16)
    units = []
    for j in range(qb):
        blk = n * qb + j
        start = pl.multiple_of(jnp.clip((blk - 1) * BLOCK, 0, t - band), BLOCK)
        bias = jnp.where(jnp.abs(start + ki - (blk * BLOCK + qi)) <= BLOCK, 0.0, NEG)
        bias = jnp.concatenate([bias] * GROUP, axis=1)
        kb = k_ref[0, pl.ds(start, band), :]
        vb = v_ref[0, pl.ds(start, band), :]
        for g in range(KV_HEADS):
            qg = jnp.concatenate([qt_ref[0, (g * GROUP + i) * HD:(g * GROUP + i + 1) * HD,
                                         j * BLOCK:(j + 1) * BLOCK] for i in range(GROUP)], axis=1)
            rhs = jnp.concatenate([qg if gg == g else zeros for gg in range(KV_HEADS)], axis=0)
            sk = jnp.zeros((1, cols), F32)
            for i in range(GROUP):
                sk = jnp.where(lane_head == i, sink_ref[g * GROUP + i] * LOG2E, sk)
            units.append(dict(j=j, g=g, kb=kb, vb=vb, rhs=rhs, sk=sk, bias=bias))
    for u in units:
        u["s"] = _dot(u["kb"], u["rhs"]) + u["bias"]
    return units


def _attention_finish(units, qb):
    for u in units:
        u["m"] = jnp.maximum(jnp.max(u["s"], axis=0, keepdims=True), u["sk"])
    for u in units:
        p = jnp.exp2(u["s"] - u["m"])
        u["r"] = 1.0 / (jnp.sum(p, axis=0, keepdims=True) + jnp.exp2(u["sk"] - u["m"]))
        u["p"] = p.astype(BF16)
    for u in units:
        g = u["g"]
        u["o"] = _dot_tn(u["vb"], u["p"])[g * HD:(g + 1) * HD] * u["r"]
    out = {(u["j"], u["g"]): u["o"] for u in units}
    return jnp.concatenate(
        [jnp.concatenate([out[(j, h // GROUP)][:, (h % GROUP) * BLOCK:(h % GROUP + 1) * BLOCK]
                          for j in range(qb)], axis=1) for h in range(HEADS)], axis=0)


def _merge_kernel(sink_ref, x_ref, mod_ref, yf_ref, yb_ref, bonus_ref, g_ref, qt_ref, k_ref, v_ref, gl_ref,
                  gnw_ref, gnb_ref, ones_ref, pr_ref, pa_ref, wo_ref, n2g_ref, wr_ref,
                  x1_ref, u2_ref, aff_ref):
    qb = x_ref.shape[1] // BLOCK
    att = _attention_scores(sink_ref, qt_ref, k_ref, v_ref, pl.program_id(1), qb)
    ones = ones_ref[...]
    y = yf_ref[0].astype(F32) + yb_ref[0].astype(F32)
    mu = _segsum(y, ones) * (1.0 / HD)
    yc = y - mu
    var = _segsum_pos(yc * yc, ones) * (1.0 / HD)
    yn = yc * lax.rsqrt(var + GN_EPS) * gnw_ref[...] + gnb_ref[...]
    ya = ((yn + bonus_ref[0].astype(F32)) * g_ref[0].astype(F32)).astype(BF16)
    pa = _dot(ya, pr_ref[...])
    gates = jax.nn.sigmoid(gl_ref[0].astype(F32))
    yatt_t = _attention_finish(att, qb).astype(BF16)
    pb = _dot_tn(yatt_t, pa_ref[...])
    m = gates[:, :D_MODEL] * pa + gates[:, D_MODEL:] * pb
    x1 = x_ref[0] + mod_ref[0, 2:3, :] * _dot(m.astype(BF16), wo_ref[...])
    x1_ref[0] = x1
    ms = jnp.mean(x1 * x1, axis=-1, keepdims=True)
    u2 = x1 * lax.rsqrt(ms + NORM_EPS) * n2g_ref[...] * (1.0 + mod_ref[0, 4:5, :]) + mod_ref[0, 3:4, :]
    u2_ref[0] = u2
    u2h = u2.astype(BF16)
    u2l = (u2 - u2h.astype(F32)).astype(BF16)
    wr = wr_ref[...]
    wrh = wr.astype(BF16)
    wrl = (wr - wrh.astype(F32)).astype(BF16)
    t1 = _dot_nt(jnp.concatenate([wrh, wrl], axis=0), u2h)
    logits = t1[:N_EXPERTS] + t1[N_EXPERTS:] + _dot_nt(wrh, u2l)
    e = jnp.exp(logits - jnp.max(logits, axis=0, keepdims=True))
    aff_ref[0] = e / jnp.sum(e, axis=0, keepdims=True)


def _merge_call(sink, x, mod3, yf, yb, bonus, g, qt, k, v, gl, gnw, gnb, ones, p_r, p_a, w_o, n2g, w_rt, tm):
    B, T, D = x.shape
    assert T >= 3 * BLOCK and tm % BLOCK == 0
    row = lambda b, i: (b, i, 0)
    const2 = lambda b, i: (0, 0)
    rw = pl.BlockSpec((1, tm, RW), row)
    kv = pl.BlockSpec((1, T, KV_W), lambda b, i: (b, 0, 0))
    return pl.pallas_call(
        _merge_kernel,
        grid=(B, T // tm),
        in_specs=[pl.BlockSpec(memory_space=pltpu.SMEM),
                  pl.BlockSpec((1, tm, D), row),
                  pl.BlockSpec((1, 6, D), lambda b, i: (b, 0, 0)),
                  rw, rw, rw, rw,
                  pl.BlockSpec((1, RW, tm), lambda b, i: (b, 0, i)), kv, kv,
                  pl.BlockSpec((1, tm, 2 * D), row),
                  pl.BlockSpec((1, RW), const2),
                  pl.BlockSpec((1, RW), const2),
                  pl.BlockSpec((RW, RW), const2),
                  pl.BlockSpec((RW, D), const2),
                  pl.BlockSpec((RW, D), const2),
                  pl.BlockSpec((D, D), const2),
                  pl.BlockSpec((1, D), const2),
                  pl.BlockSpec((N_EXPERTS, D), const2)],
        out_specs=[pl.BlockSpec((1, tm, D), row),
                   pl.BlockSpec((1, tm, D), row),
                   pl.BlockSpec((1, N_EXPERTS, tm), lambda b, i: (b, 0, i))],
        out_shape=[jax.ShapeDtypeStruct((B, T, D), F32),
                   jax.ShapeDtypeStruct((B, T, D), F32),
                   jax.ShapeDtypeStruct((B, N_EXPERTS, T), F32)],
        compiler_params=_params(("parallel", "parallel")),
        name="merge",
    )(sink, x, mod3, yf, yb, bonus, g, qt, k, v, gl, gnw, gnb, ones, p_r, p_a, w_o, n2g, w_rt)


def _excl_prefix(mask, tri):
    e, t = mask.shape
    mb = mask.astype(BF16)
    carry = jnp.zeros((e, 1), F32)
    outs = []
    for j in range(t // 128):
        tile = mb[:, j * 128:(j + 1) * 128]
        outs.append(_dot(tile, tri) + carry)
        carry = carry + jnp.sum(tile.astype(F32), axis=1, keepdims=True)
    return jnp.concatenate(outs, axis=-1)


def _route_kernel(aff_ref, tri_ref, dig_ref, idx_ref, affsel_ref, slot_ref, *, cap):
    nb, ne, t = aff_ref.shape
    a = aff_ref[...].reshape(nb * ne, t)
    e = a.shape[0]
    capf = jnp.float32(cap)

    def cond(s):
        it, lo, hi = s
        mid = 0.5 * (lo + hi)
        still_open = jnp.max(jnp.where((mid > lo) & (mid < hi), 1.0, 0.0))
        return (it < BISECT_ITERS) & (still_open > 0.0)

    def body(s):
        it, lo, hi = s
        mid = 0.5 * (lo + hi)
        ge = jnp.sum((a >= mid).astype(F32), axis=1, keepdims=True) >= capf
        return it + 1, jnp.where(ge, mid, lo), jnp.where(ge, hi, mid)

    lo0 = jnp.zeros((e, 1), F32)
    hi0 = jnp.full((e, 1), 2.0, F32)
    _, lo, hi = lax.while_loop(cond, body, (jnp.int32(0), lo0, hi0))
    gt = a >= hi
    eq = (a >= lo) & (a < hi)
    need = capf - jnp.sum(gt.astype(F32), axis=1, keepdims=True)
    tri = tri_ref[...]
    sel = gt | (eq & (_excl_prefix(eq, tri) < need))
    pos = _excl_prefix(sel, tri)
    slot_ref[...] = jnp.where(sel, pos.astype(jnp.int32), -1).reshape(nb, ne, t)

    digits = dig_ref[...]
    assert cap <= 256
    slots = lax.broadcasted_iota(jnp.int32, (cap, t), 0).astype(BF16)
    one, zero = jnp.ones((cap, t), BF16), jnp.zeros((cap, t), BF16)
    pad = jnp.zeros((3, t), F32)

    def row_body(r, carry):
        srow = slot_ref[r // ne, pl.ds(r % ne, 1), :].astype(BF16)
        arow = aff_ref[r // ne, pl.ds(r % ne, 1), :]
        a1 = arow.astype(BF16).astype(F32)
        a2 = (arow - a1).astype(BF16).astype(F32)
        a3 = arow - a1 - a2
        lhs = jnp.concatenate([digits, a1, a2, a3, pad], axis=0).astype(BF16)
        d = _dot_nt(lhs, jnp.where(slots == srow, one, zero))
        idx_ref[r] = (d[0:1] * TOKEN_RADIX + d[1:2]).astype(jnp.int32)
        affsel_ref[r] = d[2:3] + d[3:4] + d[4:5]
        return carry

    lax.fori_loop(0, nb * ne, row_body, 0, unroll=8)


def _route_call(aff, tri, digits, cap):
    B, E, T = aff.shape
    per_slot = pl.BlockSpec((B * E, 1, cap), lambda i: (0, 0, 0))
    return pl.pallas_call(
        functools.partial(_route_kernel, cap=cap),
        grid=(1,),
        in_specs=[pl.BlockSpec((B, E, T), lambda i: (0, 0, 0)),
                  pl.BlockSpec((128, 128), lambda i: (0, 0)),
                  pl.BlockSpec((2, T), lambda i: (0, 0))],
        out_specs=[per_slot, per_slot],
        out_shape=[jax.ShapeDtypeStruct((B * E, 1, cap), jnp.int32),
                   jax.ShapeDtypeStruct((B * E, 1, cap), F32)],
        scratch_shapes=[pltpu.VMEM((B, E, T), jnp.int32)],
        compiler_params=_params(("arbitrary",)),
        name="route",
    )(aff, tri, digits)


def _moe_kernel(idxp_ref, idx0_ref, idx1_ref, affp_ref, aff0_ref, u2_ref, wg_ref, wu_ref, wd_ref, o_ref,
                xa_ref, xb_ref, ya_ref, yb_ref, *acc_refs, cap):
    e = pl.program_id(1)

    def gather(idx_ref, dst_ref):
        for i in range(cap):
            dst_ref[pl.ds(i, 1), :] = u2_ref[0, pl.ds(idx_ref[0, 0, i], 1), :]

    cw = acc_refs[0].shape[1]

    def scatter(idx_ref, w_ref, src_ref):
        for i in range(cap):
            tok, w = idx_ref[0, 0, i], w_ref[0, 0, i]
            for q, acc_ref in enumerate(acc_refs):
                acc_ref[pl.ds(tok, 1), :] += w * src_ref[pl.ds(i, 1), q * cw:(q + 1) * cw]

    def expert(x_ref, y_ref):
        xe = x_ref[...].astype(BF16)
        hg = _dot(xe, wg_ref[0])
        hu = _dot(xe, wu_ref[0])
        h = (hg * jax.nn.sigmoid(hg) * hu).astype(BF16)
        y_ref[...] = _dot(h, wd_ref[0])

    @pl.when(e == 0)
    def _():
        for acc_ref in acc_refs:
            acc_ref[...] = jnp.zeros_like(acc_ref)
        yb_ref[...] = jnp.zeros_like(yb_ref)
        gather(idx0_ref, xa_ref)

    @pl.when(e % 2 == 0)
    def _():
        gather(idx1_ref, xb_ref)
        scatter(idxp_ref, affp_ref, yb_ref)
        expert(xa_ref, ya_ref)

    @pl.when(e % 2 == 1)
    def _():
        gather(idx1_ref, xa_ref)
        scatter(idxp_ref, affp_ref, ya_ref)
        expert(xb_ref, yb_ref)

    @pl.when(e == pl.num_programs(1) - 1)
    def _():
        scatter(idx0_ref, aff0_ref, yb_ref)
        o_ref[0] = jnp.concatenate([acc_ref[...] for acc_ref in acc_refs], axis=1).astype(BF16)


def _moe_call(u2, idx, affsel, wg, wu, wd, cap):
    B, T, D = u2.shape
    E = wg.shape[0]
    F = wg.shape[2]
    assert E % 2 == 0
    smem_row = lambda im: pl.BlockSpec((1, 1, cap), im, memory_space=pltpu.SMEM)
    prev = lambda b, e: (b * E + jnp.maximum(e - 1, 0), 0, 0)
    cur = lambda b, e: (b * E + e, 0, 0)
    nxt = lambda b, e: (b * E + jnp.minimum(e + 1, E - 1), 0, 0)
    rows = pltpu.VMEM((cap, D), F32)
    return pl.pallas_call(
        functools.partial(_moe_kernel, cap=cap),
        grid=(B, E),
        in_specs=[smem_row(prev), smem_row(cur), smem_row(nxt), smem_row(prev), smem_row(cur),
                  pl.BlockSpec((1, T, D), lambda b, e: (b, 0, 0)),
                  pl.BlockSpec((1, D, F), lambda b, e: (e, 0, 0)),
                  pl.BlockSpec((1, D, F), lambda b, e: (e, 0, 0)),
                  pl.BlockSpec((1, F, D), lambda b, e: (e, 0, 0))],
        out_specs=pl.BlockSpec((1, T, D), lambda b, e: (b, 0, 0)),
        out_shape=jax.ShapeDtypeStruct((B, T, D), BF16),
        scratch_shapes=[rows, rows, rows, rows] + [pltpu.VMEM((T, D // MOE_ACC_SPLIT), F32)] * MOE_ACC_SPLIT,
        compiler_params=pltpu.CompilerParams(dimension_semantics=("parallel", "arbitrary"),
                                             vmem_limit_bytes=MOE_VMEM_LIMIT),
        name="moe",
    )(idx, idx, idx, affsel, affsel, u2, wg, wu, wd)


def _final_kernel(x1_ref, mod_ref, moe_ref, o_ref):
    o_ref[0] = x1_ref[0] + mod_ref[0, 5:6, :] * moe_ref[0].astype(F32)


def _final_call(x1, mod3, moe, tm):
    B, T, D = x1.shape
    row = lambda b, i: (b, i, 0)
    return pl.pallas_call(
        _final_kernel,
        grid=(B, T // tm),
        in_specs=[pl.BlockSpec((1, tm, D), row),
                  pl.BlockSpec((1, 6, D), lambda b, i: (b, 0, 0)),
                  pl.BlockSpec((1, tm, D), row)],
        out_specs=pl.BlockSpec((1, tm, D), row),
        out_shape=jax.ShapeDtypeStruct((B, T, D), F32),
        compiler_params=_params(("parallel", "parallel")),
        name="final",
    )(x1, mod3, moe)


def _blockdiag2(w):
    z = jnp.zeros_like(w[0])
    return jnp.concatenate([jnp.concatenate([w[0], z], axis=1), jnp.concatenate([z, w[1]], axis=1)], axis=0)


def _layer(x, mod3, rope_place, rope_pad, cos_q, sin_q, ones, tri, norm1_g, w_in, mu_prev, mu_next, rwkv_w0, rwkv_w2, rwkv_a0,
           rwkv_a2, rwkv_g2, rwkv_k_k, rwkv_k_a, rwkv_r_k, rwkv_gn_w, rwkv_gn_b, q_norm_g, k_norm_g,
           attn_sink, p_rwkv, p_attn, w_out, norm2_g, w_router, w_gate, w_up, w_down):
    B, T, D = x.shape
    tm = TM_ROWS if T % TM_ROWS == 0 else min(T, 256)
    cap = CAP_FACTOR * T // N_EXPERTS
    row = lambda a: a.reshape(1, -1)
    w_b = w_in.astype(BF16)
    qg = jnp.broadcast_to(jnp.tile(q_norm_g, HEADS)[:, None], (RW, 128))
    qt, k, v, gl, r, vv, kkn, g, bonus, lw0, lw1, k0, k1, b0, b1 = _inproj_call(
        x, mod3, row(norm1_g), w_b, w_in[:, Q0:K0].T.astype(BF16), rope_place, rope_pad, cos_q, sin_q,
        qg, row(jnp.tile(k_norm_g, KV_HEADS)), ones,
        row(mu_prev), row(mu_next), row(rwkv_w0), _blockdiag2(rwkv_w2).astype(BF16),
        row(rwkv_a0), _blockdiag2(rwkv_a2).astype(BF16), rwkv_g2.astype(BF16),
        row(rwkv_k_k), row(rwkv_k_a), row(rwkv_r_k), tm)
    yf, yb = _scan_call(r, vv, kkn, lw0, lw1, k0, k1, b0, b1, SCAN_CHUNKS)
    x1, u2, aff = _merge_call(attn_sink, x, mod3, yf, yb, bonus, g, qt, k, v, gl, row(rwkv_gn_w), row(rwkv_gn_b), ones,
                              p_rwkv.astype(BF16), p_attn.astype(BF16), w_out.astype(BF16),
                              row(norm2_g), w_router.T, tm)
    assert T <= 256 * TOKEN_RADIX
    tok = jnp.arange(T)
    digits = jnp.stack([tok // TOKEN_RADIX, tok % TOKEN_RADIX]).astype(F32)
    idx, affsel = _route_call(aff, tri, digits, cap)
    moe = _moe_call(u2, idx, affsel, w_gate.astype(BF16), w_up.astype(BF16), w_down.astype(BF16), cap)
    return _final_call(x1, mod3, moe, TM_FINAL if T % TM_FINAL == 0 else tm)


def kernel(x, c, positions, w_ada, b_ada, norm1_g, w_in, mu_prev, mu_next, rwkv_w0, rwkv_w2, rwkv_a0, rwkv_a2, rwkv_g2, rwkv_k_k, rwkv_k_a, rwkv_r_k, rwkv_gn_w, rwkv_gn_b, q_norm_g, k_norm_g, attn_sink, p_rwkv, p_attn, w_out, norm2_g, w_router, w_gate, w_up, w_down):
    B, T, D = x.shape
    depth = w_ada.shape[0]
    half = ROT // 2
    inv_freq = ROPE_THETA ** (-jnp.arange(0, ROT, 2, dtype=F32) / ROT)
    ang = positions.astype(F32)[..., None] * inv_freq
    cos8, sin8 = jnp.cos(ang), jnp.sin(ang)
    lane = jnp.arange(KV_W) % HD
    freq = jnp.arange(half)[:, None]
    first, second = (lane[None, :] == freq), (lane[None, :] == freq + half)
    rope_place = jnp.stack([(first | second).astype(F32), second.astype(F32) - first.astype(F32)])
    rope_pad = (lane >= ROT).astype(F32).reshape(1, KV_W)
    seg = jnp.arange(RW) // HD
    ones = (seg[:, None] == seg[None, :]).astype(BF16)
    idx = jnp.arange(128)
    tri = (idx[:, None] < idx[None, :]).astype(BF16)
    cos_q, sin_q = jnp.swapaxes(cos8, 1, 2), jnp.swapaxes(sin8, 1, 2)
    for l in range(depth):
        mod3 = _mod_call(c, w_ada[l], b_ada[l]).reshape(B, 6, D)
        x = _layer(x, mod3, rope_place, rope_pad, cos_q, sin_q, ones, tri, norm1_g[l], w_in[l], mu_prev[l], mu_next[l],
                   rwkv_w0[l], rwkv_w2[l], rwkv_a0[l], rwkv_a2[l], rwkv_g2[l], rwkv_k_k[l], rwkv_k_a[l],
                   rwkv_r_k[l], rwkv_gn_w[l], rwkv_gn_b[l], q_norm_g[l], k_norm_g[l], attn_sink[l],
                   p_rwkv[l], p_attn[l], w_out[l], norm2_g[l], w_router[l], w_gate[l], w_up[l], w_down[l])
    return x
```

```python
import functools
import math

import jax
import jax.numpy as jnp
from jax import lax
from jax.experimental import pallas as pl
from jax.experimental.pallas import tpu as pltpu

F32 = jnp.float32
BF16 = jnp.bfloat16

D_MODEL = 1024
RW = 512
HEADS = 8
HD = 64
LORA_W = 64
LORA_A = 64
LORA_G = 128
GN_EPS = HD * 1e-5
KV_HEADS = 2
GROUP = HEADS // KV_HEADS
KV_W = KV_HEADS * HD
BLOCK = 128
ROPE_THETA = 500000.0
ROT = HD // 4
N_EXPERTS = 16
CAP_FACTOR = 2
NORM_EPS = 1e-6
RWKV_COLS = 3 * RW + 2 * LORA_W + 2 * LORA_A + LORA_G
Q0 = RWKV_COLS
K0 = Q0 + RW
V0 = K0 + KV_W
G0 = V0 + KV_W
IN_COLS = G0 + 2 * D_MODEL
CHUNK = 64
SUB = 16
SCAN_CHUNKS = 4
TM_FINAL = 2048
TM_ROWS = 512
TOKEN_RADIX = 64
BISECT_ITERS = 160
NEG = -1e30
LOG2E = math.log2(math.e)
QK_SCALE = HD ** -0.5 * LOG2E
VMEM_LIMIT = 56 * 1024 * 1024
MOE_VMEM_LIMIT = 60 * 1024 * 1024
MOE_ACC_SPLIT = 4


def _dot(a, b):
    return jnp.dot(a, b, preferred_element_type=F32)


def _dot_nt(a, b):
    return lax.dot_general(a, b, (((1,), (1,)), ((), ())), preferred_element_type=F32)


def _dot_tn(a, b):
    return lax.dot_general(a, b, (((0,), (0,)), ((), ())), preferred_element_type=F32)


def _segsum(x, ones):
    xh = x.astype(BF16)
    xl = (x - xh.astype(F32)).astype(BF16)
    return _dot(xh, ones) + _dot(xl, ones)


def _segsum_pos(x, ones):
    return _dot(x.astype(BF16), ones)


def _params(sem):
    return pltpu.CompilerParams(dimension_semantics=sem, vmem_limit_bytes=VMEM_LIMIT)


def _mod_kernel(c_ref, w_ref, b_ref, o_ref):
    c = c_ref[...]
    ca = c * jax.nn.sigmoid(c)
    o_ref[...] = jnp.dot(ca, w_ref[...], preferred_element_type=F32,
                         precision=lax.Precision.HIGHEST) + b_ref[...]


def _mod_call(c, w_ada, b_ada):
    B, D = c.shape
    n = w_ada.shape[1] // D
    return pl.pallas_call(
        _mod_kernel,
        grid=(n,),
        in_specs=[pl.BlockSpec((B, D), lambda j: (0, 0)),
                  pl.BlockSpec((D, D), lambda j: (0, j)),
                  pl.BlockSpec((1, D), lambda j: (0, j))],
        out_specs=pl.BlockSpec((B, D), lambda j: (0, j)),
        out_shape=jax.ShapeDtypeStruct((B, n * D), F32),
        compiler_params=_params(("arbitrary",)),
        name="mod",
    )(c, w_ada, b_ada.reshape(1, -1))


def _rope(xn, cos, sin, width):
    lane = lax.broadcasted_iota(jnp.int32, xn.shape, 1) & (HD - 1)
    rot = jnp.where(lane < ROT // 2, pltpu.roll(xn, width - ROT // 2, 1), pltpu.roll(xn, ROT // 2, 1))
    return xn * cos + rot * sin


HALO = 8
INPROJ_SUBTILES = 1


def _inproj_kernel(x_ref, xp_ref, xn_ref, mod_ref, g_ref, w_ref, wq_ref, place_ref, pad_ref, cosq_ref, sinq_ref,
                   qg_ref, kg_ref, ones8_ref,
                   mup_ref, mun_ref, w0_ref, w2_ref, a0_ref, a2_ref, g2_ref, kk_ref, ka_ref, rk_ref,
                   qt_ref, k_ref, v_ref, gl_ref,
                   r_o, v_o, kkn_o, g_o, bonus_o, lw0_o, lw1_o, k0_o, k1_o, b0_o, b1_o):
    tm = x_ref.shape[1]
    sub = tm // INPROJ_SUBTILES
    xx = jnp.concatenate([xp_ref[0], x_ref[0], xn_ref[0]], axis=0)
    for lo in range(0, tm, sub):
        _inproj_rows(lo, sub, xx[lo:lo + sub + 2 * HALO], tm, mod_ref, g_ref, w_ref, wq_ref, place_ref, pad_ref,
                     cosq_ref, sinq_ref, qg_ref, kg_ref, ones8_ref, mup_ref, mun_ref, w0_ref, w2_ref, a0_ref,
                     a2_ref, g2_ref, kk_ref, ka_ref, rk_ref, qt_ref, k_ref, v_ref, gl_ref,
                     (r_o, v_o, kkn_o, g_o, bonus_o, lw0_o, lw1_o, k0_o, k1_o, b0_o, b1_o))


def _inproj_rows(lo, n, xx, tm, mod_ref, g_ref, w_ref, wq_ref, place_ref, pad_ref, cosq_ref, sinq_ref,
                 qg_ref, kg_ref, ones8_ref, mup_ref, mun_ref, w0_ref, w2_ref, a0_ref, a2_ref, g2_ref,
                 kk_ref, ka_ref, rk_ref, qt_ref, k_ref, v_ref, gl_ref, rwkv_outs):
    r_o, v_o, kkn_o, g_o, bonus_o, lw0_o, lw1_o, k0_o, k1_o, b0_o, b1_o = rwkv_outs
    i = pl.program_id(1)
    last = pl.num_programs(1) - 1
    rows = slice(lo, lo + n)
    ms = jnp.mean(xx * xx, axis=-1, keepdims=True)
    y = xx * lax.rsqrt(ms + NORM_EPS) * g_ref[...]
    u_all = y * (1.0 + mod_ref[0, 1:2, :]) + mod_ref[0, 0:1, :]
    ub = u_all[HALO:HALO + n].astype(BF16)
    z_all = _dot(u_all.astype(BF16), w_ref[:, 0:RWKV_COLS])
    ones8 = ones8_ref[...]

    k = _dot(ub, w_ref[:, K0:V0])
    v_ref[0, rows, :] = _dot(ub, w_ref[:, V0:G0]).astype(BF16)
    gl_ref[0, rows, :] = _dot(ub, w_ref[:, G0:IN_COLS]).astype(BF16)
    kn = k * lax.rsqrt(_segsum_pos(k * k, ones8[:KV_W, :KV_W]) * (1.0 / HD) + NORM_EPS) * kg_ref[...]
    def place(t8, e):
        hi = t8.astype(BF16).astype(F32)
        return _dot_tn(hi, e) + _dot_tn(t8 - hi, e)

    cos_k = place(cosq_ref[0, :, rows], place_ref[0]) + pad_ref[...]
    sin_k = place(sinq_ref[0, :, rows], place_ref[1])
    k_ref[0, rows, :] = _rope(kn, cos_k, sin_k, KV_W).astype(BF16)

    z = z_all[HALO:HALO + n]
    row = lax.broadcasted_iota(jnp.int32, z.shape, 0) + lo
    zp = jnp.where((row == 0) & (i == 0), 0.0, z_all[HALO - 1:HALO - 1 + n])
    zn = jnp.where((row == tm - 1) & (i == last), 0.0, z_all[HALO + 1:HALO + 1 + n])
    zs = z + mup_ref[...] * (zp - z) + mun_ref[...] * (zn - z)
    r = zs[:, 0:RW]
    kr = zs[:, RW:2 * RW]
    vr = zs[:, 2 * RW:3 * RW]
    c0 = 3 * RW
    wd = zs[:, c0:c0 + 2 * LORA_W]
    ad = zs[:, c0 + 2 * LORA_W:c0 + 2 * LORA_W + 2 * LORA_A]
    gd = zs[:, c0 + 2 * LORA_W + 2 * LORA_A:RWKV_COLS]
    wl = _dot(jnp.tanh(wd).astype(BF16), w2_ref[...]) + w0_ref[...]
    lw = -math.exp(-0.5) * jax.nn.sigmoid(wl)
    al = jax.nn.sigmoid(_dot(ad.astype(BF16), a2_ref[...]) + a0_ref[...])
    g = _dot(jax.nn.sigmoid(gd).astype(BF16), g2_ref[...])
    kk = kr * kk_ref[...]
    kkn = kk / jnp.maximum(jnp.sqrt(_segsum_pos(kk * kk, ones8)), 1e-12)
    ka = ka_ref[...]
    a_0 = al[:, 0:RW]
    a_1 = al[:, RW:2 * RW]
    k_0 = kr * (1.0 + (a_0 - 1.0) * ka)
    k_1 = kr * (1.0 + (a_1 - 1.0) * ka)
    bonus = _segsum(r * (k_0 + k_1) * rk_ref[...], ones8) * vr
    r_o[0, rows, :] = r.astype(BF16)
    v_o[0, rows, :] = vr.astype(BF16)
    kkn_o[0, rows, :] = kkn.astype(BF16)
    g_o[0, rows, :] = g.astype(BF16)
    bonus_o[0, rows, :] = bonus.astype(BF16)
    lw0_o[0, rows, :] = lw[:, 0:RW]
    lw1_o[0, rows, :] = lw[:, RW:2 * RW]
    k0_o[0, rows, :] = k_0.astype(BF16)
    k1_o[0, rows, :] = k_1.astype(BF16)
    b0_o[0, rows, :] = (kkn * a_0).astype(BF16)
    b1_o[0, rows, :] = (kkn * a_1).astype(BF16)

    qt = _dot_nt(wq_ref[...], ub)
    qg = jnp.concatenate([qg_ref[...]] * (n // 128), axis=1)
    cq = cosq_ref[0, :, rows]
    sq = sinq_ref[0, :, rows]
    half = ROT // 2
    pieces = []
    for h in range(HEADS):
        xh = qt[h * HD:(h + 1) * HD]
        xn = (xh * lax.rsqrt(jnp.mean(xh * xh, axis=0, keepdims=True) + NORM_EPS)
              * qg[h * HD:(h + 1) * HD] * QK_SCALE)
        x1, x2 = xn[0:half], xn[half:ROT]
        pieces += [x1 * cq - x2 * sq, x2 * cq + x1 * sq, xn[ROT:]]
    qt_ref[0, :, rows] = jnp.concatenate(pieces, axis=0).astype(BF16)


def _inproj_call(x, mod3, norm1_g, w_rest, wq_t, rope_place, rope_pad, cos_q, sin_q, qg, kg, ones,
                 mup, mun, w0c, w2c, a0c, a2c, g2b, k_k, k_a, r_k, tm):
    B, T, D = x.shape
    nh = tm // HALO
    nth = T // HALO
    row = lambda b, i: (b, i, 0)
    col = lambda b, i: (b, 0, i)
    const2 = lambda b, i: (0, 0)
    half = ROT // 2
    rw = pl.BlockSpec((1, tm, RW), row)
    vec = lambda n: pl.BlockSpec((1, n), const2)
    return pl.pallas_call(
        _inproj_kernel,
        grid=(B, T // tm),
        in_specs=[pl.BlockSpec((1, tm, D), row),
                  pl.BlockSpec((1, HALO, D), lambda b, i: (b, jnp.maximum(i * nh - 1, 0), 0)),
                  pl.BlockSpec((1, HALO, D), lambda b, i: (b, jnp.minimum((i + 1) * nh, nth - 1), 0)),
                  pl.BlockSpec((1, 6, D), lambda b, i: (b, 0, 0)),
                  vec(D),
                  pl.BlockSpec(w_rest.shape, const2),
                  pl.BlockSpec((RW, D), const2),
                  pl.BlockSpec((2, half, KV_W), lambda b, i: (0, 0, 0)),
                  vec(KV_W),
                  pl.BlockSpec((1, half, tm), col),
                  pl.BlockSpec((1, half, tm), col),
                  pl.BlockSpec((RW, 128), const2),
                  vec(KV_W),
                  pl.BlockSpec((RW, RW), const2),
                  vec(RWKV_COLS), vec(RWKV_COLS),
                  vec(2 * RW), pl.BlockSpec((2 * LORA_W, 2 * RW), const2),
                  vec(2 * RW), pl.BlockSpec((2 * LORA_A, 2 * RW), const2),
                  pl.BlockSpec((LORA_G, RW), const2),
                  vec(RW), vec(RW), vec(RW)],
        out_specs=[pl.BlockSpec((1, RW, tm), col),
                   pl.BlockSpec((1, tm, KV_W), row),
                   pl.BlockSpec((1, tm, KV_W), row),
                   pl.BlockSpec((1, tm, 2 * D), row)] + [rw] * 11,
        out_shape=[jax.ShapeDtypeStruct((B, RW, T), BF16),
                   jax.ShapeDtypeStruct((B, T, KV_W), BF16),
                   jax.ShapeDtypeStruct((B, T, KV_W), BF16),
                   jax.ShapeDtypeStruct((B, T, 2 * D), BF16)]
        + [jax.ShapeDtypeStruct((B, T, RW), F32 if i in (5, 6) else BF16) for i in range(11)],
        compiler_params=_params(("parallel", "parallel")),
        name="inproj",
    )(x, x, x, mod3, norm1_g, w_rest, wq_t, rope_place, rope_pad, cos_q, sin_q, qg, kg, ones,
      mup, mun, w0c, w2c, a0c, a2c, g2b, k_k, k_a, r_k)


def _dir_operands(rev, nch, r, v, kk, lw, kd, bd):
    n = r.shape[0]
    ti = lax.broadcasted_iota(jnp.int32, (n, n), 0)
    si = lax.broadcasted_iota(jnp.int32, (n, n), 1)
    same = (ti // CHUNK) == (si // CHUNK)
    incl = ((si >= ti) if rev else (si <= ti)) & same
    mi = incl.astype(BF16)
    l1 = lw.astype(BF16)
    l2 = (lw - l1.astype(F32)).astype(BF16)
    cum = _dot(mi, l1) + _dot(mi, l2)
    wt = jnp.exp(cum)
    winv = jnp.exp(-cum)
    ops = dict(ah=-(kk * jnp.exp(cum - lw)), rt=r * wt, bt=bd * winv, kt=kd * winv, v=v)
    wcs, bhs, khs = [], [], []
    for j in range(nch):
        last = j * CHUNK if rev else (j + 1) * CHUNK - 1
        wc = wt[last:last + 1, :]
        rows = slice(j * CHUNK, (j + 1) * CHUNK)
        wcs.append(wc)
        bhs.append(ops["bt"][rows] * wc)
        khs.append(ops["kt"][rows] * wc)
    return ops, wcs, bhs, khs


def _scan_kernel(rf, vf, kkf, lwf, kf, bf, rb, vb, kkb, lwb, kb, bb, yf_ref, yb_ref, st_ref, *, nch):
    @pl.when(pl.program_id(1) == 0)
    def _():
        st_ref[...] = jnp.zeros_like(st_ref)

    c = CHUNK
    pw = 2 * HD
    npair = HEADS // 2
    ti = lax.broadcasted_iota(jnp.int32, (c, pw), 0)
    li = lax.broadcasted_iota(jnp.int32, (c, pw), 1)
    si = li & (HD - 1)
    head0 = li < HD
    blk = (ti // SUB) == (si // SUB)
    eye = ti == si
    two = lambda m: jnp.concatenate([m, m], axis=1)
    masks = {False: (two(si < ti), two(si <= ti)), True: (two(si > ti), two(si >= ti))}

    def bd(y):
        yb = y.astype(BF16)
        zero = jnp.zeros_like(yb)
        return jnp.concatenate([jnp.where(head0, yb, zero), jnp.where(head0, zero, yb)], axis=0)

    def bdcat(*ys):
        return jnp.concatenate([bd(y) for y in ys], axis=1)

    def diag_blocks(full):
        return jnp.where(head0, full[:c], full[c:])

    dirs = [(False, rf, vf, kkf, lwf, kf, bf), (True, rb, vb, kkb, lwb, kb, bb)]
    units = []
    for d, (rev, r_, v_, kk_, lw_, k_, b_) in enumerate(dirs):
        f32 = lambda ref: ref[0].astype(F32)
        ops, wcs, bhs, khs = _dir_operands(rev, nch, f32(r_), f32(v_), f32(kk_), lw_[0], f32(k_), f32(b_))
        for j in range(nch):
            rows = slice(j * c, (j + 1) * c)
            for p in range(npair):
                sl = slice(p * pw, (p + 1) * pw)
                u = dict(d=d, j=j, p=p, rev=rev, wc=wcs[j][:, sl],
                         bkh=jnp.concatenate([bhs[j][:, sl], khs[j][:, sl]], axis=0).astype(BF16))
                for name in ("ah", "rt", "bt", "kt", "v"):
                    u[name] = ops[name][rows, sl]
                units.append(u)

    for u in units:
        x1 = jnp.concatenate([u["ah"], u["rt"]], axis=0).astype(BF16)
        u["lbk"] = _dot_nt(x1, jnp.concatenate([bd(u["bt"]), bd(u["kt"])], axis=0))
    for u in units:
        strict, incl = masks[u["rev"]]
        top = jnp.where(strict, u["lbk"][:c], 0.0)
        bot = jnp.where(incl, u["lbk"][c:], 0.0)
        lab = top[:, :pw]
        u["mrb"] = bot[:, :pw].astype(BF16)
        u["ld"] = jnp.where(blk, lab, 0.0)
        u["lo"] = lab - u["ld"]
        u["lm"] = jnp.concatenate([top[:, pw:], bot[:, pw:]], axis=0).astype(BF16)
    for u in units:
        u["a2"] = _dot(u["ld"].astype(BF16), bd(u["ld"]))
        u["lkv"] = _dot(u["lm"], bd(u["v"]))
    for u in units:
        s = _dot(jnp.concatenate([u["a2"], u["ld"]], axis=0).astype(BF16), bd(u["a2"]))
        u["a4"] = s[:c]
        u["t"] = jnp.where(eye, 1.0, 0.0) + u["ld"] + u["a2"] + s[c:]
    for u in units:
        s = _dot(jnp.concatenate([u["a4"], u["t"]], axis=0).astype(BF16), bd(u["a4"]))
        u["a8"] = s[:c]
        u["t"] = u["t"] + s[c:]
    for u in units:
        u["td"] = u["t"] + _dot(u["t"].astype(BF16), bd(u["a8"]))
    for u in units:
        x = _dot(u["td"].astype(BF16), bdcat(u["lo"], u["ah"], u["lkv"][:c]))
        u["n"] = x[:, :pw]
        u["z"] = x[:, pw:]
    for u in units:
        s = _dot(u["n"].astype(BF16), bdcat(u["n"], u["z"][:, :pw], u["z"][:, pw:]))
        u["n2"] = s[:, :pw]
        u["z"] = u["z"] + s[:, pw:]
    for u in units:
        u["z"] = u["z"] + _dot(u["n2"].astype(BF16), bdcat(u["z"][:, :pw], u["z"][:, pw:]))
    for u in units:
        z = u["z"]
        y10 = (jnp.concatenate([u["rt"], u["lkv"][c:]], axis=1)
               + _dot(u["mrb"], bdcat(z[:, :pw], z[:, pw:])))
        vpad = jnp.concatenate([jnp.zeros_like(u["v"]), u["v"]], axis=1)
        full = _dot_tn(u["bkh"], jnp.concatenate([z, vpad], axis=0).astype(BF16))
        g = diag_blocks(full[:, :pw]) + jnp.where(eye, u["wc"], 0.0)
        u["gy"] = jnp.concatenate([g, y10[:, :pw]], axis=0).astype(BF16)
        u["h0"] = diag_blocks(full[:, pw:])
        u["y0"] = y10[:, pw:]

    ys = {}
    for d in range(2):
        order = range(nch - 1, -1, -1) if d == 1 else range(nch)
        zs = [st_ref[d, p] for p in range(npair)]
        for j in order:
            for p in range(npair):
                u = units[(d * nch + j) * npair + p]
                m = _dot(u["gy"], bd(zs[p]))
                zs[p] = m[:c] + u["h0"]
                ys[(d, j, p)] = m[c:] + u["y0"]
        for p in range(npair):
            st_ref[d, p] = zs[p]
    for d, y_ref in enumerate((yf_ref, yb_ref)):
        y_ref[0] = jnp.concatenate(
            [jnp.concatenate([ys[(d, j, p)] for p in range(npair)], axis=-1) for j in range(nch)],
            axis=0).astype(BF16)


def _scan_call(r, v, kkn, lw0, lw1, k0, k1, b0, b1, nch):
    B, T, _ = r.shape
    rows = nch * CHUNK
    ns = T // rows
    fwd = pl.BlockSpec((1, rows, RW), lambda b, c: (b, c, 0))
    bwd = pl.BlockSpec((1, rows, RW), lambda b, c: (b, ns - 1 - c, 0))
    return pl.pallas_call(
        functools.partial(_scan_kernel, nch=nch),
        grid=(B, ns),
        in_specs=[fwd] * 6 + [bwd] * 6,
        out_specs=[fwd, bwd],
        out_shape=[jax.ShapeDtypeStruct((B, T, RW), BF16)] * 2,
        scratch_shapes=[pltpu.VMEM((2, HEADS // 2, HD, 2 * HD), F32)],
        compiler_params=_params(("parallel", "arbitrary")),
        name="scan",
    )(r, v, kkn, lw0, k0, b0, r, v, kkn, lw1, k1, b1)


def _attention_scores(sink_ref, qt_ref, k_ref, v_ref, n, qb):
    t = k_ref.shape[1]
    band = 3 * BLOCK
    cols = GROUP * BLOCK
    ki = lax.broadcasted_iota(jnp.int32, (band, BLOCK), 0)
    qi = lax.broadcasted_iota(jnp.int32, (band, BLOCK), 1)
    lane_head = lax.broadcasted_iota(jnp.int32, (1, cols), 1) // BLOCK
    zeros = jnp.zeros((HD, cols), BF16)
    units = []
    for j in range(qb):
        blk = n * qb + j
        start = pl.multiple_of(jnp.clip((blk - 1) * BLOCK, 0, t - band), BLOCK)
        bias = jnp.where(jnp.abs(start + ki - (blk * BLOCK + qi)) <= BLOCK, 0.0, NEG)
        bias = jnp.concatenate([bias] * GROUP, axis=1)
        kb = k_ref[0, pl.ds(start, band), :]
        vb = v_ref[0, pl.ds(start, band), :]
        for g in range(KV_HEADS):
            qg = jnp.concatenate([qt_ref[0, (g * GROUP + i) * HD:(g * GROUP + i + 1) * HD,
                                         j * BLOCK:(j + 1) * BLOCK] for i in range(GROUP)], axis=1)
            rhs = jnp.concatenate([qg if gg == g else zeros for gg in range(KV_HEADS)], axis=0)
            sk = jnp.zeros((1, cols), F32)
            for i in range(GROUP):
                sk = jnp.where(lane_head == i, sink_ref[g * GROUP + i] * LOG2E, sk)
            units.append(dict(j=j, g=g, kb=kb, vb=vb, rhs=rhs, sk=sk, bias=bias))
    for u in units:
        u["s"] = _dot(u["kb"], u["rhs"]) + u["bias"]
    return units


def _attention_finish(units, qb):
    for u in units:
        u["m"] = jnp.maximum(jnp.max(u["s"], axis=0, keepdims=True), u["sk"])
    for u in units:
        p = jnp.exp2(u["s"] - u["m"])
        u["r"] = 1.0 / (jnp.sum(p, axis=0, keepdims=True) + jnp.exp2(u["sk"] - u["m"]))
        u["p"] = p.astype(BF16)
    for u in units:
        g = u["g"]
        u["o"] = _dot_tn(u["vb"], u["p"])[g * HD:(g + 1) * HD] * u["r"]
    out = {(u["j"], u["g"]): u["o"] for u in units}
    return jnp.concatenate(
        [jnp.concatenate([out[(j, h // GROUP)][:, (h % GROUP) * BLOCK:(h % GROUP + 1) * BLOCK]
                          for j in range(qb)], axis=1) for h in range(HEADS)], axis=0)


def _merge_kernel(sink_ref, x_ref, mod_ref, yf_ref, yb_ref, bonus_ref, g_ref, qt_ref, k_ref, v_ref, gl_ref,
                  gnw_ref, gnb_ref, ones_ref, pr_ref, pa_ref, wo_ref, n2g_ref, wr_ref,
                  x1_ref, u2_ref, aff_ref):
    qb = x_ref.shape[1] // BLOCK
    att = _attention_scores(sink_ref, qt_ref, k_ref, v_ref, pl.program_id(1), qb)
    ones = ones_ref[...]
    y = yf_ref[0].astype(F32) + yb_ref[0].astype(F32)
    mu = _segsum(y, ones) * (1.0 / HD)
    yc = y - mu
    var = _segsum_pos(yc * yc, ones) * (1.0 / HD)
    yn = yc * lax.rsqrt(var + GN_EPS) * gnw_ref[...] + gnb_ref[...]
    ya = ((yn + bonus_ref[0].astype(F32)) * g_ref[0].astype(F32)).astype(BF16)
    pa = _dot(ya, pr_ref[...])
    gates = jax.nn.sigmoid(gl_ref[0].astype(F32))
    yatt_t = _attention_finish(att, qb).astype(BF16)
    pb = _dot_tn(yatt_t, pa_ref[...])
    m = gates[:, :D_MODEL] * pa + gates[:, D_MODEL:] * pb
    x1 = x_ref[0] + mod_ref[0, 2:3, :] * _dot(m.astype(BF16), wo_ref[...])
    x1_ref[0] = x1
    ms = jnp.mean(x1 * x1, axis=-1, keepdims=True)
    u2 = x1 * lax.rsqrt(ms + NORM_EPS) * n2g_ref[...] * (1.0 + mod_ref[0, 4:5, :]) + mod_ref[0, 3:4, :]
    u2_ref[0] = u2
    u2h = u2.astype(BF16)
    u2l = (u2 - u2h.astype(F32)).astype(BF16)
    wr = wr_ref[...]
    wrh = wr.astype(BF16)
    wrl = (wr - wrh.astype(F32)).astype(BF16)
    t1 = _dot_nt(jnp.concatenate([wrh, wrl], axis=0), u2h)
    logits = t1[:N_EXPERTS] + t1[N_EXPERTS:] + _dot_nt(wrh, u2l)
    e = jnp.exp(logits - jnp.max(logits, axis=0, keepdims=True))
    aff_ref[0] = e / jnp.sum(e, axis=0, keepdims=True)


def _merge_call(sink, x, mod3, yf, yb, bonus, g, qt, k, v, gl, gnw, gnb, ones, p_r, p_a, w_o, n2g, w_rt, tm):
    B, T, D = x.shape
    assert T >= 3 * BLOCK and tm % BLOCK == 0
    row = lambda b, i: (b, i, 0)
    const2 = lambda b, i: (0, 0)
    rw = pl.BlockSpec((1, tm, RW), row)
    kv = pl.BlockSpec((1, T, KV_W), lambda b, i: (b, 0, 0))
    return pl.pallas_call(
        _merge_kernel,
        grid=(B, T // tm),
        in_specs=[pl.BlockSpec(memory_space=pltpu.SMEM),
                  pl.BlockSpec((1, tm, D), row),
                  pl.BlockSpec((1, 6, D), lambda b, i: (b, 0, 0)),
                  rw, rw, rw, rw,
                  pl.BlockSpec((1, RW, tm), lambda b, i: (b, 0, i)), kv, kv,
                  pl.BlockSpec((1, tm, 2 * D), row),
                  pl.BlockSpec((1, RW), const2),
                  pl.BlockSpec((1, RW), const2),
                  pl.BlockSpec((RW, RW), const2),
                  pl.BlockSpec((RW, D), const2),
                  pl.BlockSpec((RW, D), const2),
                  pl.BlockSpec((D, D), const2),
                  pl.BlockSpec((1, D), const2),
                  pl.BlockSpec((N_EXPERTS, D), const2)],
        out_specs=[pl.BlockSpec((1, tm, D), row),
                   pl.BlockSpec((1, tm, D), row),
                   pl.BlockSpec((1, N_EXPERTS, tm), lambda b, i: (b, 0, i))],
        out_shape=[jax.ShapeDtypeStruct((B, T, D), F32),
                   jax.ShapeDtypeStruct((B, T, D), F32),
                   jax.ShapeDtypeStruct((B, N_EXPERTS, T), F32)],
        compiler_params=_params(("parallel", "parallel")),
        name="merge",
    )(sink, x, mod3, yf, yb, bonus, g, qt, k, v, gl, gnw, gnb, ones, p_r, p_a, w_o, n2g, w_rt)


def _excl_prefix(mask, tri):
    e, t = mask.shape
    mb = mask.astype(BF16)
    carry = jnp.zeros((e, 1), F32)
    outs = []
    for j in range(t // 128):
        tile = mb[:, j * 128:(j + 1) * 128]
        outs.append(_dot(tile, tri) + carry)
        carry = carry + jnp.sum(tile.astype(F32), axis=1, keepdims=True)
    return jnp.concatenate(outs, axis=-1)


def _route_kernel(aff_ref, tri_ref, dig_ref, idx_ref, affsel_ref, slot_ref, *, cap):
    nb, ne, t = aff_ref.shape
    a = aff_ref[...].reshape(nb * ne, t)
    e = a.shape[0]
    capf = jnp.float32(cap)

    def cond(s):
        it, lo, hi = s
        mid = 0.5 * (lo + hi)
        still_open = jnp.max(jnp.where((mid > lo) & (mid < hi), 1.0, 0.0))
        return (it < BISECT_ITERS) & (still_open > 0.0)

    def body(s):
        it, lo, hi = s
        mid = 0.5 * (lo + hi)
        ge = jnp.sum((a >= mid).astype(F32), axis=1, keepdims=True) >= capf
        return it + 1, jnp.where(ge, mid, lo), jnp.where(ge, hi, mid)

    lo0 = jnp.zeros((e, 1), F32)
    hi0 = jnp.full((e, 1), 2.0, F32)
    _, lo, hi = lax.while_loop(cond, body, (jnp.int32(0), lo0, hi0))
    gt = a >= hi
    eq = (a >= lo) & (a < hi)
    need = capf - jnp.sum(gt.astype(F32), axis=1, keepdims=True)
    tri = tri_ref[...]
    sel = gt | (eq & (_excl_prefix(eq, tri) < need))
    pos = _excl_prefix(sel, tri)
    slot_ref[...] = jnp.where(sel, pos.astype(jnp.int32), -1).reshape(nb, ne, t)

    digits = dig_ref[...]
    assert cap <= 256
    slots = lax.broadcasted_iota(jnp.int32, (cap, t), 0).astype(BF16)
    one, zero = jnp.ones((cap, t), BF16), jnp.zeros((cap, t), BF16)
    pad = jnp.zeros((3, t), F32)

    def row_body(r, carry):
        srow = slot_ref[r // ne, pl.ds(r % ne, 1), :].astype(BF16)
        arow = aff_ref[r // ne, pl.ds(r % ne, 1), :]
        a1 = arow.astype(BF16).astype(F32)
        a2 = (arow - a1).astype(BF16).astype(F32)
        a3 = arow - a1 - a2
        lhs = jnp.concatenate([digits, a1, a2, a3, pad], axis=0).astype(BF16)
        d = _dot_nt(lhs, jnp.where(slots == srow, one, zero))
        idx_ref[r] = (d[0:1] * TOKEN_RADIX + d[1:2]).astype(jnp.int32)
        affsel_ref[r] = d[2:3] + d[3:4] + d[4:5]
        return carry

    lax.fori_loop(0, nb * ne, row_body, 0, unroll=8)


def _route_call(aff, tri, digits, cap):
    B, E, T = aff.shape
    per_slot = pl.BlockSpec((B * E, 1, cap), lambda i: (0, 0, 0))
    return pl.pallas_call(
        functools.partial(_route_kernel, cap=cap),
        grid=(1,),
        in_specs=[pl.BlockSpec((B, E, T), lambda i: (0, 0, 0)),
                  pl.BlockSpec((128, 128), lambda i: (0, 0)),
                  pl.BlockSpec((2, T), lambda i: (0, 0))],
        out_specs=[per_slot, per_slot],
        out_shape=[jax.ShapeDtypeStruct((B * E, 1, cap), jnp.int32),
                   jax.ShapeDtypeStruct((B * E, 1, cap), F32)],
        scratch_shapes=[pltpu.VMEM((B, E, T), jnp.int32)],
        compiler_params=_params(("arbitrary",)),
        name="route",
    )(aff, tri, digits)


def _moe_kernel(idxp_ref, idx0_ref, idx1_ref, affp_ref, aff0_ref, u2_ref, wg_ref, wu_ref, wd_ref, o_ref,
                xa_ref, xb_ref, ya_ref, yb_ref, *acc_refs, cap):
    e = pl.program_id(1)

    def gather(idx_ref, dst_ref):
        for i in range(cap):
            dst_ref[pl.ds(i, 1), :] = u2_ref[0, pl.ds(idx_ref[0, 0, i], 1), :]

    cw = acc_refs[0].shape[1]

    def scatter(idx_ref, w_ref, src_ref):
        for i in range(cap):
            tok, w = idx_ref[0, 0, i], w_ref[0, 0, i]
            for q, acc_ref in enumerate(acc_refs):
                acc_ref[pl.ds(tok, 1), :] += w * src_ref[pl.ds(i, 1), q * cw:(q + 1) * cw]

    def expert(x_ref, y_ref):
        xe = x_ref[...].astype(BF16)
        hg = _dot(xe, wg_ref[0])
        hu = _dot(xe, wu_ref[0])
        h = (hg * jax.nn.sigmoid(hg) * hu).astype(BF16)
        y_ref[...] = _dot(h, wd_ref[0])

    @pl.when(e == 0)
    def _():
        for acc_ref in acc_refs:
            acc_ref[...] = jnp.zeros_like(acc_ref)
        yb_ref[...] = jnp.zeros_like(yb_ref)
        gather(idx0_ref, xa_ref)

    @pl.when(e % 2 == 0)
    def _():
        gather(idx1_ref, xb_ref)
        scatter(idxp_ref, affp_ref, yb_ref)
        expert(xa_ref, ya_ref)

    @pl.when(e % 2 == 1)
    def _():
        gather(idx1_ref, xa_ref)
        scatter(idxp_ref, affp_ref, ya_ref)
        expert(xb_ref, yb_ref)

    @pl.when(e == pl.num_programs(1) - 1)
    def _():
        scatter(idx0_ref, aff0_ref, yb_ref)
        o_ref[0] = jnp.concatenate([acc_ref[...] for acc_ref in acc_refs], axis=1).astype(BF16)


def _moe_call(u2, idx, affsel, wg, wu, wd, cap):
    B, T, D = u2.shape
    E = wg.shape[0]
    F = wg.shape[2]
    assert E % 2 == 0
    smem_row = lambda im: pl.BlockSpec((1, 1, cap), im, memory_space=pltpu.SMEM)
    prev = lambda b, e: (b * E + jnp.maximum(e - 1, 0), 0, 0)
    cur = lambda b, e: (b * E + e, 0, 0)
    nxt = lambda b, e: (b * E + jnp.minimum(e + 1, E - 1), 0, 0)
    rows = pltpu.VMEM((cap, D), F32)
    return pl.pallas_call(
        functools.partial(_moe_kernel, cap=cap),
        grid=(B, E),
        in_specs=[smem_row(prev), smem_row(cur), smem_row(nxt), smem_row(prev), smem_row(cur),
                  pl.BlockSpec((1, T, D), lambda b, e: (b, 0, 0)),
                  pl.BlockSpec((1, D, F), lambda b, e: (e, 0, 0)),
                  pl.BlockSpec((1, D, F), lambda b, e: (e, 0, 0)),
                  pl.BlockSpec((1, F, D), lambda b, e: (e, 0, 0))],
        out_specs=pl.BlockSpec((1, T, D), lambda b, e: (b, 0, 0)),
        out_shape=jax.ShapeDtypeStruct((B, T, D), BF16),
        scratch_shapes=[rows, rows, rows, rows] + [pltpu.VMEM((T, D // MOE_ACC_SPLIT), F32)] * MOE_ACC_SPLIT,
        compiler_params=pltpu.CompilerParams(dimension_semantics=("parallel", "arbitrary"),
                                             vmem_limit_bytes=MOE_VMEM_LIMIT),
        name="moe",
    )(idx, idx, idx, affsel, affsel, u2, wg, wu, wd)


def _final_kernel(x1_ref, mod_ref, moe_ref, o_ref):
    o_ref[0] = x1_ref[0] + mod_ref[0, 5:6, :] * moe_ref[0].astype(F32)


def _final_call(x1, mod3, moe, tm):
    B, T, D = x1.shape
    row = lambda b, i: (b, i, 0)
    return pl.pallas_call(
        _final_kernel,
        grid=(B, T // tm),
        in_specs=[pl.BlockSpec((1, tm, D), row),
                  pl.BlockSpec((1, 6, D), lambda b, i: (b, 0, 0)),
                  pl.BlockSpec((1, tm, D), row)],
        out_specs=pl.BlockSpec((1, tm, D), row),
        out_shape=jax.ShapeDtypeStruct((B, T, D), F32),
        compiler_params=_params(("parallel", "parallel")),
        name="final",
    )(x1, mod3, moe)


def _blockdiag2(w):
    z = jnp.zeros_like(w[0])
    return jnp.concatenate([jnp.concatenate([w[0], z], axis=1), jnp.concatenate([z, w[1]], axis=1)], axis=0)


def _layer(x, mod3, rope_place, rope_pad, cos_q, sin_q, ones, tri, norm1_g, w_in, mu_prev, mu_next, rwkv_w0, rwkv_w2, rwkv_a0,
           rwkv_a2, rwkv_g2, rwkv_k_k, rwkv_k_a, rwkv_r_k, rwkv_gn_w, rwkv_gn_b, q_norm_g, k_norm_g,
           attn_sink, p_rwkv, p_attn, w_out, norm2_g, w_router, w_gate, w_up, w_down):
    B, T, D = x.shape
    tm = TM_ROWS if T % TM_ROWS == 0 else min(T, 256)
    cap = CAP_FACTOR * T // N_EXPERTS
    row = lambda a: a.reshape(1, -1)
    w_b = w_in.astype(BF16)
    wq_t = lax.optimization_barrier(w_in[:, Q0:K0]).T.astype(BF16)
    qg = jnp.broadcast_to(jnp.tile(q_norm_g, HEADS)[:, None], (RW, 128))
    qt, k, v, gl, r, vv, kkn, g, bonus, lw0, lw1, k0, k1, b0, b1 = _inproj_call(
        x, mod3, row(norm1_g), w_b, wq_t, rope_place, rope_pad, cos_q, sin_q,
        qg, row(jnp.tile(k_norm_g, KV_HEADS)), ones,
        row(mu_prev), row(mu_next), row(rwkv_w0), _blockdiag2(rwkv_w2).astype(BF16),
        row(rwkv_a0), _blockdiag2(rwkv_a2).astype(BF16), rwkv_g2.astype(BF16),
        row(rwkv_k_k), row(rwkv_k_a), row(rwkv_r_k), tm)
    yf, yb = _scan_call(r, vv, kkn, lw0, lw1, k0, k1, b0, b1, SCAN_CHUNKS)
    x1, u2, aff = _merge_call(attn_sink, x, mod3, yf, yb, bonus, g, qt, k, v, gl, row(rwkv_gn_w), row(rwkv_gn_b), ones,
                              p_rwkv.astype(BF16), p_attn.astype(BF16), w_out.astype(BF16),
                              row(norm2_g), w_router.T, tm)
    assert T <= 256 * TOKEN_RADIX
    tok = jnp.arange(T)
    digits = jnp.stack([tok // TOKEN_RADIX, tok % TOKEN_RADIX]).astype(F32)
    idx, affsel = _route_call(aff, tri, digits, cap)
    moe = _moe_call(u2, idx, affsel, w_gate.astype(BF16), w_up.astype(BF16), w_down.astype(BF16), cap)
    return _final_call(x1, mod3, moe, TM_FINAL if T % TM_FINAL == 0 else tm)


def kernel(x, c, positions, w_ada, b_ada, norm1_g, w_in, mu_prev, mu_next, rwkv_w0, rwkv_w2, rwkv_a0, rwkv_a2, rwkv_g2, rwkv_k_k, rwkv_k_a, rwkv_r_k, rwkv_gn_w, rwkv_gn_b, q_norm_g, k_norm_g, attn_sink, p_rwkv, p_attn, w_out, norm2_g, w_router, w_gate, w_up, w_down):
    B, T, D = x.shape
    depth = w_ada.shape[0]
    half = ROT // 2
    inv_freq = ROPE_THETA ** (-jnp.arange(0, ROT, 2, dtype=F32) / ROT)
    ang = positions.astype(F32)[..., None] * inv_freq
    cos8, sin8 = jnp.cos(ang), jnp.sin(ang)
    lane = jnp.arange(KV_W) % HD
    freq = jnp.arange(half)[:, None]
    first, second = (lane[None, :] == freq), (lane[None, :] == freq + half)
    rope_place = jnp.stack([(first | second).astype(F32), second.astype(F32) - first.astype(F32)])
    rope_pad = (lane >= ROT).astype(F32).reshape(1, KV_W)
    seg = jnp.arange(RW) // HD
    ones = (seg[:, None] == seg[None, :]).astype(BF16)
    idx = jnp.arange(128)
    tri = (idx[:, None] < idx[None, :]).astype(BF16)
    cos_q, sin_q = jnp.swapaxes(cos8, 1, 2), jnp.swapaxes(sin8, 1, 2)
    for l in range(depth):
        mod3 = _mod_call(c, w_ada[l], b_ada[l]).reshape(B, 6, D)
        x = _layer(x, mod3, rope_place, rope_pad, cos_q, sin_q, ones, tri, norm1_g[l], w_in[l], mu_prev[l], mu_next[l],
                   rwkv_w0[l], rwkv_w2[l], rwkv_a0[l], rwkv_a2[l], rwkv_g2[l], rwkv_k_k[l], rwkv_k_a[l],
                   rwkv_r_k[l], rwkv_gn_w[l], rwkv_gn_b[l], q_norm_g[l], k_norm_g[l], attn_sink[l],
                   p_rwkv[l], p_attn[l], w_out[l], norm2_g[l], w_router[l], w_gate[l], w_up[l], w_down[l])
    return x
```

```python
import functools
import math

import jax
import jax.numpy as jnp
from jax import lax
from jax.experimental import pallas as pl
from jax.experimental.pallas import tpu as pltpu

F32 = jnp.float32
BF16 = jnp.bfloat16

D_MODEL = 1024
RW = 512
HEADS = 8
HD = 64
LORA_W = 64
LORA_A = 64
LORA_G = 128
GN_EPS = HD * 1e-5
KV_HEADS = 2
GROUP = HEADS // KV_HEADS
KV_W = KV_HEADS * HD
BLOCK = 128
ROPE_THETA = 500000.0
ROT = HD // 4
N_EXPERTS = 16
CAP_FACTOR = 2
NORM_EPS = 1e-6
RWKV_COLS = 3 * RW + 2 * LORA_W + 2 * LORA_A + LORA_G
Q0 = RWKV_COLS
K0 = Q0 + RW
V0 = K0 + KV_W
G0 = V0 + KV_W
IN_COLS = G0 + 2 * D_MODEL
CHUNK = 64
SUB = 16
SCAN_CHUNKS = 4
TM_FINAL = 2048
TM_ROWS = 512
TOKEN_RADIX = 64
BISECT_ITERS = 160
NEG = -1e30
LOG2E = math.log2(math.e)
QK_SCALE = HD ** -0.5 * LOG2E
VMEM_LIMIT = 56 * 1024 * 1024
MOE_VMEM_LIMIT = 60 * 1024 * 1024
MOE_ACC_SPLIT = 4


def _dot(a, b):
    return jnp.dot(a, b, preferred_element_type=F32)


def _dot_nt(a, b):
    return lax.dot_general(a, b, (((1,), (1,)), ((), ())), preferred_element_type=F32)


def _dot_tn(a, b):
    return lax.dot_general(a, b, (((0,), (0,)), ((), ())), preferred_element_type=F32)


def _segsum(x, ones):
    xh = x.astype(BF16)
    xl = (x - xh.astype(F32)).astype(BF16)
    return _dot(xh, ones) + _dot(xl, ones)


def _segsum_pos(x, ones):
    return _dot(x.astype(BF16), ones)


def _params(sem):
    return pltpu.CompilerParams(dimension_semantics=sem, vmem_limit_bytes=VMEM_LIMIT)


def _mod_kernel(c_ref, w_ref, b_ref, o_ref):
    c = c_ref[...]
    ca = c * jax.nn.sigmoid(c)
    o_ref[...] = jnp.dot(ca, w_ref[...], preferred_element_type=F32,
                         precision=lax.Precision.HIGHEST) + b_ref[...]


def _mod_call(c, w_ada, b_ada):
    B, D = c.shape
    n = w_ada.shape[1] // D
    return pl.pallas_call(
        _mod_kernel,
        grid=(n,),
        in_specs=[pl.BlockSpec((B, D), lambda j: (0, 0)),
                  pl.BlockSpec((D, D), lambda j: (0, j)),
                  pl.BlockSpec((1, D), lambda j: (0, j))],
        out_specs=pl.BlockSpec((B, D), lambda j: (0, j)),
        out_shape=jax.ShapeDtypeStruct((B, n * D), F32),
        compiler_params=_params(("arbitrary",)),
        name="mod",
    )(c, w_ada, b_ada.reshape(1, -1))


def _rope(xn, cos, sin, width):
    lane = lax.broadcasted_iota(jnp.int32, xn.shape, 1) & (HD - 1)
    rot = jnp.where(lane < ROT // 2, pltpu.roll(xn, width - ROT // 2, 1), pltpu.roll(xn, ROT // 2, 1))
    return xn * cos + rot * sin


HALO = 8
INPROJ_SUBTILES = 2


def _inproj_kernel(x_ref, xp_ref, xn_ref, mod_ref, g_ref, w_ref, wq_ref, place_ref, pad_ref, cosq_ref, sinq_ref,
                   qg_ref, kg_ref, ones8_ref,
                   mup_ref, mun_ref, w0_ref, w2_ref, a0_ref, a2_ref, g2_ref, kk_ref, ka_ref, rk_ref,
                   qt_ref, k_ref, v_ref, gl_ref,
                   r_o, v_o, kkn_o, g_o, bonus_o, lw0_o, lw1_o, k0_o, k1_o, b0_o, b1_o):
    tm = x_ref.shape[1]
    sub = tm // INPROJ_SUBTILES
    xx = jnp.concatenate([xp_ref[0], x_ref[0], xn_ref[0]], axis=0)
    subs = [_inproj_rows(lo, sub, xx[lo:lo + sub + 2 * HALO], tm, mod_ref, g_ref, w_ref, wq_ref, place_ref, pad_ref,
                         cosq_ref, sinq_ref, qg_ref, kg_ref, ones8_ref, mup_ref, mun_ref, w0_ref, w2_ref, a0_ref,
                         a2_ref, g2_ref, kk_ref, ka_ref, rk_ref, qt_ref, k_ref, v_ref, gl_ref,
                         (r_o, v_o, kkn_o, g_o, bonus_o, lw0_o, lw1_o, k0_o, k1_o, b0_o, b1_o))
            for lo in range(0, tm, sub)]
    for s in subs:
        next(s)
    for s in subs:
        next(s)
        next(s)
    for s in subs:
        next(s, None)


def _inproj_rows(lo, n, xx, tm, mod_ref, g_ref, w_ref, wq_ref, place_ref, pad_ref, cosq_ref, sinq_ref,
                 qg_ref, kg_ref, ones8_ref, mup_ref, mun_ref, w0_ref, w2_ref, a0_ref, a2_ref, g2_ref,
                 kk_ref, ka_ref, rk_ref, qt_ref, k_ref, v_ref, gl_ref, rwkv_outs):
    r_o, v_o, kkn_o, g_o, bonus_o, lw0_o, lw1_o, k0_o, k1_o, b0_o, b1_o = rwkv_outs
    i = pl.program_id(1)
    last = pl.num_programs(1) - 1
    rows = slice(lo, lo + n)
    ms = jnp.mean(xx * xx, axis=-1, keepdims=True)
    y = xx * lax.rsqrt(ms + NORM_EPS) * g_ref[...]
    u_all = y * (1.0 + mod_ref[0, 1:2, :]) + mod_ref[0, 0:1, :]
    ub = u_all[HALO:HALO + n].astype(BF16)
    z_all = _dot(u_all.astype(BF16), w_ref[:, 0:RWKV_COLS])
    ones8 = ones8_ref[...]
    yield

    z = z_all[HALO:HALO + n]
    row = lax.broadcasted_iota(jnp.int32, z.shape, 0) + lo
    zp = jnp.where((row == 0) & (i == 0), 0.0, z_all[HALO - 1:HALO - 1 + n])
    zn = jnp.where((row == tm - 1) & (i == last), 0.0, z_all[HALO + 1:HALO + 1 + n])
    zs = z + mup_ref[...] * (zp - z) + mun_ref[...] * (zn - z)
    r = zs[:, 0:RW]
    kr = zs[:, RW:2 * RW]
    vr = zs[:, 2 * RW:3 * RW]
    c0 = 3 * RW
    wd = zs[:, c0:c0 + 2 * LORA_W]
    ad = zs[:, c0 + 2 * LORA_W:c0 + 2 * LORA_W + 2 * LORA_A]
    gd = zs[:, c0 + 2 * LORA_W + 2 * LORA_A:RWKV_COLS]
    wl = _dot(jnp.tanh(wd).astype(BF16), w2_ref[...]) + w0_ref[...]
    lw = -math.exp(-0.5) * jax.nn.sigmoid(wl)
    al = jax.nn.sigmoid(_dot(ad.astype(BF16), a2_ref[...]) + a0_ref[...])
    g = _dot(jax.nn.sigmoid(gd).astype(BF16), g2_ref[...])
    yield

    k = _dot(ub, w_ref[:, K0:V0])
    v_ref[0, rows, :] = _dot(ub, w_ref[:, V0:G0]).astype(BF16)
    gl_ref[0, rows, :] = _dot(ub, w_ref[:, G0:IN_COLS]).astype(BF16)
    kn = k * lax.rsqrt(_segsum_pos(k * k, ones8[:KV_W, :KV_W]) * (1.0 / HD) + NORM_EPS) * kg_ref[...]
    def place(t8, e):
        hi = t8.astype(BF16).astype(F32)
        return _dot_tn(hi, e) + _dot_tn(t8 - hi, e)

    cos_k = place(cosq_ref[0, :, rows], place_ref[0]) + pad_ref[...]
    sin_k = place(sinq_ref[0, :, rows], place_ref[1])
    k_ref[0, rows, :] = _rope(kn, cos_k, sin_k, KV_W).astype(BF16)

    kk = kr * kk_ref[...]
    kkn = kk / jnp.maximum(jnp.sqrt(_segsum_pos(kk * kk, ones8)), 1e-12)
    ka = ka_ref[...]
    a_0 = al[:, 0:RW]
    a_1 = al[:, RW:2 * RW]
    k_0 = kr * (1.0 + (a_0 - 1.0) * ka)
    k_1 = kr * (1.0 + (a_1 - 1.0) * ka)
    bonus = _segsum(r * (k_0 + k_1) * rk_ref[...], ones8) * vr
    r_o[0, rows, :] = r.astype(BF16)
    v_o[0, rows, :] = vr.astype(BF16)
    kkn_o[0, rows, :] = kkn.astype(BF16)
    g_o[0, rows, :] = g.astype(BF16)
    bonus_o[0, rows, :] = bonus.astype(BF16)
    lw0_o[0, rows, :] = lw[:, 0:RW]
    lw1_o[0, rows, :] = lw[:, RW:2 * RW]
    k0_o[0, rows, :] = k_0.astype(BF16)
    k1_o[0, rows, :] = k_1.astype(BF16)
    b0_o[0, rows, :] = (kkn * a_0).astype(BF16)
    b1_o[0, rows, :] = (kkn * a_1).astype(BF16)
    yield

    qt = _dot_nt(wq_ref[...], ub)
    qg = jnp.concatenate([qg_ref[...]] * (n // 128), axis=1)
    cq = cosq_ref[0, :, rows]
    sq = sinq_ref[0, :, rows]
    half = ROT // 2
    pieces = []
    for h in range(HEADS):
        xh = qt[h * HD:(h + 1) * HD]
        xn = (xh * lax.rsqrt(jnp.mean(xh * xh, axis=0, keepdims=True) + NORM_EPS)
              * qg[h * HD:(h + 1) * HD] * QK_SCALE)
        x1, x2 = xn[0:half], xn[half:ROT]
        pieces += [x1 * cq - x2 * sq, x2 * cq + x1 * sq, xn[ROT:]]
    qt_ref[0, :, rows] = jnp.concatenate(pieces, axis=0).astype(BF16)


def _inproj_call(x, mod3, norm1_g, w_rest, wq_t, rope_place, rope_pad, cos_q, sin_q, qg, kg, ones,
                 mup, mun, w0c, w2c, a0c, a2c, g2b, k_k, k_a, r_k, tm):
    B, T, D = x.shape
    nh = tm // HALO
    nth = T // HALO
    row = lambda b, i: (b, i, 0)
    col = lambda b, i: (b, 0, i)
    const2 = lambda b, i: (0, 0)
    half = ROT // 2
    rw = pl.BlockSpec((1, tm, RW), row)
    vec = lambda n: pl.BlockSpec((1, n), const2)
    return pl.pallas_call(
        _inproj_kernel,
        grid=(B, T // tm),
        in_specs=[pl.BlockSpec((1, tm, D), row),
                  pl.BlockSpec((1, HALO, D), lambda b, i: (b, jnp.maximum(i * nh - 1, 0), 0)),
                  pl.BlockSpec((1, HALO, D), lambda b, i: (b, jnp.minimum((i + 1) * nh, nth - 1), 0)),
                  pl.BlockSpec((1, 6, D), lambda b, i: (b, 0, 0)),
                  vec(D),
                  pl.BlockSpec(w_rest.shape, const2),
                  pl.BlockSpec((RW, D), const2),
                  pl.BlockSpec((2, half, KV_W), lambda b, i: (0, 0, 0)),
                  vec(KV_W),
                  pl.BlockSpec((1, half, tm), col),
                  pl.BlockSpec((1, half, tm), col),
                  pl.BlockSpec((RW, 128), const2),
                  vec(KV_W),
                  pl.BlockSpec((RW, RW), const2),
                  vec(RWKV_COLS), vec(RWKV_COLS),
                  vec(2 * RW), pl.BlockSpec((2 * LORA_W, 2 * RW), const2),
                  vec(2 * RW), pl.BlockSpec((2 * LORA_A, 2 * RW), const2),
                  pl.BlockSpec((LORA_G, RW), const2),
                  vec(RW), vec(RW), vec(RW)],
        out_specs=[pl.BlockSpec((1, RW, tm), col),
                   pl.BlockSpec((1, tm, KV_W), row),
                   pl.BlockSpec((1, tm, KV_W), row),
                   pl.BlockSpec((1, tm, 2 * D), row)] + [rw] * 11,
        out_shape=[jax.ShapeDtypeStruct((B, RW, T), BF16),
                   jax.ShapeDtypeStruct((B, T, KV_W), BF16),
                   jax.ShapeDtypeStruct((B, T, KV_W), BF16),
                   jax.ShapeDtypeStruct((B, T, 2 * D), BF16)]
        + [jax.ShapeDtypeStruct((B, T, RW), F32 if i in (5, 6) else BF16) for i in range(11)],
        compiler_params=_params(("parallel", "parallel")),
        name="inproj",
    )(x, x, x, mod3, norm1_g, w_rest, wq_t, rope_place, rope_pad, cos_q, sin_q, qg, kg, ones,
      mup, mun, w0c, w2c, a0c, a2c, g2b, k_k, k_a, r_k)


def _dir_operands(rev, nch, r, v, kk, lw, kd, bd):
    n = r.shape[0]
    ti = lax.broadcasted_iota(jnp.int32, (n, n), 0)
    si = lax.broadcasted_iota(jnp.int32, (n, n), 1)
    same = (ti // CHUNK) == (si // CHUNK)
    incl = ((si >= ti) if rev else (si <= ti)) & same
    mi = incl.astype(BF16)
    l1 = lw.astype(BF16)
    l2 = (lw - l1.astype(F32)).astype(BF16)
    cum = _dot(mi, l1) + _dot(mi, l2)
    wt = jnp.exp(cum)
    winv = jnp.exp(-cum)
    ops = dict(ah=-(kk * jnp.exp(cum - lw)), rt=r * wt, bt=bd * winv, kt=kd * winv, v=v)
    wcs, bhs, khs = [], [], []
    for j in range(nch):
        last = j * CHUNK if rev else (j + 1) * CHUNK - 1
        wc = wt[last:last + 1, :]
        rows = slice(j * CHUNK, (j + 1) * CHUNK)
        wcs.append(wc)
        bhs.append(ops["bt"][rows] * wc)
        khs.append(ops["kt"][rows] * wc)
    return ops, wcs, bhs, khs


def _scan_kernel(rf, vf, kkf, lwf, kf, bf, rb, vb, kkb, lwb, kb, bb, yf_ref, yb_ref, st_ref, *, nch):
    @pl.when(pl.program_id(1) == 0)
    def _():
        st_ref[...] = jnp.zeros_like(st_ref)

    c = CHUNK
    pw = 2 * HD
    npair = HEADS // 2
    ti = lax.broadcasted_iota(jnp.int32, (c, pw), 0)
    li = lax.broadcasted_iota(jnp.int32, (c, pw), 1)
    si = li & (HD - 1)
    head0 = li < HD
    blk = (ti // SUB) == (si // SUB)
    eye = ti == si
    two = lambda m: jnp.concatenate([m, m], axis=1)
    masks = {False: (two(si < ti), two(si <= ti)), True: (two(si > ti), two(si >= ti))}

    def bd(y):
        yb = y.astype(BF16)
        zero = jnp.zeros_like(yb)
        return jnp.concatenate([jnp.where(head0, yb, zero), jnp.where(head0, zero, yb)], axis=0)

    def bdcat(*ys):
        return jnp.concatenate([bd(y) for y in ys], axis=1)

    def diag_blocks(full):
        return jnp.where(head0, full[:c], full[c:])

    dirs = [(False, rf, vf, kkf, lwf, kf, bf), (True, rb, vb, kkb, lwb, kb, bb)]
    units = []
    for d, (rev, r_, v_, kk_, lw_, k_, b_) in enumerate(dirs):
        f32 = lambda ref: ref[0].astype(F32)
        ops, wcs, bhs, khs = _dir_operands(rev, nch, f32(r_), f32(v_), f32(kk_), lw_[0], f32(k_), f32(b_))
        for j in range(nch):
            rows = slice(j * c, (j + 1) * c)
            for p in range(npair):
                sl = slice(p * pw, (p + 1) * pw)
                u = dict(d=d, j=j, p=p, rev=rev, wc=wcs[j][:, sl],
                         bkh=jnp.concatenate([bhs[j][:, sl], khs[j][:, sl]], axis=0).astype(BF16))
                for name in ("ah", "rt", "bt", "kt", "v"):
                    u[name] = ops[name][rows, sl]
                units.append(u)

    for u in units:
        x1 = jnp.concatenate([u["ah"], u["rt"]], axis=0).astype(BF16)
        u["lbk"] = _dot_nt(x1, jnp.concatenate([bd(u["bt"]), bd(u["kt"])], axis=0))
    for u in units:
        strict, incl = masks[u["rev"]]
        top = jnp.where(strict, u["lbk"][:c], 0.0)
        bot = jnp.where(incl, u["lbk"][c:], 0.0)
        lab = top[:, :pw]
        u["mrb"] = bot[:, :pw].astype(BF16)
        u["ld"] = jnp.where(blk, lab, 0.0)
        u["lo"] = lab - u["ld"]
        u["lm"] = jnp.concatenate([top[:, pw:], bot[:, pw:]], axis=0).astype(BF16)
    for u in units:
        u["a2"] = _dot(u["ld"].astype(BF16), bd(u["ld"]))
        u["lkv"] = _dot(u["lm"], bd(u["v"]))
    for u in units:
        s = _dot(jnp.concatenate([u["a2"], u["ld"]], axis=0).astype(BF16), bd(u["a2"]))
        u["a4"] = s[:c]
        u["t"] = jnp.where(eye, 1.0, 0.0) + u["ld"] + u["a2"] + s[c:]
    for u in units:
        s = _dot(jnp.concatenate([u["a4"], u["t"]], axis=0).astype(BF16), bd(u["a4"]))
        u["a8"] = s[:c]
        u["t"] = u["t"] + s[c:]
    for u in units:
        u["td"] = u["t"] + _dot(u["t"].astype(BF16), bd(u["a8"]))
    for u in units:
        x = _dot(u["td"].astype(BF16), bdcat(u["lo"], u["ah"], u["lkv"][:c]))
        u["n"] = x[:, :pw]
        u["z"] = x[:, pw:]
    for u in units:
        s = _dot(u["n"].astype(BF16), bdcat(u["n"], u["z"][:, :pw], u["z"][:, pw:]))
        u["n2"] = s[:, :pw]
        u["z"] = u["z"] + s[:, pw:]
    for u in units:
        u["z"] = u["z"] + _dot(u["n2"].astype(BF16), bdcat(u["z"][:, :pw], u["z"][:, pw:]))
    for u in units:
        z = u["z"]
        y10 = (jnp.concatenate([u["rt"], u["lkv"][c:]], axis=1)
               + _dot(u["mrb"], bdcat(z[:, :pw], z[:, pw:])))
        vpad = jnp.concatenate([jnp.zeros_like(u["v"]), u["v"]], axis=1)
        full = _dot_tn(u["bkh"], jnp.concatenate([z, vpad], axis=0).astype(BF16))
        g = diag_blocks(full[:, :pw]) + jnp.where(eye, u["wc"], 0.0)
        u["gy"] = jnp.concatenate([g, y10[:, :pw]], axis=0).astype(BF16)
        u["h0"] = diag_blocks(full[:, pw:])
        u["y0"] = y10[:, pw:]

    ys = {}
    for d in range(2):
        order = range(nch - 1, -1, -1) if d == 1 else range(nch)
        zs = [st_ref[d, p] for p in range(npair)]
        for j in order:
            for p in range(npair):
                u = units[(d * nch + j) * npair + p]
                m = _dot(u["gy"], bd(zs[p]))
                zs[p] = m[:c] + u["h0"]
                ys[(d, j, p)] = m[c:] + u["y0"]
        for p in range(npair):
            st_ref[d, p] = zs[p]
    for d, y_ref in enumerate((yf_ref, yb_ref)):
        y_ref[0] = jnp.concatenate(
            [jnp.concatenate([ys[(d, j, p)] for p in range(npair)], axis=-1) for j in range(nch)],
            axis=0).astype(BF16)


def _scan_call(r, v, kkn, lw0, lw1, k0, k1, b0, b1, nch):
    B, T, _ = r.shape
    rows = nch * CHUNK
    ns = T // rows
    fwd = pl.BlockSpec((1, rows, RW), lambda b, c: (b, c, 0))
    bwd = pl.BlockSpec((1, rows, RW), lambda b, c: (b, ns - 1 - c, 0))
    return pl.pallas_call(
        functools.partial(_scan_kernel, nch=nch),
        grid=(B, ns),
        in_specs=[fwd] * 6 + [bwd] * 6,
        out_specs=[fwd, bwd],
        out_shape=[jax.ShapeDtypeStruct((B, T, RW), BF16)] * 2,
        scratch_shapes=[pltpu.VMEM((2, HEADS // 2, HD, 2 * HD), F32)],
        compiler_params=_params(("parallel", "arbitrary")),
        name="scan",
    )(r, v, kkn, lw0, k0, b0, r, v, kkn, lw1, k1, b1)


def _attention_scores(sink_ref, qt_ref, k_ref, v_ref, n, qb):
    t = k_ref.shape[1]
    band = 3 * BLOCK
    cols = GROUP * BLOCK
    ki = lax.broadcasted_iota(jnp.int32, (band, BLOCK), 0)
    qi = lax.broadcasted_iota(jnp.int32, (band, BLOCK), 1)
    lane_head = lax.broadcasted_iota(jnp.int32, (1, cols), 1) // BLOCK
    zeros = jnp.zeros((HD, cols), BF16)
    units = []
    for j in range(qb):
        blk = n * qb + j
        start = pl.multiple_of(jnp.clip((blk - 1) * BLOCK, 0, t - band), BLOCK)
        bias = jnp.where(jnp.abs(start + ki - (blk * BLOCK + qi)) <= BLOCK, 0.0, NEG)
        bias = jnp.concatenate([bias] * GROUP, axis=1)
        kb = k_ref[0, pl.ds(start, band), :]
        vb = v_ref[0, pl.ds(start, band), :]
        for g in range(KV_HEADS):
            qg = jnp.concatenate([qt_ref[0, (g * GROUP + i) * HD:(g * GROUP + i + 1) * HD,
                                         j * BLOCK:(j + 1) * BLOCK] for i in range(GROUP)], axis=1)
            rhs = jnp.concatenate([qg if gg == g else zeros for gg in range(KV_HEADS)], axis=0)
            sk = jnp.zeros((1, cols), F32)
            for i in range(GROUP):
                sk = jnp.where(lane_head == i, sink_ref[g * GROUP + i] * LOG2E, sk)
            units.append(dict(j=j, g=g, kb=kb, vb=vb, rhs=rhs, sk=sk, bias=bias))
    for u in units:
        u["s"] = _dot(u["kb"], u["rhs"]) + u["bias"]
    return units


def _attention_finish(units, qb):
    for u in units:
        u["m"] = jnp.maximum(jnp.max(u["s"], axis=0, keepdims=True), u["sk"])
    for u in units:
        p = jnp.exp2(u["s"] - u["m"])
        u["r"] = 1.0 / (jnp.sum(p, axis=0, keepdims=True) + jnp.exp2(u["sk"] - u["m"]))
        u["p"] = p.astype(BF16)
    for u in units:
        g = u["g"]
        u["o"] = _dot_tn(u["vb"], u["p"])[g * HD:(g + 1) * HD] * u["r"]
    out = {(u["j"], u["g"]): u["o"] for u in units}
    return jnp.concatenate(
        [jnp.concatenate([out[(j, h // GROUP)][:, (h % GROUP) * BLOCK:(h % GROUP + 1) * BLOCK]
                          for j in range(qb)], axis=1) for h in range(HEADS)], axis=0)


def _merge_kernel(sink_ref, x_ref, mod_ref, yf_ref, yb_ref, bonus_ref, g_ref, qt_ref, k_ref, v_ref, gl_ref,
                  gnw_ref, gnb_ref, ones_ref, pr_ref, pa_ref, wo_ref, n2g_ref, wr_ref,
                  x1_ref, u2_ref, aff_ref):
    qb = x_ref.shape[1] // BLOCK
    att = _attention_scores(sink_ref, qt_ref, k_ref, v_ref, pl.program_id(1), qb)
    ones = ones_ref[...]
    y = yf_ref[0].astype(F32) + yb_ref[0].astype(F32)
    mu = _segsum(y, ones) * (1.0 / HD)
    yc = y - mu
    var = _segsum_pos(yc * yc, ones) * (1.0 / HD)
    yn = yc * lax.rsqrt(var + GN_EPS) * gnw_ref[...] + gnb_ref[...]
    ya = ((yn + bonus_ref[0].astype(F32)) * g_ref[0].astype(F32)).astype(BF16)
    pa = _dot(ya, pr_ref[...])
    gates = jax.nn.sigmoid(gl_ref[0].astype(F32))
    yatt_t = _attention_finish(att, qb).astype(BF16)
    pb = _dot_tn(yatt_t, pa_ref[...])
    m = gates[:, :D_MODEL] * pa + gates[:, D_MODEL:] * pb
    x1 = x_ref[0] + mod_ref[0, 2:3, :] * _dot(m.astype(BF16), wo_ref[...])
    x1_ref[0] = x1
    ms = jnp.mean(x1 * x1, axis=-1, keepdims=True)
    u2 = x1 * lax.rsqrt(ms + NORM_EPS) * n2g_ref[...] * (1.0 + mod_ref[0, 4:5, :]) + mod_ref[0, 3:4, :]
    u2_ref[0] = u2
    u2h = u2.astype(BF16)
    u2l = (u2 - u2h.astype(F32)).astype(BF16)
    wr = wr_ref[...]
    wrh = wr.astype(BF16)
    wrl = (wr - wrh.astype(F32)).astype(BF16)
    t1 = _dot_nt(jnp.concatenate([wrh, wrl], axis=0), u2h)
    logits = t1[:N_EXPERTS] + t1[N_EXPERTS:] + _dot_nt(wrh, u2l)
    e = jnp.exp(logits - jnp.max(logits, axis=0, keepdims=True))
    aff_ref[0] = e / jnp.sum(e, axis=0, keepdims=True)


def _merge_call(sink, x, mod3, yf, yb, bonus, g, qt, k, v, gl, gnw, gnb, ones, p_r, p_a, w_o, n2g, w_rt, tm):
    B, T, D = x.shape
    assert T >= 3 * BLOCK and tm % BLOCK == 0
    row = lambda b, i: (b, i, 0)
    const2 = lambda b, i: (0, 0)
    rw = pl.BlockSpec((1, tm, RW), row)
    kv = pl.BlockSpec((1, T, KV_W), lambda b, i: (b, 0, 0))
    return pl.pallas_call(
        _merge_kernel,
        grid=(B, T // tm),
        in_specs=[pl.BlockSpec(memory_space=pltpu.SMEM),
                  pl.BlockSpec((1, tm, D), row),
                  pl.BlockSpec((1, 6, D), lambda b, i: (b, 0, 0)),
                  rw, rw, rw, rw,
                  pl.BlockSpec((1, RW, tm), lambda b, i: (b, 0, i)), kv, kv,
                  pl.BlockSpec((1, tm, 2 * D), row),
                  pl.BlockSpec((1, RW), const2),
                  pl.BlockSpec((1, RW), const2),
                  pl.BlockSpec((RW, RW), const2),
                  pl.BlockSpec((RW, D), const2),
                  pl.BlockSpec((RW, D), const2),
                  pl.BlockSpec((D, D), const2),
                  pl.BlockSpec((1, D), const2),
                  pl.BlockSpec((N_EXPERTS, D), const2)],
        out_specs=[pl.BlockSpec((1, tm, D), row),
                   pl.BlockSpec((1, tm, D), row),
                   pl.BlockSpec((1, N_EXPERTS, tm), lambda b, i: (b, 0, i))],
        out_shape=[jax.ShapeDtypeStruct((B, T, D), F32),
                   jax.ShapeDtypeStruct((B, T, D), F32),
                   jax.ShapeDtypeStruct((B, N_EXPERTS, T), F32)],
        compiler_params=_params(("parallel", "parallel")),
        name="merge",
    )(sink, x, mod3, yf, yb, bonus, g, qt, k, v, gl, gnw, gnb, ones, p_r, p_a, w_o, n2g, w_rt)


def _excl_prefix(mask, tri):
    e, t = mask.shape
    mb = mask.astype(BF16)
    carry = jnp.zeros((e, 1), F32)
    outs = []
    for j in range(t // 128):
        tile = mb[:, j * 128:(j + 1) * 128]
        outs.append(_dot(tile, tri) + carry)
        carry = carry + jnp.sum(tile.astype(F32), axis=1, keepdims=True)
    return jnp.concatenate(outs, axis=-1)


def _route_kernel(aff_ref, tri_ref, dig_ref, idx_ref, affsel_ref, slot_ref, *, cap):
    nb, ne, t = aff_ref.shape
    a = aff_ref[...].reshape(nb * ne, t)
    e = a.shape[0]
    capf = jnp.float32(cap)

    def cond(s):
        it, lo, hi = s
        mid = 0.5 * (lo + hi)
        still_open = jnp.max(jnp.where((mid > lo) & (mid < hi), 1.0, 0.0))
        return (it < BISECT_ITERS) & (still_open > 0.0)

    def body(s):
        it, lo, hi = s
        mid = 0.5 * (lo + hi)
        ge = jnp.sum((a >= mid).astype(F32), axis=1, keepdims=True) >= capf
        return it + 1, jnp.where(ge, mid, lo), jnp.where(ge, hi, mid)

    lo0 = jnp.zeros((e, 1), F32)
    hi0 = jnp.full((e, 1), 2.0, F32)
    _, lo, hi = lax.while_loop(cond, body, (jnp.int32(0), lo0, hi0))
    gt = a >= hi
    eq = (a >= lo) & (a < hi)
    need = capf - jnp.sum(gt.astype(F32), axis=1, keepdims=True)
    tri = tri_ref[...]
    sel = gt | (eq & (_excl_prefix(eq, tri) < need))
    pos = _excl_prefix(sel, tri)
    slot_ref[...] = jnp.where(sel, pos.astype(jnp.int32), -1).reshape(nb, ne, t)

    digits = dig_ref[...]
    assert cap <= 256
    slots = lax.broadcasted_iota(jnp.int32, (cap, t), 0).astype(BF16)
    one, zero = jnp.ones((cap, t), BF16), jnp.zeros((cap, t), BF16)
    pad = jnp.zeros((3, t), F32)

    def row_body(r, carry):
        srow = slot_ref[r // ne, pl.ds(r % ne, 1), :].astype(BF16)
        arow = aff_ref[r // ne, pl.ds(r % ne, 1), :]
        a1 = arow.astype(BF16).astype(F32)
        a2 = (arow - a1).astype(BF16).astype(F32)
        a3 = arow - a1 - a2
        lhs = jnp.concatenate([digits, a1, a2, a3, pad], axis=0).astype(BF16)
        d = _dot_nt(lhs, jnp.where(slots == srow, one, zero))
        idx_ref[r] = (d[0:1] * TOKEN_RADIX + d[1:2]).astype(jnp.int32)
        affsel_ref[r] = d[2:3] + d[3:4] + d[4:5]
        return carry

    lax.fori_loop(0, nb * ne, row_body, 0, unroll=8)


def _route_call(aff, tri, digits, cap):
    B, E, T = aff.shape
    per_slot = pl.BlockSpec((B * E, 1, cap), lambda i: (0, 0, 0))
    return pl.pallas_call(
        functools.partial(_route_kernel, cap=cap),
        grid=(1,),
        in_specs=[pl.BlockSpec((B, E, T), lambda i: (0, 0, 0)),
                  pl.BlockSpec((128, 128), lambda i: (0, 0)),
                  pl.BlockSpec((2, T), lambda i: (0, 0))],
        out_specs=[per_slot, per_slot],
        out_shape=[jax.ShapeDtypeStruct((B * E, 1, cap), jnp.int32),
                   jax.ShapeDtypeStruct((B * E, 1, cap), F32)],
        scratch_shapes=[pltpu.VMEM((B, E, T), jnp.int32)],
        compiler_params=_params(("arbitrary",)),
        name="route",
    )(aff, tri, digits)


def _moe_kernel(idxp_ref, idx0_ref, idx1_ref, affp_ref, aff0_ref, u2_ref, wg_ref, wu_ref, wd_ref, o_ref,
                xa_ref, xb_ref, ya_ref, yb_ref, *acc_refs, cap):
    e = pl.program_id(1)

    def gather(idx_ref, dst_ref):
        for i in range(cap):
            dst_ref[pl.ds(i, 1), :] = u2_ref[0, pl.ds(idx_ref[0, 0, i], 1), :]

    cw = acc_refs[0].shape[1]

    def scatter(idx_ref, w_ref, src_ref):
        for i in range(cap):
            tok, w = idx_ref[0, 0, i], w_ref[0, 0, i]
            for q, acc_ref in enumerate(acc_refs):
                acc_ref[pl.ds(tok, 1), :] += w * src_ref[pl.ds(i, 1), q * cw:(q + 1) * cw]

    def expert(x_ref, y_ref):
        xe = x_ref[...].astype(BF16)
        hg = _dot(xe, wg_ref[0])
        hu = _dot(xe, wu_ref[0])
        h = (hg * jax.nn.sigmoid(hg) * hu).astype(BF16)
        y_ref[...] = _dot(h, wd_ref[0])

    @pl.when(e == 0)
    def _():
        for acc_ref in acc_refs:
            acc_ref[...] = jnp.zeros_like(acc_ref)
        yb_ref[...] = jnp.zeros_like(yb_ref)
        gather(idx0_ref, xa_ref)

    @pl.when(e % 2 == 0)
    def _():
        scatter(idxp_ref, affp_ref, yb_ref)
        gather(idx1_ref, xb_ref)
        expert(xa_ref, ya_ref)

    @pl.when(e % 2 == 1)
    def _():
        scatter(idxp_ref, affp_ref, ya_ref)
        gather(idx1_ref, xa_ref)
        expert(xb_ref, yb_ref)

    @pl.when(e == pl.num_programs(1) - 1)
    def _():
        scatter(idx0_ref, aff0_ref, yb_ref)
        o_ref[0] = jnp.concatenate([acc_ref[...] for acc_ref in acc_refs], axis=1).astype(BF16)


def _moe_call(u2, idx, affsel, wg, wu, wd, cap):
    B, T, D = u2.shape
    E = wg.shape[0]
    F = wg.shape[2]
    assert E % 2 == 0
    smem_row = lambda im: pl.BlockSpec((1, 1, cap), im, memory_space=pltpu.SMEM)
    prev = lambda b, e: (b * E + jnp.maximum(e - 1, 0), 0, 0)
    cur = lambda b, e: (b * E + e, 0, 0)
    nxt = lambda b, e: (b * E + jnp.minimum(e + 1, E - 1), 0, 0)
    rows = pltpu.VMEM((cap, D), F32)
    return pl.pallas_call(
        functools.partial(_moe_kernel, cap=cap),
        grid=(B, E),
        in_specs=[smem_row(prev), smem_row(cur), smem_row(nxt), smem_row(prev), smem_row(cur),
                  pl.BlockSpec((1, T, D), lambda b, e: (b, 0, 0)),
                  pl.BlockSpec((1, D, F), lambda b, e: (e, 0, 0)),
                  pl.BlockSpec((1, D, F), lambda b, e: (e, 0, 0)),
                  pl.BlockSpec((1, F, D), lambda b, e: (e, 0, 0))],
        out_specs=pl.BlockSpec((1, T, D), lambda b, e: (b, 0, 0)),
        out_shape=jax.ShapeDtypeStruct((B, T, D), BF16),
        scratch_shapes=[rows, rows, rows, rows] + [pltpu.VMEM((T, D // MOE_ACC_SPLIT), F32)] * MOE_ACC_SPLIT,
        compiler_params=pltpu.CompilerParams(dimension_semantics=("parallel", "arbitrary"),
                                             vmem_limit_bytes=MOE_VMEM_LIMIT),
        name="moe",
    )(idx, idx, idx, affsel, affsel, u2, wg, wu, wd)


def _final_kernel(x1_ref, mod_ref, moe_ref, o_ref):
    o_ref[0] = x1_ref[0] + mod_ref[0, 5:6, :] * moe_ref[0].astype(F32)


def _final_call(x1, mod3, moe, tm):
    B, T, D = x1.shape
    row = lambda b, i: (b, i, 0)
    return pl.pallas_call(
        _final_kernel,
        grid=(B, T // tm),
        in_specs=[pl.BlockSpec((1, tm, D), row),
                  pl.BlockSpec((1, 6, D), lambda b, i: (b, 0, 0)),
                  pl.BlockSpec((1, tm, D), row)],
        out_specs=pl.BlockSpec((1, tm, D), row),
        out_shape=jax.ShapeDtypeStruct((B, T, D), F32),
        compiler_params=_params(("parallel", "parallel")),
        name="final",
    )(x1, mod3, moe)


def _blockdiag2(w):
    z = jnp.zeros_like(w[0])
    return jnp.concatenate([jnp.concatenate([w[0], z], axis=1), jnp.concatenate([z, w[1]], axis=1)], axis=0)


def _layer(x, mod3, rope_place, rope_pad, cos_q, sin_q, ones, tri, norm1_g, w_in, mu_prev, mu_next, rwkv_w0, rwkv_w2, rwkv_a0,
           rwkv_a2, rwkv_g2, rwkv_k_k, rwkv_k_a, rwkv_r_k, rwkv_gn_w, rwkv_gn_b, q_norm_g, k_norm_g,
           attn_sink, p_rwkv, p_attn, w_out, norm2_g, w_router, w_gate, w_up, w_down):
    B, T, D = x.shape
    tm = TM_ROWS if T % TM_ROWS == 0 else min(T, 256)
    cap = CAP_FACTOR * T // N_EXPERTS
    row = lambda a: a.reshape(1, -1)
    w_b = w_in.astype(BF16)
    qg = jnp.broadcast_to(jnp.tile(q_norm_g, HEADS)[:, None], (RW, 128))
    qt, k, v, gl, r, vv, kkn, g, bonus, lw0, lw1, k0, k1, b0, b1 = _inproj_call(
        x, mod3, row(norm1_g), w_b, w_in[:, Q0:K0].T.astype(BF16), rope_place, rope_pad, cos_q, sin_q,
        qg, row(jnp.tile(k_norm_g, KV_HEADS)), ones,
        row(mu_prev), row(mu_next), row(rwkv_w0), _blockdiag2(rwkv_w2).astype(BF16),
        row(rwkv_a0), _blockdiag2(rwkv_a2).astype(BF16), rwkv_g2.astype(BF16),
        row(rwkv_k_k), row(rwkv_k_a), row(rwkv_r_k), tm)
    yf, yb = _scan_call(r, vv, kkn, lw0, lw1, k0, k1, b0, b1, SCAN_CHUNKS)
    x1, u2, aff = _merge_call(attn_sink, x, mod3, yf, yb, bonus, g, qt, k, v, gl, row(rwkv_gn_w), row(rwkv_gn_b), ones,
                              p_rwkv.astype(BF16), p_attn.astype(BF16), w_out.astype(BF16),
                              row(norm2_g), w_router.T, tm)
    assert T <= 256 * TOKEN_RADIX
    tok = jnp.arange(T)
    digits = jnp.stack([tok // TOKEN_RADIX, tok % TOKEN_RADIX]).astype(F32)
    idx, affsel = _route_call(aff, tri, digits, cap)
    moe = _moe_call(u2, idx, affsel, w_gate.astype(BF16), w_up.astype(BF16), w_down.astype(BF16), cap)
    return _final_call(x1, mod3, moe, TM_FINAL if T % TM_FINAL == 0 else tm)


def kernel(x, c, positions, w_ada, b_ada, norm1_g, w_in, mu_prev, mu_next, rwkv_w0, rwkv_w2, rwkv_a0, rwkv_a2, rwkv_g2, rwkv_k_k, rwkv_k_a, rwkv_r_k, rwkv_gn_w, rwkv_gn_b, q_norm_g, k_norm_g, attn_sink, p_rwkv, p_attn, w_out, norm2_g, w_router, w_gate, w_up, w_down):
    B, T, D = x.shape
    depth = w_ada.shape[0]
    half = ROT // 2
    inv_freq = ROPE_THETA ** (-jnp.arange(0, ROT, 2, dtype=F32) / ROT)
    ang = positions.astype(F32)[..., None] * inv_freq
    cos8, sin8 = jnp.cos(ang), jnp.sin(ang)
    lane = jnp.arange(KV_W) % HD
    freq = jnp.arange(half)[:, None]
    first, second = (lane[None, :] == freq), (lane[None, :] == freq + half)
    rope_place = jnp.stack([(first | second).astype(F32), second.astype(F32) - first.astype(F32)])
    rope_pad = (lane >= ROT).astype(F32).reshape(1, KV_W)
    seg = jnp.arange(RW) // HD
    ones = (seg[:, None] == seg[None, :]).astype(BF16)
    idx = jnp.arange(128)
    tri = (idx[:, None] < idx[None, :]).astype(BF16)
    cos_q, sin_q = jnp.swapaxes(cos8, 1, 2), jnp.swapaxes(sin8, 1, 2)
    for l in range(depth):
        mod3 = _mod_call(c, w_ada[l], b_ada[l]).reshape(B, 6, D)
        x = _layer(x, mod3, rope_place, rope_pad, cos_q, sin_q, ones, tri, norm1_g[l], w_in[l], mu_prev[l], mu_next[l],
                   rwkv_w0[l], rwkv_w2[l], rwkv_a0[l], rwkv_a2[l], rwkv_g2[l], rwkv_k_k[l], rwkv_k_a[l],
                   rwkv_r_k[l], rwkv_gn_w[l], rwkv_gn_b[l], q_norm_g[l], k_norm_g[l], attn_sink[l],
                   p_rwkv[l], p_attn[l], w_out[l], norm2_g[l], w_router[l], w_gate[l], w_up[l], w_down[l])
    return x
```

```python
import functools
import math

import jax
import jax.numpy as jnp
from jax import lax
from jax.experimental import pallas as pl
from jax.experimental.pallas import tpu as pltpu

F32 = jnp.float32
BF16 = jnp.bfloat16

D_MODEL = 1024
RW = 512
HEADS = 8
HD = 64
LORA_W = 64
LORA_A = 64
LORA_G = 128
GN_EPS = HD * 1e-5
KV_HEADS = 2
GROUP = HEADS // KV_HEADS
KV_W = KV_HEADS * HD
BLOCK = 128
ROPE_THETA = 500000.0
ROT = HD // 4
N_EXPERTS = 16
CAP_FACTOR = 2
NORM_EPS = 1e-6
RWKV_COLS = 3 * RW + 2 * LORA_W + 2 * LORA_A + LORA_G
Q0 = RWKV_COLS
K0 = Q0 + RW
V0 = K0 + KV_W
G0 = V0 + KV_W
IN_COLS = G0 + 2 * D_MODEL
CHUNK = 64
SUB = 16
SCAN_CHUNKS = 4
TM_FINAL = 2048
TM_ROWS = 512
TOKEN_RADIX = 64
BISECT_ITERS = 160
NEG = -1e30
LOG2E = math.log2(math.e)
QK_SCALE = HD ** -0.5 * LOG2E
VMEM_LIMIT = 56 * 1024 * 1024
MOE_VMEM_LIMIT = 60 * 1024 * 1024
MOE_ACC_SPLIT = 4


def _dot(a, b):
    return jnp.dot(a, b, preferred_element_type=F32)


def _dot_nt(a, b):
    return lax.dot_general(a, b, (((1,), (1,)), ((), ())), preferred_element_type=F32)


def _dot_tn(a, b):
    return lax.dot_general(a, b, (((0,), (0,)), ((), ())), preferred_element_type=F32)


def _segsum(x, ones):
    xh = x.astype(BF16)
    xl = (x - xh.astype(F32)).astype(BF16)
    return _dot(xh, ones) + _dot(xl, ones)


def _segsum_pos(x, ones):
    return _dot(x.astype(BF16), ones)


def _params(sem):
    return pltpu.CompilerParams(dimension_semantics=sem, vmem_limit_bytes=VMEM_LIMIT)


def _mod_kernel(c_ref, w_ref, b_ref, o_ref):
    c = c_ref[...]
    ca = c * jax.nn.sigmoid(c)
    o_ref[...] = jnp.dot(ca, w_ref[...], preferred_element_type=F32,
                         precision=lax.Precision.HIGHEST) + b_ref[...]


def _mod_call(c, w_ada, b_ada):
    B, D = c.shape
    n = w_ada.shape[1] // D
    return pl.pallas_call(
        _mod_kernel,
        grid=(n,),
        in_specs=[pl.BlockSpec((B, D), lambda j: (0, 0)),
                  pl.BlockSpec((D, D), lambda j: (0, j)),
                  pl.BlockSpec((1, D), lambda j: (0, j))],
        out_specs=pl.BlockSpec((B, D), lambda j: (0, j)),
        out_shape=jax.ShapeDtypeStruct((B, n * D), F32),
        compiler_params=_params(("arbitrary",)),
        name="mod",
    )(c, w_ada, b_ada.reshape(1, -1))


def _rope(xn, cos, sin, width):
    lane = lax.broadcasted_iota(jnp.int32, xn.shape, 1) & (HD - 1)
    rot = jnp.where(lane < ROT // 2, pltpu.roll(xn, width - ROT // 2, 1), pltpu.roll(xn, ROT // 2, 1))
    return xn * cos + rot * sin


HALO = 8
INPROJ_SUBTILES = 2


def _inproj_kernel(x_ref, xp_ref, xn_ref, mod_ref, g_ref, w_ref, wq_ref, place_ref, pad_ref, cosq_ref, sinq_ref,
                   qg_ref, kg_ref, ones8_ref,
                   mup_ref, mun_ref, w0_ref, w2_ref, a0_ref, a2_ref, g2_ref, kk_ref, ka_ref, rk_ref,
                   qt_ref, k_ref, v_ref, gl_ref,
                   r_o, v_o, kkn_o, g_o, bonus_o, lw0_o, lw1_o, k0_o, k1_o, b0_o, b1_o):
    tm = x_ref.shape[1]
    sub = tm // INPROJ_SUBTILES
    xx = jnp.concatenate([xp_ref[0], x_ref[0], xn_ref[0]], axis=0)
    subs = [_inproj_rows(lo, sub, xx[lo:lo + sub + 2 * HALO], tm, mod_ref, g_ref, w_ref, wq_ref, place_ref, pad_ref,
                         cosq_ref, sinq_ref, qg_ref, kg_ref, ones8_ref, mup_ref, mun_ref, w0_ref, w2_ref, a0_ref,
                         a2_ref, g2_ref, kk_ref, ka_ref, rk_ref, qt_ref, k_ref, v_ref, gl_ref,
                         (r_o, v_o, kkn_o, g_o, bonus_o, lw0_o, lw1_o, k0_o, k1_o, b0_o, b1_o))
            for lo in range(0, tm, sub)]
    for s in subs:
        next(s)
    for s in subs:
        next(s)
        next(s)
    for s in subs:
        next(s, None)


def _inproj_rows(lo, n, xx, tm, mod_ref, g_ref, w_ref, wq_ref, place_ref, pad_ref, cosq_ref, sinq_ref,
                 qg_ref, kg_ref, ones8_ref, mup_ref, mun_ref, w0_ref, w2_ref, a0_ref, a2_ref, g2_ref,
                 kk_ref, ka_ref, rk_ref, qt_ref, k_ref, v_ref, gl_ref, rwkv_outs):
    r_o, v_o, kkn_o, g_o, bonus_o, lw0_o, lw1_o, k0_o, k1_o, b0_o, b1_o = rwkv_outs
    i = pl.program_id(1)
    last = pl.num_programs(1) - 1
    rows = slice(lo, lo + n)
    ms = jnp.mean(xx * xx, axis=-1, keepdims=True)
    y = xx * lax.rsqrt(ms + NORM_EPS) * g_ref[...]
    u_all = y * (1.0 + mod_ref[0, 1:2, :]) + mod_ref[0, 0:1, :]
    ub = u_all[HALO:HALO + n].astype(BF16)
    z_all = _dot(u_all.astype(BF16), w_ref[:, 0:RWKV_COLS])
    ones8 = ones8_ref[...]
    yield

    z = z_all[HALO:HALO + n]
    row = lax.broadcasted_iota(jnp.int32, z.shape, 0) + lo
    zp = jnp.where((row == 0) & (i == 0), 0.0, z_all[HALO - 1:HALO - 1 + n])
    zn = jnp.where((row == tm - 1) & (i == last), 0.0, z_all[HALO + 1:HALO + 1 + n])
    zs = z + mup_ref[...] * (zp - z) + mun_ref[...] * (zn - z)
    r = zs[:, 0:RW]
    kr = zs[:, RW:2 * RW]
    vr = zs[:, 2 * RW:3 * RW]
    c0 = 3 * RW
    wd = zs[:, c0:c0 + 2 * LORA_W]
    ad = zs[:, c0 + 2 * LORA_W:c0 + 2 * LORA_W + 2 * LORA_A]
    gd = zs[:, c0 + 2 * LORA_W + 2 * LORA_A:RWKV_COLS]
    wl = _dot(jnp.tanh(wd).astype(BF16), w2_ref[...]) + w0_ref[...]
    lw = -math.exp(-0.5) * jax.nn.sigmoid(wl)
    al = jax.nn.sigmoid(_dot(ad.astype(BF16), a2_ref[...]) + a0_ref[...])
    g = _dot(jax.nn.sigmoid(gd).astype(BF16), g2_ref[...])
    yield

    k = _dot(ub, w_ref[:, K0:V0])
    v_ref[0, rows, :] = _dot(ub, w_ref[:, V0:G0]).astype(BF16)
    gl_ref[0, rows, :] = _dot(ub, w_ref[:, G0:IN_COLS]).astype(BF16)
    kn = k * lax.rsqrt(_segsum_pos(k * k, ones8[:KV_W, :KV_W]) * (1.0 / HD) + NORM_EPS) * kg_ref[...]
    def place(t8, e):
        hi = t8.astype(BF16).astype(F32)
        return _dot_tn(hi, e) + _dot_tn(t8 - hi, e)

    cos_k = place(cosq_ref[0, :, rows], place_ref[0]) + pad_ref[...]
    sin_k = place(sinq_ref[0, :, rows], place_ref[1])
    k_ref[0, rows, :] = _rope(kn, cos_k, sin_k, KV_W).astype(BF16)

    kk = kr * kk_ref[...]
    kkn = kk / jnp.maximum(jnp.sqrt(_segsum_pos(kk * kk, ones8)), 1e-12)
    ka = ka_ref[...]
    a_0 = al[:, 0:RW]
    a_1 = al[:, RW:2 * RW]
    k_0 = kr * (1.0 + (a_0 - 1.0) * ka)
    k_1 = kr * (1.0 + (a_1 - 1.0) * ka)
    bonus = _segsum(r * (k_0 + k_1) * rk_ref[...], ones8) * vr
    r_o[0, rows, :] = r.astype(BF16)
    v_o[0, rows, :] = vr.astype(BF16)
    kkn_o[0, rows, :] = kkn.astype(BF16)
    g_o[0, rows, :] = g.astype(BF16)
    bonus_o[0, rows, :] = bonus.astype(BF16)
    lw0_o[0, rows, :] = lw[:, 0:RW]
    lw1_o[0, rows, :] = lw[:, RW:2 * RW]
    k0_o[0, rows, :] = k_0.astype(BF16)
    k1_o[0, rows, :] = k_1.astype(BF16)
    b0_o[0, rows, :] = (kkn * a_0).astype(BF16)
    b1_o[0, rows, :] = (kkn * a_1).astype(BF16)
    yield

    qt = _dot_nt(wq_ref[...], ub)
    qg = jnp.concatenate([qg_ref[...]] * (n // 128), axis=1)
    cq = cosq_ref[0, :, rows]
    sq = sinq_ref[0, :, rows]
    half = ROT // 2
    pieces = []
    for h in range(HEADS):
        xh = qt[h * HD:(h + 1) * HD]
        xn = (xh * lax.rsqrt(jnp.mean(xh * xh, axis=0, keepdims=True) + NORM_EPS)
              * qg[h * HD:(h + 1) * HD] * QK_SCALE)
        x1, x2 = xn[0:half], xn[half:ROT]
        pieces += [x1 * cq - x2 * sq, x2 * cq + x1 * sq, xn[ROT:]]
    qt_ref[0, :, rows] = jnp.concatenate(pieces, axis=0).astype(BF16)


def _inproj_call(x, mod3, norm1_g, w_all, wq_t, rope_place, rope_pad, cos_q, sin_q, qg, kg, ones,
                 mup, mun, w0c, w2c, a0c, a2c, g2b, k_k, k_a, r_k, tm):
    B, T, D = x.shape
    nh = tm // HALO
    nth = T // HALO
    row = lambda b, i: (b, i, 0)
    col = lambda b, i: (b, 0, i)
    const2 = lambda b, i: (0, 0)
    half = ROT // 2
    rw = pl.BlockSpec((1, tm, RW), row)
    vec = lambda n: pl.BlockSpec((1, n), const2)
    return pl.pallas_call(
        _inproj_kernel,
        grid=(B, T // tm),
        in_specs=[pl.BlockSpec((1, tm, D), row),
                  pl.BlockSpec((1, HALO, D), lambda b, i: (b, jnp.maximum(i * nh - 1, 0), 0)),
                  pl.BlockSpec((1, HALO, D), lambda b, i: (b, jnp.minimum((i + 1) * nh, nth - 1), 0)),
                  pl.BlockSpec((1, 6, D), lambda b, i: (b, 0, 0)),
                  vec(D),
                  pl.BlockSpec(w_all.shape, const2),
                  pl.BlockSpec((RW, D), const2),
                  pl.BlockSpec((2, half, KV_W), lambda b, i: (0, 0, 0)),
                  vec(KV_W),
                  pl.BlockSpec((1, half, tm), col),
                  pl.BlockSpec((1, half, tm), col),
                  pl.BlockSpec((RW, 128), const2),
                  vec(KV_W),
                  pl.BlockSpec((RW, RW), const2),
                  vec(RWKV_COLS), vec(RWKV_COLS),
                  vec(2 * RW), pl.BlockSpec((2 * LORA_W, 2 * RW), const2),
                  vec(2 * RW), pl.BlockSpec((2 * LORA_A, 2 * RW), const2),
                  pl.BlockSpec((LORA_G, RW), const2),
                  vec(RW), vec(RW), vec(RW)],
        out_specs=[pl.BlockSpec((1, RW, tm), col),
                   pl.BlockSpec((1, tm, KV_W), row),
                   pl.BlockSpec((1, tm, KV_W), row),
                   pl.BlockSpec((1, tm, 2 * D), row)] + [rw] * 11,
        out_shape=[jax.ShapeDtypeStruct((B, RW, T), BF16),
                   jax.ShapeDtypeStruct((B, T, KV_W), BF16),
                   jax.ShapeDtypeStruct((B, T, KV_W), BF16),
                   jax.ShapeDtypeStruct((B, T, 2 * D), BF16)]
        + [jax.ShapeDtypeStruct((B, T, RW), F32 if i in (5, 6) else BF16) for i in range(11)],
        compiler_params=_params(("parallel", "parallel")),
        name="inproj",
    )(x, x, x, mod3, norm1_g, w_all, wq_t, rope_place, rope_pad, cos_q, sin_q, qg, kg, ones,
      mup, mun, w0c, w2c, a0c, a2c, g2b, k_k, k_a, r_k)


def _dir_operands(rev, nch, r, v, kk, lw, kd, bd):
    n = r.shape[0]
    ti = lax.broadcasted_iota(jnp.int32, (n, n), 0)
    si = lax.broadcasted_iota(jnp.int32, (n, n), 1)
    same = (ti // CHUNK) == (si // CHUNK)
    incl = ((si >= ti) if rev else (si <= ti)) & same
    mi = incl.astype(BF16)
    l1 = lw.astype(BF16)
    l2 = (lw - l1.astype(F32)).astype(BF16)
    cum = _dot(mi, l1) + _dot(mi, l2)
    wt = jnp.exp(cum)
    winv = jnp.exp(-cum)
    ops = dict(ah=-(kk * jnp.exp(cum - lw)), rt=r * wt, bt=bd * winv, kt=kd * winv, v=v)
    wcs, bhs, khs = [], [], []
    for j in range(nch):
        last = j * CHUNK if rev else (j + 1) * CHUNK - 1
        wc = wt[last:last + 1, :]
        rows = slice(j * CHUNK, (j + 1) * CHUNK)
        wcs.append(wc)
        bhs.append(ops["bt"][rows] * wc)
        khs.append(ops["kt"][rows] * wc)
    return ops, wcs, bhs, khs


def _scan_kernel(rf, vf, kkf, lwf, kf, bf, rb, vb, kkb, lwb, kb, bb, yf_ref, yb_ref, st_ref, *, nch):
    @pl.when(pl.program_id(1) == 0)
    def _():
        st_ref[...] = jnp.zeros_like(st_ref)

    c = CHUNK
    pw = 2 * HD
    npair = HEADS // 2
    ti = lax.broadcasted_iota(jnp.int32, (c, pw), 0)
    li = lax.broadcasted_iota(jnp.int32, (c, pw), 1)
    si = li & (HD - 1)
    head0 = li < HD
    blk = (ti // SUB) == (si // SUB)
    eye = ti == si
    two = lambda m: jnp.concatenate([m, m], axis=1)
    masks = {False: (two(si < ti), two(si <= ti)), True: (two(si > ti), two(si >= ti))}

    def bd(y):
        yb = y.astype(BF16)
        zero = jnp.zeros_like(yb)
        return jnp.concatenate([jnp.where(head0, yb, zero), jnp.where(head0, zero, yb)], axis=0)

    def bdcat(*ys):
        return jnp.concatenate([bd(y) for y in ys], axis=1)

    def diag_blocks(full):
        return jnp.where(head0, full[:c], full[c:])

    dirs = [(False, rf, vf, kkf, lwf, kf, bf), (True, rb, vb, kkb, lwb, kb, bb)]
    units = []
    for d, (rev, r_, v_, kk_, lw_, k_, b_) in enumerate(dirs):
        f32 = lambda ref: ref[0].astype(F32)
        ops, wcs, bhs, khs = _dir_operands(rev, nch, f32(r_), f32(v_), f32(kk_), lw_[0], f32(k_), f32(b_))
        for j in range(nch):
            rows = slice(j * c, (j + 1) * c)
            for p in range(npair):
                sl = slice(p * pw, (p + 1) * pw)
                u = dict(d=d, j=j, p=p, rev=rev, wc=wcs[j][:, sl],
                         bkh=jnp.concatenate([bhs[j][:, sl], khs[j][:, sl]], axis=0).astype(BF16))
                for name in ("ah", "rt", "bt", "kt", "v"):
                    u[name] = ops[name][rows, sl]
                units.append(u)

    for u in units:
        x1 = jnp.concatenate([u["ah"], u["rt"]], axis=0).astype(BF16)
        u["lbk"] = _dot_nt(x1, jnp.concatenate([bd(u["bt"]), bd(u["kt"])], axis=0))
    for u in units:
        strict, incl = masks[u["rev"]]
        top = jnp.where(strict, u["lbk"][:c], 0.0)
        bot = jnp.where(incl, u["lbk"][c:], 0.0)
        lab = top[:, :pw]
        u["mrb"] = bot[:, :pw].astype(BF16)
        u["ld"] = jnp.where(blk, lab, 0.0)
        u["lo"] = lab - u["ld"]
        u["lm"] = jnp.concatenate([top[:, pw:], bot[:, pw:]], axis=0).astype(BF16)
    for u in units:
        u["a2"] = _dot(u["ld"].astype(BF16), bd(u["ld"]))
        u["lkv"] = _dot(u["lm"], bd(u["v"]))
    for u in units:
        s = _dot(jnp.concatenate([u["a2"], u["ld"]], axis=0).astype(BF16), bd(u["a2"]))
        u["a4"] = s[:c]
        u["t"] = jnp.where(eye, 1.0, 0.0) + u["ld"] + u["a2"] + s[c:]
    for u in units:
        s = _dot(jnp.concatenate([u["a4"], u["t"]], axis=0).astype(BF16), bd(u["a4"]))
        u["a8"] = s[:c]
        u["t"] = u["t"] + s[c:]
    for u in units:
        u["td"] = u["t"] + _dot(u["t"].astype(BF16), bd(u["a8"]))
    for u in units:
        x = _dot(u["td"].astype(BF16), bdcat(u["lo"], u["ah"], u["lkv"][:c]))
        u["n"] = x[:, :pw]
        u["z"] = x[:, pw:]
    for u in units:
        s = _dot(u["n"].astype(BF16), bdcat(u["n"], u["z"][:, :pw], u["z"][:, pw:]))
        u["n2"] = s[:, :pw]
        u["z"] = u["z"] + s[:, pw:]
    for u in units:
        u["z"] = u["z"] + _dot(u["n2"].astype(BF16), bdcat(u["z"][:, :pw], u["z"][:, pw:]))
    for u in units:
        z = u["z"]
        y10 = (jnp.concatenate([u["rt"], u["lkv"][c:]], axis=1)
               + _dot(u["mrb"], bdcat(z[:, :pw], z[:, pw:])))
        vpad = jnp.concatenate([jnp.zeros_like(u["v"]), u["v"]], axis=1)
        full = _dot_tn(u["bkh"], jnp.concatenate([z, vpad], axis=0).astype(BF16))
        g = diag_blocks(full[:, :pw]) + jnp.where(eye, u["wc"], 0.0)
        u["gy"] = jnp.concatenate([g, y10[:, :pw]], axis=0).astype(BF16)
        u["h0"] = diag_blocks(full[:, pw:])
        u["y0"] = y10[:, pw:]

    ys = {}
    for d in range(2):
        order = range(nch - 1, -1, -1) if d == 1 else range(nch)
        zs = [st_ref[d, p] for p in range(npair)]
        for j in order:
            for p in range(npair):
                u = units[(d * nch + j) * npair + p]
                m = _dot(u["gy"], bd(zs[p]))
                zs[p] = m[:c] + u["h0"]
                ys[(d, j, p)] = m[c:] + u["y0"]
        for p in range(npair):
            st_ref[d, p] = zs[p]
    for d, y_ref in enumerate((yf_ref, yb_ref)):
        y_ref[0] = jnp.concatenate(
            [jnp.concatenate([ys[(d, j, p)] for p in range(npair)], axis=-1) for j in range(nch)],
            axis=0).astype(BF16)


def _scan_call(r, v, kkn, lw0, lw1, k0, k1, b0, b1, nch):
    B, T, _ = r.shape
    rows = nch * CHUNK
    ns = T // rows
    fwd = pl.BlockSpec((1, rows, RW), lambda b, c: (b, c, 0))
    bwd = pl.BlockSpec((1, rows, RW), lambda b, c: (b, ns - 1 - c, 0))
    return pl.pallas_call(
        functools.partial(_scan_kernel, nch=nch),
        grid=(B, ns),
        in_specs=[fwd] * 6 + [bwd] * 6,
        out_specs=[fwd, bwd],
        out_shape=[jax.ShapeDtypeStruct((B, T, RW), BF16)] * 2,
        scratch_shapes=[pltpu.VMEM((2, HEADS // 2, HD, 2 * HD), F32)],
        compiler_params=_params(("parallel", "arbitrary")),
        name="scan",
    )(r, v, kkn, lw0, k0, b0, r, v, kkn, lw1, k1, b1)


def _attention_scores(sink_ref, qt_ref, k_ref, v_ref, n, qb):
    t = k_ref.shape[1]
    band = 3 * BLOCK
    cols = GROUP * BLOCK
    ki = lax.broadcasted_iota(jnp.int32, (band, BLOCK), 0)
    qi = lax.broadcasted_iota(jnp.int32, (band, BLOCK), 1)
    lane_head = lax.broadcasted_iota(jnp.int32, (1, cols), 1) // BLOCK
    zeros = jnp.zeros((HD, cols), BF16)
    units = []
    for j in range(qb):
        blk = n * qb + j
        start = pl.multiple_of(jnp.clip((blk - 1) * BLOCK, 0, t - band), BLOCK)
        bias = jnp.where(jnp.abs(start + ki - (blk * BLOCK + qi)) <= BLOCK, 0.0, NEG)
        bias = jnp.concatenate([bias] * GROUP, axis=1)
        kb = k_ref[0, pl.ds(start, band), :]
        vb = v_ref[0, pl.ds(start, band), :]
        for g in range(KV_HEADS):
            qg = jnp.concatenate([qt_ref[0, (g * GROUP + i) * HD:(g * GROUP + i + 1) * HD,
                                         j * BLOCK:(j + 1) * BLOCK] for i in range(GROUP)], axis=1)
            rhs = jnp.concatenate([qg if gg == g else zeros for gg in range(KV_HEADS)], axis=0)
            sk = jnp.zeros((1, cols), F32)
            for i in range(GROUP):
                sk = jnp.where(lane_head == i, sink_ref[g * GROUP + i] * LOG2E, sk)
            units.append(dict(j=j, g=g, kb=kb, vb=vb, rhs=rhs, sk=sk, bias=bias))
    for u in units:
        u["s"] = _dot(u["kb"], u["rhs"]) + u["bias"]
    return units


def _attention_finish(units, qb):
    for u in units:
        u["m"] = jnp.maximum(jnp.max(u["s"], axis=0, keepdims=True), u["sk"])
    for u in units:
        p = jnp.exp2(u["s"] - u["m"])
        u["r"] = 1.0 / (jnp.sum(p, axis=0, keepdims=True) + jnp.exp2(u["sk"] - u["m"]))
        u["p"] = p.astype(BF16)
    for u in units:
        g = u["g"]
        u["o"] = _dot_tn(u["vb"], u["p"])[g * HD:(g + 1) * HD] * u["r"]
    out = {(u["j"], u["g"]): u["o"] for u in units}
    return jnp.concatenate(
        [jnp.concatenate([out[(j, h // GROUP)][:, (h % GROUP) * BLOCK:(h % GROUP + 1) * BLOCK]
                          for j in range(qb)], axis=1) for h in range(HEADS)], axis=0)


def _merge_kernel(sink_ref, x_ref, mod_ref, yf_ref, yb_ref, bonus_ref, g_ref, qt_ref, k_ref, v_ref, gl_ref,
                  gnw_ref, gnb_ref, ones_ref, pr_ref, pa_ref, wo_ref, n2g_ref, wr_ref,
                  x1_ref, u2_ref, aff_ref):
    qb = x_ref.shape[1] // BLOCK
    att = _attention_scores(sink_ref, qt_ref, k_ref, v_ref, pl.program_id(1), qb)
    ones = ones_ref[...]
    y = yf_ref[0].astype(F32) + yb_ref[0].astype(F32)
    mu = _segsum(y, ones) * (1.0 / HD)
    yc = y - mu
    var = _segsum_pos(yc * yc, ones) * (1.0 / HD)
    yn = yc * lax.rsqrt(var + GN_EPS) * gnw_ref[...] + gnb_ref[...]
    ya = ((yn + bonus_ref[0].astype(F32)) * g_ref[0].astype(F32)).astype(BF16)
    pa = _dot(ya, pr_ref[...])
    gates = jax.nn.sigmoid(gl_ref[0].astype(F32))
    yatt_t = _attention_finish(att, qb).astype(BF16)
    pb = _dot_tn(yatt_t, pa_ref[...])
    m = gates[:, :D_MODEL] * pa + gates[:, D_MODEL:] * pb
    x1 = x_ref[0] + mod_ref[0, 2:3, :] * _dot(m.astype(BF16), wo_ref[...])
    x1_ref[0] = x1
    ms = jnp.mean(x1 * x1, axis=-1, keepdims=True)
    u2 = x1 * lax.rsqrt(ms + NORM_EPS) * n2g_ref[...] * (1.0 + mod_ref[0, 4:5, :]) + mod_ref[0, 3:4, :]
    u2_ref[0] = u2
    u2h = u2.astype(BF16)
    u2l = (u2 - u2h.astype(F32)).astype(BF16)
    wr = wr_ref[...]
    wrh = wr.astype(BF16)
    wrl = (wr - wrh.astype(F32)).astype(BF16)
    t1 = _dot_nt(jnp.concatenate([wrh, wrl], axis=0), u2h)
    logits = t1[:N_EXPERTS] + t1[N_EXPERTS:] + _dot_nt(wrh, u2l)
    e = jnp.exp(logits - jnp.max(logits, axis=0, keepdims=True))
    aff_ref[0] = e / jnp.sum(e, axis=0, keepdims=True)


def _merge_call(sink, x, mod3, yf, yb, bonus, g, qt, k, v, gl, gnw, gnb, ones, p_r, p_a, w_o, n2g, w_rt, tm):
    B, T, D = x.shape
    assert T >= 3 * BLOCK and tm % BLOCK == 0
    row = lambda b, i: (b, i, 0)
    const2 = lambda b, i: (0, 0)
    rw = pl.BlockSpec((1, tm, RW), row)
    kv = pl.BlockSpec((1, T, KV_W), lambda b, i: (b, 0, 0))
    return pl.pallas_call(
        _merge_kernel,
        grid=(B, T // tm),
        in_specs=[pl.BlockSpec(memory_space=pltpu.SMEM),
                  pl.BlockSpec((1, tm, D), row),
                  pl.BlockSpec((1, 6, D), lambda b, i: (b, 0, 0)),
                  rw, rw, rw, rw,
                  pl.BlockSpec((1, RW, tm), lambda b, i: (b, 0, i)), kv, kv,
                  pl.BlockSpec((1, tm, 2 * D), row),
                  pl.BlockSpec((1, RW), const2),
                  pl.BlockSpec((1, RW), const2),
                  pl.BlockSpec((RW, RW), const2),
                  pl.BlockSpec((RW, D), const2),
                  pl.BlockSpec((RW, D), const2),
                  pl.BlockSpec((D, D), const2),
                  pl.BlockSpec((1, D), const2),
                  pl.BlockSpec((N_EXPERTS, D), const2)],
        out_specs=[pl.BlockSpec((1, tm, D), row),
                   pl.BlockSpec((1, tm, D), row),
                   pl.BlockSpec((1, N_EXPERTS, tm), lambda b, i: (b, 0, i))],
        out_shape=[jax.ShapeDtypeStruct((B, T, D), F32),
                   jax.ShapeDtypeStruct((B, T, D), F32),
                   jax.ShapeDtypeStruct((B, N_EXPERTS, T), F32)],
        compiler_params=_params(("parallel", "parallel")),
        name="merge",
    )(sink, x, mod3, yf, yb, bonus, g, qt, k, v, gl, gnw, gnb, ones, p_r, p_a, w_o, n2g, w_rt)


def _excl_prefix(mask, tri):
    e, t = mask.shape
    mb = mask.astype(BF16)
    carry = jnp.zeros((e, 1), F32)
    outs = []
    for j in range(t // 128):
        tile = mb[:, j * 128:(j + 1) * 128]
        outs.append(_dot(tile, tri) + carry)
        carry = carry + jnp.sum(tile.astype(F32), axis=1, keepdims=True)
    return jnp.concatenate(outs, axis=-1)


def _route_kernel(aff_ref, tri_ref, dig_ref, idx_ref, affsel_ref, slot_ref, *, cap):
    nb, ne, t = aff_ref.shape
    a = aff_ref[...].reshape(nb * ne, t)
    e = a.shape[0]
    capf = jnp.float32(cap)

    def cond(s):
        it, lo, hi = s
        mid = 0.5 * (lo + hi)
        still_open = jnp.max(jnp.where((mid > lo) & (mid < hi), 1.0, 0.0))
        return (it < BISECT_ITERS) & (still_open > 0.0)

    def body(s):
        it, lo, hi = s
        mid = 0.5 * (lo + hi)
        ge = jnp.sum((a >= mid).astype(F32), axis=1, keepdims=True) >= capf
        return it + 1, jnp.where(ge, mid, lo), jnp.where(ge, hi, mid)

    lo0 = jnp.zeros((e, 1), F32)
    hi0 = jnp.full((e, 1), 2.0, F32)
    _, lo, hi = lax.while_loop(cond, body, (jnp.int32(0), lo0, hi0))
    gt = a >= hi
    eq = (a >= lo) & (a < hi)
    need = capf - jnp.sum(gt.astype(F32), axis=1, keepdims=True)
    tri = tri_ref[...]
    sel = gt | (eq & (_excl_prefix(eq, tri) < need))
    pos = _excl_prefix(sel, tri)
    slot_ref[...] = jnp.where(sel, pos.astype(jnp.int32), -1).reshape(nb, ne, t)

    digits = dig_ref[...]
    assert cap <= 256
    slots = lax.broadcasted_iota(jnp.int32, (cap, t), 0).astype(BF16)
    one, zero = jnp.ones((cap, t), BF16), jnp.zeros((cap, t), BF16)
    pad = jnp.zeros((3, t), F32)

    def row_body(r, carry):
        srow = slot_ref[r // ne, pl.ds(r % ne, 1), :].astype(BF16)
        arow = aff_ref[r // ne, pl.ds(r % ne, 1), :]
        a1 = arow.astype(BF16).astype(F32)
        a2 = (arow - a1).astype(BF16).astype(F32)
        a3 = arow - a1 - a2
        lhs = jnp.concatenate([digits, a1, a2, a3, pad], axis=0).astype(BF16)
        d = _dot_nt(lhs, jnp.where(slots == srow, one, zero))
        idx_ref[r] = (d[0:1] * TOKEN_RADIX + d[1:2]).astype(jnp.int32)
        affsel_ref[r] = d[2:3] + d[3:4] + d[4:5]
        return carry

    lax.fori_loop(0, nb * ne, row_body, 0, unroll=8)


def _route_call(aff, tri, digits, cap):
    B, E, T = aff.shape
    per_slot = pl.BlockSpec((B * E, 1, cap), lambda i: (0, 0, 0))
    return pl.pallas_call(
        functools.partial(_route_kernel, cap=cap),
        grid=(1,),
        in_specs=[pl.BlockSpec((B, E, T), lambda i: (0, 0, 0)),
                  pl.BlockSpec((128, 128), lambda i: (0, 0)),
                  pl.BlockSpec((2, T), lambda i: (0, 0))],
        out_specs=[per_slot, per_slot],
        out_shape=[jax.ShapeDtypeStruct((B * E, 1, cap), jnp.int32),
                   jax.ShapeDtypeStruct((B * E, 1, cap), F32)],
        scratch_shapes=[pltpu.VMEM((B, E, T), jnp.int32)],
        compiler_params=_params(("arbitrary",)),
        name="route",
    )(aff, tri, digits)


def _moe_kernel(idxp_ref, idx0_ref, idx1_ref, affp_ref, aff0_ref, u2_ref, wg_ref, wu_ref, wd_ref, o_ref,
                xa_ref, xb_ref, ya_ref, yb_ref, *acc_refs, cap):
    e = pl.program_id(1)

    def gather(idx_ref, dst_ref):
        for i in range(cap):
            dst_ref[pl.ds(i, 1), :] = u2_ref[0, pl.ds(idx_ref[0, 0, i], 1), :]

    cw = acc_refs[0].shape[1]

    def scatter(idx_ref, w_ref, src_ref):
        for i in range(cap):
            tok, w = idx_ref[0, 0, i], w_ref[0, 0, i]
            for q, acc_ref in enumerate(acc_refs):
                acc_ref[pl.ds(tok, 1), :] += w * src_ref[pl.ds(i, 1), q * cw:(q + 1) * cw]

    def expert(x_ref, y_ref):
        xe = x_ref[...].astype(BF16)
        hg = _dot(xe, wg_ref[0])
        hu = _dot(xe, wu_ref[0])
        h = (hg * jax.nn.sigmoid(hg) * hu).astype(BF16)
        y_ref[...] = _dot(h, wd_ref[0])

    @pl.when(e == 0)
    def _():
        for acc_ref in acc_refs:
            acc_ref[...] = jnp.zeros_like(acc_ref)
        yb_ref[...] = jnp.zeros_like(yb_ref)
        gather(idx0_ref, xa_ref)

    @pl.when(e % 2 == 0)
    def _():
        scatter(idxp_ref, affp_ref, yb_ref)
        gather(idx1_ref, xb_ref)
        expert(xa_ref, ya_ref)

    @pl.when(e % 2 == 1)
    def _():
        scatter(idxp_ref, affp_ref, ya_ref)
        gather(idx1_ref, xa_ref)
        expert(xb_ref, yb_ref)

    @pl.when(e == pl.num_programs(1) - 1)
    def _():
        scatter(idx0_ref, aff0_ref, yb_ref)
        o_ref[0] = jnp.concatenate([acc_ref[...] for acc_ref in acc_refs], axis=1).astype(BF16)


def _moe_call(u2, idx, affsel, wg, wu, wd, cap):
    B, T, D = u2.shape
    E = wg.shape[0]
    F = wg.shape[2]
    assert E % 2 == 0
    smem_row = lambda im: pl.BlockSpec((1, 1, cap), im, memory_space=pltpu.SMEM)
    prev = lambda b, e: (b * E + jnp.maximum(e - 1, 0), 0, 0)
    cur = lambda b, e: (b * E + e, 0, 0)
    nxt = lambda b, e: (b * E + jnp.minimum(e + 1, E - 1), 0, 0)
    rows = pltpu.VMEM((cap, D), F32)
    return pl.pallas_call(
        functools.partial(_moe_kernel, cap=cap),
        grid=(B, E),
        in_specs=[smem_row(prev), smem_row(cur), smem_row(nxt), smem_row(prev), smem_row(cur),
                  pl.BlockSpec((1, T, D), lambda b, e: (b, 0, 0)),
                  pl.BlockSpec((1, D, F), lambda b, e: (e, 0, 0)),
                  pl.BlockSpec((1, D, F), lambda b, e: (e, 0, 0)),
                  pl.BlockSpec((1, F, D), lambda b, e: (e, 0, 0))],
        out_specs=pl.BlockSpec((1, T, D), lambda b, e: (b, 0, 0)),
        out_shape=jax.ShapeDtypeStruct((B, T, D), BF16),
        scratch_shapes=[rows, rows, rows, rows] + [pltpu.VMEM((T, D // MOE_ACC_SPLIT), F32)] * MOE_ACC_SPLIT,
        compiler_params=pltpu.CompilerParams(dimension_semantics=("parallel", "arbitrary"),
                                             vmem_limit_bytes=MOE_VMEM_LIMIT),
        name="moe",
    )(idx, idx, idx, affsel, affsel, u2, wg, wu, wd)


def _final_kernel(x1_ref, mod_ref, moe_ref, o_ref):
    o_ref[0] = x1_ref[0] + mod_ref[0, 5:6, :] * moe_ref[0].astype(F32)


def _final_call(x1, mod3, moe, tm):
    B, T, D = x1.shape
    row = lambda b, i: (b, i, 0)
    return pl.pallas_call(
        _final_kernel,
        grid=(B, T // tm),
        in_specs=[pl.BlockSpec((1, tm, D), row),
                  pl.BlockSpec((1, 6, D), lambda b, i: (b, 0, 0)),
                  pl.BlockSpec((1, tm, D), row)],
        out_specs=pl.BlockSpec((1, tm, D), row),
        out_shape=jax.ShapeDtypeStruct((B, T, D), F32),
        compiler_params=_params(("parallel", "parallel")),
        name="final",
    )(x1, mod3, moe)


def _blockdiag2(w):
    z = jnp.zeros_like(w[0])
    return jnp.concatenate([jnp.concatenate([w[0], z], axis=1), jnp.concatenate([z, w[1]], axis=1)], axis=0)


def _layer(x, mod3, rope_place, rope_pad, cos_q, sin_q, ones, tri, norm1_g, w_in, mu_prev, mu_next, rwkv_w0, rwkv_w2, rwkv_a0,
           rwkv_a2, rwkv_g2, rwkv_k_k, rwkv_k_a, rwkv_r_k, rwkv_gn_w, rwkv_gn_b, q_norm_g, k_norm_g,
           attn_sink, p_rwkv, p_attn, w_out, norm2_g, w_router, w_gate, w_up, w_down):
    B, T, D = x.shape
    tm = TM_ROWS if T % TM_ROWS == 0 else min(T, 256)
    cap = CAP_FACTOR * T // N_EXPERTS
    row = lambda a: a.reshape(1, -1)
    w_b = w_in.astype(BF16)
    qg = jnp.broadcast_to(jnp.tile(q_norm_g, HEADS)[:, None], (RW, 128))
    qt, k, v, gl, r, vv, kkn, g, bonus, lw0, lw1, k0, k1, b0, b1 = _inproj_call(
        x, mod3, row(norm1_g), w_b, w_in[:, Q0:K0].T.astype(BF16), rope_place, rope_pad, cos_q, sin_q,
        qg, row(jnp.tile(k_norm_g, KV_HEADS)), ones,
        row(mu_prev), row(mu_next), row(rwkv_w0), _blockdiag2(rwkv_w2).astype(BF16),
        row(rwkv_a0), _blockdiag2(rwkv_a2).astype(BF16), rwkv_g2.astype(BF16),
        row(rwkv_k_k), row(rwkv_k_a), row(rwkv_r_k), tm)
    yf, yb = _scan_call(r, vv, kkn, lw0, lw1, k0, k1, b0, b1, SCAN_CHUNKS)
    x1, u2, aff = _merge_call(attn_sink, x, mod3, yf, yb, bonus, g, qt, k, v, gl, row(rwkv_gn_w), row(rwkv_gn_b), ones,
                              p_rwkv.astype(BF16), p_attn.astype(BF16), w_out.astype(BF16),
                              row(norm2_g), w_router.T, tm)
    assert T <= 256 * TOKEN_RADIX
    tok = jnp.arange(T)
    digits = jnp.stack([tok // TOKEN_RADIX, tok % TOKEN_RADIX]).astype(F32)
    idx, affsel = _route_call(aff, tri, digits, cap)
    moe = _moe_call(u2, idx, affsel, w_gate.astype(BF16), w_up.astype(BF16), w_down.astype(BF16), cap)
    return _final_call(x1, mod3, moe, TM_FINAL if T % TM_FINAL == 0 else tm)


def kernel(x, c, positions, w_ada, b_ada, norm1_g, w_in, mu_prev, mu_next, rwkv_w0, rwkv_w2, rwkv_a0, rwkv_a2, rwkv_g2, rwkv_k_k, rwkv_k_a, rwkv_r_k, rwkv_gn_w, rwkv_gn_b, q_norm_g, k_norm_g, attn_sink, p_rwkv, p_attn, w_out, norm2_g, w_router, w_gate, w_up, w_down):
    B, T, D = x.shape
    depth = w_ada.shape[0]
    half = ROT // 2
    inv_freq = ROPE_THETA ** (-jnp.arange(0, ROT, 2, dtype=F32) / ROT)
    ang = positions.astype(F32)[..., None] * inv_freq
    cos8, sin8 = jnp.cos(ang), jnp.sin(ang)
    lane = jnp.arange(KV_W) % HD
    freq = jnp.arange(half)[:, None]
    first, second = (lane[None, :] == freq), (lane[None, :] == freq + half)
    rope_place = jnp.stack([(first | second).astype(F32), second.astype(F32) - first.astype(F32)])
    rope_pad = (lane >= ROT).astype(F32).reshape(1, KV_W)
    seg = jnp.arange(RW) // HD
    ones = (seg[:, None] == seg[None, :]).astype(BF16)
    idx = jnp.arange(128)
    tri = (idx[:, None] < idx[None, :]).astype(BF16)
    cos_q, sin_q = jnp.swapaxes(cos8, 1, 2), jnp.swapaxes(sin8, 1, 2)
    for l in range(depth):
        mod3 = _mod_call(c, w_ada[l], b_ada[l]).reshape(B, 6, D)
        x = _layer(x, mod3, rope_place, rope_pad, cos_q, sin_q, ones, tri, norm1_g[l], w_in[l], mu_prev[l], mu_next[l],
                   rwkv_w0[l], rwkv_w2[l], rwkv_a0[l], rwkv_a2[l], rwkv_g2[l], rwkv_k_k[l], rwkv_k_a[l],
                   rwkv_r_k[l], rwkv_gn_w[l], rwkv_gn_b[l], q_norm_g[l], k_norm_g[l], attn_sink[l],
                   p_rwkv[l], p_attn[l], w_out[l], norm2_g[l], w_router[l], w_gate[l], w_up[l], w_down[l])
    return x
```
